```python
import math
import jax
import jax.numpy as jnp
from jax import lax
import numpy as np

D_MODEL = 1024
BATCH = 1
SEQ = 16384
DEPTH = 2
DEC_BATCH = 8
DEC_SEQ = 32
PAST_LEN = 2048

CHUNK = 64
D_MIX = D_MODEL
A_WIDTH = D_MIX // 4
A_GROUPS = 4
A_GROUP_DIM = A_WIDTH // A_GROUPS
GMLP_CHUNK = 128
B_HEAD_DIM = 64
B_WIDTH = 3 * D_MIX // 8
B_HEADS = B_WIDTH // B_HEAD_DIM
SB_BLOCK = 128
C_HEADS = 4
C_WIDTH = D_MIX - A_WIDTH - B_WIDTH
C_HEAD_DIM = C_WIDTH // C_HEADS
MLSTM_BLOCK = CHUNK
FORGET_BIAS = 3.0
D_FF = ((8 * D_MODEL // 3 + 127) // 128) * 128
N_EXPERTS = 8
TOP_K = 2
N_DENSE = (DEPTH + 1) // 2
N_MOE = DEPTH // 2
EPS = 1e-6
OFF_AU = 0
OFF_AV = OFF_AU + A_WIDTH
OFF_BQ = OFF_AV + A_WIDTH
OFF_BK = OFF_BQ + B_WIDTH
OFF_BV = OFF_BK + B_WIDTH
OFF_CQ = OFF_BV + B_WIDTH
OFF_CK = OFF_CQ + C_WIDTH
OFF_CV = OFF_CK + C_WIDTH
OFF_CO = OFF_CV + C_WIDTH
OFF_CG = OFF_CO + C_WIDTH
PROJ_DIM = OFF_CG + 2 * C_HEADS

kernel_name = 'hybrid_gmlp_stickbreak_mlstm_stream_step'


def rmsnorm(x, g):
    xf = x.astype(jnp.float32)
    xf = xf * lax.rsqrt(jnp.mean(xf * xf, axis=-1, keepdims=True) + EPS)
    return (xf * g.astype(jnp.float32)).astype(x.dtype)


def head_rmsnorm(h, g):
    n_h, d = h.shape[-2], h.shape[-1]
    hf = h.astype(jnp.float32)
    hf = hf * lax.rsqrt(jnp.mean(hf * hf, axis=-1, keepdims=True) + EPS)
    hf = hf * g.reshape(n_h, d).astype(jnp.float32)
    return hf.reshape(h.shape[:-2] + (n_h * d,))


def layernorm(x, g, b):
    xf = x.astype(jnp.float32)
    xc = xf - jnp.mean(xf, axis=-1, keepdims=True)
    xc = xc * lax.rsqrt(jnp.mean(xc * xc, axis=-1, keepdims=True) + EPS)
    return (xc * g.astype(jnp.float32) + b.astype(jnp.float32)).astype(x.dtype)


def project(xn, w_in, b_gate, ln_g, ln_b):
    z = jnp.einsum('bsd,dp->bsp', xn, w_in)
    bsz, n = z.shape[0], z.shape[1]
    cut = lambda o, w: z[..., o:o + w]
    u = jax.nn.gelu(cut(OFF_AU, A_WIDTH), approximate=False)
    va = layernorm(jax.nn.gelu(cut(OFF_AV, A_WIDTH), approximate=False), ln_g, ln_b)
    hb = lambda o: cut(o, B_WIDTH).reshape(bsz, n, B_HEADS, B_HEAD_DIM)
    hc = lambda o: cut(o, C_WIDTH).reshape(bsz, n, C_HEADS, C_HEAD_DIM)
    gates = cut(OFF_CG, 2 * C_HEADS).astype(jnp.float32) + b_gate.astype(jnp.float32)
    ig = gates[..., :C_HEADS]
    lf = jax.nn.log_sigmoid(gates[..., C_HEADS:])
    return (u, va, hb(OFF_BQ), hb(OFF_BK), hb(OFF_BV),
            hc(OFF_CQ), hc(OFF_CK), hc(OFF_CV), cut(OFF_CO, C_WIDTH), ig, lf)


def gmlp_mask():
    pos = jnp.arange(GMLP_CHUNK)
    return (pos[None, :] // CHUNK) <= (pos[:, None] // CHUNK)


def gmlp_spatial_prompt(v, w_s, b_s):
    bsz, s_len, _ = v.shape
    vb = v.reshape(bsz, s_len // GMLP_CHUNK, GMLP_CHUNK, A_GROUPS, A_GROUP_DIM)
    w = jnp.where(gmlp_mask()[None], w_s, 0).astype(v.dtype)
    s = jnp.einsum('gij,bnjgc->bnigc', w, vb) + b_s.T[None, None, :, :, None].astype(v.dtype)
    return s.reshape(bsz, s_len, A_WIDTH)


def gmlp_spatial_sample(v, w_s, b_s):
    bsz, n_new, _ = v.shape
    vb = v.reshape(bsz, n_new, A_GROUPS, A_GROUP_DIM)
    w = jnp.where(gmlp_mask()[None], w_s, 0)[:, :n_new, :n_new].astype(v.dtype)
    s = jnp.einsum('gij,bjgc->bigc', w, vb) + b_s.T[None, :n_new, :, None].astype(v.dtype)
    return s.reshape(bsz, n_new, A_WIDTH)


def stick_breaking(q, k, v, q_pos, k_pos):
    z = jnp.einsum('bqhd,bkhd->bhqk', q.astype(jnp.float32), k.astype(jnp.float32)) / math.sqrt(q.shape[-1])
    causal = k_pos[None, :] < q_pos[:, None]
    log_keep = jnp.where(causal, jax.nn.log_sigmoid(-z), 0.0)
    log_w = jax.nn.log_sigmoid(z) + lax.cumsum(log_keep, axis=3, reverse=True) - log_keep
    a = jnp.where(causal, jnp.exp(log_w), 0.0)
    return jnp.einsum('bhqk,bkhd->bqhd', a, v.astype(jnp.float32)).astype(v.dtype)


def stick_breaking_prompt(q, k, v):
    bsz, s_len = q.shape[0], q.shape[1]
    n_blk = s_len // SB_BLOCK
    q_blocks = jnp.moveaxis(q.reshape(bsz, n_blk, SB_BLOCK, B_HEADS, B_HEAD_DIM), 1, 0)
    starts = jnp.arange(n_blk) * SB_BLOCK
    k_pos = jnp.arange(s_len)

    def one_block(args):
        qb, start = args
        return stick_breaking(qb, k, v, start + jnp.arange(SB_BLOCK), k_pos)

    out = lax.map(one_block, (q_blocks, starts))
    return jnp.moveaxis(out, 0, 1).reshape(bsz, s_len, B_HEADS, B_HEAD_DIM)


def mlstm_chunk(state, q, k, v, ig, lf):
    c_prev, n_prev, m_prev = state
    f32 = jnp.float32
    qh = jnp.moveaxis(q.astype(f32), 2, 1)
    kh = jnp.moveaxis(k.astype(f32), 2, 1) * (C_HEAD_DIM ** -0.5)
    vh = jnp.moveaxis(v.astype(f32), 2, 1)
    igh = jnp.moveaxis(ig, 2, 1)
    bcum = jnp.cumsum(jnp.moveaxis(lf, 2, 1), axis=-1)
    n_blk = q.shape[1]
    causal = jnp.tril(jnp.ones((n_blk, n_blk), dtype=bool))
    log_d = jnp.where(causal, bcum[..., :, None] - bcum[..., None, :] + igh[..., None, :], -jnp.inf)
    log_past = bcum + m_prev[..., None]
    m_t = jnp.maximum(log_past, jnp.max(log_d, axis=-1))
    w = jnp.exp(log_d - m_t[..., None]) * jnp.einsum('bhtd,bhsd->bhts', qh, kh)
    past = jnp.exp(log_past - m_t)
    num = past[..., None] * jnp.einsum('bhtd,bhde->bhte', qh, c_prev) + jnp.einsum('bhts,bhse->bhte', w, vh)
    den = past * jnp.einsum('bhtd,bhd->bht', qh, n_prev) + jnp.sum(w, axis=-1)
    h = num / jnp.maximum(jnp.abs(den), jnp.exp(-m_t))[..., None]
    m_new = m_t[..., -1]
    w_end = jnp.exp(bcum[..., -1:] - bcum + igh - m_new[..., None])
    decay = jnp.exp(log_past[..., -1] - m_new)
    c_new = decay[..., None, None] * c_prev + jnp.einsum('bhs,bhsd,bhse->bhde', w_end, kh, vh)
    n_new = decay[..., None] * n_prev + jnp.einsum('bhs,bhsd->bhd', w_end, kh)
    return (c_new, n_new, m_new), jnp.moveaxis(h, 1, 2)


def mlstm_prompt(q, k, v, ig, lf):
    bsz, s_len = q.shape[0], q.shape[1]
    n_c = s_len // MLSTM_BLOCK
    to_blocks = lambda a: jnp.moveaxis(a.reshape((bsz, n_c, MLSTM_BLOCK) + a.shape[2:]), 1, 0)
    f32 = jnp.float32
    init = (jnp.zeros((bsz, C_HEADS, C_HEAD_DIM, C_HEAD_DIM), f32),
            jnp.zeros((bsz, C_HEADS, C_HEAD_DIM), f32),
            jnp.zeros((bsz, C_HEADS), f32))

    def step(state, blk):
        return mlstm_chunk(state, *blk)

    state, h = lax.scan(step, init, tuple(to_blocks(a) for a in (q, k, v, ig, lf)))
    h = jnp.moveaxis(h, 0, 1).reshape(bsz, s_len, C_HEADS, C_HEAD_DIM)
    return state, h


def merge(ya, yb, hc, o_pre, gn_b, gn_c, w_out):
    yb = head_rmsnorm(yb, gn_b).astype(ya.dtype)
    yc = (head_rmsnorm(hc, gn_c) * jax.nn.sigmoid(o_pre.astype(jnp.float32))).astype(ya.dtype)
    return jnp.einsum('bsm,md->bsd', jnp.concatenate([ya, yb, yc], axis=-1), w_out)


def swiglu(x, w_gate, w_up, w_down):
    hid = jax.nn.silu(jnp.einsum('bsd,df->bsf', x, w_gate)) * jnp.einsum('bsd,df->bsf', x, w_up)
    return jnp.einsum('bsf,fd->bsd', hid, w_down)


def moe_swiglu(x, w_router, w_gate, w_up, w_down):
    logits = jnp.einsum('bsd,de->bse', x, w_router).astype(jnp.float32)
    top_val, top_idx = lax.top_k(logits, TOP_K)
    gates = jax.nn.softmax(top_val, axis=-1)
    combine = jnp.einsum('bsk,bske->bse', gates,
                         jax.nn.one_hot(top_idx, N_EXPERTS, dtype=jnp.float32)).astype(x.dtype)
    out = jnp.zeros_like(x)
    for e in range(N_EXPERTS):
        out = out + combine[..., e:e + 1] * swiglu(x, w_gate[e], w_up[e], w_down[e])
    return out


def setup_inputs(seed: int = 0) -> dict:
    key = jax.random.key(seed)
    ks = jax.random.split(key, 27)
    nrm = lambda i, shape, scale: scale * jax.random.normal(ks[i], shape, jnp.float32)
    return {
        'x_prompt': nrm(0, (BATCH, SEQ, D_MODEL), 1.0),
        'x_sample': nrm(1, (DEC_BATCH, DEC_SEQ, D_MODEL), 1.0),
        'cache_k_b': nrm(2, (DEPTH, DEC_BATCH, PAST_LEN, B_HEADS, B_HEAD_DIM), 1.0),
        'cache_v_b': nrm(3, (DEPTH, DEC_BATCH, PAST_LEN, B_HEADS, B_HEAD_DIM), 1.0),
        'state_c_mlstm': nrm(4, (DEPTH, DEC_BATCH, C_HEADS, C_HEAD_DIM, C_HEAD_DIM), 0.3),
        'state_n_mlstm': nrm(5, (DEPTH, DEC_BATCH, C_HEADS, C_HEAD_DIM), 0.3),
        'state_m_mlstm': nrm(6, (DEPTH, DEC_BATCH, C_HEADS), 1.0),
        'g_mix': 1.0 + nrm(7, (DEPTH, D_MODEL), 0.05),
        'w_in': nrm(8, (DEPTH, D_MODEL, PROJ_DIM), D_MODEL ** -0.5),
        'b_gate': jnp.concatenate([nrm(9, (DEPTH, C_HEADS), 0.1),
                                   FORGET_BIAS + nrm(10, (DEPTH, C_HEADS), 0.5)], axis=-1),
        'ln_a_g': 1.0 + nrm(11, (DEPTH, A_WIDTH), 0.05),
        'ln_a_b': nrm(12, (DEPTH, A_WIDTH), 0.02),
        'w_s': nrm(13, (DEPTH, A_GROUPS, GMLP_CHUNK, GMLP_CHUNK), GMLP_CHUNK ** -0.5),
        'b_s': 1.0 + nrm(14, (DEPTH, A_GROUPS, GMLP_CHUNK), 0.1),
        'gn_b': 1.0 + nrm(15, (DEPTH, B_WIDTH), 0.05),
        'gn_c': 1.0 + nrm(16, (DEPTH, C_WIDTH), 0.05),
        'w_out': nrm(17, (DEPTH, D_MIX, D_MODEL), D_MIX ** -0.5),
        'g_ffn': 1.0 + nrm(18, (DEPTH, D_MODEL), 0.05),
        'w_gate_d': nrm(19, (N_DENSE, D_MODEL, D_FF), D_MODEL ** -0.5),
        'w_up_d': nrm(20, (N_DENSE, D_MODEL, D_FF), D_MODEL ** -0.5),
        'w_down_d': nrm(21, (N_DENSE, D_FF, D_MODEL), D_FF ** -0.5),
        'w_router': nrm(22, (N_MOE, D_MODEL, N_EXPERTS), D_MODEL ** -0.5),
        'w_gate_e': nrm(23, (N_MOE, N_EXPERTS, D_MODEL, D_FF), D_MODEL ** -0.5),
        'w_up_e': nrm(24, (N_MOE, N_EXPERTS, D_MODEL, D_FF), D_MODEL ** -0.5),
        'w_down_e': nrm(25, (N_MOE, N_EXPERTS, D_FF, D_MODEL), D_FF ** -0.5),
        'g_final': 1.0 + nrm(26, (D_MODEL,), 0.05),
    }


def reference(x_prompt, x_sample, cache_k_b, cache_v_b, state_c_mlstm, state_n_mlstm, state_m_mlstm,
              g_mix, w_in, b_gate, ln_a_g, ln_a_b, w_s, b_s, gn_b, gn_c, w_out,
              g_ffn, w_gate_d, w_up_d, w_down_d, w_router, w_gate_e, w_up_e, w_down_e, g_final):
    f32 = jnp.float32
    past = cache_k_b.shape[2]
    n_new = x_sample.shape[1]
    q_pos_s = past + jnp.arange(n_new)
    k_pos_s = jnp.arange(past + n_new)

    def channel_mixer(x, l):
        xn = rmsnorm(x, g_ffn[l])
        j = l // 2
        if l % 2 == 0:
            return swiglu(xn, w_gate_d[j], w_up_d[j], w_down_d[j])
        return moe_swiglu(xn, w_router[j], w_gate_e[j], w_up_e[j], w_down_e[j])

    kbp, vbp, cp, npr, mp = [], [], [], [], []
    kbs, vbs, cs, nsm, ms, vas = [], [], [], [], [], []
    yp, ys = x_prompt, x_sample
    for l in range(DEPTH):
        u, va, qb, kb, vb, qc, kc, vc, oc, ig, lf = project(
            rmsnorm(yp, g_mix[l]), w_in[l], b_gate[l], ln_a_g[l], ln_a_b[l])
        ya = u * gmlp_spatial_prompt(va, w_s[l], b_s[l])
        yb = stick_breaking_prompt(qb, kb, vb)
        (c_f, n_f, m_f), hc = mlstm_prompt(qc, kc, vc, ig, lf)
        yp = yp + merge(ya, yb, hc, oc, gn_b[l], gn_c[l], w_out[l])
        yp = yp + channel_mixer(yp, l)
        kbp.append(kb)
        vbp.append(vb)
        cp.append(c_f)
        npr.append(n_f)
        mp.append(m_f)

        u, va, qb, kb, vb, qc, kc, vc, oc, ig, lf = project(
            rmsnorm(ys, g_mix[l]), w_in[l], b_gate[l], ln_a_g[l], ln_a_b[l])
        ya = u * gmlp_spatial_sample(va, w_s[l], b_s[l])
        k_all = jnp.concatenate([cache_k_b[l].astype(kb.dtype), kb], axis=1)
        v_all = jnp.concatenate([cache_v_b[l].astype(vb.dtype), vb], axis=1)
        yb = stick_breaking(qb, k_all, v_all, q_pos_s, k_pos_s)
        state0 = (state_c_mlstm[l].astype(f32), state_n_mlstm[l].astype(f32), state_m_mlstm[l].astype(f32))
        (c_u, n_u, m_u), hc = mlstm_chunk(state0, qc, kc, vc, ig, lf)
        ys = ys + merge(ya, yb, hc, oc, gn_b[l], gn_c[l], w_out[l])
        ys = ys + channel_mixer(ys, l)
        kbs.append(kb)
        vbs.append(vb)
        cs.append(c_u)
        nsm.append(n_u)
        ms.append(m_u)
        vas.append(va)

    y_prompt = rmsnorm(yp, g_final)
    y_sample = rmsnorm(ys, g_final)
    return (y_prompt, y_sample,
            jnp.stack(kbp), jnp.stack(vbp), jnp.stack(cp), jnp.stack(npr), jnp.stack(mp),
            jnp.stack(kbs), jnp.stack(vbs), jnp.stack(cs), jnp.stack(nsm), jnp.stack(ms),
            jnp.stack(vas))
```

```python
import functools
import math

import jax
import jax.numpy as jnp
from jax import lax
from jax.experimental import pallas as pl
from jax.experimental.pallas import tpu as pltpu

F32 = jnp.float32
BF16 = jnp.bfloat16

D_MODEL = 1024
DEPTH = 2
EPS = 1e-6
CHUNK = 64
A_WIDTH = 256
A_GROUPS = 4
A_GROUP_DIM = 64
GMLP_CHUNK = 128
B_HEAD_DIM = 64
B_WIDTH = 384
B_HEADS = 6
B_PAIRS = 3
C_HEADS = 4
C_HEAD_DIM = 96
C_WIDTH = 384
D_FF = 2816
N_EXPERTS = 8

LANES = 128
C_HEAD_PAD = LANES
C_WIDTH_PAD = C_HEADS * C_HEAD_PAD

OFF_AU, OFF_AV, OFF_BQ, OFF_BK, OFF_BV = 0, 256, 512, 896, 1280
OFF_CQ, OFF_CK, OFF_CV, OFF_CO, OFF_CG = 1664, 2048, 2432, 2816, 3200
P_A = 0
P_BQ = 512
P_BK = P_BQ + B_WIDTH
P_BV = P_BK + B_WIDTH
P_CQ = P_BV + B_WIDTH
P_CK = P_CQ + C_WIDTH_PAD
P_CV = P_CK + C_WIDTH_PAD
P_CO = P_CV + C_WIDTH_PAD
P_CG = P_CO + C_WIDTH_PAD
P_DIM = P_CG + LANES

SB_DEAD_LOG_WEIGHT = -110.0
SB_BLOCK = 256
MLSTM_BLOCK = 128
NEG_BIG = -1e30

VMEM_LIMIT = 56 * 1024 * 1024


def _cparams(sem):
    return pltpu.CompilerParams(dimension_semantics=sem, vmem_limit_bytes=VMEM_LIMIT)


def _gelu(x):
    return 0.5 * x * (1.0 + lax.erf(x * (1.0 / math.sqrt(2.0))))


def _log_sigmoid(x):
    return jnp.minimum(x, 0.0) - jnp.log(1.0 + jnp.exp(-jnp.abs(x)))


def _sigmoid(x):
    return 1.0 / (1.0 + jnp.exp(-x))


def _split_bf16(x):
    hi = x.astype(BF16)
    lo = (x - hi.astype(F32)).astype(BF16)
    return hi, lo


def _dot(a, b):
    return jnp.dot(a, b, preferred_element_type=F32)


def _dot_nt(a, b):
    return lax.dot_general(a, b, (((1,), (1,)), ((), ())), preferred_element_type=F32)


def _dot_tn(a, b):
    return lax.dot_general(a, b, (((0,), (0,)), ((), ())), preferred_element_type=F32)


def _proj_kernel(n_chunks, chunk, x_ref, gmix_ref, w_ref, bg_ref, lng_ref, lnb_ref, ws_ref, bs_ref,
                 ya_ref, va_ref, qb_ref, kbf_ref, vbf_ref, kbh_ref, vbh_ref,
                 qc_ref, kc_ref, vc_ref, oc_ref, gt_ref):
    x = x_ref[...]
    xn = (x * lax.rsqrt(jnp.mean(x * x, axis=-1, keepdims=True) + EPS) * gmix_ref[...]).astype(BF16)

    def proj(off, width):
        return _dot(xn, w_ref[:, off:off + width])

    za = proj(P_A, 2 * A_WIDTH)
    u = _gelu(za[:, :A_WIDTH])
    gv = _gelu(za[:, A_WIDTH:])
    xc = gv - jnp.mean(gv, axis=-1, keepdims=True)
    va = xc * lax.rsqrt(jnp.mean(xc * xc, axis=-1, keepdims=True) + EPS) * lng_ref[...] + lnb_ref[...]
    va_ref[...] = va
    vab = va.astype(BF16)
    lane_group = lax.broadcasted_iota(jnp.int32, (chunk, A_WIDTH), 1) // A_GROUP_DIM
    for c in range(n_chunks):
        rows = slice(c * chunk, (c + 1) * chunk)
        vch = vab[rows]
        s = jnp.zeros((chunk, A_WIDTH), F32)
        for g in range(A_GROUPS):
            s = jnp.where(lane_group == g, _dot(ws_ref[g], vch), s)
        ya_ref[rows, :] = (u[rows] * (s + bs_ref[...])).astype(BF16)

    qb_ref[...] = (proj(P_BQ, B_WIDTH) * (1.0 / math.sqrt(B_HEAD_DIM))).astype(BF16)
    zk = proj(P_BK, B_WIDTH)
    kbf_ref[...] = zk
    kbh_ref[...] = zk.astype(BF16)
    zv = proj(P_BV, B_WIDTH)
    vbf_ref[...] = zv
    vbh_ref[...] = zv.astype(BF16)

    qc_ref[...] = proj(P_CQ, C_WIDTH_PAD).astype(BF16)
    kc_ref[...] = (proj(P_CK, C_WIDTH_PAD) * (C_HEAD_DIM ** -0.5)).astype(BF16)
    vc_ref[...] = proj(P_CV, C_WIDTH_PAD).astype(BF16)
    oc_ref[...] = proj(P_CO, C_WIDTH_PAD)
    g = proj(P_CG, LANES) + bg_ref[...]
    lane = lax.broadcasted_iota(jnp.int32, g.shape, 1)
    gt_ref[...] = jnp.where(lane < C_HEADS, g, _log_sigmoid(g))


def _proj(x, gmix, wp, bg, lng, lnb, ws, bs, tm, chunk):
    n = x.shape[0]
    row = lambda w: pl.BlockSpec((tm, w), lambda i: (i, 0))
    full = lambda a: pl.BlockSpec(a.shape, lambda i: (0,) * a.ndim)
    widths = [(A_WIDTH, BF16), (A_WIDTH, F32), (B_WIDTH, BF16), (B_WIDTH, F32), (B_WIDTH, F32),
              (B_WIDTH, BF16), (B_WIDTH, BF16), (C_WIDTH_PAD, BF16), (C_WIDTH_PAD, BF16),
              (C_WIDTH_PAD, BF16), (C_WIDTH_PAD, F32), (LANES, F32)]
    return pl.pallas_call(
        functools.partial(_proj_kernel, tm // chunk, chunk),
        out_shape=[jax.ShapeDtypeStruct((n, w), dt) for w, dt in widths],
        grid=(n // tm,),
        in_specs=[row(D_MODEL), full(gmix), full(wp), full(bg), full(lng), full(lnb), full(ws), full(bs)],
        out_specs=[row(w) for w, _ in widths],
        compiler_params=_cparams(("parallel",)),
        name="proj",
    )(x, gmix, wp, bg, lng, lnb, ws, bs)


def _sb_step(qh, kblk, vblk, carry, acc, tri, ones_m, mask):
    tk = kblk.shape[0]
    z = _dot_nt(qh, kblk)
    lk = jnp.minimum(-z, 0.0) - jnp.log(1.0 + jnp.exp(-jnp.abs(z)))
    if mask is not None:
        lk = jnp.where(mask, lk, 0.0)
    hi, lo = _split_bf16(lk)
    cs = _dot(hi, tri) + _dot(lo, tri)
    tot = _dot(hi, ones_m) + _dot(lo, ones_m)
    if tk >= LANES:
        carry_b = jnp.concatenate([carry] * (tk // LANES), axis=1)
    else:
        carry_b = carry[:, :tk]
    a = jnp.exp(z + cs + carry_b)
    if mask is not None:
        a = jnp.where(mask, a, 0.0)
    acc = acc + _dot(a.astype(BF16), vblk)
    return carry + tot, acc


def _tri_ones(tk):
    j = lax.broadcasted_iota(jnp.int32, (tk, tk), 0)
    s = lax.broadcasted_iota(jnp.int32, (tk, tk), 1)
    return (j >= s).astype(BF16), jnp.ones((tk, LANES), BF16)


def _sb_finish(acc_s, gn_ref, o_ref, head0):
    out = jnp.where(head0, acc_s[0], acc_s[1])
    sq = out * out
    ms0 = jnp.sum(jnp.where(head0, sq, 0.0), axis=-1, keepdims=True) * (1.0 / B_HEAD_DIM)
    ms1 = jnp.sum(jnp.where(head0, 0.0, sq), axis=-1, keepdims=True) * (1.0 / B_HEAD_DIM)
    rs = jnp.where(head0, lax.rsqrt(ms0 + EPS), lax.rsqrt(ms1 + EPS))
    return (out * rs * gn_ref[...]).astype(o_ref.dtype)


def _sb_walk(qh, load_kv, first_block, carry_s, acc_s, tri, ones_m):
    def alive():
        return jnp.maximum(jnp.max(carry_s[0]), jnp.max(carry_s[1])) >= SB_DEAD_LOG_WEIGHT

    def cond(st):
        j, live = st
        return jnp.logical_and(j >= 0, live)

    def body(st):
        j, _ = st
        kblk, vblk = load_kv(j)
        for h in range(2):
            carry, acc = _sb_step(qh[h], kblk, vblk, carry_s[h], acc_s[h], tri, ones_m, None)
            carry_s[h] = carry
            acc_s[h] = acc
        return j - 1, alive()

    lax.while_loop(cond, body, (first_block, alive()))


def _sb_prompt_kernel(q_ref, k_ref, v_ref, gn_ref, o_ref, carry_s, acc_s):
    tq = q_ref.shape[0]
    i = pl.program_id(1)
    head0 = lax.broadcasted_iota(jnp.int32, (tq, LANES), 1) < B_HEAD_DIM
    q = q_ref[...]
    zero = jnp.zeros_like(q)
    qh = [jnp.where(head0, q, zero), jnp.where(head0, zero, q)]
    tri, ones_m = _tri_ones(tq)
    t = lax.broadcasted_iota(jnp.int32, (tq, tq), 0)
    s = lax.broadcasted_iota(jnp.int32, (tq, tq), 1)
    causal = s < t

    def load_kv(j):
        rows = pl.ds(pl.multiple_of(j * tq, tq), tq)
        return k_ref[rows, :], v_ref[rows, :]

    kd, vd = load_kv(i)
    zeros = jnp.zeros((tq, LANES), F32)
    for h in range(2):
        carry, acc = _sb_step(qh[h], kd, vd, zeros, zeros, tri, ones_m, causal)
        carry_s[h] = carry
        acc_s[h] = acc
    _sb_walk(qh, load_kv, i - 1, carry_s, acc_s, tri, ones_m)
    o_ref[...] = _sb_finish(acc_s, gn_ref, o_ref, head0)


def _sb_prompt(q, k, v, gn):
    n = q.shape[0]
    tq = SB_BLOCK
    blk = pl.BlockSpec((tq, LANES), lambda p, i: (i, p))
    seq = pl.BlockSpec((n, LANES), lambda p, i: (0, p))
    return pl.pallas_call(
        _sb_prompt_kernel,
        out_shape=jax.ShapeDtypeStruct((n, B_WIDTH), BF16),
        grid=(B_PAIRS, n // tq),
        in_specs=[blk, seq, seq, pl.BlockSpec((1, LANES), lambda p, i: (0, p))],
        out_specs=blk,
        scratch_shapes=[pltpu.VMEM((2, tq, LANES), F32), pltpu.VMEM((2, tq, LANES), F32)],
        compiler_params=_cparams(("parallel", "parallel")),
        name="sb_prompt",
    )(q, k, v, gn)


def _sb_sample_kernel(q_ref, kn_ref, vn_ref, kc_ref, vc_ref, gn_ref, o_ref, carry_s, acc_s):
    tq = q_ref.shape[1]
    tk = SB_BLOCK
    head0 = lax.broadcasted_iota(jnp.int32, (tq, LANES), 1) < B_HEAD_DIM
    q = q_ref[0]
    zero = jnp.zeros_like(q)
    qh = [jnp.where(head0, q, zero), jnp.where(head0, zero, q)]
    tri_n, ones_n = _tri_ones(tq)
    t = lax.broadcasted_iota(jnp.int32, (tq, tq), 0)
    s = lax.broadcasted_iota(jnp.int32, (tq, tq), 1)
    causal = s < t
    zeros = jnp.zeros((tq, LANES), F32)
    for h in range(2):
        carry, acc = _sb_step(qh[h], kn_ref[0], vn_ref[0], zeros, zeros, tri_n, ones_n, causal)
        carry_s[h] = carry
        acc_s[h] = acc
    tri, ones_m = _tri_ones(tk)

    def load_kv(j):
        rows = pl.ds(pl.multiple_of(j * tk, tk), tk)
        return kc_ref[0, rows, :].astype(BF16), vc_ref[0, rows, :].astype(BF16)

    _sb_walk(qh, load_kv, kc_ref.shape[1] // tk - 1, carry_s, acc_s, tri, ones_m)
    o_ref[0] = _sb_finish(acc_s, gn_ref, o_ref, head0)


def _sb_sample(q, kn, vn, kc, vc, gn):
    nb, tq, _ = q.shape
    past = kc.shape[1]
    new = pl.BlockSpec((1, tq, LANES), lambda b, p: (b, 0, p))
    old = pl.BlockSpec((1, past, LANES), lambda b, p: (b, 0, p))
    return pl.pallas_call(
        _sb_sample_kernel,
        out_shape=jax.ShapeDtypeStruct((nb, tq, B_WIDTH), BF16),
        grid=(nb, B_PAIRS),
        in_specs=[new, new, new, old, old, pl.BlockSpec((1, LANES), lambda b, p: (0, p))],
        out_specs=new,
        scratch_shapes=[pltpu.VMEM((2, tq, LANES), F32), pltpu.VMEM((2, tq, LANES), F32)],
        compiler_params=_cparams(("parallel", "parallel")),
        name="sb_sample",
    )(q, kn, vn, kc, vc, gn)


def _mlstm_kernel(n_sub, q_ref, k_ref, v_ref, o_ref, gt_ref, gn_ref, c0_ref, n0_ref, m0_ref,
                  yc_ref, c_out, n_out, m_out, c_s, n_s, m_s):
    L = MLSTM_BLOCK
    t_blk = pl.program_id(1)

    @pl.when(t_blk == 0)
    def _():
        c_s[...] = c0_ref[0]
        n_s[...] = n0_ref[0]
        m_s[...] = m0_ref[0]

    r = lax.broadcasted_iota(jnp.int32, (L, L), 0)
    c = lax.broadcasted_iota(jnp.int32, (L, L), 1)
    causal = c <= r
    tril = causal.astype(BF16)

    for sub in range(n_sub):
        rows = slice(sub * L, (sub + 1) * L)
        gt = gt_ref[0, rows, :]
        hi, lo = _split_bf16(gt)
        bc = _dot(tril, hi) + _dot(tril, lo)
        gt_t = gt.T
        bc_t = bc.T
        for h in range(C_HEADS):
            lanes = slice(h * C_HEAD_PAD, (h + 1) * C_HEAD_PAD)
            q = q_ref[0, rows, lanes]
            k = k_ref[0, rows, lanes]
            v = v_ref[0, rows, lanes]
            ig_col = gt[:, h:h + 1]
            ig_row = gt_t[h:h + 1, :]
            b_col = bc[:, C_HEADS + h:C_HEADS + h + 1]
            b_row = bc_t[C_HEADS + h:C_HEADS + h + 1, :]
            m_prev = m_s[h][:, :1]
            c_prev = c_s[h]
            n_prev = n_s[h]

            log_d = jnp.where(causal, b_col - b_row + ig_row, -jnp.inf)
            log_past = b_col + m_prev
            m_t = jnp.maximum(log_past, jnp.max(log_d, axis=-1, keepdims=True))
            w = jnp.exp(log_d - m_t) * _dot_nt(q, k)
            past = jnp.exp(log_past - m_t)
            num = past * _dot(q, c_prev.astype(BF16)) + _dot(w.astype(BF16), v)
            den = past * jnp.sum(q.astype(F32) * n_prev, axis=-1, keepdims=True) \
                + jnp.sum(w, axis=-1, keepdims=True)
            hh = num / jnp.maximum(jnp.abs(den), jnp.exp(-m_t))

            m_new = m_t[L - 1:L, :]
            w_end = jnp.exp(b_col[L - 1:L, :] - b_col + ig_col - m_new)
            decay = jnp.exp(log_past[L - 1:L, :] - m_new)
            kw = k.astype(F32) * w_end
            c_s[h] = decay * c_prev + _dot_tn(kw.astype(BF16), v)
            n_s[h] = decay * n_prev + jnp.sum(kw, axis=0, keepdims=True)
            m_s[h] = jnp.broadcast_to(m_new, (1, LANES))

            ms = jnp.sum(hh * hh, axis=-1, keepdims=True) * (1.0 / C_HEAD_DIM)
            hn = hh * lax.rsqrt(ms + EPS) * gn_ref[:, lanes]
            yc_ref[0, rows, lanes] = (hn * _sigmoid(o_ref[0, rows, lanes])).astype(BF16)

    @pl.when(t_blk == pl.num_programs(1) - 1)
    def _():
        c_out[0] = c_s[...]
        n_out[0] = n_s[...]
        m_out[0] = m_s[...]


def _mlstm(q, k, v, o, gt, gn, c0, n0, m0, n_sub):
    nb, n, _ = q.shape
    tb = n_sub * MLSTM_BLOCK
    seq = lambda w: pl.BlockSpec((1, tb, w), lambda b, t: (b, t, 0))
    st = lambda a: pl.BlockSpec((1,) + a.shape[1:], lambda b, t: (b,) + (0,) * (a.ndim - 1))
    return pl.pallas_call(
        functools.partial(_mlstm_kernel, n_sub),
        out_shape=[jax.ShapeDtypeStruct((nb, n, C_WIDTH_PAD), BF16),
                   jax.ShapeDtypeStruct(c0.shape, F32),
                   jax.ShapeDtypeStruct(n0.shape, F32),
                   jax.ShapeDtypeStruct(m0.shape, F32)],
        grid=(nb, n // tb),
        in_specs=[seq(C_WIDTH_PAD), seq(C_WIDTH_PAD), seq(C_WIDTH_PAD), seq(C_WIDTH_PAD), seq(LANES),
                  pl.BlockSpec((1, C_WIDTH_PAD), lambda b, t: (0, 0)), st(c0), st(n0), st(m0)],
        out_specs=[seq(C_WIDTH_PAD), st(c0), st(n0), st(m0)],
        scratch_shapes=[pltpu.VMEM(c0.shape[1:], F32), pltpu.VMEM(n0.shape[1:], F32),
                        pltpu.VMEM(m0.shape[1:], F32)],
        compiler_params=_cparams(("parallel", "arbitrary")),
        name="mlstm",
    )(q, k, v, o, gt, gn, c0, n0, m0)


def _merge_kernel(moe, x_ref, ya_ref, yb_ref, yc_ref, wa_ref, wb_ref, wc_ref, g_ref, wr_ref,
                  xmid_ref, xn_ref, comb_ref):
    y = _dot(ya_ref[...], wa_ref[...]) + _dot(yb_ref[...], wb_ref[...]) + _dot(yc_ref[...], wc_ref[...])
    x = x_ref[...] + y
    xmid_ref[...] = x
    xn = x * lax.rsqrt(jnp.mean(x * x, axis=-1, keepdims=True) + EPS) * g_ref[...]
    xn_ref[...] = xn.astype(BF16)
    if moe:
        logits = jnp.dot(xn, wr_ref[...], preferred_element_type=F32, precision=lax.Precision.HIGHEST)
        lane = lax.broadcasted_iota(jnp.int32, logits.shape, 1)
        lg = jnp.where(lane < N_EXPERTS, logits, -jnp.inf)
        m1 = jnp.max(lg, axis=-1, keepdims=True)
        i1 = jnp.min(jnp.where(lg == m1, lane, LANES), axis=-1, keepdims=True)
        lg2 = jnp.where(lane == i1, -jnp.inf, lg)
        m2 = jnp.max(lg2, axis=-1, keepdims=True)
        i2 = jnp.min(jnp.where(lg2 == m2, lane, LANES), axis=-1, keepdims=True)
        e2 = jnp.exp(m2 - m1)
        g1 = 1.0 / (1.0 + e2)
        comb_ref[...] = jnp.where(lane == i1, g1, 0.0) + jnp.where(lane == i2, e2 * g1, 0.0)
    else:
        comb_ref[...] = jnp.zeros(comb_ref.shape, F32)


def _merge(x, ya, yb, yc, wa, wb, wc, g, wr, tm, moe):
    n = x.shape[0]
    row = lambda w: pl.BlockSpec((tm, w), lambda i: (i, 0))
    full = lambda a: pl.BlockSpec(a.shape, lambda i: (0,) * a.ndim)
    return pl.pallas_call(
        functools.partial(_merge_kernel, moe),
        out_shape=[jax.ShapeDtypeStruct((n, D_MODEL), F32), jax.ShapeDtypeStruct((n, D_MODEL), BF16),
                   jax.ShapeDtypeStruct((n, LANES), F32)],
        grid=(n // tm,),
        in_specs=[row(D_MODEL), row(A_WIDTH), row(B_WIDTH), row(C_WIDTH_PAD),
                  full(wa), full(wb), full(wc), full(g), full(wr)],
        out_specs=[row(D_MODEL), row(D_MODEL), row(LANES)],
        compiler_params=_cparams(("parallel",)),
        name="merge",
    )(x, ya, yb, yc, wa, wb, wc, g, wr)


def _ffn_kernel(weighted, final, xn_ref, comb_ref, wg_ref, wu_ref, wd_ref, xmid_ref, gf_ref,
                o_ref, acc_s):
    e = pl.program_id(1)
    f = pl.program_id(2)

    @pl.when(jnp.logical_and(e == 0, f == 0))
    def _():
        acc_s[...] = jnp.zeros_like(acc_s)

    xn = xn_ref[...]
    g = _dot(xn, wg_ref[0])
    u = _dot(xn, wu_ref[0])
    hid = g * _sigmoid(g) * u
    if weighted:
        lane = lax.broadcasted_iota(jnp.int32, comb_ref.shape, 1)
        hid = hid * jnp.sum(jnp.where(lane == e, comb_ref[...], 0.0), axis=-1, keepdims=True)
    acc_s[...] += _dot(hid.astype(BF16), wd_ref[0])

    @pl.when(jnp.logical_and(e == pl.num_programs(1) - 1, f == pl.num_programs(2) - 1))
    def _():
        y = xmid_ref[...] + acc_s[...]
        if final:
            y = y * lax.rsqrt(jnp.mean(y * y, axis=-1, keepdims=True) + EPS) * gf_ref[...]
        o_ref[...] = y


def _ffn(xn, comb, wg, wu, wd, xmid, gf, tm, tf, weighted, final):
    n = xn.shape[0]
    n_e = wg.shape[0]
    row = lambda w: pl.BlockSpec((tm, w), lambda i, e, f: (i, 0))
    return pl.pallas_call(
        functools.partial(_ffn_kernel, weighted, final),
        out_shape=jax.ShapeDtypeStruct((n, D_MODEL), F32),
        grid=(n // tm, n_e, D_FF // tf),
        in_specs=[row(D_MODEL), row(LANES),
                  pl.BlockSpec((1, D_MODEL, tf), lambda i, e, f: (e, 0, f)),
                  pl.BlockSpec((1, D_MODEL, tf), lambda i, e, f: (e, 0, f)),
                  pl.BlockSpec((1, tf, D_MODEL), lambda i, e, f: (e, f, 0)),
                  row(D_MODEL), pl.BlockSpec((1, D_MODEL), lambda i, e, f: (0, 0))],
        out_specs=row(D_MODEL),
        scratch_shapes=[pltpu.VMEM((tm, D_MODEL), F32)],
        compiler_params=_cparams(("parallel", "arbitrary", "arbitrary")),
        name="ffn",
    )(xn, comb, wg, wu, wd, xmid, gf)


def _pad_heads_cols(w):
    w = w.reshape(w.shape[0], C_HEADS, C_HEAD_DIM)
    return jnp.pad(w, ((0, 0), (0, 0), (0, C_HEAD_PAD - C_HEAD_DIM))).reshape(w.shape[0], C_WIDTH_PAD)


def _layer_params(l, w_in, b_gate, w_s, b_s, gn_c, w_out):
    w = w_in[l]
    gates = jnp.pad(w[:, OFF_CG:OFF_CG + 2 * C_HEADS], ((0, 0), (0, LANES - 2 * C_HEADS)))
    wp = jnp.concatenate(
        [w[:, :OFF_CQ]] + [_pad_heads_cols(w[:, o:o + C_WIDTH]) for o in (OFF_CQ, OFF_CK, OFF_CV, OFF_CO)]
        + [gates], axis=1).astype(BF16)
    bg = jnp.pad(b_gate[l], (0, LANES - 2 * C_HEADS))[None, :]
    pos = jnp.arange(GMLP_CHUNK)
    mask = (pos[None, :] // CHUNK) <= (pos[:, None] // CHUNK)
    wm = jnp.where(mask[None], w_s[l], 0.0)
    bs = jnp.repeat(b_s[l].T, A_GROUP_DIM, axis=1)
    wo = w_out[l]
    wc = wo[A_WIDTH + B_WIDTH:].reshape(C_HEADS, C_HEAD_DIM, D_MODEL)
    wc = jnp.pad(wc, ((0, 0), (0, C_HEAD_PAD - C_HEAD_DIM), (0, 0))).reshape(C_WIDTH_PAD, D_MODEL)
    gnc = _pad_heads_cols(gn_c[l][None, :])
    return dict(wp=wp, bg=bg, wm=wm, bs=bs, wa=wo[:A_WIDTH].astype(BF16),
                wb=wo[A_WIDTH:A_WIDTH + B_WIDTH].astype(BF16), wc=wc.astype(BF16), gnc=gnc)


def _pad_state(c, n, m):
    p = C_HEAD_PAD - C_HEAD_DIM
    c = jnp.pad(c, ((0, 0), (0, 0), (0, p), (0, p)))
    n = jnp.pad(n, ((0, 0), (0, 0), (0, p)))[:, :, None, :]
    m = jnp.broadcast_to(m[:, :, None, None], m.shape + (1, LANES))
    return c, n, m


def _unpad_state(c, n, m):
    return c[:, :, :C_HEAD_DIM, :C_HEAD_DIM], n[:, :, 0, :C_HEAD_DIM], m[:, :, 0, 0]


def kernel(x_prompt, x_sample, cache_k_b, cache_v_b, state_c_mlstm, state_n_mlstm, state_m_mlstm,
           g_mix, w_in, b_gate, ln_a_g, ln_a_b, w_s, b_s, gn_b, gn_c, w_out,
           g_ffn, w_gate_d, w_up_d, w_down_d, w_router, w_gate_e, w_up_e, w_down_e, g_final):
    n_seq = x_prompt.shape[1]
    n_dec, n_new = x_sample.shape[0], x_sample.shape[1]
    past = cache_k_b.shape[2]
    n_samp = n_dec * n_new

    xp = x_prompt.reshape(n_seq, D_MODEL)
    xs = x_sample.reshape(n_samp, D_MODEL)
    gfin = g_final[None, :]
    wr_zero = jnp.zeros((D_MODEL, LANES), F32)
    ones_e = None

    outs = {k: [] for k in ("kbp", "vbp", "cp", "np", "mp", "kbs", "vbs", "cs", "ns", "ms", "vas")}
    for l in range(DEPTH):
        p = _layer_params(l, w_in, b_gate, w_s, b_s, gn_c, w_out)
        gmix = g_mix[l][None, :]
        lng, lnb = ln_a_g[l][None, :], ln_a_b[l][None, :]
        gnb = gn_b[l][None, :]
        gffn = g_ffn[l][None, :]
        moe = l % 2 == 1
        j = l // 2
        if moe:
            wg, wu, wd = w_gate_e[j].astype(BF16), w_up_e[j].astype(BF16), w_down_e[j].astype(BF16)
            wr = jnp.pad(w_router[j], ((0, 0), (0, LANES - N_EXPERTS)))
        else:
            wg, wu, wd = (w_gate_d[j][None].astype(BF16), w_up_d[j][None].astype(BF16),
                          w_down_d[j][None].astype(BF16))
            wr = wr_zero
        final = l == DEPTH - 1

        wm_s = jnp.kron(jnp.eye(n_dec, dtype=F32), p["wm"][:, :n_new, :n_new])
        bs_s = jnp.tile(p["bs"][:n_new], (n_dec, 1))

        (ya, _, qb, kbf, vbf, kbh, vbh, qc, kc, vc, oc, gt) = _proj(
            xp, gmix, p["wp"], p["bg"], lng, lnb, p["wm"].astype(BF16), p["bs"], 512, GMLP_CHUNK)
        yb = _sb_prompt(qb, kbh, vbh, gnb)
        c0, n0, m0 = _pad_state(jnp.zeros((1, C_HEADS, C_HEAD_DIM, C_HEAD_DIM), F32),
                                jnp.zeros((1, C_HEADS, C_HEAD_DIM), F32), jnp.zeros((1, C_HEADS), F32))
        yc, c_f, n_f, m_f = _mlstm(qc[None], kc[None], vc[None], oc[None], gt[None], p["gnc"],
                                   c0, n0, m0, 2)
        xmid, xn, comb = _merge(xp, ya, yb, yc[0], p["wa"], p["wb"], p["wc"], gffn, wr, 512, moe)
        xp = _ffn(xn, comb, wg, wu, wd, xmid, gfin, 512, 1408, moe, final)
        c_f, n_f, m_f = _unpad_state(c_f, n_f, m_f)
        outs["kbp"].append(kbf.reshape(1, n_seq, B_HEADS, B_HEAD_DIM))
        outs["vbp"].append(vbf.reshape(1, n_seq, B_HEADS, B_HEAD_DIM))
        outs["cp"].append(c_f)
        outs["np"].append(n_f)
        outs["mp"].append(m_f)

        (ya, va, qb, kbf, vbf, kbh, vbh, qc, kc, vc, oc, gt) = _proj(
            xs, gmix, p["wp"], p["bg"], lng, lnb, wm_s.astype(BF16), bs_s, n_samp, n_samp)
        r3 = lambda a: a.reshape(n_dec, n_new, a.shape[-1])
        yb = _sb_sample(r3(qb), r3(kbh), r3(vbh), cache_k_b[l].reshape(n_dec, past, B_WIDTH),
                        cache_v_b[l].reshape(n_dec, past, B_WIDTH), gnb)
        padr = lambda a: jnp.pad(r3(a), ((0, 0), (0, MLSTM_BLOCK - n_new), (0, 0)))
        lane = jnp.arange(LANES)
        gt_pad = jnp.broadcast_to(jnp.where(lane < C_HEADS, NEG_BIG, 0.0).astype(F32),
                                  (n_dec, MLSTM_BLOCK - n_new, LANES))
        gt_s = jnp.concatenate([r3(gt), gt_pad], axis=1)
        c0, n0, m0 = _pad_state(state_c_mlstm[l], state_n_mlstm[l], state_m_mlstm[l])
        yc, c_u, n_u, m_u = _mlstm(padr(qc), padr(kc), padr(vc), padr(oc), gt_s, p["gnc"], c0, n0, m0, 1)
        yc = yc[:, :n_new].reshape(n_samp, C_WIDTH_PAD)
        xmid, xn, comb = _merge(xs, ya, yb.reshape(n_samp, B_WIDTH), yc, p["wa"], p["wb"], p["wc"],
                                gffn, wr, n_samp, moe)
        xs = _ffn(xn, comb, wg, wu, wd, xmid, gfin, n_samp, 1408, moe, final)
        c_u, n_u, m_u = _unpad_state(c_u, n_u, m_u)
        outs["kbs"].append(kbf.reshape(n_dec, n_new, B_HEADS, B_HEAD_DIM))
        outs["vbs"].append(vbf.reshape(n_dec, n_new, B_HEADS, B_HEAD_DIM))
        outs["cs"].append(c_u)
        outs["ns"].append(n_u)
        outs["ms"].append(m_u)
        outs["vas"].append(va.reshape(n_dec, n_new, A_WIDTH))

    st = lambda k: jnp.stack(outs[k])
    return (xp.reshape(1, n_seq, D_MODEL), xs.reshape(n_dec, n_new, D_MODEL),
            st("kbp"), st("vbp"), st("cp"), st("np"), st("mp"),
            st("kbs"), st("vbs"), st("cs"), st("ns"), st("ms"), st("vas"))
```

```python
import functools
import math

import jax
import jax.numpy as jnp
from jax import lax
from jax.experimental import pallas as pl
from jax.experimental.pallas import tpu as pltpu

F32 = jnp.float32
BF16 = jnp.bfloat16

D_MODEL = 1024
DEPTH = 2
EPS = 1e-6
CHUNK = 64
A_WIDTH = 256
A_GROUPS = 4
A_GROUP_DIM = 64
GMLP_CHUNK = 128
B_HEAD_DIM = 64
B_WIDTH = 384
B_HEADS = 6
B_PAIRS = 3
C_HEADS = 4
C_HEAD_DIM = 96
C_WIDTH = 384
D_FF = 2816
N_EXPERTS = 8

LANES = 128
C_HEAD_PAD = LANES
C_WIDTH_PAD = C_HEADS * C_HEAD_PAD

OFF_AU, OFF_AV, OFF_BQ, OFF_BK, OFF_BV = 0, 256, 512, 896, 1280
OFF_CQ, OFF_CK, OFF_CV, OFF_CO, OFF_CG = 1664, 2048, 2432, 2816, 3200
P_A = 0
P_BQ = 512
P_BK = P_BQ + B_WIDTH
P_BV = P_BK + B_WIDTH
P_CQ = P_BV + B_WIDTH
P_CK = P_CQ + C_WIDTH_PAD
P_CV = P_CK + C_WIDTH_PAD
P_CO = P_CV + C_WIDTH_PAD
P_CG = P_CO + C_WIDTH_PAD
P_DIM = P_CG + LANES

SB_DEAD_LOG_WEIGHT = -110.0
SB_BLOCK = 256
MLSTM_BLOCK = 128
NEG_BIG = -1e30
ROW_TILE = 8
MOE_TILE = 512
FFN_F_TILE = 1408
ROW_BLOCK = 512
MOE_COMBINE_BLOCK = 256

VMEM_LIMIT = 56 * 1024 * 1024


def _cparams(sem):
    return pltpu.CompilerParams(dimension_semantics=sem, vmem_limit_bytes=VMEM_LIMIT)


def _gelu(x):
    return 0.5 * x * (1.0 + lax.erf(x * (1.0 / math.sqrt(2.0))))


def _log_sigmoid(x):
    return jnp.minimum(x, 0.0) - jnp.log(1.0 + jnp.exp(-jnp.abs(x)))


def _sigmoid(x):
    return 1.0 / (1.0 + jnp.exp(-x))


def _split_bf16(x):
    hi = x.astype(BF16)
    lo = (x - hi.astype(F32)).astype(BF16)
    return hi, lo


def _dot(a, b):
    return jnp.dot(a, b, preferred_element_type=F32)


def _dot_nt(a, b):
    return lax.dot_general(a, b, (((1,), (1,)), ((), ())), preferred_element_type=F32)


def _dot_tn(a, b):
    return lax.dot_general(a, b, (((0,), (0,)), ((), ())), preferred_element_type=F32)


def _proj_kernel(n_chunks, chunk, with_va, x_ref, gmix_ref, w_ref, bg_ref, lng_ref, lnb_ref, ws_ref, bs_ref,
                 ya_ref, qb_ref, kbf_ref, vbf_ref, kbh_ref, vbh_ref,
                 qc_ref, kc_ref, vc_ref, oc_ref, gt_ref, *va_ref):
    x = x_ref[...]
    xn = (x * lax.rsqrt(jnp.mean(x * x, axis=-1, keepdims=True) + EPS) * gmix_ref[...]).astype(BF16)

    def proj(off, width):
        return _dot(xn, w_ref[:, off:off + width])

    za = proj(P_A, 2 * A_WIDTH)
    u = _gelu(za[:, :A_WIDTH])
    gv = _gelu(za[:, A_WIDTH:])
    xc = gv - jnp.mean(gv, axis=-1, keepdims=True)
    va = xc * lax.rsqrt(jnp.mean(xc * xc, axis=-1, keepdims=True) + EPS) * lng_ref[...] + lnb_ref[...]
    if with_va:
        va_ref[0][...] = va
    vab = va.astype(BF16)
    lane_group = lax.broadcasted_iota(jnp.int32, (chunk, A_WIDTH), 1) // A_GROUP_DIM
    for c in range(n_chunks):
        rows = slice(c * chunk, (c + 1) * chunk)
        vch = vab[rows]
        s = jnp.zeros((chunk, A_WIDTH), F32)
        for g in range(A_GROUPS):
            s = jnp.where(lane_group == g, _dot(ws_ref[g], vch), s)
        ya_ref[rows, :] = (u[rows] * (s + bs_ref[...])).astype(BF16)

    qb_ref[...] = (proj(P_BQ, B_WIDTH) * (1.0 / math.sqrt(B_HEAD_DIM))).astype(BF16)
    zk = proj(P_BK, B_WIDTH)
    kbf_ref[...] = zk
    kbh_ref[...] = zk.astype(BF16)
    zv = proj(P_BV, B_WIDTH)
    vbf_ref[...] = zv
    vbh_ref[...] = zv.astype(BF16)

    qc_ref[...] = proj(P_CQ, C_WIDTH_PAD).astype(BF16)
    kc_ref[...] = (proj(P_CK, C_WIDTH_PAD) * (C_HEAD_DIM ** -0.5)).astype(BF16)
    vc_ref[...] = proj(P_CV, C_WIDTH_PAD).astype(BF16)
    oc_ref[...] = proj(P_CO, C_WIDTH_PAD)
    g = proj(P_CG, LANES) + bg_ref[...]
    lane = lax.broadcasted_iota(jnp.int32, g.shape, 1)
    gt_ref[...] = jnp.where(lane < C_HEADS, g, _log_sigmoid(g))


def _proj(x, gmix, wp, bg, lng, lnb, ws, bs, tm, chunk, with_va):
    n = x.shape[0]
    row = lambda w: pl.BlockSpec((tm, w), lambda i: (i, 0))
    full = lambda a: pl.BlockSpec(a.shape, lambda i: (0,) * a.ndim)
    widths = [(A_WIDTH, BF16), (B_WIDTH, BF16), (B_WIDTH, F32), (B_WIDTH, F32),
              (B_WIDTH, BF16), (B_WIDTH, BF16), (C_WIDTH_PAD, BF16), (C_WIDTH_PAD, BF16),
              (C_WIDTH_PAD, BF16), (C_WIDTH_PAD, F32), (LANES, F32)]
    if with_va:
        widths.append((A_WIDTH, F32))
    return pl.pallas_call(
        functools.partial(_proj_kernel, tm // chunk, chunk, with_va),
        out_shape=[jax.ShapeDtypeStruct((n, w), dt) for w, dt in widths],
        grid=(n // tm,),
        in_specs=[row(D_MODEL), full(gmix), full(wp), full(bg), full(lng), full(lnb), full(ws), full(bs)],
        out_specs=[row(w) for w, _ in widths],
        compiler_params=_cparams(("parallel",)),
        name="proj",
    )(x, gmix, wp, bg, lng, lnb, ws, bs)


def _sb_step(qh, kblk, vblk, carry, acc, tri, ones_m, mask):
    tk = kblk.shape[0]
    z = _dot_nt(qh, kblk)
    lk = jnp.minimum(-z, 0.0) - jnp.log(1.0 + jnp.exp(-jnp.abs(z)))
    if mask is not None:
        lk = jnp.where(mask, lk, 0.0)
    hi, lo = _split_bf16(lk)
    cs = _dot(hi, tri) + _dot(lo, tri)
    tot = _dot(hi, ones_m) + _dot(lo, ones_m)
    if tk >= LANES:
        carry_b = jnp.concatenate([carry] * (tk // LANES), axis=1)
    else:
        carry_b = carry[:, :tk]
    a = jnp.exp(z + cs + carry_b)
    if mask is not None:
        a = jnp.where(mask, a, 0.0)
    acc = acc + _dot(a.astype(BF16), vblk)
    return carry + tot, acc


def _tri_ones(tk):
    j = lax.broadcasted_iota(jnp.int32, (tk, tk), 0)
    s = lax.broadcasted_iota(jnp.int32, (tk, tk), 1)
    return (j >= s).astype(BF16), jnp.ones((tk, LANES), BF16)


def _sb_finish(acc_s, gn_ref, o_ref, head0):
    out = jnp.where(head0, acc_s[0], acc_s[1])
    sq = out * out
    ms0 = jnp.sum(jnp.where(head0, sq, 0.0), axis=-1, keepdims=True) * (1.0 / B_HEAD_DIM)
    ms1 = jnp.sum(jnp.where(head0, 0.0, sq), axis=-1, keepdims=True) * (1.0 / B_HEAD_DIM)
    rs = jnp.where(head0, lax.rsqrt(ms0 + EPS), lax.rsqrt(ms1 + EPS))
    return (out * rs * gn_ref[...]).astype(o_ref.dtype)


def _sb_walk(qh, load_kv, first_block, carry_s, acc_s, tri, ones_m):
    def alive():
        return jnp.maximum(jnp.max(carry_s[0]), jnp.max(carry_s[1])) >= SB_DEAD_LOG_WEIGHT

    def cond(st):
        j, live = st
        return jnp.logical_and(j >= 0, live)

    def body(st):
        j, _ = st
        kblk, vblk = load_kv(j)
        for h in range(2):
            carry, acc = _sb_step(qh[h], kblk, vblk, carry_s[h], acc_s[h], tri, ones_m, None)
            carry_s[h] = carry
            acc_s[h] = acc
        return j - 1, alive()

    lax.while_loop(cond, body, (first_block, alive()))


def _sb_prompt_kernel(q_ref, k_ref, v_ref, gn_ref, o_ref, carry_s, acc_s):
    tq = q_ref.shape[0]
    i = pl.program_id(1)
    head0 = lax.broadcasted_iota(jnp.int32, (tq, LANES), 1) < B_HEAD_DIM
    q = q_ref[...]
    zero = jnp.zeros_like(q)
    qh = [jnp.where(head0, q, zero), jnp.where(head0, zero, q)]
    tri, ones_m = _tri_ones(tq)
    t = lax.broadcasted_iota(jnp.int32, (tq, tq), 0)
    s = lax.broadcasted_iota(jnp.int32, (tq, tq), 1)
    causal = s < t

    def load_kv(j):
        rows = pl.ds(pl.multiple_of(j * tq, tq), tq)
        return k_ref[rows, :], v_ref[rows, :]

    kd, vd = load_kv(i)
    zeros = jnp.zeros((tq, LANES), F32)
    for h in range(2):
        carry, acc = _sb_step(qh[h], kd, vd, zeros, zeros, tri, ones_m, causal)
        carry_s[h] = carry
        acc_s[h] = acc
    _sb_walk(qh, load_kv, i - 1, carry_s, acc_s, tri, ones_m)
    o_ref[...] = _sb_finish(acc_s, gn_ref, o_ref, head0)


def _sb_prompt(q, k, v, gn):
    n = q.shape[0]
    tq = SB_BLOCK
    blk = pl.BlockSpec((tq, LANES), lambda p, i: (i, p))
    seq = pl.BlockSpec((n, LANES), lambda p, i: (0, p))
    return pl.pallas_call(
        _sb_prompt_kernel,
        out_shape=jax.ShapeDtypeStruct((n, B_WIDTH), BF16),
        grid=(B_PAIRS, n // tq),
        in_specs=[blk, seq, seq, pl.BlockSpec((1, LANES), lambda p, i: (0, p))],
        out_specs=blk,
        scratch_shapes=[pltpu.VMEM((2, tq, LANES), F32), pltpu.VMEM((2, tq, LANES), F32)],
        compiler_params=_cparams(("parallel", "parallel")),
        name="sb_prompt",
    )(q, k, v, gn)


def _sb_sample_kernel(q_ref, kn_ref, vn_ref, kc_ref, vc_ref, gn_ref, o_ref, carry_s, acc_s):
    tq = q_ref.shape[1]
    tk = SB_BLOCK
    head0 = lax.broadcasted_iota(jnp.int32, (tq, LANES), 1) < B_HEAD_DIM
    q = q_ref[0]
    zero = jnp.zeros_like(q)
    qh = [jnp.where(head0, q, zero), jnp.where(head0, zero, q)]
    tri_n, ones_n = _tri_ones(tq)
    t = lax.broadcasted_iota(jnp.int32, (tq, tq), 0)
    s = lax.broadcasted_iota(jnp.int32, (tq, tq), 1)
    causal = s < t
    zeros = jnp.zeros((tq, LANES), F32)
    for h in range(2):
        carry, acc = _sb_step(qh[h], kn_ref[0], vn_ref[0], zeros, zeros, tri_n, ones_n, causal)
        carry_s[h] = carry
        acc_s[h] = acc
    tri, ones_m = _tri_ones(tk)

    def load_kv(j):
        rows = pl.ds(pl.multiple_of(j * tk, tk), tk)
        return kc_ref[0, rows, :].astype(BF16), vc_ref[0, rows, :].astype(BF16)

    _sb_walk(qh, load_kv, kc_ref.shape[1] // tk - 1, carry_s, acc_s, tri, ones_m)
    o_ref[0] = _sb_finish(acc_s, gn_ref, o_ref, head0)


def _sb_sample(q, kn, vn, kc, vc, gn):
    nb, tq, _ = q.shape
    past = kc.shape[1]
    new = pl.BlockSpec((1, tq, LANES), lambda b, p: (b, 0, p))
    old = pl.BlockSpec((1, past, LANES), lambda b, p: (b, 0, p))
    return pl.pallas_call(
        _sb_sample_kernel,
        out_shape=jax.ShapeDtypeStruct((nb, tq, B_WIDTH), BF16),
        grid=(nb, B_PAIRS),
        in_specs=[new, new, new, old, old, pl.BlockSpec((1, LANES), lambda b, p: (0, p))],
        out_specs=new,
        scratch_shapes=[pltpu.VMEM((2, tq, LANES), F32), pltpu.VMEM((2, tq, LANES), F32)],
        compiler_params=_cparams(("parallel", "parallel")),
        name="sb_sample",
    )(q, kn, vn, kc, vc, gn)


def _mlstm_kernel(n_sub, q_ref, k_ref, v_ref, o_ref, gt_ref, gn_ref, c0_ref, n0_ref, m0_ref,
                  yc_ref, c_out, n_out, m_out, c_s, n_s, m_s):
    L = MLSTM_BLOCK
    t_blk = pl.program_id(1)

    @pl.when(t_blk == 0)
    def _():
        c_s[...] = c0_ref[0]
        n_s[...] = n0_ref[0]
        m_s[...] = m0_ref[0]

    r = lax.broadcasted_iota(jnp.int32, (L, L), 0)
    c = lax.broadcasted_iota(jnp.int32, (L, L), 1)
    causal = c <= r
    tril = causal.astype(BF16)

    for sub in range(n_sub):
        rows = slice(sub * L, (sub + 1) * L)
        gt = gt_ref[0, rows, :]
        hi, lo = _split_bf16(gt)
        bc = _dot(tril, hi) + _dot(tril, lo)
        gt_t = gt.T
        bc_t = bc.T
        for h in range(C_HEADS):
            lanes = slice(h * C_HEAD_PAD, (h + 1) * C_HEAD_PAD)
            q = q_ref[0, rows, lanes]
            k = k_ref[0, rows, lanes]
            v = v_ref[0, rows, lanes]
            ig_col = gt[:, h:h + 1]
            ig_row = gt_t[h:h + 1, :]
            b_col = bc[:, C_HEADS + h:C_HEADS + h + 1]
            b_row = bc_t[C_HEADS + h:C_HEADS + h + 1, :]
            m_prev = m_s[h][:, :1]
            c_prev = c_s[h]
            n_prev = n_s[h]

            log_d = jnp.where(causal, b_col - b_row + ig_row, -jnp.inf)
            log_past = b_col + m_prev
            m_t = jnp.maximum(log_past, jnp.max(log_d, axis=-1, keepdims=True))
            w = jnp.exp(log_d - m_t) * _dot_nt(q, k)
            past = jnp.exp(log_past - m_t)
            num = past * _dot(q, c_prev.astype(BF16)) + _dot(w.astype(BF16), v)
            den = past * jnp.sum(q.astype(F32) * n_prev, axis=-1, keepdims=True) \
                + jnp.sum(w, axis=-1, keepdims=True)
            hh = num / jnp.maximum(jnp.abs(den), jnp.exp(-m_t))

            m_new = m_t[L - 1:L, :]
            w_end = jnp.exp(b_col[L - 1:L, :] - b_col + ig_col - m_new)
            decay = jnp.exp(log_past[L - 1:L, :] - m_new)
            kw = k.astype(F32) * w_end
            c_s[h] = decay * c_prev + _dot_tn(kw.astype(BF16), v)
            n_s[h] = decay * n_prev + jnp.sum(kw, axis=0, keepdims=True)
            m_s[h] = jnp.broadcast_to(m_new, (1, LANES))

            ms = jnp.sum(hh * hh, axis=-1, keepdims=True) * (1.0 / C_HEAD_DIM)
            hn = hh * lax.rsqrt(ms + EPS) * gn_ref[:, lanes]
            yc_ref[0, rows, lanes] = (hn * _sigmoid(o_ref[0, rows, lanes])).astype(BF16)

    @pl.when(t_blk == pl.num_programs(1) - 1)
    def _():
        c_out[0] = c_s[...]
        n_out[0] = n_s[...]
        m_out[0] = m_s[...]


def _mlstm(q, k, v, o, gt, gn, c0, n0, m0, n_sub):
    nb, n, _ = q.shape
    tb = n_sub * MLSTM_BLOCK
    seq = lambda w: pl.BlockSpec((1, tb, w), lambda b, t: (b, t, 0))
    st = lambda a: pl.BlockSpec((1,) + a.shape[1:], lambda b, t: (b,) + (0,) * (a.ndim - 1))
    return pl.pallas_call(
        functools.partial(_mlstm_kernel, n_sub),
        out_shape=[jax.ShapeDtypeStruct((nb, n, C_WIDTH_PAD), BF16),
                   jax.ShapeDtypeStruct(c0.shape, F32),
                   jax.ShapeDtypeStruct(n0.shape, F32),
                   jax.ShapeDtypeStruct(m0.shape, F32)],
        grid=(nb, n // tb),
        in_specs=[seq(C_WIDTH_PAD), seq(C_WIDTH_PAD), seq(C_WIDTH_PAD), seq(C_WIDTH_PAD), seq(LANES),
                  pl.BlockSpec((1, C_WIDTH_PAD), lambda b, t: (0, 0)), st(c0), st(n0), st(m0)],
        out_specs=[seq(C_WIDTH_PAD), st(c0), st(n0), st(m0)],
        scratch_shapes=[pltpu.VMEM(c0.shape[1:], F32), pltpu.VMEM(n0.shape[1:], F32),
                        pltpu.VMEM(m0.shape[1:], F32)],
        compiler_params=_cparams(("parallel", "arbitrary")),
        name="mlstm",
    )(q, k, v, o, gt, gn, c0, n0, m0)


def _mixer_out(x_ref, ya_ref, yb_ref, yc_ref, wa_ref, wb_ref, wc_ref, g_ref, xmid_ref):
    y = _dot(ya_ref[...], wa_ref[...]) + _dot(yb_ref[...], wb_ref[...]) + _dot(yc_ref[...], wc_ref[...])
    x = x_ref[...] + y
    xmid_ref[...] = x
    return x * lax.rsqrt(jnp.mean(x * x, axis=-1, keepdims=True) + EPS) * g_ref[...]


def _merge_dense_kernel(x_ref, ya_ref, yb_ref, yc_ref, wa_ref, wb_ref, wc_ref, g_ref, xmid_ref, xn_ref):
    xn_ref[...] = _mixer_out(x_ref, ya_ref, yb_ref, yc_ref, wa_ref, wb_ref, wc_ref, g_ref,
                             xmid_ref).astype(BF16)


def _merge_moe_kernel(x_ref, ya_ref, yb_ref, yc_ref, wa_ref, wb_ref, wc_ref, g_ref, wrh_ref, wrl_ref,
                      xmid_ref, xrow_ref, ri_ref, rf_ref, cnt_ref, run_s):
    tm = x_ref.shape[0]

    @pl.when(pl.program_id(0) == 0)
    def _():
        run_s[...] = jnp.zeros_like(run_s)

    xn = _mixer_out(x_ref, ya_ref, yb_ref, yc_ref, wa_ref, wb_ref, wc_ref, g_ref, xmid_ref)
    for s in range(D_MODEL // LANES):
        xrow_ref[pl.ds(s, tm, stride=ROW_TILE), :] = xn[:, s * LANES:(s + 1) * LANES]

    hi, lo = _split_bf16(xn)
    logits = _dot(hi, wrh_ref[...]) + _dot(hi, wrl_ref[...]) + _dot(lo, wrh_ref[...])
    lane = lax.broadcasted_iota(jnp.int32, logits.shape, 1)
    lg = jnp.where(lane < N_EXPERTS, logits, -jnp.inf)
    m1 = jnp.max(lg, axis=-1, keepdims=True)
    i1 = jnp.min(jnp.where(lg == m1, lane, LANES), axis=-1, keepdims=True)
    lg2 = jnp.where(lane == i1, -jnp.inf, lg)
    m2 = jnp.max(lg2, axis=-1, keepdims=True)
    i2 = jnp.min(jnp.where(lg2 == m2, lane, LANES), axis=-1, keepdims=True)
    e2 = jnp.exp(m2 - m1)
    g1 = 1.0 / (1.0 + e2)
    g2 = e2 * g1

    sel1 = lane == i1
    sel2 = lane == i2
    onehot = jnp.logical_or(sel1, sel2)
    r = lax.broadcasted_iota(jnp.int32, (tm, tm), 0)
    c = lax.broadcasted_iota(jnp.int32, (tm, tm), 1)
    before = _dot((c < r).astype(BF16), onehot.astype(BF16)) + run_s[...]
    rank1 = jnp.sum(jnp.where(sel1, before, 0.0), axis=-1, keepdims=True).astype(jnp.int32)
    rank2 = jnp.sum(jnp.where(sel2, before, 0.0), axis=-1, keepdims=True).astype(jnp.int32)
    run_s[...] += jnp.sum(onehot.astype(F32), axis=0, keepdims=True)
    cnt_ref[...] = run_s[...].astype(jnp.int32)
    ri_ref[...] = jnp.where(lane == 0, i1, jnp.where(lane == 1, i2,
                            jnp.where(lane == 2, rank1, jnp.where(lane == 3, rank2, 0))))
    rf_ref[...] = jnp.where(lane == 0, g1, jnp.where(lane == 1, g2, 0.0))


def _merge(x, ya, yb, yc, wa, wb, wc, g, wr, tm):
    n = x.shape[0]
    row = lambda w: pl.BlockSpec((tm, w), lambda i: (i, 0))
    full = lambda a: pl.BlockSpec(a.shape, lambda i: (0,) * a.ndim)
    ins = [x, ya, yb, yc, wa, wb, wc, g]
    in_specs = [row(D_MODEL), row(A_WIDTH), row(B_WIDTH), row(C_WIDTH_PAD), full(wa), full(wb), full(wc), full(g)]
    if wr is None:
        return pl.pallas_call(
            _merge_dense_kernel,
            out_shape=[jax.ShapeDtypeStruct((n, D_MODEL), F32), jax.ShapeDtypeStruct((n, D_MODEL), BF16)],
            grid=(n // tm,), in_specs=in_specs, out_specs=[row(D_MODEL), row(D_MODEL)],
            compiler_params=_cparams(("parallel",)), name="merge_dense",
        )(*ins)
    wrh, wrl = wr
    return pl.pallas_call(
        _merge_moe_kernel,
        out_shape=[jax.ShapeDtypeStruct((n, D_MODEL), F32),
                   jax.ShapeDtypeStruct((n * ROW_TILE, LANES), F32),
                   jax.ShapeDtypeStruct((n, LANES), jnp.int32),
                   jax.ShapeDtypeStruct((n, LANES), F32),
                   jax.ShapeDtypeStruct((1, LANES), jnp.int32)],
        grid=(n // tm,), in_specs=in_specs + [full(wrh), full(wrl)],
        out_specs=[row(D_MODEL), pl.BlockSpec((tm * ROW_TILE, LANES), lambda i: (i, 0)),
                   row(LANES), row(LANES), pl.BlockSpec((1, LANES), lambda i: (0, 0))],
        scratch_shapes=[pltpu.VMEM((1, LANES), F32)],
        compiler_params=_cparams(("arbitrary",)), name="merge_moe",
    )(*ins, wrh, wrl)


def _dispatch_kernel(dest_ref, src_ref, xs_in_ref, xs_ref, sem):
    del xs_in_ref
    tt = dest_ref.shape[2] // 2
    base = pl.program_id(0) * tt

    def issue(t, carry):
        for k in range(2):
            pltpu.make_async_copy(src_ref.at[base + t], xs_ref.at[dest_ref[0, 0, 2 * t + k]], sem).start()
        return carry

    lax.fori_loop(0, tt, issue, 0, unroll=8)

    def drain(t, carry):
        for k in range(2):
            pltpu.make_async_copy(src_ref.at[0], xs_ref.at[0], sem).wait()
        return carry

    lax.fori_loop(0, tt, drain, 0, unroll=8)


def _dispatch(dest, src, xs, tt):
    n = src.shape[0]
    dest3 = dest.reshape(n // tt, 1, 2 * tt)
    return pl.pallas_call(
        _dispatch_kernel,
        out_shape=jax.ShapeDtypeStruct(xs.shape, xs.dtype),
        grid=(n // tt,),
        in_specs=[pl.BlockSpec((1, 1, 2 * tt), lambda i: (i, 0, 0), memory_space=pltpu.SMEM),
                  pl.BlockSpec(memory_space=pl.ANY), pl.BlockSpec(memory_space=pl.ANY)],
        out_specs=pl.BlockSpec(memory_space=pl.ANY),
        scratch_shapes=[pltpu.SemaphoreType.DMA(())],
        input_output_aliases={2: 0},
        compiler_params=_cparams(("arbitrary",)), name="moe_dispatch",
    )(dest3, src, xs)


def _combine_kernel(final, dest_ref, ys_ref, rf_ref, xmid_ref, gf_ref, o_ref, buf_s, sem):
    tt = xmid_ref.shape[0]

    def issue(t, carry):
        for k in range(2):
            rows = pl.ds(pl.multiple_of(t * ROW_TILE, ROW_TILE), ROW_TILE)
            pltpu.make_async_copy(ys_ref.at[dest_ref[0, 0, 2 * t + k]], buf_s.at[k, rows], sem).start()
        return carry

    lax.fori_loop(0, tt, issue, 0, unroll=8)

    def drain(t, carry):
        for k in range(2):
            pltpu.make_async_copy(ys_ref.at[0], buf_s.at[0, pl.ds(0, ROW_TILE)], sem).wait()
        return carry

    lax.fori_loop(0, tt, drain, 0, unroll=8)

    g1 = rf_ref[:, 0:1]
    g2 = rf_ref[:, 1:2]
    parts = []
    for s in range(D_MODEL // LANES):
        lanes = slice(s * LANES, (s + 1) * LANES)
        sub = pl.ds(s, tt, stride=ROW_TILE)
        parts.append(xmid_ref[:, lanes] + g1 * buf_s[0, sub, :] + g2 * buf_s[1, sub, :])
    y = jnp.concatenate(parts, axis=1)
    if final:
        y = y * lax.rsqrt(jnp.mean(y * y, axis=-1, keepdims=True) + EPS) * gf_ref[...]
    o_ref[...] = y


def _combine(dest, ys, rf, xmid, gf, tt, final):
    n = xmid.shape[0]
    dest3 = dest.reshape(n // tt, 1, 2 * tt)
    row = lambda w: pl.BlockSpec((tt, w), lambda i: (i, 0))
    return pl.pallas_call(
        functools.partial(_combine_kernel, final),
        out_shape=jax.ShapeDtypeStruct((n, D_MODEL), F32),
        grid=(n // tt,),
        in_specs=[pl.BlockSpec((1, 1, 2 * tt), lambda i: (i, 0, 0), memory_space=pltpu.SMEM),
                  pl.BlockSpec(memory_space=pl.ANY), row(LANES), row(D_MODEL),
                  pl.BlockSpec((1, D_MODEL), lambda i: (0, 0))],
        out_specs=row(D_MODEL),
        scratch_shapes=[pltpu.VMEM((2, tt * ROW_TILE, LANES), F32), pltpu.SemaphoreType.DMA(())],
        compiler_params=_cparams(("arbitrary",)), name="moe_combine",
    )(dest3, ys, rf, xmid, gf)


def _swiglu_acc(xn, wg_ref, wu_ref, wd_ref, acc_s):
    g = _dot(xn, wg_ref[0])
    u = _dot(xn, wu_ref[0])
    acc_s[...] += _dot((g * _sigmoid(g) * u).astype(BF16), wd_ref[0])


def _ffn_dense_kernel(final, xn_ref, wg_ref, wu_ref, wd_ref, xmid_ref, gf_ref, o_ref, acc_s):
    f = pl.program_id(1)

    @pl.when(f == 0)
    def _():
        acc_s[...] = jnp.zeros_like(acc_s)

    _swiglu_acc(xn_ref[...], wg_ref, wu_ref, wd_ref, acc_s)

    @pl.when(f == pl.num_programs(1) - 1)
    def _():
        y = xmid_ref[...] + acc_s[...]
        if final:
            y = y * lax.rsqrt(jnp.mean(y * y, axis=-1, keepdims=True) + EPS) * gf_ref[...]
        o_ref[...] = y


def _ffn_dense(xn, wg, wu, wd, xmid, gf, tm, final):
    n = xn.shape[0]
    tf = FFN_F_TILE
    row = lambda w: pl.BlockSpec((tm, w), lambda i, f: (i, 0))
    return pl.pallas_call(
        functools.partial(_ffn_dense_kernel, final),
        out_shape=jax.ShapeDtypeStruct((n, D_MODEL), F32),
        grid=(n // tm, D_FF // tf),
        in_specs=[row(D_MODEL),
                  pl.BlockSpec((1, D_MODEL, tf), lambda i, f: (0, 0, f)),
                  pl.BlockSpec((1, D_MODEL, tf), lambda i, f: (0, 0, f)),
                  pl.BlockSpec((1, tf, D_MODEL), lambda i, f: (0, f, 0)),
                  row(D_MODEL), pl.BlockSpec((1, D_MODEL), lambda i, f: (0, 0))],
        out_specs=row(D_MODEL),
        scratch_shapes=[pltpu.VMEM((tm, D_MODEL), F32)],
        compiler_params=_cparams(("parallel", "arbitrary")),
        name="ffn_dense",
    )(xn, wg, wu, wd, xmid, gf)


def _ffn_routed_kernel(te_ref, nu_ref, xs_ref, wg_ref, wu_ref, wd_ref, ys_ref, xb_s, acc_s):
    del te_ref
    tm = xb_s.shape[0]
    i = pl.program_id(0)
    f = pl.program_id(1)

    @pl.when(i < nu_ref[0])
    def _():
        @pl.when(f == 0)
        def _():
            acc_s[...] = jnp.zeros_like(acc_s)
            for s in range(D_MODEL // LANES):
                xb_s[:, s * LANES:(s + 1) * LANES] = xs_ref[pl.ds(s, tm, stride=ROW_TILE), :].astype(BF16)

        _swiglu_acc(xb_s[...], wg_ref, wu_ref, wd_ref, acc_s)

        @pl.when(f == pl.num_programs(1) - 1)
        def _():
            for s in range(D_MODEL // LANES):
                ys_ref[pl.ds(s, tm, stride=ROW_TILE), :] = acc_s[:, s * LANES:(s + 1) * LANES]

    @pl.when(jnp.logical_and(i >= nu_ref[0], f == 0))
    def _():
        ys_ref[...] = jnp.zeros_like(ys_ref)


def _ffn_routed(tile_expert, n_used, xs, wg, wu, wd):
    n_tiles = tile_expert.shape[0]
    tm, tf = MOE_TILE, FFN_F_TILE
    n_f = D_FF // tf
    last = lambda i, nu: jnp.minimum(i, nu[0] - 1)
    fcol = lambda i, f, nu: jnp.where(i < nu[0], f, n_f - 1)
    return pl.pallas_call(
        _ffn_routed_kernel,
        out_shape=jax.ShapeDtypeStruct(xs.shape, F32),
        grid_spec=pltpu.PrefetchScalarGridSpec(
            num_scalar_prefetch=2,
            grid=(n_tiles, n_f),
            in_specs=[pl.BlockSpec((tm * ROW_TILE, LANES), lambda i, f, te, nu: (last(i, nu), 0)),
                      pl.BlockSpec((1, D_MODEL, tf), lambda i, f, te, nu: (te[last(i, nu)], 0, fcol(i, f, nu))),
                      pl.BlockSpec((1, D_MODEL, tf), lambda i, f, te, nu: (te[last(i, nu)], 0, fcol(i, f, nu))),
                      pl.BlockSpec((1, tf, D_MODEL), lambda i, f, te, nu: (te[last(i, nu)], fcol(i, f, nu), 0))],
            out_specs=pl.BlockSpec((tm * ROW_TILE, LANES), lambda i, f, te, nu: (i, 0)),
            scratch_shapes=[pltpu.VMEM((tm, D_MODEL), BF16), pltpu.VMEM((tm, D_MODEL), F32)]),
        compiler_params=_cparams(("arbitrary", "arbitrary")),
        name="ffn_routed",
    )(tile_expert, n_used, xs, wg, wu, wd)


def _route_plan(ri_p, cnt_p, ri_s, cnt_s, n_tiles):
    cnt_p, cnt_s = cnt_p[0, :N_EXPERTS], cnt_s[0, :N_EXPERTS]
    tiles = (cnt_p + cnt_s + MOE_TILE - 1) // MOE_TILE
    ends = jnp.cumsum(tiles)
    start = (ends - tiles) * MOE_TILE
    lookup = lambda table, idx: jnp.sum(
        jnp.where(idx[..., None] == jnp.arange(N_EXPERTS, dtype=jnp.int32), table, 0), axis=-1)
    dest_p = lookup(start, ri_p[:, 0:2]) + ri_p[:, 2:4]
    dest_s = lookup(start + cnt_p, ri_s[:, 0:2]) + ri_s[:, 2:4]
    tile_expert = jnp.minimum(jnp.sum(jnp.arange(n_tiles, dtype=jnp.int32)[:, None] >= ends[None, :], axis=-1),
                              N_EXPERTS - 1).astype(jnp.int32)
    return dest_p, dest_s, tile_expert, ends[-1:].astype(jnp.int32)


def _pad_heads_cols(w):
    w = w.reshape(w.shape[0], C_HEADS, C_HEAD_DIM)
    return jnp.pad(w, ((0, 0), (0, 0), (0, C_HEAD_PAD - C_HEAD_DIM))).reshape(w.shape[0], C_WIDTH_PAD)


def _layer_params(l, w_in, b_gate, w_s, b_s, gn_c, w_out):
    w = w_in[l]
    gates = jnp.pad(w[:, OFF_CG:OFF_CG + 2 * C_HEADS], ((0, 0), (0, LANES - 2 * C_HEADS)))
    wp = jnp.concatenate(
        [w[:, :OFF_CQ]] + [_pad_heads_cols(w[:, o:o + C_WIDTH]) for o in (OFF_CQ, OFF_CK, OFF_CV, OFF_CO)]
        + [gates], axis=1).astype(BF16)
    bg = jnp.pad(b_gate[l], (0, LANES - 2 * C_HEADS))[None, :]
    pos = jnp.arange(GMLP_CHUNK)
    mask = (pos[None, :] // CHUNK) <= (pos[:, None] // CHUNK)
    wm = jnp.where(mask[None], w_s[l], 0.0)
    bs = jnp.repeat(b_s[l].T, A_GROUP_DIM, axis=1)
    wo = w_out[l]
    wc = wo[A_WIDTH + B_WIDTH:].reshape(C_HEADS, C_HEAD_DIM, D_MODEL)
    wc = jnp.pad(wc, ((0, 0), (0, C_HEAD_PAD - C_HEAD_DIM), (0, 0))).reshape(C_WIDTH_PAD, D_MODEL)
    gnc = _pad_heads_cols(gn_c[l][None, :])
    return dict(wp=wp, bg=bg, wm=wm, bs=bs, wa=wo[:A_WIDTH].astype(BF16),
                wb=wo[A_WIDTH:A_WIDTH + B_WIDTH].astype(BF16), wc=wc.astype(BF16), gnc=gnc)


def _pad_state(c, n, m):
    p = C_HEAD_PAD - C_HEAD_DIM
    c = jnp.pad(c, ((0, 0), (0, 0), (0, p), (0, p)))
    n = jnp.pad(n, ((0, 0), (0, 0), (0, p)))[:, :, None, :]
    m = jnp.broadcast_to(m[:, :, None, None], m.shape + (1, LANES))
    return c, n, m


def _unpad_state(c, n, m):
    return c[:, :, :C_HEAD_DIM, :C_HEAD_DIM], n[:, :, 0, :C_HEAD_DIM], m[:, :, 0, 0]


def kernel(x_prompt, x_sample, cache_k_b, cache_v_b, state_c_mlstm, state_n_mlstm, state_m_mlstm,
           g_mix, w_in, b_gate, ln_a_g, ln_a_b, w_s, b_s, gn_b, gn_c, w_out,
           g_ffn, w_gate_d, w_up_d, w_down_d, w_router, w_gate_e, w_up_e, w_down_e, g_final):
    n_seq = x_prompt.shape[1]
    n_dec, n_new = x_sample.shape[0], x_sample.shape[1]
    past = cache_k_b.shape[2]
    n_samp = n_dec * n_new

    xp = x_prompt.reshape(n_seq, D_MODEL)
    xs = x_sample.reshape(n_samp, D_MODEL)
    gfin = g_final[None, :]

    outs = {k: [] for k in ("kbp", "vbp", "cp", "np", "mp", "kbs", "vbs", "cs", "ns", "ms", "vas")}
    for l in range(DEPTH):
        p = _layer_params(l, w_in, b_gate, w_s, b_s, gn_c, w_out)
        gmix = g_mix[l][None, :]
        lng, lnb = ln_a_g[l][None, :], ln_a_b[l][None, :]
        gnb = gn_b[l][None, :]
        gffn = g_ffn[l][None, :]
        moe = l % 2 == 1
        j = l // 2
        if moe:
            wg, wu, wd = w_gate_e[j].astype(BF16), w_up_e[j].astype(BF16), w_down_e[j].astype(BF16)
            wr = _split_bf16(jnp.pad(w_router[j], ((0, 0), (0, LANES - N_EXPERTS))))
        else:
            wg, wu, wd = (w_gate_d[j][None].astype(BF16), w_up_d[j][None].astype(BF16),
                          w_down_d[j][None].astype(BF16))
            wr = None
        final = l == DEPTH - 1

        wm_s = jnp.kron(jnp.eye(n_dec, dtype=F32), p["wm"][:, :n_new, :n_new])
        bs_s = jnp.tile(p["bs"][:n_new], (n_dec, 1))

        (ya, qb, kbf, vbf, kbh, vbh, qc, kc, vc, oc, gt) = _proj(
            xp, gmix, p["wp"], p["bg"], lng, lnb, p["wm"].astype(BF16), p["bs"], ROW_BLOCK, GMLP_CHUNK, False)
        yb = _sb_prompt(qb, kbh, vbh, gnb)
        c0, n0, m0 = _pad_state(jnp.zeros((1, C_HEADS, C_HEAD_DIM, C_HEAD_DIM), F32),
                                jnp.zeros((1, C_HEADS, C_HEAD_DIM), F32), jnp.zeros((1, C_HEADS), F32))
        yc, c_f, n_f, m_f = _mlstm(qc[None], kc[None], vc[None], oc[None], gt[None], p["gnc"],
                                   c0, n0, m0, 2)
        merged_p = _merge(xp, ya, yb, yc[0], p["wa"], p["wb"], p["wc"], gffn, wr, ROW_BLOCK)
        c_f, n_f, m_f = _unpad_state(c_f, n_f, m_f)
        outs["kbp"].append(kbf.reshape(1, n_seq, B_HEADS, B_HEAD_DIM))
        outs["vbp"].append(vbf.reshape(1, n_seq, B_HEADS, B_HEAD_DIM))
        outs["cp"].append(c_f)
        outs["np"].append(n_f)
        outs["mp"].append(m_f)

        (ya, qb, kbf, vbf, kbh, vbh, qc, kc, vc, oc, gt, va) = _proj(
            xs, gmix, p["wp"], p["bg"], lng, lnb, wm_s.astype(BF16), bs_s, n_samp, n_samp, True)
        r3 = lambda a: a.reshape(n_dec, n_new, a.shape[-1])
        yb = _sb_sample(r3(qb), r3(kbh), r3(vbh), cache_k_b[l].reshape(n_dec, past, B_WIDTH),
                        cache_v_b[l].reshape(n_dec, past, B_WIDTH), gnb)
        padr = lambda a: jnp.pad(r3(a), ((0, 0), (0, MLSTM_BLOCK - n_new), (0, 0)))
        lane = jnp.arange(LANES)
        gt_pad = jnp.broadcast_to(jnp.where(lane < C_HEADS, NEG_BIG, 0.0).astype(F32),
                                  (n_dec, MLSTM_BLOCK - n_new, LANES))
        gt_s = jnp.concatenate([r3(gt), gt_pad], axis=1)
        c0, n0, m0 = _pad_state(state_c_mlstm[l], state_n_mlstm[l], state_m_mlstm[l])
        yc, c_u, n_u, m_u = _mlstm(padr(qc), padr(kc), padr(vc), padr(oc), gt_s, p["gnc"], c0, n0, m0, 1)
        yc = yc[:, :n_new].reshape(n_samp, C_WIDTH_PAD)
        merged_s = _merge(xs, ya, yb.reshape(n_samp, B_WIDTH), yc, p["wa"], p["wb"], p["wc"],
                          gffn, wr, n_samp)
        c_u, n_u, m_u = _unpad_state(c_u, n_u, m_u)

        if moe:
            xmid_p, xrow_p, ri_p, rf_p, cnt_p = merged_p
            xmid_s, xrow_s, ri_s, rf_s, cnt_s = merged_s
            n_tiles = 2 * (n_seq + n_samp) // MOE_TILE + N_EXPERTS
            dest_p, dest_s, tile_expert, n_used = _route_plan(ri_p, cnt_p, ri_s, cnt_s, n_tiles)
            as_tiles = lambda a: a.reshape(a.shape[0] // ROW_TILE, ROW_TILE, LANES)
            xsort = jnp.zeros((n_tiles * MOE_TILE, ROW_TILE, LANES), F32)
            xsort = _dispatch(dest_p, as_tiles(xrow_p), xsort, ROW_BLOCK)
            xsort = _dispatch(dest_s, as_tiles(xrow_s), xsort, n_samp)
            ysort = _ffn_routed(tile_expert, n_used, xsort.reshape(-1, LANES), wg, wu, wd)
            xp = _combine(dest_p, as_tiles(ysort), rf_p, xmid_p, gfin, MOE_COMBINE_BLOCK, final)
            xs = _combine(dest_s, as_tiles(ysort), rf_s, xmid_s, gfin, n_samp, final)
        else:
            xp = _ffn_dense(merged_p[1], wg, wu, wd, merged_p[0], gfin, ROW_BLOCK, final)
            xs = _ffn_dense(merged_s[1], wg, wu, wd, merged_s[0], gfin, n_samp, final)

        outs["kbs"].append(kbf.reshape(n_dec, n_new, B_HEADS, B_HEAD_DIM))
        outs["vbs"].append(vbf.reshape(n_dec, n_new, B_HEADS, B_HEAD_DIM))
        outs["cs"].append(c_u)
        outs["ns"].append(n_u)
        outs["ms"].append(m_u)
        outs["vas"].append(va.reshape(n_dec, n_new, A_WIDTH))

    st = lambda k: jnp.stack(outs[k])
    return (xp.reshape(1, n_seq, D_MODEL), xs.reshape(n_dec, n_new, D_MODEL),
            st("kbp"), st("vbp"), st("cp"), st("np"), st("mp"),
            st("kbs"), st("vbs"), st("cs"), st("ns"), st("ms"), st("vas"))
```

```python
import functools
import math

import jax
import jax.numpy as jnp
from jax import lax
from jax.experimental import pallas as pl
from jax.experimental.pallas import tpu as pltpu

F32 = jnp.float32
BF16 = jnp.bfloat16

D_MODEL = 1024
DEPTH = 2
EPS = 1e-6
CHUNK = 64
A_WIDTH = 256
A_GROUPS = 4
A_GROUP_DIM = 64
GMLP_CHUNK = 128
B_HEAD_DIM = 64
B_WIDTH = 384
B_HEADS = 6
B_PAIRS = 3
C_HEADS = 4
C_HEAD_DIM = 96
C_WIDTH = 384
D_FF = 2816
N_EXPERTS = 8

LANES = 128
C_HEAD_PAD = LANES
C_WIDTH_PAD = C_HEADS * C_HEAD_PAD

OFF_AU, OFF_AV, OFF_BQ, OFF_BK, OFF_BV = 0, 256, 512, 896, 1280
OFF_CQ, OFF_CK, OFF_CV, OFF_CO, OFF_CG = 1664, 2048, 2432, 2816, 3200
P_A = 0
P_BQ = 512
P_BK = P_BQ + B_WIDTH
P_BV = P_BK + B_WIDTH
P_CQ = P_BV + B_WIDTH
P_CK = P_CQ + C_WIDTH_PAD
P_CV = P_CK + C_WIDTH_PAD
P_CO = P_CV + C_WIDTH_PAD
P_CG = P_CO + C_WIDTH_PAD
P_DIM = P_CG + LANES

SB_DEAD_LOG_WEIGHT = -110.0
SB_BLOCK = 256
MLSTM_BLOCK = 128
NEG_BIG = -1e30
ROW_TILE = 8
MOE_TILE = 512
FFN_F_TILE = 1408
ROW_BLOCK = 512
MOE_COMBINE_BLOCK = 256

VMEM_LIMIT = 56 * 1024 * 1024


def _cparams(sem):
    return pltpu.CompilerParams(dimension_semantics=sem, vmem_limit_bytes=VMEM_LIMIT)


def _gelu(x):
    return 0.5 * x * (1.0 + lax.erf(x * (1.0 / math.sqrt(2.0))))


def _log_sigmoid(x):
    return jnp.minimum(x, 0.0) - jnp.log(1.0 + jnp.exp(-jnp.abs(x)))


def _sigmoid(x):
    return 1.0 / (1.0 + jnp.exp(-x))


def _split_bf16(x):
    hi = x.astype(BF16)
    lo = (x - hi.astype(F32)).astype(BF16)
    return hi, lo


def _dot(a, b):
    return jnp.dot(a, b, preferred_element_type=F32)


def _dot_nt(a, b):
    return lax.dot_general(a, b, (((1,), (1,)), ((), ())), preferred_element_type=F32)


def _dot_tn(a, b):
    return lax.dot_general(a, b, (((0,), (0,)), ((), ())), preferred_element_type=F32)


def _proj_kernel(n_chunks, chunk, sample, x_ref, gmix_ref, w_ref, bg_ref, lng_ref, lnb_ref, ws_ref, bs_ref,
                 ya_ref, qb_ref, qc_ref, kc_ref, vc_ref, oc_ref, gt_ref, *kv_refs):
    x = x_ref[...]
    xn = (x * lax.rsqrt(jnp.mean(x * x, axis=-1, keepdims=True) + EPS) * gmix_ref[...]).astype(BF16)

    def proj(off, width):
        return _dot(xn, w_ref[:, off:off + width])

    za = proj(P_A, 2 * A_WIDTH)
    u = _gelu(za[:, :A_WIDTH])
    gv = _gelu(za[:, A_WIDTH:])
    xc = gv - jnp.mean(gv, axis=-1, keepdims=True)
    va = xc * lax.rsqrt(jnp.mean(xc * xc, axis=-1, keepdims=True) + EPS) * lng_ref[...] + lnb_ref[...]
    if sample:
        kv_refs[4][...] = va
    vab = va.astype(BF16)
    lane_group = lax.broadcasted_iota(jnp.int32, (chunk, A_WIDTH), 1) // A_GROUP_DIM
    for c in range(n_chunks):
        rows = slice(c * chunk, (c + 1) * chunk)
        vch = vab[rows]
        s = jnp.zeros((chunk, A_WIDTH), F32)
        for g in range(A_GROUPS):
            s = jnp.where(lane_group == g, _dot(ws_ref[g], vch), s)
        ya_ref[rows, :] = (u[rows] * (s + bs_ref[...])).astype(BF16)

    qb_ref[...] = (proj(P_BQ, B_WIDTH) * (1.0 / math.sqrt(B_HEAD_DIM))).astype(BF16)
    zk = proj(P_BK, B_WIDTH)
    zv = proj(P_BV, B_WIDTH)
    if sample:
        kv_refs[0][...] = zk
        kv_refs[1][...] = zv
        kv_refs[2][...] = zk.astype(BF16)
        kv_refs[3][...] = zv.astype(BF16)
    else:
        kv_refs[0][...] = zk.T
        kv_refs[1][...] = zv.T

    qc_ref[...] = proj(P_CQ, C_WIDTH_PAD).astype(BF16)
    kc_ref[...] = (proj(P_CK, C_WIDTH_PAD) * (C_HEAD_DIM ** -0.5)).astype(BF16)
    vc_ref[...] = proj(P_CV, C_WIDTH_PAD).astype(BF16)
    oc_ref[...] = proj(P_CO, C_WIDTH_PAD)
    g = proj(P_CG, LANES) + bg_ref[...]
    lane = lax.broadcasted_iota(jnp.int32, g.shape, 1)
    gt_ref[...] = jnp.where(lane < C_HEADS, g, _log_sigmoid(g))


def _proj(x, gmix, wp, bg, lng, lnb, ws, bs, tm, chunk, sample):
    n = x.shape[0]
    row = lambda w: pl.BlockSpec((tm, w), lambda i: (i, 0))
    full = lambda a: pl.BlockSpec(a.shape, lambda i: (0,) * a.ndim)
    widths = [(A_WIDTH, BF16), (B_WIDTH, BF16), (C_WIDTH_PAD, BF16), (C_WIDTH_PAD, BF16),
              (C_WIDTH_PAD, BF16), (C_WIDTH_PAD, F32), (LANES, F32)]
    out_shape = [jax.ShapeDtypeStruct((n, w), dt) for w, dt in widths]
    out_specs = [row(w) for w, _ in widths]
    if sample:
        extra = [(B_WIDTH, F32), (B_WIDTH, F32), (B_WIDTH, BF16), (B_WIDTH, BF16), (A_WIDTH, F32)]
        out_shape += [jax.ShapeDtypeStruct((n, w), dt) for w, dt in extra]
        out_specs += [row(w) for w, _ in extra]
    else:
        out_shape += [jax.ShapeDtypeStruct((B_WIDTH, n), F32)] * 2
        out_specs += [pl.BlockSpec((B_WIDTH, tm), lambda i: (0, i))] * 2
    return pl.pallas_call(
        functools.partial(_proj_kernel, tm // chunk, chunk, sample),
        out_shape=out_shape,
        grid=(n // tm,),
        in_specs=[row(D_MODEL), full(gmix), full(wp), full(bg), full(lng), full(lnb), full(ws), full(bs)],
        out_specs=out_specs,
        compiler_params=_cparams(("parallel",)),
        name="proj",
    )(x, gmix, wp, bg, lng, lnb, ws, bs)


def _sb_step(qh, kblk, vblk, carry, acc, tri, mask, transposed):
    z = _dot(qh, kblk) if transposed else _dot_nt(qh, kblk)
    tk = z.shape[1]
    drop = jnp.maximum(z, 0.0) + jnp.log(1.0 + jnp.exp(-jnp.abs(z)))
    if mask is not None:
        drop = jnp.where(mask, drop, 0.0)
    hi, lo = _split_bf16(drop)
    cs = _dot(hi, tri) + _dot(lo, tri)
    if tk >= LANES:
        carry_b = jnp.concatenate([carry] * (tk // LANES), axis=1)
    else:
        carry_b = carry[:, :tk]
    a = jnp.exp(z - cs - carry_b)
    if mask is not None:
        a = jnp.where(mask, a, 0.0)
    a = a.astype(BF16)
    acc = acc + (_dot_nt(a, vblk) if transposed else _dot(a, vblk))
    return carry + jnp.broadcast_to(cs[:, :1], carry.shape), acc


def _tri(tk):
    j = lax.broadcasted_iota(jnp.int32, (tk, tk), 0)
    s = lax.broadcasted_iota(jnp.int32, (tk, tk), 1)
    return (j >= s).astype(BF16)


def _sb_finish(acc_s, gn_ref, o_ref, head0):
    out = jnp.where(head0, acc_s[0], acc_s[1])
    r = lax.broadcasted_iota(jnp.int32, (LANES, LANES), 0) // B_HEAD_DIM
    c = lax.broadcasted_iota(jnp.int32, (LANES, LANES), 1) // B_HEAD_DIM
    same_head = (r == c).astype(BF16)
    hi, lo = _split_bf16(out * out)
    ms = (_dot(hi, same_head) + _dot(lo, same_head)) * (1.0 / B_HEAD_DIM)
    return (out * lax.rsqrt(ms + EPS) * gn_ref[...]).astype(o_ref.dtype)


def _sb_walk(qh, load_kv, first_block, carry_s, acc_s, tri):
    def alive():
        return jnp.minimum(jnp.min(carry_s[0]), jnp.min(carry_s[1])) <= -SB_DEAD_LOG_WEIGHT

    def cond(st):
        j, live = st
        return jnp.logical_and(j >= 0, live)

    def body(st):
        j, _ = st
        kblk, vblk = load_kv(j)
        for h in range(2):
            carry, acc = _sb_step(qh[h], kblk, vblk, carry_s[h], acc_s[h], tri, None, True)
            carry_s[h] = carry
            acc_s[h] = acc
        return j - 1, alive()

    lax.while_loop(cond, body, (first_block, alive()))


def _sb_prompt_kernel(q_ref, k_ref, v_ref, gn_ref, o_ref, carry_s, acc_s):
    tq = q_ref.shape[0]
    i = pl.program_id(1)
    head0 = lax.broadcasted_iota(jnp.int32, (tq, LANES), 1) < B_HEAD_DIM
    q = q_ref[...]
    zero = jnp.zeros_like(q)
    qh = [jnp.where(head0, q, zero), jnp.where(head0, zero, q)]
    tri = _tri(tq)
    t = lax.broadcasted_iota(jnp.int32, (tq, tq), 0)
    s = lax.broadcasted_iota(jnp.int32, (tq, tq), 1)
    causal = s < t

    def load_kv(j):
        cols = pl.ds(pl.multiple_of(j * tq, tq), tq)
        return k_ref[:, cols].astype(BF16), v_ref[:, cols].astype(BF16)

    kd, vd = load_kv(i)
    kp, vp = load_kv(jnp.maximum(i - 1, 0))
    has_prev = t >= jnp.where(i >= 1, 0, tq)
    zeros = jnp.zeros((tq, LANES), F32)
    for h in range(2):
        carry, acc = _sb_step(qh[h], kd, vd, zeros, zeros, tri, causal, True)
        carry, acc = _sb_step(qh[h], kp, vp, carry, acc, tri, has_prev, True)
        carry_s[h] = carry
        acc_s[h] = acc
    _sb_walk(qh, load_kv, i - 2, carry_s, acc_s, tri)
    o_ref[...] = _sb_finish(acc_s, gn_ref, o_ref, head0)


def _sb_prompt(q, k, v, gn):
    n = q.shape[0]
    tq = SB_BLOCK
    blk = pl.BlockSpec((tq, LANES), lambda p, i: (i, p))
    seq = pl.BlockSpec((LANES, n), lambda p, i: (p, 0))
    return pl.pallas_call(
        _sb_prompt_kernel,
        out_shape=jax.ShapeDtypeStruct((n, B_WIDTH), BF16),
        grid=(B_PAIRS, n // tq),
        in_specs=[blk, seq, seq, pl.BlockSpec((1, LANES), lambda p, i: (0, p))],
        out_specs=blk,
        scratch_shapes=[pltpu.VMEM((2, tq, LANES), F32), pltpu.VMEM((2, tq, LANES), F32)],
        compiler_params=_cparams(("parallel", "parallel")),
        name="sb_prompt",
    )(q, k, v, gn)


def _sb_sample_kernel(q_ref, kn_ref, vn_ref, kc_ref, vc_ref, gn_ref, o_ref, carry_s, acc_s):
    tq = q_ref.shape[1]
    tk = SB_BLOCK
    head0 = lax.broadcasted_iota(jnp.int32, (tq, LANES), 1) < B_HEAD_DIM
    q = q_ref[0]
    zero = jnp.zeros_like(q)
    qh = [jnp.where(head0, q, zero), jnp.where(head0, zero, q)]
    t = lax.broadcasted_iota(jnp.int32, (tq, tq), 0)
    s = lax.broadcasted_iota(jnp.int32, (tq, tq), 1)
    causal = s < t
    zeros = jnp.zeros((tq, LANES), F32)
    def load_kv(j):
        cols = pl.ds(pl.multiple_of(j * tk, tk), tk)
        return kc_ref[0, :, cols].astype(BF16), vc_ref[0, :, cols].astype(BF16)

    last = kc_ref.shape[2] // tk - 1
    kp, vp = load_kv(last)
    tri = _tri(tk)
    for h in range(2):
        carry, acc = _sb_step(qh[h], kn_ref[0], vn_ref[0], zeros, zeros, _tri(tq), causal, False)
        carry, acc = _sb_step(qh[h], kp, vp, carry, acc, tri, None, True)
        carry_s[h] = carry
        acc_s[h] = acc
    _sb_walk(qh, load_kv, last - 1, carry_s, acc_s, tri)
    o_ref[0] = _sb_finish(acc_s, gn_ref, o_ref, head0)


def _sb_sample(q, kn, vn, kc, vc, gn):
    nb, tq, _ = q.shape
    past = kc.shape[2]
    new = pl.BlockSpec((1, tq, LANES), lambda b, p: (b, 0, p))
    old = pl.BlockSpec((1, LANES, past), lambda b, p: (b, p, 0))
    return pl.pallas_call(
        _sb_sample_kernel,
        out_shape=jax.ShapeDtypeStruct((nb, tq, B_WIDTH), BF16),
        grid=(nb, B_PAIRS),
        in_specs=[new, new, new, old, old, pl.BlockSpec((1, LANES), lambda b, p: (0, p))],
        out_specs=new,
        scratch_shapes=[pltpu.VMEM((2, tq, LANES), F32), pltpu.VMEM((2, tq, LANES), F32)],
        compiler_params=_cparams(("parallel", "parallel")),
        name="sb_sample",
    )(q, kn, vn, kc, vc, gn)


def _mlstm_kernel(n_sub, q_ref, k_ref, v_ref, o_ref, gt_ref, gn_ref, c0_ref, n0_ref, m0_ref,
                  yc_ref, c_out, n_out, m_out, c_s, n_s, m_s):
    L = MLSTM_BLOCK
    t_blk = pl.program_id(1)

    @pl.when(t_blk == 0)
    def _():
        c_s[...] = c0_ref[0]
        n_s[...] = n0_ref[0]
        m_s[...] = m0_ref[0]

    r = lax.broadcasted_iota(jnp.int32, (L, L), 0)
    c = lax.broadcasted_iota(jnp.int32, (L, L), 1)
    causal = c <= r
    tril = causal.astype(BF16)

    for sub in range(n_sub):
        rows = slice(sub * L, (sub + 1) * L)
        gt = gt_ref[0, rows, :]
        hi, lo = _split_bf16(gt)
        bc = _dot(tril, hi) + _dot(tril, lo)
        gt_t = gt.T
        bc_t = bc.T
        for h in range(C_HEADS):
            lanes = slice(h * C_HEAD_PAD, (h + 1) * C_HEAD_PAD)
            q = q_ref[0, rows, lanes]
            k = k_ref[0, rows, lanes]
            v = v_ref[0, rows, lanes]
            ig_col = gt[:, h:h + 1]
            ig_row = gt_t[h:h + 1, :]
            b_col = bc[:, C_HEADS + h:C_HEADS + h + 1]
            b_row = bc_t[C_HEADS + h:C_HEADS + h + 1, :]
            m_prev = m_s[h][:, :1]
            c_prev = c_s[h]
            n_prev = n_s[h]

            log_d = jnp.where(causal, b_col - b_row + ig_row, -jnp.inf)
            log_past = b_col + m_prev
            m_t = jnp.maximum(log_past, jnp.max(log_d, axis=-1, keepdims=True))
            w = jnp.exp(log_d - m_t) * _dot_nt(q, k)
            past = jnp.exp(log_past - m_t)
            num = past * _dot(q, c_prev.astype(BF16)) + _dot(w.astype(BF16), v)
            den = past * jnp.sum(q.astype(F32) * n_prev, axis=-1, keepdims=True) \
                + jnp.sum(w, axis=-1, keepdims=True)
            hh = num / jnp.maximum(jnp.abs(den), jnp.exp(-m_t))

            m_new = m_t[L - 1:L, :]
            w_end = jnp.exp(b_col[L - 1:L, :] - b_col + ig_col - m_new)
            decay = jnp.exp(log_past[L - 1:L, :] - m_new)
            kw = k.astype(F32) * w_end
            c_s[h] = decay * c_prev + _dot_tn(kw.astype(BF16), v)
            n_s[h] = decay * n_prev + jnp.sum(kw, axis=0, keepdims=True)
            m_s[h] = jnp.broadcast_to(m_new, (1, LANES))

            ms = jnp.sum(hh * hh, axis=-1, keepdims=True) * (1.0 / C_HEAD_DIM)
            hn = hh * lax.rsqrt(ms + EPS) * gn_ref[:, lanes]
            yc_ref[0, rows, lanes] = (hn * _sigmoid(o_ref[0, rows, lanes])).astype(BF16)

    @pl.when(t_blk == pl.num_programs(1) - 1)
    def _():
        c_out[0] = c_s[...]
        n_out[0] = n_s[...]
        m_out[0] = m_s[...]


def _mlstm(q, k, v, o, gt, gn, c0, n0, m0, n_sub):
    nb, n, _ = q.shape
    tb = n_sub * MLSTM_BLOCK
    seq = lambda w: pl.BlockSpec((1, tb, w), lambda b, t: (b, t, 0))
    st = lambda a: pl.BlockSpec((1,) + a.shape[1:], lambda b, t: (b,) + (0,) * (a.ndim - 1))
    return pl.pallas_call(
        functools.partial(_mlstm_kernel, n_sub),
        out_shape=[jax.ShapeDtypeStruct((nb, n, C_WIDTH_PAD), BF16),
                   jax.ShapeDtypeStruct(c0.shape, F32),
                   jax.ShapeDtypeStruct(n0.shape, F32),
                   jax.ShapeDtypeStruct(m0.shape, F32)],
        grid=(nb, n // tb),
        in_specs=[seq(C_WIDTH_PAD), seq(C_WIDTH_PAD), seq(C_WIDTH_PAD), seq(C_WIDTH_PAD), seq(LANES),
                  pl.BlockSpec((1, C_WIDTH_PAD), lambda b, t: (0, 0)), st(c0), st(n0), st(m0)],
        out_specs=[seq(C_WIDTH_PAD), st(c0), st(n0), st(m0)],
        scratch_shapes=[pltpu.VMEM(c0.shape[1:], F32), pltpu.VMEM(n0.shape[1:], F32),
                        pltpu.VMEM(m0.shape[1:], F32)],
        compiler_params=_cparams(("parallel", "arbitrary")),
        name="mlstm",
    )(q, k, v, o, gt, gn, c0, n0, m0)


def _mixer_out(x_ref, ya_ref, yb_ref, yc_ref, wa_ref, wb_ref, wc_ref, g_ref, xmid_ref):
    y = _dot(ya_ref[...], wa_ref[...]) + _dot(yb_ref[...], wb_ref[...]) + _dot(yc_ref[...], wc_ref[...])
    x = x_ref[...] + y
    xmid_ref[...] = x
    return x * lax.rsqrt(jnp.mean(x * x, axis=-1, keepdims=True) + EPS) * g_ref[...]


def _merge_dense_kernel(x_ref, ya_ref, yb_ref, yc_ref, wa_ref, wb_ref, wc_ref, g_ref, xmid_ref, xn_ref):
    xn_ref[...] = _mixer_out(x_ref, ya_ref, yb_ref, yc_ref, wa_ref, wb_ref, wc_ref, g_ref,
                             xmid_ref).astype(BF16)


def _merge_moe_kernel(x_ref, ya_ref, yb_ref, yc_ref, wa_ref, wb_ref, wc_ref, g_ref, wrh_ref, wrl_ref,
                      xmid_ref, xrow_ref, ri_ref, rf_ref, cnt_ref, run_s):
    tm = x_ref.shape[0]

    @pl.when(pl.program_id(0) == 0)
    def _():
        run_s[...] = jnp.zeros_like(run_s)

    xn = _mixer_out(x_ref, ya_ref, yb_ref, yc_ref, wa_ref, wb_ref, wc_ref, g_ref, xmid_ref)
    for s in range(D_MODEL // LANES):
        xrow_ref[pl.ds(s, tm, stride=ROW_TILE), :] = xn[:, s * LANES:(s + 1) * LANES]

    hi, lo = _split_bf16(xn)
    logits = _dot(hi, wrh_ref[...]) + _dot(hi, wrl_ref[...]) + _dot(lo, wrh_ref[...])
    lane = lax.broadcasted_iota(jnp.int32, logits.shape, 1)
    lg = jnp.where(lane < N_EXPERTS, logits, -jnp.inf)
    m1 = jnp.max(lg, axis=-1, keepdims=True)
    i1 = jnp.min(jnp.where(lg == m1, lane, LANES), axis=-1, keepdims=True)
    lg2 = jnp.where(lane == i1, -jnp.inf, lg)
    m2 = jnp.max(lg2, axis=-1, keepdims=True)
    i2 = jnp.min(jnp.where(lg2 == m2, lane, LANES), axis=-1, keepdims=True)
    e2 = jnp.exp(m2 - m1)
    g1 = 1.0 / (1.0 + e2)
    g2 = e2 * g1

    sel1 = lane == i1
    sel2 = lane == i2
    onehot = jnp.logical_or(sel1, sel2)
    r = lax.broadcasted_iota(jnp.int32, (tm, tm), 0)
    c = lax.broadcasted_iota(jnp.int32, (tm, tm), 1)
    before = _dot((c < r).astype(BF16), onehot.astype(BF16)) + run_s[...]
    rank1 = jnp.sum(jnp.where(sel1, before, 0.0), axis=-1, keepdims=True).astype(jnp.int32)
    rank2 = jnp.sum(jnp.where(sel2, before, 0.0), axis=-1, keepdims=True).astype(jnp.int32)
    run_s[...] += jnp.sum(onehot.astype(F32), axis=0, keepdims=True)
    cnt_ref[...] = run_s[...].astype(jnp.int32)
    ri_ref[...] = jnp.where(lane == 0, i1, jnp.where(lane == 1, i2,
                            jnp.where(lane == 2, rank1, jnp.where(lane == 3, rank2, 0))))
    rf_ref[...] = jnp.where(lane == 0, g1, jnp.where(lane == 1, g2, 0.0))


def _merge(x, ya, yb, yc, wa, wb, wc, g, wr, tm):
    n = x.shape[0]
    row = lambda w: pl.BlockSpec((tm, w), lambda i: (i, 0))
    full = lambda a: pl.BlockSpec(a.shape, lambda i: (0,) * a.ndim)
    ins = [x, ya, yb, yc, wa, wb, wc, g]
    in_specs = [row(D_MODEL), row(A_WIDTH), row(B_WIDTH), row(C_WIDTH_PAD), full(wa), full(wb), full(wc), full(g)]
    if wr is None:
        return pl.pallas_call(
            _merge_dense_kernel,
            out_shape=[jax.ShapeDtypeStruct((n, D_MODEL), F32), jax.ShapeDtypeStruct((n, D_MODEL), BF16)],
            grid=(n // tm,), in_specs=in_specs, out_specs=[row(D_MODEL), row(D_MODEL)],
            compiler_params=_cparams(("parallel",)), name="merge_dense",
        )(*ins)
    wrh, wrl = wr
    return pl.pallas_call(
        _merge_moe_kernel,
        out_shape=[jax.ShapeDtypeStruct((n, D_MODEL), F32),
                   jax.ShapeDtypeStruct((n * ROW_TILE, LANES), F32),
                   jax.ShapeDtypeStruct((n, LANES), jnp.int32),
                   jax.ShapeDtypeStruct((n, LANES), F32),
                   jax.ShapeDtypeStruct((1, LANES), jnp.int32)],
        grid=(n // tm,), in_specs=in_specs + [full(wrh), full(wrl)],
        out_specs=[row(D_MODEL), pl.BlockSpec((tm * ROW_TILE, LANES), lambda i: (i, 0)),
                   row(LANES), row(LANES), pl.BlockSpec((1, LANES), lambda i: (0, 0))],
        scratch_shapes=[pltpu.VMEM((1, LANES), F32)],
        compiler_params=_cparams(("arbitrary",)), name="merge_moe",
    )(*ins, wrh, wrl)


def _dispatch_kernel(dest_ref, src_ref, xs_in_ref, xs_ref, sem):
    del xs_in_ref
    tt = dest_ref.shape[2] // 2

    def issue(t, carry):
        rows = pl.ds(pl.multiple_of(t * ROW_TILE, ROW_TILE), ROW_TILE)
        for k in range(2):
            pltpu.make_async_copy(src_ref.at[rows], xs_ref.at[dest_ref[0, 0, 2 * t + k]], sem).start()
        return carry

    lax.fori_loop(0, tt, issue, 0, unroll=8)

    def drain(t, carry):
        for k in range(2):
            pltpu.make_async_copy(src_ref.at[pl.ds(0, ROW_TILE)], xs_ref.at[0], sem).wait()
        return carry

    lax.fori_loop(0, tt, drain, 0, unroll=8)


def _dispatch(dest, src, xs, tt):
    n = src.shape[0] // ROW_TILE
    dest3 = dest.reshape(n // tt, 1, 2 * tt)
    return pl.pallas_call(
        _dispatch_kernel,
        out_shape=jax.ShapeDtypeStruct(xs.shape, xs.dtype),
        grid=(n // tt,),
        in_specs=[pl.BlockSpec((1, 1, 2 * tt), lambda i: (i, 0, 0), memory_space=pltpu.SMEM),
                  pl.BlockSpec((tt * ROW_TILE, LANES), lambda i: (i, 0)), pl.BlockSpec(memory_space=pl.ANY)],
        out_specs=pl.BlockSpec(memory_space=pl.ANY),
        scratch_shapes=[pltpu.SemaphoreType.DMA(())],
        input_output_aliases={2: 0},
        compiler_params=_cparams(("arbitrary",)), name="moe_dispatch",
    )(dest3, src, xs)


def _combine_kernel(final, dest_ref, ys_ref, rf_ref, xmid_ref, gf_ref, o_ref, buf_s, sem):
    tt = xmid_ref.shape[0]

    def issue(t, carry):
        for k in range(2):
            rows = pl.ds(pl.multiple_of(t * ROW_TILE, ROW_TILE), ROW_TILE)
            pltpu.make_async_copy(ys_ref.at[dest_ref[0, 0, 2 * t + k]], buf_s.at[k, rows], sem).start()
        return carry

    lax.fori_loop(0, tt, issue, 0, unroll=8)

    def drain(t, carry):
        for k in range(2):
            pltpu.make_async_copy(ys_ref.at[0], buf_s.at[0, pl.ds(0, ROW_TILE)], sem).wait()
        return carry

    lax.fori_loop(0, tt, drain, 0, unroll=8)

    g1 = rf_ref[:, 0:1]
    g2 = rf_ref[:, 1:2]
    parts = []
    for s in range(D_MODEL // LANES):
        lanes = slice(s * LANES, (s + 1) * LANES)
        sub = pl.ds(s, tt, stride=ROW_TILE)
        parts.append(xmid_ref[:, lanes] + g1 * buf_s[0, sub, :] + g2 * buf_s[1, sub, :])
    y = jnp.concatenate(parts, axis=1)
    if final:
        y = y * lax.rsqrt(jnp.mean(y * y, axis=-1, keepdims=True) + EPS) * gf_ref[...]
    o_ref[...] = y


def _combine(dest, ys, rf, xmid, gf, tt, final):
    n = xmid.shape[0]
    dest3 = dest.reshape(n // tt, 1, 2 * tt)
    row = lambda w: pl.BlockSpec((tt, w), lambda i: (i, 0))
    return pl.pallas_call(
        functools.partial(_combine_kernel, final),
        out_shape=jax.ShapeDtypeStruct((n, D_MODEL), F32),
        grid=(n // tt,),
        in_specs=[pl.BlockSpec((1, 1, 2 * tt), lambda i: (i, 0, 0), memory_space=pltpu.SMEM),
                  pl.BlockSpec(memory_space=pl.ANY), row(LANES), row(D_MODEL),
                  pl.BlockSpec((1, D_MODEL), lambda i: (0, 0))],
        out_specs=row(D_MODEL),
        scratch_shapes=[pltpu.VMEM((2, tt * ROW_TILE, LANES), F32), pltpu.SemaphoreType.DMA(())],
        compiler_params=_cparams(("arbitrary",)), name="moe_combine",
    )(dest3, ys, rf, xmid, gf)


def _swiglu_acc(xn, wg_ref, wu_ref, wd_ref, acc_s):
    g = _dot(xn, wg_ref[0])
    u = _dot(xn, wu_ref[0])
    acc_s[...] += _dot((g * _sigmoid(g) * u).astype(BF16), wd_ref[0])


def _ffn_dense_kernel(final, xn_ref, wg_ref, wu_ref, wd_ref, xmid_ref, gf_ref, o_ref, acc_s):
    f = pl.program_id(1)

    @pl.when(f == 0)
    def _():
        acc_s[...] = jnp.zeros_like(acc_s)

    _swiglu_acc(xn_ref[...], wg_ref, wu_ref, wd_ref, acc_s)

    @pl.when(f == pl.num_programs(1) - 1)
    def _():
        y = xmid_ref[...] + acc_s[...]
        if final:
            y = y * lax.rsqrt(jnp.mean(y * y, axis=-1, keepdims=True) + EPS) * gf_ref[...]
        o_ref[...] = y


def _ffn_dense(xn, wg, wu, wd, xmid, gf, tm, final):
    n = xn.shape[0]
    tf = FFN_F_TILE
    row = lambda w: pl.BlockSpec((tm, w), lambda i, f: (i, 0))
    return pl.pallas_call(
        functools.partial(_ffn_dense_kernel, final),
        out_shape=jax.ShapeDtypeStruct((n, D_MODEL), F32),
        grid=(n // tm, D_FF // tf),
        in_specs=[row(D_MODEL),
                  pl.BlockSpec((1, D_MODEL, tf), lambda i, f: (0, 0, f)),
                  pl.BlockSpec((1, D_MODEL, tf), lambda i, f: (0, 0, f)),
                  pl.BlockSpec((1, tf, D_MODEL), lambda i, f: (0, f, 0)),
                  row(D_MODEL), pl.BlockSpec((1, D_MODEL), lambda i, f: (0, 0))],
        out_specs=row(D_MODEL),
        scratch_shapes=[pltpu.VMEM((tm, D_MODEL), F32)],
        compiler_params=_cparams(("parallel", "arbitrary")),
        name="ffn_dense",
    )(xn, wg, wu, wd, xmid, gf)


def _ffn_routed_kernel(te_ref, nu_ref, xs_ref, wg_ref, wu_ref, wd_ref, ys_ref, xb_s, acc_s):
    del te_ref
    tm = xb_s.shape[0]
    i = pl.program_id(0)
    f = pl.program_id(1)

    @pl.when(i < nu_ref[0])
    def _():
        @pl.when(f == 0)
        def _():
            acc_s[...] = jnp.zeros_like(acc_s)
            for s in range(D_MODEL // LANES):
                xb_s[:, s * LANES:(s + 1) * LANES] = xs_ref[pl.ds(s, tm, stride=ROW_TILE), :].astype(BF16)

        _swiglu_acc(xb_s[...], wg_ref, wu_ref, wd_ref, acc_s)

        @pl.when(f == pl.num_programs(1) - 1)
        def _():
            for s in range(D_MODEL // LANES):
                ys_ref[pl.ds(s, tm, stride=ROW_TILE), :] = acc_s[:, s * LANES:(s + 1) * LANES]

    @pl.when(jnp.logical_and(i >= nu_ref[0], f == 0))
    def _():
        ys_ref[...] = jnp.zeros_like(ys_ref)


def _ffn_routed(tile_expert, n_used, xs, wg, wu, wd):
    n_tiles = tile_expert.shape[0]
    tm, tf = MOE_TILE, FFN_F_TILE
    n_f = D_FF // tf
    last = lambda i, nu: jnp.minimum(i, nu[0] - 1)
    fcol = lambda i, f, nu: jnp.where(i < nu[0], f, n_f - 1)
    return pl.pallas_call(
        _ffn_routed_kernel,
        out_shape=jax.ShapeDtypeStruct(xs.shape, F32),
        grid_spec=pltpu.PrefetchScalarGridSpec(
            num_scalar_prefetch=2,
            grid=(n_tiles, n_f),
            in_specs=[pl.BlockSpec((tm * ROW_TILE, LANES), lambda i, f, te, nu: (last(i, nu), 0)),
                      pl.BlockSpec((1, D_MODEL, tf), lambda i, f, te, nu: (te[last(i, nu)], 0, fcol(i, f, nu))),
                      pl.BlockSpec((1, D_MODEL, tf), lambda i, f, te, nu: (te[last(i, nu)], 0, fcol(i, f, nu))),
                      pl.BlockSpec((1, tf, D_MODEL), lambda i, f, te, nu: (te[last(i, nu)], fcol(i, f, nu), 0))],
            out_specs=pl.BlockSpec((tm * ROW_TILE, LANES), lambda i, f, te, nu: (i, 0)),
            scratch_shapes=[pltpu.VMEM((tm, D_MODEL), BF16), pltpu.VMEM((tm, D_MODEL), F32)]),
        compiler_params=_cparams(("arbitrary", "arbitrary")),
        name="ffn_routed",
    )(tile_expert, n_used, xs, wg, wu, wd)


def _route_plan(ri_p, cnt_p, ri_s, cnt_s, n_tiles):
    cnt_p, cnt_s = cnt_p[0, :N_EXPERTS], cnt_s[0, :N_EXPERTS]
    tiles = (cnt_p + cnt_s + MOE_TILE - 1) // MOE_TILE
    ends = jnp.cumsum(tiles)
    start = (ends - tiles) * MOE_TILE
    lookup = lambda table, idx: jnp.sum(
        jnp.where(idx[..., None] == jnp.arange(N_EXPERTS, dtype=jnp.int32), table, 0), axis=-1)
    dest_p = lookup(start, ri_p[:, 0:2]) + ri_p[:, 2:4]
    dest_s = lookup(start + cnt_p, ri_s[:, 0:2]) + ri_s[:, 2:4]
    tile_expert = jnp.minimum(jnp.sum(jnp.arange(n_tiles, dtype=jnp.int32)[:, None] >= ends[None, :], axis=-1),
                              N_EXPERTS - 1).astype(jnp.int32)
    return dest_p, dest_s, tile_expert, ends[-1:].astype(jnp.int32)


def _pad_heads_cols(w):
    w = w.reshape(w.shape[0], C_HEADS, C_HEAD_DIM)
    return jnp.pad(w, ((0, 0), (0, 0), (0, C_HEAD_PAD - C_HEAD_DIM))).reshape(w.shape[0], C_WIDTH_PAD)


def _layer_params(l, w_in, b_gate, w_s, b_s, gn_c, w_out):
    w = w_in[l]
    gates = jnp.pad(w[:, OFF_CG:OFF_CG + 2 * C_HEADS], ((0, 0), (0, LANES - 2 * C_HEADS)))
    wp = jnp.concatenate(
        [w[:, :OFF_CQ]] + [_pad_heads_cols(w[:, o:o + C_WIDTH]) for o in (OFF_CQ, OFF_CK, OFF_CV, OFF_CO)]
        + [gates], axis=1).astype(BF16)
    bg = jnp.pad(b_gate[l], (0, LANES - 2 * C_HEADS))[None, :]
    pos = jnp.arange(GMLP_CHUNK)
    mask = (pos[None, :] // CHUNK) <= (pos[:, None] // CHUNK)
    wm = jnp.where(mask[None], w_s[l], 0.0)
    bs = jnp.repeat(b_s[l].T, A_GROUP_DIM, axis=1)
    wo = w_out[l]
    wc = wo[A_WIDTH + B_WIDTH:].reshape(C_HEADS, C_HEAD_DIM, D_MODEL)
    wc = jnp.pad(wc, ((0, 0), (0, C_HEAD_PAD - C_HEAD_DIM), (0, 0))).reshape(C_WIDTH_PAD, D_MODEL)
    gnc = _pad_heads_cols(gn_c[l][None, :])
    return dict(wp=wp, bg=bg, wm=wm, bs=bs, wa=wo[:A_WIDTH].astype(BF16),
                wb=wo[A_WIDTH:A_WIDTH + B_WIDTH].astype(BF16), wc=wc.astype(BF16), gnc=gnc)


def _pad_state(c, n, m):
    p = C_HEAD_PAD - C_HEAD_DIM
    c = jnp.pad(c, ((0, 0), (0, 0), (0, p), (0, p)))
    n = jnp.pad(n, ((0, 0), (0, 0), (0, p)))[:, :, None, :]
    m = jnp.broadcast_to(m[:, :, None, None], m.shape + (1, LANES))
    return c, n, m


def _unpad_state(c, n, m):
    return c[:, :, :C_HEAD_DIM, :C_HEAD_DIM], n[:, :, 0, :C_HEAD_DIM], m[:, :, 0, 0]


def kernel(x_prompt, x_sample, cache_k_b, cache_v_b, state_c_mlstm, state_n_mlstm, state_m_mlstm,
           g_mix, w_in, b_gate, ln_a_g, ln_a_b, w_s, b_s, gn_b, gn_c, w_out,
           g_ffn, w_gate_d, w_up_d, w_down_d, w_router, w_gate_e, w_up_e, w_down_e, g_final):
    n_seq = x_prompt.shape[1]
    n_dec, n_new = x_sample.shape[0], x_sample.shape[1]
    past = cache_k_b.shape[2]
    n_samp = n_dec * n_new

    xp = x_prompt.reshape(n_seq, D_MODEL)
    xs = x_sample.reshape(n_samp, D_MODEL)
    gfin = g_final[None, :]

    outs = {k: [] for k in ("kbp", "vbp", "cp", "np", "mp", "kbs", "vbs", "cs", "ns", "ms", "vas")}
    for l in range(DEPTH):
        p = _layer_params(l, w_in, b_gate, w_s, b_s, gn_c, w_out)
        gmix = g_mix[l][None, :]
        lng, lnb = ln_a_g[l][None, :], ln_a_b[l][None, :]
        gnb = gn_b[l][None, :]
        gffn = g_ffn[l][None, :]
        moe = l % 2 == 1
        j = l // 2
        if moe:
            wg, wu, wd = w_gate_e[j].astype(BF16), w_up_e[j].astype(BF16), w_down_e[j].astype(BF16)
            wr = _split_bf16(jnp.pad(w_router[j], ((0, 0), (0, LANES - N_EXPERTS))))
        else:
            wg, wu, wd = (w_gate_d[j][None].astype(BF16), w_up_d[j][None].astype(BF16),
                          w_down_d[j][None].astype(BF16))
            wr = None
        final = l == DEPTH - 1

        wm_s = jnp.kron(jnp.eye(n_dec, dtype=F32), p["wm"][:, :n_new, :n_new])
        bs_s = jnp.tile(p["bs"][:n_new], (n_dec, 1))

        (ya, qb, qc, kc, vc, oc, gt, kbt, vbt) = _proj(
            xp, gmix, p["wp"], p["bg"], lng, lnb, p["wm"].astype(BF16), p["bs"], ROW_BLOCK, GMLP_CHUNK, False)
        yb = _sb_prompt(qb, kbt, vbt, gnb)
        c0, n0, m0 = _pad_state(jnp.zeros((1, C_HEADS, C_HEAD_DIM, C_HEAD_DIM), F32),
                                jnp.zeros((1, C_HEADS, C_HEAD_DIM), F32), jnp.zeros((1, C_HEADS), F32))
        yc, c_f, n_f, m_f = _mlstm(qc[None], kc[None], vc[None], oc[None], gt[None], p["gnc"],
                                   c0, n0, m0, 2)
        merged_p = _merge(xp, ya, yb, yc[0], p["wa"], p["wb"], p["wc"], gffn, wr, ROW_BLOCK)
        c_f, n_f, m_f = _unpad_state(c_f, n_f, m_f)
        heads_last = lambda a: jnp.transpose(a.reshape(B_HEADS, B_HEAD_DIM, n_seq), (2, 0, 1))[None]
        outs["kbp"].append(heads_last(kbt))
        outs["vbp"].append(heads_last(vbt))
        outs["cp"].append(c_f)
        outs["np"].append(n_f)
        outs["mp"].append(m_f)

        (ya, qb, qc, kc, vc, oc, gt, kbf, vbf, kbh, vbh, va) = _proj(
            xs, gmix, p["wp"], p["bg"], lng, lnb, wm_s.astype(BF16), bs_s, n_samp, n_samp, True)
        r3 = lambda a: a.reshape(n_dec, n_new, a.shape[-1])
        keys_last = lambda a: jnp.transpose(a, (0, 2, 3, 1)).reshape(n_dec, B_WIDTH, past)
        yb = _sb_sample(r3(qb), r3(kbh), r3(vbh), keys_last(cache_k_b[l]), keys_last(cache_v_b[l]), gnb)
        padr = lambda a: jnp.pad(r3(a), ((0, 0), (0, MLSTM_BLOCK - n_new), (0, 0)))
        lane = jnp.arange(LANES)
        gt_pad = jnp.broadcast_to(jnp.where(lane < C_HEADS, NEG_BIG, 0.0).astype(F32),
                                  (n_dec, MLSTM_BLOCK - n_new, LANES))
        gt_s = jnp.concatenate([r3(gt), gt_pad], axis=1)
        c0, n0, m0 = _pad_state(state_c_mlstm[l], state_n_mlstm[l], state_m_mlstm[l])
        yc, c_u, n_u, m_u = _mlstm(padr(qc), padr(kc), padr(vc), padr(oc), gt_s, p["gnc"], c0, n0, m0, 1)
        yc = yc[:, :n_new].reshape(n_samp, C_WIDTH_PAD)
        merged_s = _merge(xs, ya, yb.reshape(n_samp, B_WIDTH), yc, p["wa"], p["wb"], p["wc"],
                          gffn, wr, n_samp)
        c_u, n_u, m_u = _unpad_state(c_u, n_u, m_u)

        if moe:
            xmid_p, xrow_p, ri_p, rf_p, cnt_p = merged_p
            xmid_s, xrow_s, ri_s, rf_s, cnt_s = merged_s
            n_tiles = 2 * (n_seq + n_samp) // MOE_TILE + N_EXPERTS
            dest_p, dest_s, tile_expert, n_used = _route_plan(ri_p, cnt_p, ri_s, cnt_s, n_tiles)
            as_tiles = lambda a: a.reshape(a.shape[0] // ROW_TILE, ROW_TILE, LANES)
            xsort = jnp.zeros((n_tiles * MOE_TILE, ROW_TILE, LANES), F32)
            xsort = _dispatch(dest_p, xrow_p, xsort, ROW_BLOCK)
            xsort = _dispatch(dest_s, xrow_s, xsort, n_samp)
            ysort = _ffn_routed(tile_expert, n_used, xsort.reshape(-1, LANES), wg, wu, wd)
            xp = _combine(dest_p, as_tiles(ysort), rf_p, xmid_p, gfin, MOE_COMBINE_BLOCK, final)
            xs = _combine(dest_s, as_tiles(ysort), rf_s, xmid_s, gfin, n_samp, final)
        else:
            xp = _ffn_dense(merged_p[1], wg, wu, wd, merged_p[0], gfin, ROW_BLOCK, final)
            xs = _ffn_dense(merged_s[1], wg, wu, wd, merged_s[0], gfin, n_samp, final)

        outs["kbs"].append(kbf.reshape(n_dec, n_new, B_HEADS, B_HEAD_DIM))
        outs["vbs"].append(vbf.reshape(n_dec, n_new, B_HEADS, B_HEAD_DIM))
        outs["cs"].append(c_u)
        outs["ns"].append(n_u)
        outs["ms"].append(m_u)
        outs["vas"].append(va.reshape(n_dec, n_new, A_WIDTH))

    st = lambda k: jnp.stack(outs[k])
    return (xp.reshape(1, n_seq, D_MODEL), xs.reshape(n_dec, n_new, D_MODEL),
            st("kbp"), st("vbp"), st("cp"), st("np"), st("mp"),
            st("kbs"), st("vbs"), st("cs"), st("ns"), st("ms"), st("vas"))
```

```python
import functools
import math

import jax
import jax.numpy as jnp
from jax import lax
from jax.experimental import pallas as pl
from jax.experimental.pallas import tpu as pltpu

F32 = jnp.float32
BF16 = jnp.bfloat16

D_MODEL = 1024
DEPTH = 2
EPS = 1e-6
CHUNK = 64
A_WIDTH = 256
A_GROUPS = 4
A_GROUP_DIM = 64
GMLP_CHUNK = 128
B_HEAD_DIM = 64
B_WIDTH = 384
B_HEADS = 6
B_PAIRS = 3
C_HEADS = 4
C_HEAD_DIM = 96
C_WIDTH = 384
D_FF = 2816
N_EXPERTS = 8

LANES = 128
C_HEAD_PAD = LANES
C_WIDTH_PAD = C_HEADS * C_HEAD_PAD

OFF_AU, OFF_AV, OFF_BQ, OFF_BK, OFF_BV = 0, 256, 512, 896, 1280
OFF_CQ, OFF_CK, OFF_CV, OFF_CO, OFF_CG = 1664, 2048, 2432, 2816, 3200
P_A = 0
P_BQ = 512
P_BK = P_BQ + B_WIDTH
P_BV = P_BK + B_WIDTH
P_CQ = P_BV + B_WIDTH
P_CK = P_CQ + C_WIDTH_PAD
P_CV = P_CK + C_WIDTH_PAD
P_CO = P_CV + C_WIDTH_PAD
P_CG = P_CO + C_WIDTH_PAD
P_DIM = P_CG + LANES

SB_DEAD_LOG_WEIGHT = -110.0
SB_BLOCK = 256
MLSTM_BLOCK = 128
MLSTM_SUB_BLOCKS = 4
GATE_ROWS = 2 * C_HEADS
NEG_BIG = -1e30
ROW_TILE = 8
MOE_TILE = 512
FFN_F_TILE = 1408
ROW_BLOCK = 512
MOE_COMBINE_BLOCK = 256

VMEM_LIMIT = 56 * 1024 * 1024


def _cparams(sem):
    return pltpu.CompilerParams(dimension_semantics=sem, vmem_limit_bytes=VMEM_LIMIT)


def _gelu(x):
    return 0.5 * x * (1.0 + lax.erf(x * (1.0 / math.sqrt(2.0))))


def _log_sigmoid(x):
    return jnp.minimum(x, 0.0) - jnp.log(1.0 + jnp.exp(-jnp.abs(x)))


def _sigmoid(x):
    return 1.0 / (1.0 + jnp.exp(-x))


def _split_bf16(x):
    hi = x.astype(BF16)
    lo = (x - hi.astype(F32)).astype(BF16)
    return hi, lo


def _dot(a, b):
    return jnp.dot(a, b, preferred_element_type=F32)


def _dot_nt(a, b):
    return lax.dot_general(a, b, (((1,), (1,)), ((), ())), preferred_element_type=F32)


def _dot_tn(a, b):
    return lax.dot_general(a, b, (((0,), (0,)), ((), ())), preferred_element_type=F32)


def _proj_kernel(n_chunks, chunk, sample, x_ref, gmix_ref, w_ref, bg_ref, lng_ref, lnb_ref, ws_ref, bs_ref,
                 ya_ref, qb_ref, qc_ref, kc_ref, kct_ref, vc_ref, oc_ref, gt_ref, gtt_ref, *kv_refs):
    x = x_ref[...]
    xn = (x * lax.rsqrt(jnp.mean(x * x, axis=-1, keepdims=True) + EPS) * gmix_ref[...]).astype(BF16)

    def proj(off, width):
        return _dot(xn, w_ref[:, off:off + width])

    za = proj(P_A, 2 * A_WIDTH)
    u = _gelu(za[:, :A_WIDTH])
    gv = _gelu(za[:, A_WIDTH:])
    xc = gv - jnp.mean(gv, axis=-1, keepdims=True)
    va = xc * lax.rsqrt(jnp.mean(xc * xc, axis=-1, keepdims=True) + EPS) * lng_ref[...] + lnb_ref[...]
    if sample:
        kv_refs[4][...] = va
    vab = va.astype(BF16)
    lane_group = lax.broadcasted_iota(jnp.int32, (chunk, A_WIDTH), 1) // A_GROUP_DIM
    for c in range(n_chunks):
        rows = slice(c * chunk, (c + 1) * chunk)
        vch = vab[rows]
        s = jnp.zeros((chunk, A_WIDTH), F32)
        for g in range(A_GROUPS):
            s = jnp.where(lane_group == g, _dot(ws_ref[g], vch), s)
        ya_ref[rows, :] = (u[rows] * (s + bs_ref[...])).astype(BF16)

    qb_ref[...] = (proj(P_BQ, B_WIDTH) * (1.0 / math.sqrt(B_HEAD_DIM))).astype(BF16)
    zk = proj(P_BK, B_WIDTH)
    zv = proj(P_BV, B_WIDTH)
    if sample:
        kv_refs[0][...] = zk
        kv_refs[1][...] = zv
        kv_refs[2][...] = zk.astype(BF16)
        kv_refs[3][...] = zv.astype(BF16)
    else:
        kv_refs[0][...] = zk.T
        kv_refs[1][...] = zv.T

    qc_ref[...] = proj(P_CQ, C_WIDTH_PAD).astype(BF16)
    zk = proj(P_CK, C_WIDTH_PAD) * (C_HEAD_DIM ** -0.5)
    kc_ref[...] = zk.astype(BF16)
    kct_ref[...] = zk.T.astype(BF16)
    vc_ref[...] = proj(P_CV, C_WIDTH_PAD).astype(BF16)
    oc_ref[...] = proj(P_CO, C_WIDTH_PAD)
    g = proj(P_CG, LANES) + bg_ref[...]
    lane = lax.broadcasted_iota(jnp.int32, g.shape, 1)
    gates = jnp.where(lane < C_HEADS, g, _log_sigmoid(g))
    gt_ref[...] = gates
    gtt_ref[...] = gates.T[:GATE_ROWS, :]


def _proj(x, gmix, wp, bg, lng, lnb, ws, bs, tm, chunk, sample):
    n = x.shape[0]
    row = lambda w: pl.BlockSpec((tm, w), lambda i: (i, 0))
    full = lambda a: pl.BlockSpec(a.shape, lambda i: (0,) * a.ndim)
    col = lambda h: pl.BlockSpec((h, tm), lambda i: (0, i))
    outs = [((n, A_WIDTH), BF16, row(A_WIDTH)), ((n, B_WIDTH), BF16, row(B_WIDTH)),
            ((n, C_WIDTH_PAD), BF16, row(C_WIDTH_PAD)), ((n, C_WIDTH_PAD), BF16, row(C_WIDTH_PAD)),
            ((C_WIDTH_PAD, n), BF16, col(C_WIDTH_PAD)), ((n, C_WIDTH_PAD), BF16, row(C_WIDTH_PAD)),
            ((n, C_WIDTH_PAD), F32, row(C_WIDTH_PAD)), ((n, LANES), F32, row(LANES)),
            ((GATE_ROWS, n), F32, col(GATE_ROWS))]
    out_shape = [jax.ShapeDtypeStruct(s, dt) for s, dt, _ in outs]
    out_specs = [spec for _, _, spec in outs]
    if sample:
        extra = [(B_WIDTH, F32), (B_WIDTH, F32), (B_WIDTH, BF16), (B_WIDTH, BF16), (A_WIDTH, F32)]
        out_shape += [jax.ShapeDtypeStruct((n, w), dt) for w, dt in extra]
        out_specs += [row(w) for w, _ in extra]
    else:
        out_shape += [jax.ShapeDtypeStruct((B_WIDTH, n), F32)] * 2
        out_specs += [pl.BlockSpec((B_WIDTH, tm), lambda i: (0, i))] * 2
    return pl.pallas_call(
        functools.partial(_proj_kernel, tm // chunk, chunk, sample),
        out_shape=out_shape,
        grid=(n // tm,),
        in_specs=[row(D_MODEL), full(gmix), full(wp), full(bg), full(lng), full(lnb), full(ws), full(bs)],
        out_specs=out_specs,
        compiler_params=_cparams(("parallel",)),
        name="proj",
    )(x, gmix, wp, bg, lng, lnb, ws, bs)


def _sb_step(qh, kblk, vblk, carry, acc, tri, mask, transposed):
    z = _dot(qh, kblk) if transposed else _dot_nt(qh, kblk)
    tk = z.shape[1]
    drop = jnp.maximum(z, 0.0) + jnp.log(1.0 + jnp.exp(-jnp.abs(z)))
    if mask is not None:
        drop = jnp.where(mask, drop, 0.0)
    hi, lo = _split_bf16(drop)
    cs = _dot(hi, tri) + _dot(lo, tri)
    if tk >= LANES:
        carry_b = jnp.concatenate([carry] * (tk // LANES), axis=1)
    else:
        carry_b = carry[:, :tk]
    a = jnp.exp(z - cs - carry_b)
    if mask is not None:
        a = jnp.where(mask, a, 0.0)
    a = a.astype(BF16)
    acc = acc + (_dot_nt(a, vblk) if transposed else _dot(a, vblk))
    return carry + jnp.broadcast_to(cs[:, :1], carry.shape), acc


def _tri(tk):
    j = lax.broadcasted_iota(jnp.int32, (tk, tk), 0)
    s = lax.broadcasted_iota(jnp.int32, (tk, tk), 1)
    return (j >= s).astype(BF16)


def _sb_finish(acc_s, gn_ref, o_ref, head0):
    out = jnp.where(head0, acc_s[0], acc_s[1])
    r = lax.broadcasted_iota(jnp.int32, (LANES, LANES), 0) // B_HEAD_DIM
    c = lax.broadcasted_iota(jnp.int32, (LANES, LANES), 1) // B_HEAD_DIM
    same_head = (r == c).astype(BF16)
    hi, lo = _split_bf16(out * out)
    ms = (_dot(hi, same_head) + _dot(lo, same_head)) * (1.0 / B_HEAD_DIM)
    return (out * lax.rsqrt(ms + EPS) * gn_ref[...]).astype(o_ref.dtype)


def _sb_walk(qh, load_kv, first_block, carry_s, acc_s, tri):
    def alive():
        return jnp.minimum(jnp.min(carry_s[0]), jnp.min(carry_s[1])) <= -SB_DEAD_LOG_WEIGHT

    def cond(st):
        j, live = st
        return jnp.logical_and(j >= 0, live)

    def body(st):
        j, _ = st
        kblk, vblk = load_kv(j)
        for h in range(2):
            carry, acc = _sb_step(qh[h], kblk, vblk, carry_s[h], acc_s[h], tri, None, True)
            carry_s[h] = carry
            acc_s[h] = acc
        return j - 1, alive()

    lax.while_loop(cond, body, (first_block, alive()))


def _sb_prompt_kernel(q_ref, k_ref, v_ref, gn_ref, o_ref, carry_s, acc_s):
    tq = q_ref.shape[0]
    i = pl.program_id(1)
    head0 = lax.broadcasted_iota(jnp.int32, (tq, LANES), 1) < B_HEAD_DIM
    q = q_ref[...]
    zero = jnp.zeros_like(q)
    qh = [jnp.where(head0, q, zero), jnp.where(head0, zero, q)]
    tri = _tri(tq)
    t = lax.broadcasted_iota(jnp.int32, (tq, tq), 0)
    s = lax.broadcasted_iota(jnp.int32, (tq, tq), 1)
    causal = s < t

    def load_kv(j):
        cols = pl.ds(pl.multiple_of(j * tq, tq), tq)
        return k_ref[:, cols].astype(BF16), v_ref[:, cols].astype(BF16)

    kd, vd = load_kv(i)
    kp, vp = load_kv(jnp.maximum(i - 1, 0))
    has_prev = t >= jnp.where(i >= 1, 0, tq)
    zeros = jnp.zeros((tq, LANES), F32)
    for h in range(2):
        carry, acc = _sb_step(qh[h], kd, vd, zeros, zeros, tri, causal, True)
        carry, acc = _sb_step(qh[h], kp, vp, carry, acc, tri, has_prev, True)
        carry_s[h] = carry
        acc_s[h] = acc
    _sb_walk(qh, load_kv, i - 2, carry_s, acc_s, tri)
    o_ref[...] = _sb_finish(acc_s, gn_ref, o_ref, head0)


def _sb_prompt(q, k, v, gn):
    n = q.shape[0]
    tq = SB_BLOCK
    blk = pl.BlockSpec((tq, LANES), lambda p, i: (i, p))
    seq = pl.BlockSpec((LANES, n), lambda p, i: (p, 0))
    return pl.pallas_call(
        _sb_prompt_kernel,
        out_shape=jax.ShapeDtypeStruct((n, B_WIDTH), BF16),
        grid=(B_PAIRS, n // tq),
        in_specs=[blk, seq, seq, pl.BlockSpec((1, LANES), lambda p, i: (0, p))],
        out_specs=blk,
        scratch_shapes=[pltpu.VMEM((2, tq, LANES), F32), pltpu.VMEM((2, tq, LANES), F32)],
        compiler_params=_cparams(("parallel", "parallel")),
        name="sb_prompt",
    )(q, k, v, gn)


def _sb_sample_kernel(q_ref, kn_ref, vn_ref, kc_ref, vc_ref, gn_ref, o_ref, carry_s, acc_s):
    tq = q_ref.shape[1]
    tk = SB_BLOCK
    head0 = lax.broadcasted_iota(jnp.int32, (tq, LANES), 1) < B_HEAD_DIM
    q = q_ref[0]
    zero = jnp.zeros_like(q)
    qh = [jnp.where(head0, q, zero), jnp.where(head0, zero, q)]
    t = lax.broadcasted_iota(jnp.int32, (tq, tq), 0)
    s = lax.broadcasted_iota(jnp.int32, (tq, tq), 1)
    causal = s < t
    zeros = jnp.zeros((tq, LANES), F32)
    def load_kv(j):
        cols = pl.ds(pl.multiple_of(j * tk, tk), tk)
        return kc_ref[0, :, cols].astype(BF16), vc_ref[0, :, cols].astype(BF16)

    last = kc_ref.shape[2] // tk - 1
    kp, vp = load_kv(last)
    tri = _tri(tk)
    for h in range(2):
        carry, acc = _sb_step(qh[h], kn_ref[0], vn_ref[0], zeros, zeros, _tri(tq), causal, False)
        carry, acc = _sb_step(qh[h], kp, vp, carry, acc, tri, None, True)
        carry_s[h] = carry
        acc_s[h] = acc
    _sb_walk(qh, load_kv, last - 1, carry_s, acc_s, tri)
    o_ref[0] = _sb_finish(acc_s, gn_ref, o_ref, head0)


def _sb_sample(q, kn, vn, kc, vc, gn):
    nb, tq, _ = q.shape
    past = kc.shape[2]
    new = pl.BlockSpec((1, tq, LANES), lambda b, p: (b, 0, p))
    old = pl.BlockSpec((1, LANES, past), lambda b, p: (b, p, 0))
    return pl.pallas_call(
        _sb_sample_kernel,
        out_shape=jax.ShapeDtypeStruct((nb, tq, B_WIDTH), BF16),
        grid=(nb, B_PAIRS),
        in_specs=[new, new, new, old, old, pl.BlockSpec((1, LANES), lambda b, p: (0, p))],
        out_specs=new,
        scratch_shapes=[pltpu.VMEM((2, tq, LANES), F32), pltpu.VMEM((2, tq, LANES), F32)],
        compiler_params=_cparams(("parallel", "parallel")),
        name="sb_sample",
    )(q, kn, vn, kc, vc, gn)


def _split3(x):
    h1 = x.astype(BF16)
    r1 = x - h1.astype(F32)
    h2 = r1.astype(BF16)
    return h1, h2, (r1 - h2.astype(F32)).astype(BF16)


def _dot3(x, rhs01):
    return _dot(jnp.concatenate(_split3(x), axis=1), jnp.concatenate([rhs01] * 3, axis=0))


def _dot3_left(lhs01, x):
    return _dot(jnp.concatenate([lhs01] * 3, axis=1), jnp.concatenate(_split3(x), axis=0))


def _mlstm_kernel(n_sub, q_ref, k_ref, kt_ref, v_ref, o_ref, gt_ref, gtt_ref, gn_ref, c0_ref, n0_ref, m0_ref,
                  yc_ref, c_out, n_out, m_out, c_s, n_s, m_s):
    L = MLSTM_BLOCK
    t_blk = pl.program_id(1)

    @pl.when(t_blk == 0)
    def _():
        c_s[...] = c0_ref[0]
        n_s[...] = n0_ref[0]
        m_s[...] = m0_ref[0]

    r = lax.broadcasted_iota(jnp.int32, (L, L), 0)
    c = lax.broadcasted_iota(jnp.int32, (L, L), 1)
    causal = c <= r
    upper = (r <= c).astype(BF16)
    ones_sq = jnp.ones((L, LANES), BF16)
    sel_r = lax.broadcasted_iota(jnp.int32, (LANES, C_WIDTH_PAD), 0)
    sel_c = lax.broadcasted_iota(jnp.int32, (LANES, C_WIDTH_PAD), 1) // C_HEAD_PAD
    sel_p = (sel_r == sel_c).astype(BF16)
    sel_b = (sel_r == sel_c + C_HEADS).astype(BF16)
    lane = c

    state = [(c_s[h], n_s[h], m_s[h]) for h in range(C_HEADS)]
    for sub in range(n_sub):
        rows = slice(sub * L, (sub + 1) * L)
        gtt = gtt_ref[0, :, rows]
        bct = _dot3(gtt, upper)
        a_rows = gtt[:C_HEADS] - bct[C_HEADS:]
        gt = gt_ref[0, rows, :]
        bc = _dot3_left(causal.astype(BF16), gt)
        pmax = gt - pltpu.roll(bc, LANES - C_HEADS, axis=1)
        for sh in (1, 2, 4, 8, 16, 32, 64):
            pmax = jnp.maximum(pmax, jnp.where(r >= sh, pltpu.roll(pmax, sh, axis=0), -jnp.inf))
        pb_cols = jnp.where(lane < C_HEADS, pmax, bc)
        p_all = _dot3(pb_cols, sel_p)
        b_all = _dot3(pb_cols, sel_b)
        for h in range(C_HEADS):
            lanes = slice(h * C_HEAD_PAD, (h + 1) * C_HEAD_PAD)
            q = q_ref[0, rows, lanes]
            v1 = jnp.concatenate([v_ref[0, rows, lanes], ones_sq], axis=1)
            p_rep = p_all[:, lanes]
            b_rep = b_all[:, lanes]
            a_row = a_rows[h:h + 1, :]
            p_last = p_rep[L - 1:L, :]
            c_prev, n_prev, m_prev = state[h]

            w = jnp.exp(jnp.where(causal, a_row - p_rep, -jnp.inf)) * _dot_nt(q, k_ref[0, rows, lanes])
            here_sums = _dot(w.astype(BF16), v1)
            past_sums = _dot(q, jnp.concatenate([c_prev, n_prev], axis=1).astype(BF16))
            top = jnp.maximum(m_prev, p_rep)
            past = jnp.exp(m_prev - top)
            here = jnp.exp(p_rep - top)
            mix = jnp.concatenate([past, past], axis=1) * past_sums + jnp.concatenate([here, here], axis=1) * here_sums
            den = jnp.maximum(jnp.abs(mix[:, LANES:]), jnp.exp(-(b_rep + top)))
            hh = mix[:, :LANES] / den

            kwt = (kt_ref[0, lanes, rows].astype(F32) * jnp.exp(a_row - p_last)).astype(BF16)
            fresh = _dot(kwt, v1)
            top_last = jnp.maximum(m_prev, p_last)
            decay = jnp.exp(m_prev - top_last)
            gain = jnp.exp(p_last - top_last)
            state[h] = (decay * c_prev + gain * fresh[:, :LANES], decay * n_prev + gain * fresh[:, LANES:],
                        b_rep[L - 1:L, :] + top_last)

            ms = _dot(jnp.concatenate(_split_bf16(hh * hh), axis=1),
                      jnp.concatenate([ones_sq, ones_sq], axis=0)) * (1.0 / C_HEAD_DIM)
            hn = hh * lax.rsqrt(ms + EPS) * gn_ref[:, lanes]
            yc_ref[0, rows, lanes] = (hn * _sigmoid(o_ref[0, rows, lanes])).astype(BF16)

    for h in range(C_HEADS):
        c_s[h], n_s[h], m_s[h] = state[h]

    @pl.when(t_blk == pl.num_programs(1) - 1)
    def _():
        c_out[0] = c_s[...]
        n_out[0] = n_s[...]
        m_out[0] = m_s[...]


def _mlstm(q, k, kt, v, o, gt, gtt, gn, c0, n0, m0, n_sub):
    nb, n, _ = q.shape
    tb = n_sub * MLSTM_BLOCK
    seq = lambda w: pl.BlockSpec((1, tb, w), lambda b, t: (b, t, 0))
    seq_t = lambda h: pl.BlockSpec((1, h, tb), lambda b, t: (b, 0, t))
    st = lambda a: pl.BlockSpec((1,) + a.shape[1:], lambda b, t: (b,) + (0,) * (a.ndim - 1))
    return pl.pallas_call(
        functools.partial(_mlstm_kernel, n_sub),
        out_shape=[jax.ShapeDtypeStruct((nb, n, C_WIDTH_PAD), BF16),
                   jax.ShapeDtypeStruct(c0.shape, F32),
                   jax.ShapeDtypeStruct(n0.shape, F32),
                   jax.ShapeDtypeStruct(m0.shape, F32)],
        grid=(nb, n // tb),
        in_specs=[seq(C_WIDTH_PAD), seq(C_WIDTH_PAD), seq_t(C_WIDTH_PAD), seq(C_WIDTH_PAD), seq(C_WIDTH_PAD),
                  seq(LANES), seq_t(GATE_ROWS), pl.BlockSpec((1, C_WIDTH_PAD), lambda b, t: (0, 0)),
                  st(c0), st(n0), st(m0)],
        out_specs=[seq(C_WIDTH_PAD), st(c0), st(n0), st(m0)],
        scratch_shapes=[pltpu.VMEM(c0.shape[1:], F32), pltpu.VMEM(n0.shape[1:], F32),
                        pltpu.VMEM(m0.shape[1:], F32)],
        compiler_params=_cparams(("parallel", "arbitrary")),
        name="mlstm",
    )(q, k, kt, v, o, gt, gtt, gn, c0, n0, m0)


def _mixer_out(x_ref, ya_ref, yb_ref, yc_ref, wa_ref, wb_ref, wc_ref, g_ref, xmid_ref):
    y = _dot(ya_ref[...], wa_ref[...]) + _dot(yb_ref[...], wb_ref[...]) + _dot(yc_ref[...], wc_ref[...])
    x = x_ref[...] + y
    xmid_ref[...] = x
    return x * lax.rsqrt(jnp.mean(x * x, axis=-1, keepdims=True) + EPS) * g_ref[...]


def _merge_dense_kernel(x_ref, ya_ref, yb_ref, yc_ref, wa_ref, wb_ref, wc_ref, g_ref, xmid_ref, xn_ref):
    xn_ref[...] = _mixer_out(x_ref, ya_ref, yb_ref, yc_ref, wa_ref, wb_ref, wc_ref, g_ref,
                             xmid_ref).astype(BF16)


def _merge_moe_kernel(x_ref, ya_ref, yb_ref, yc_ref, wa_ref, wb_ref, wc_ref, g_ref, wrh_ref, wrl_ref,
                      xmid_ref, xrow_ref, ri_ref, rf_ref, cnt_ref, run_s):
    tm = x_ref.shape[0]

    @pl.when(pl.program_id(0) == 0)
    def _():
        run_s[...] = jnp.zeros_like(run_s)

    xn = _mixer_out(x_ref, ya_ref, yb_ref, yc_ref, wa_ref, wb_ref, wc_ref, g_ref, xmid_ref)
    for s in range(D_MODEL // LANES):
        xrow_ref[pl.ds(s, tm, stride=ROW_TILE), :] = xn[:, s * LANES:(s + 1) * LANES]

    hi, lo = _split_bf16(xn)
    logits = _dot(hi, wrh_ref[...]) + _dot(hi, wrl_ref[...]) + _dot(lo, wrh_ref[...])
    lane = lax.broadcasted_iota(jnp.int32, logits.shape, 1)
    lg = jnp.where(lane < N_EXPERTS, logits, -jnp.inf)
    m1 = jnp.max(lg, axis=-1, keepdims=True)
    i1 = jnp.min(jnp.where(lg == m1, lane, LANES), axis=-1, keepdims=True)
    lg2 = jnp.where(lane == i1, -jnp.inf, lg)
    m2 = jnp.max(lg2, axis=-1, keepdims=True)
    i2 = jnp.min(jnp.where(lg2 == m2, lane, LANES), axis=-1, keepdims=True)
    e2 = jnp.exp(m2 - m1)
    g1 = 1.0 / (1.0 + e2)
    g2 = e2 * g1

    sel1 = lane == i1
    sel2 = lane == i2
    onehot = jnp.logical_or(sel1, sel2)
    r = lax.broadcasted_iota(jnp.int32, (tm, tm), 0)
    c = lax.broadcasted_iota(jnp.int32, (tm, tm), 1)
    before = _dot((c < r).astype(BF16), onehot.astype(BF16)) + run_s[...]
    rank1 = jnp.sum(jnp.where(sel1, before, 0.0), axis=-1, keepdims=True).astype(jnp.int32)
    rank2 = jnp.sum(jnp.where(sel2, before, 0.0), axis=-1, keepdims=True).astype(jnp.int32)
    run_s[...] += jnp.sum(onehot.astype(F32), axis=0, keepdims=True)
    cnt_ref[...] = run_s[...].astype(jnp.int32)
    ri_ref[...] = jnp.where(lane == 0, i1, jnp.where(lane == 1, i2,
                            jnp.where(lane == 2, rank1, jnp.where(lane == 3, rank2, 0))))
    rf_ref[...] = jnp.where(lane == 0, g1, jnp.where(lane == 1, g2, 0.0))


def _merge(x, ya, yb, yc, wa, wb, wc, g, wr, tm):
    n = x.shape[0]
    row = lambda w: pl.BlockSpec((tm, w), lambda i: (i, 0))
    full = lambda a: pl.BlockSpec(a.shape, lambda i: (0,) * a.ndim)
    ins = [x, ya, yb, yc, wa, wb, wc, g]
    in_specs = [row(D_MODEL), row(A_WIDTH), row(B_WIDTH), row(C_WIDTH_PAD), full(wa), full(wb), full(wc), full(g)]
    if wr is None:
        return pl.pallas_call(
            _merge_dense_kernel,
            out_shape=[jax.ShapeDtypeStruct((n, D_MODEL), F32), jax.ShapeDtypeStruct((n, D_MODEL), BF16)],
            grid=(n // tm,), in_specs=in_specs, out_specs=[row(D_MODEL), row(D_MODEL)],
            compiler_params=_cparams(("parallel",)), name="merge_dense",
        )(*ins)
    wrh, wrl = wr
    return pl.pallas_call(
        _merge_moe_kernel,
        out_shape=[jax.ShapeDtypeStruct((n, D_MODEL), F32),
                   jax.ShapeDtypeStruct((n * ROW_TILE, LANES), F32),
                   jax.ShapeDtypeStruct((n, LANES), jnp.int32),
                   jax.ShapeDtypeStruct((n, LANES), F32),
                   jax.ShapeDtypeStruct((1, LANES), jnp.int32)],
        grid=(n // tm,), in_specs=in_specs + [full(wrh), full(wrl)],
        out_specs=[row(D_MODEL), pl.BlockSpec((tm * ROW_TILE, LANES), lambda i: (i, 0)),
                   row(LANES), row(LANES), pl.BlockSpec((1, LANES), lambda i: (0, 0))],
        scratch_shapes=[pltpu.VMEM((1, LANES), F32)],
        compiler_params=_cparams(("arbitrary",)), name="merge_moe",
    )(*ins, wrh, wrl)


def _dispatch_kernel(dest_ref, src_ref, xs_in_ref, xs_ref, sem):
    del xs_in_ref
    tt = dest_ref.shape[2] // 2

    def issue(t, carry):
        rows = pl.ds(pl.multiple_of(t * ROW_TILE, ROW_TILE), ROW_TILE)
        for k in range(2):
            pltpu.make_async_copy(src_ref.at[rows], xs_ref.at[dest_ref[0, 0, 2 * t + k]], sem).start(priority=k)
        return carry

    lax.fori_loop(0, tt, issue, 0, unroll=8)

    def drain(t, carry):
        for k in range(2):
            pltpu.make_async_copy(src_ref.at[pl.ds(0, ROW_TILE)], xs_ref.at[0], sem).wait()
        return carry

    lax.fori_loop(0, tt, drain, 0, unroll=8)


def _dispatch(dest, src, xs, tt):
    n = src.shape[0] // ROW_TILE
    dest3 = dest.reshape(n // tt, 1, 2 * tt)
    return pl.pallas_call(
        _dispatch_kernel,
        out_shape=jax.ShapeDtypeStruct(xs.shape, xs.dtype),
        grid=(n // tt,),
        in_specs=[pl.BlockSpec((1, 1, 2 * tt), lambda i: (i, 0, 0), memory_space=pltpu.SMEM),
                  pl.BlockSpec((tt * ROW_TILE, LANES), lambda i: (i, 0)), pl.BlockSpec(memory_space=pl.ANY)],
        out_specs=pl.BlockSpec(memory_space=pl.ANY),
        scratch_shapes=[pltpu.SemaphoreType.DMA(())],
        input_output_aliases={2: 0},
        compiler_params=_cparams(("arbitrary",)), name="moe_dispatch",
    )(dest3, src, xs)


def _combine_kernel(final, dest_ref, ys_ref, rf_ref, xmid_ref, gf_ref, o_ref, buf_s, sem):
    tt = xmid_ref.shape[0]

    def issue(t, carry):
        for k in range(2):
            rows = pl.ds(pl.multiple_of(t * ROW_TILE, ROW_TILE), ROW_TILE)
            pltpu.make_async_copy(ys_ref.at[dest_ref[0, 0, 2 * t + k]], buf_s.at[k, rows], sem).start(priority=k)
        return carry

    lax.fori_loop(0, tt, issue, 0, unroll=8)

    def drain(t, carry):
        for k in range(2):
            pltpu.make_async_copy(ys_ref.at[0], buf_s.at[0, pl.ds(0, ROW_TILE)], sem).wait()
        return carry

    lax.fori_loop(0, tt, drain, 0, unroll=8)

    g1 = rf_ref[:, 0:1]
    g2 = rf_ref[:, 1:2]
    parts = []
    for s in range(D_MODEL // LANES):
        lanes = slice(s * LANES, (s + 1) * LANES)
        sub = pl.ds(s, tt, stride=ROW_TILE)
        parts.append(xmid_ref[:, lanes] + g1 * buf_s[0, sub, :] + g2 * buf_s[1, sub, :])
    y = jnp.concatenate(parts, axis=1)
    if final:
        y = y * lax.rsqrt(jnp.mean(y * y, axis=-1, keepdims=True) + EPS) * gf_ref[...]
    o_ref[...] = y


def _combine(dest, ys, rf, xmid, gf, tt, final):
    n = xmid.shape[0]
    dest3 = dest.reshape(n // tt, 1, 2 * tt)
    row = lambda w: pl.BlockSpec((tt, w), lambda i: (i, 0))
    return pl.pallas_call(
        functools.partial(_combine_kernel, final),
        out_shape=jax.ShapeDtypeStruct((n, D_MODEL), F32),
        grid=(n // tt,),
        in_specs=[pl.BlockSpec((1, 1, 2 * tt), lambda i: (i, 0, 0), memory_space=pltpu.SMEM),
                  pl.BlockSpec(memory_space=pl.ANY), row(LANES), row(D_MODEL),
                  pl.BlockSpec((1, D_MODEL), lambda i: (0, 0))],
        out_specs=row(D_MODEL),
        scratch_shapes=[pltpu.VMEM((2, tt * ROW_TILE, LANES), F32), pltpu.SemaphoreType.DMA(())],
        compiler_params=_cparams(("arbitrary",)), name="moe_combine",
    )(dest3, ys, rf, xmid, gf)


def _swiglu_acc(xn, wg_ref, wu_ref, wd_ref, acc_s):
    g = _dot(xn, wg_ref[0])
    u = _dot(xn, wu_ref[0])
    acc_s[...] += _dot((g * _sigmoid(g) * u).astype(BF16), wd_ref[0])


def _ffn_dense_kernel(final, xn_ref, wg_ref, wu_ref, wd_ref, xmid_ref, gf_ref, o_ref, acc_s):
    f = pl.program_id(1)

    @pl.when(f == 0)
    def _():
        acc_s[...] = jnp.zeros_like(acc_s)

    _swiglu_acc(xn_ref[...], wg_ref, wu_ref, wd_ref, acc_s)

    @pl.when(f == pl.num_programs(1) - 1)
    def _():
        y = xmid_ref[...] + acc_s[...]
        if final:
            y = y * lax.rsqrt(jnp.mean(y * y, axis=-1, keepdims=True) + EPS) * gf_ref[...]
        o_ref[...] = y


def _ffn_dense(xn, wg, wu, wd, xmid, gf, tm, final):
    n = xn.shape[0]
    tf = FFN_F_TILE
    row = lambda w: pl.BlockSpec((tm, w), lambda i, f: (i, 0))
    return pl.pallas_call(
        functools.partial(_ffn_dense_kernel, final),
        out_shape=jax.ShapeDtypeStruct((n, D_MODEL), F32),
        grid=(n // tm, D_FF // tf),
        in_specs=[row(D_MODEL),
                  pl.BlockSpec((1, D_MODEL, tf), lambda i, f: (0, 0, f)),
                  pl.BlockSpec((1, D_MODEL, tf), lambda i, f: (0, 0, f)),
                  pl.BlockSpec((1, tf, D_MODEL), lambda i, f: (0, f, 0)),
                  row(D_MODEL), pl.BlockSpec((1, D_MODEL), lambda i, f: (0, 0))],
        out_specs=row(D_MODEL),
        scratch_shapes=[pltpu.VMEM((tm, D_MODEL), F32)],
        compiler_params=_cparams(("parallel", "arbitrary")),
        name="ffn_dense",
    )(xn, wg, wu, wd, xmid, gf)


def _ffn_routed_kernel(te_ref, nu_ref, xs_ref, wg_ref, wu_ref, wd_ref, ys_ref, xb_s, acc_s):
    del te_ref
    tm = xb_s.shape[0]
    i = pl.program_id(0)
    f = pl.program_id(1)

    @pl.when(i < nu_ref[0])
    def _():
        @pl.when(f == 0)
        def _():
            acc_s[...] = jnp.zeros_like(acc_s)
            for s in range(D_MODEL // LANES):
                xb_s[:, s * LANES:(s + 1) * LANES] = xs_ref[pl.ds(s, tm, stride=ROW_TILE), :].astype(BF16)

        _swiglu_acc(xb_s[...], wg_ref, wu_ref, wd_ref, acc_s)

        @pl.when(f == pl.num_programs(1) - 1)
        def _():
            for s in range(D_MODEL // LANES):
                ys_ref[pl.ds(s, tm, stride=ROW_TILE), :] = acc_s[:, s * LANES:(s + 1) * LANES]

    @pl.when(jnp.logical_and(i >= nu_ref[0], f == 0))
    def _():
        ys_ref[...] = jnp.zeros_like(ys_ref)


def _ffn_routed(tile_expert, n_used, xs, wg, wu, wd):
    n_tiles = tile_expert.shape[0]
    tm, tf = MOE_TILE, FFN_F_TILE
    n_f = D_FF // tf
    last = lambda i, nu: jnp.minimum(i, nu[0] - 1)
    fcol = lambda i, f, nu: jnp.where(i < nu[0], f, n_f - 1)
    return pl.pallas_call(
        _ffn_routed_kernel,
        out_shape=jax.ShapeDtypeStruct(xs.shape, F32),
        grid_spec=pltpu.PrefetchScalarGridSpec(
            num_scalar_prefetch=2,
            grid=(n_tiles, n_f),
            in_specs=[pl.BlockSpec((tm * ROW_TILE, LANES), lambda i, f, te, nu: (last(i, nu), 0)),
                      pl.BlockSpec((1, D_MODEL, tf), lambda i, f, te, nu: (te[last(i, nu)], 0, fcol(i, f, nu))),
                      pl.BlockSpec((1, D_MODEL, tf), lambda i, f, te, nu: (te[last(i, nu)], 0, fcol(i, f, nu))),
                      pl.BlockSpec((1, tf, D_MODEL), lambda i, f, te, nu: (te[last(i, nu)], fcol(i, f, nu), 0))],
            out_specs=pl.BlockSpec((tm * ROW_TILE, LANES), lambda i, f, te, nu: (i, 0)),
            scratch_shapes=[pltpu.VMEM((tm, D_MODEL), BF16), pltpu.VMEM((tm, D_MODEL), F32)]),
        compiler_params=_cparams(("arbitrary", "arbitrary")),
        name="ffn_routed",
    )(tile_expert, n_used, xs, wg, wu, wd)


def _route_plan(ri_p, cnt_p, ri_s, cnt_s, n_tiles):
    cnt_p, cnt_s = cnt_p[0, :N_EXPERTS], cnt_s[0, :N_EXPERTS]
    tiles = (cnt_p + cnt_s + MOE_TILE - 1) // MOE_TILE
    ends = jnp.cumsum(tiles)
    start = (ends - tiles) * MOE_TILE
    lookup = lambda table, idx: jnp.sum(
        jnp.where(idx[..., None] == jnp.arange(N_EXPERTS, dtype=jnp.int32), table, 0), axis=-1)
    dest_p = lookup(start, ri_p[:, 0:2]) + ri_p[:, 2:4]
    dest_s = lookup(start + cnt_p, ri_s[:, 0:2]) + ri_s[:, 2:4]
    tile_expert = jnp.minimum(jnp.sum(jnp.arange(n_tiles, dtype=jnp.int32)[:, None] >= ends[None, :], axis=-1),
                              N_EXPERTS - 1).astype(jnp.int32)
    return dest_p, dest_s, tile_expert, ends[-1:].astype(jnp.int32)


def _pad_heads_cols(w):
    w = w.reshape(w.shape[0], C_HEADS, C_HEAD_DIM)
    return jnp.pad(w, ((0, 0), (0, 0), (0, C_HEAD_PAD - C_HEAD_DIM))).reshape(w.shape[0], C_WIDTH_PAD)


def _layer_params(l, w_in, b_gate, w_s, b_s, gn_c, w_out):
    w = w_in[l]
    gates = jnp.pad(w[:, OFF_CG:OFF_CG + 2 * C_HEADS], ((0, 0), (0, LANES - 2 * C_HEADS)))
    wp = jnp.concatenate(
        [w[:, :OFF_CQ]] + [_pad_heads_cols(w[:, o:o + C_WIDTH]) for o in (OFF_CQ, OFF_CK, OFF_CV, OFF_CO)]
        + [gates], axis=1).astype(BF16)
    bg = jnp.pad(b_gate[l], (0, LANES - 2 * C_HEADS))[None, :]
    pos = jnp.arange(GMLP_CHUNK)
    mask = (pos[None, :] // CHUNK) <= (pos[:, None] // CHUNK)
    wm = jnp.where(mask[None], w_s[l], 0.0)
    bs = jnp.repeat(b_s[l].T, A_GROUP_DIM, axis=1)
    wo = w_out[l]
    wc = wo[A_WIDTH + B_WIDTH:].reshape(C_HEADS, C_HEAD_DIM, D_MODEL)
    wc = jnp.pad(wc, ((0, 0), (0, C_HEAD_PAD - C_HEAD_DIM), (0, 0))).reshape(C_WIDTH_PAD, D_MODEL)
    gnc = _pad_heads_cols(gn_c[l][None, :])
    return dict(wp=wp, bg=bg, wm=wm, bs=bs, wa=wo[:A_WIDTH].astype(BF16),
                wb=wo[A_WIDTH:A_WIDTH + B_WIDTH].astype(BF16), wc=wc.astype(BF16), gnc=gnc)


def _pad_state(c, n, m):
    p = C_HEAD_PAD - C_HEAD_DIM
    c = jnp.pad(c, ((0, 0), (0, 0), (0, p), (0, p)))
    n = jnp.pad(n, ((0, 0), (0, 0), (0, p)))
    n = jnp.broadcast_to(n[:, :, :, None], n.shape + (LANES,))
    m = jnp.broadcast_to(m[:, :, None, None], m.shape + (1, LANES))
    return c, n, m


def _unpad_state(c, n, m):
    return c[:, :, :C_HEAD_DIM, :C_HEAD_DIM], n[:, :, :C_HEAD_DIM, 0], m[:, :, 0, 0]


def kernel(x_prompt, x_sample, cache_k_b, cache_v_b, state_c_mlstm, state_n_mlstm, state_m_mlstm,
           g_mix, w_in, b_gate, ln_a_g, ln_a_b, w_s, b_s, gn_b, gn_c, w_out,
           g_ffn, w_gate_d, w_up_d, w_down_d, w_router, w_gate_e, w_up_e, w_down_e, g_final):
    n_seq = x_prompt.shape[1]
    n_dec, n_new = x_sample.shape[0], x_sample.shape[1]
    past = cache_k_b.shape[2]
    n_samp = n_dec * n_new

    xp = x_prompt.reshape(n_seq, D_MODEL)
    xs = x_sample.reshape(n_samp, D_MODEL)
    gfin = g_final[None, :]

    outs = {k: [] for k in ("kbp", "vbp", "cp", "np", "mp", "kbs", "vbs", "cs", "ns", "ms", "vas")}
    for l in range(DEPTH):
        p = _layer_params(l, w_in, b_gate, w_s, b_s, gn_c, w_out)
        gmix = g_mix[l][None, :]
        lng, lnb = ln_a_g[l][None, :], ln_a_b[l][None, :]
        gnb = gn_b[l][None, :]
        gffn = g_ffn[l][None, :]
        moe = l % 2 == 1
        j = l // 2
        if moe:
            wg, wu, wd = w_gate_e[j].astype(BF16), w_up_e[j].astype(BF16), w_down_e[j].astype(BF16)
            wr = _split_bf16(jnp.pad(w_router[j], ((0, 0), (0, LANES - N_EXPERTS))))
        else:
            wg, wu, wd = (w_gate_d[j][None].astype(BF16), w_up_d[j][None].astype(BF16),
                          w_down_d[j][None].astype(BF16))
            wr = None
        final = l == DEPTH - 1

        wm_s = jnp.kron(jnp.eye(n_dec, dtype=F32), p["wm"][:, :n_new, :n_new])
        bs_s = jnp.tile(p["bs"][:n_new], (n_dec, 1))

        (ya, qb, qc, kc, kct, vc, oc, gt, gtt, kbt, vbt) = _proj(
            xp, gmix, p["wp"], p["bg"], lng, lnb, p["wm"].astype(BF16), p["bs"], ROW_BLOCK, GMLP_CHUNK, False)
        yb = _sb_prompt(qb, kbt, vbt, gnb)
        c0, n0, m0 = _pad_state(jnp.zeros((1, C_HEADS, C_HEAD_DIM, C_HEAD_DIM), F32),
                                jnp.zeros((1, C_HEADS, C_HEAD_DIM), F32), jnp.zeros((1, C_HEADS), F32))
        yc, c_f, n_f, m_f = _mlstm(qc[None], kc[None], kct[None], vc[None], oc[None], gt[None], gtt[None],
                                   p["gnc"], c0, n0, m0, MLSTM_SUB_BLOCKS)
        merged_p = _merge(xp, ya, yb, yc[0], p["wa"], p["wb"], p["wc"], gffn, wr, ROW_BLOCK)
        c_f, n_f, m_f = _unpad_state(c_f, n_f, m_f)
        heads_last = lambda a: jnp.transpose(a.reshape(B_HEADS, B_HEAD_DIM, n_seq), (2, 0, 1))[None]
        outs["kbp"].append(heads_last(kbt))
        outs["vbp"].append(heads_last(vbt))
        outs["cp"].append(c_f)
        outs["np"].append(n_f)
        outs["mp"].append(m_f)

        (ya, qb, qc, kc, kct, vc, oc, gt, gtt, kbf, vbf, kbh, vbh, va) = _proj(
            xs, gmix, p["wp"], p["bg"], lng, lnb, wm_s.astype(BF16), bs_s, n_samp, n_samp, True)
        r3 = lambda a: a.reshape(n_dec, n_new, a.shape[-1])
        keys_last = lambda a: jnp.transpose(a, (0, 2, 3, 1)).reshape(n_dec, B_WIDTH, past)
        yb = _sb_sample(r3(qb), r3(kbh), r3(vbh), keys_last(cache_k_b[l]), keys_last(cache_v_b[l]), gnb)
        n_pad = MLSTM_BLOCK - n_new
        padr = lambda a: jnp.pad(r3(a), ((0, 0), (0, n_pad), (0, 0)))
        per_stream = lambda a: jnp.transpose(a.reshape(a.shape[0], n_dec, n_new), (1, 0, 2))
        kct_s = jnp.pad(per_stream(kct), ((0, 0), (0, 0), (0, n_pad)))
        gate_row = jnp.arange(GATE_ROWS)[None, :, None]
        gtt_pad = jnp.broadcast_to(jnp.where(gate_row < C_HEADS, NEG_BIG, 0.0).astype(F32),
                                   (n_dec, GATE_ROWS, n_pad))
        gtt_s = jnp.concatenate([per_stream(gtt), gtt_pad], axis=2)
        gt_s = jnp.concatenate([r3(gt), jnp.broadcast_to(
            jnp.where(jnp.arange(LANES) < C_HEADS, NEG_BIG, 0.0).astype(F32), (n_dec, n_pad, LANES))], axis=1)
        c0, n0, m0 = _pad_state(state_c_mlstm[l], state_n_mlstm[l], state_m_mlstm[l])
        yc, c_u, n_u, m_u = _mlstm(padr(qc), padr(kc), kct_s, padr(vc), padr(oc), gt_s, gtt_s, p["gnc"],
                                   c0, n0, m0, 1)
        yc = yc[:, :n_new].reshape(n_samp, C_WIDTH_PAD)
        merged_s = _merge(xs, ya, yb.reshape(n_samp, B_WIDTH), yc, p["wa"], p["wb"], p["wc"],
                          gffn, wr, n_samp)
        c_u, n_u, m_u = _unpad_state(c_u, n_u, m_u)

        if moe:
            xmid_p, xrow_p, ri_p, rf_p, cnt_p = merged_p
            xmid_s, xrow_s, ri_s, rf_s, cnt_s = merged_s
            n_tiles = 2 * (n_seq + n_samp) // MOE_TILE + N_EXPERTS
            dest_p, dest_s, tile_expert, n_used = _route_plan(ri_p, cnt_p, ri_s, cnt_s, n_tiles)
            as_tiles = lambda a: a.reshape(a.shape[0] // ROW_TILE, ROW_TILE, LANES)
            xsort = jnp.zeros((n_tiles * MOE_TILE, ROW_TILE, LANES), F32)
            xsort = _dispatch(dest_p, xrow_p, xsort, ROW_BLOCK)
            xsort = _dispatch(dest_s, xrow_s, xsort, n_samp)
            ysort = _ffn_routed(tile_expert, n_used, xsort.reshape(-1, LANES), wg, wu, wd)
            xp = _combine(dest_p, as_tiles(ysort), rf_p, xmid_p, gfin, MOE_COMBINE_BLOCK, final)
            xs = _combine(dest_s, as_tiles(ysort), rf_s, xmid_s, gfin, n_samp, final)
        else:
            xp = _ffn_dense(merged_p[1], wg, wu, wd, merged_p[0], gfin, ROW_BLOCK, final)
            xs = _ffn_dense(merged_s[1], wg, wu, wd, merged_s[0], gfin, n_samp, final)

        outs["kbs"].append(kbf.reshape(n_dec, n_new, B_HEADS, B_HEAD_DIM))
        outs["vbs"].append(vbf.reshape(n_dec, n_new, B_HEADS, B_HEAD_DIM))
        outs["cs"].append(c_u)
        outs["ns"].append(n_u)
        outs["ms"].append(m_u)
        outs["vas"].append(va.reshape(n_dec, n_new, A_WIDTH))

    st = lambda k: jnp.stack(outs[k])
    return (xp.reshape(1, n_seq, D_MODEL), xs.reshape(n_dec, n_new, D_MODEL),
            st("kbp"), st("vbp"), st("cp"), st("np"), st("mp"),
            st("kbs"), st("vbs"), st("cs"), st("ns"), st("ms"), st("vas"))
```

```python
import functools
import math

import jax
import jax.numpy as jnp
from jax import lax
from jax.experimental import pallas as pl
from jax.experimental.pallas import tpu as pltpu

F32 = jnp.float32
BF16 = jnp.bfloat16

D_MODEL = 1024
DEPTH = 2
EPS = 1e-6
CHUNK = 64
A_WIDTH = 256
A_GROUPS = 4
A_GROUP_DIM = 64
GMLP_CHUNK = 128
B_HEAD_DIM = 64
B_WIDTH = 384
B_HEADS = 6
B_PAIRS = 3
C_HEADS = 4
C_HEAD_DIM = 96
C_WIDTH = 384
D_FF = 2816
N_EXPERTS = 8

LANES = 128
C_HEAD_PAD = LANES
C_WIDTH_PAD = C_HEADS * C_HEAD_PAD

OFF_AU, OFF_AV, OFF_BQ, OFF_BK, OFF_BV = 0, 256, 512, 896, 1280
OFF_CQ, OFF_CK, OFF_CV, OFF_CO, OFF_CG = 1664, 2048, 2432, 2816, 3200
P_A = 0
P_BQ = 512
P_BK = P_BQ + B_WIDTH
P_BV = P_BK + B_WIDTH
P_CQ = P_BV + B_WIDTH
P_CK = P_CQ + C_WIDTH_PAD
P_CV = P_CK + C_WIDTH_PAD
P_CO = P_CV + C_WIDTH_PAD
P_CG = P_CO + C_WIDTH_PAD
P_DIM = P_CG + LANES

SB_DEAD_LOG_WEIGHT = -110.0
SB_BLOCK = 256
SB_BLOCKS_PER_STEP = 4
MLSTM_BLOCK = 128
MLSTM_SUB_BLOCKS = 4
GATE_ROWS = 2 * C_HEADS
NEG_BIG = -1e30
ROW_TILE = 8
MOE_TILE = 512
FFN_F_TILE = 1408
ROW_BLOCK = 512
MOE_COMBINE_BLOCK = 256

VMEM_LIMIT = 56 * 1024 * 1024


def _cparams(sem):
    return pltpu.CompilerParams(dimension_semantics=sem, vmem_limit_bytes=VMEM_LIMIT)


def _gelu(x):
    return 0.5 * x * (1.0 + lax.erf(x * (1.0 / math.sqrt(2.0))))


def _log_sigmoid(x):
    return jnp.minimum(x, 0.0) - jnp.log(1.0 + jnp.exp(-jnp.abs(x)))


def _sigmoid(x):
    return 1.0 / (1.0 + jnp.exp(-x))


def _split_bf16(x):
    hi = x.astype(BF16)
    lo = (x - hi.astype(F32)).astype(BF16)
    return hi, lo


def _dot(a, b):
    return jnp.dot(a, b, preferred_element_type=F32)


def _dot_nt(a, b):
    return lax.dot_general(a, b, (((1,), (1,)), ((), ())), preferred_element_type=F32)


def _dot_tn(a, b):
    return lax.dot_general(a, b, (((0,), (0,)), ((), ())), preferred_element_type=F32)


def _proj_kernel(n_chunks, chunk, sample, x_ref, gmix_ref, w_ref, bg_ref, lng_ref, lnb_ref, ws_ref, bs_ref,
                 ya_ref, qb_ref, qc_ref, kc_ref, kct_ref, vc_ref, oc_ref, gt_ref, gtt_ref, *kv_refs):
    x = x_ref[...]
    xn = (x * lax.rsqrt(jnp.mean(x * x, axis=-1, keepdims=True) + EPS) * gmix_ref[...]).astype(BF16)

    def proj(off, width):
        return _dot(xn, w_ref[:, off:off + width])

    za = proj(P_A, 2 * A_WIDTH)
    u = _gelu(za[:, :A_WIDTH])
    gv = _gelu(za[:, A_WIDTH:])
    xc = gv - jnp.mean(gv, axis=-1, keepdims=True)
    va = xc * lax.rsqrt(jnp.mean(xc * xc, axis=-1, keepdims=True) + EPS) * lng_ref[...] + lnb_ref[...]
    if sample:
        kv_refs[4][...] = va
    vab = va.astype(BF16)
    lane_group = lax.broadcasted_iota(jnp.int32, (chunk, A_WIDTH), 1) // A_GROUP_DIM
    for c in range(n_chunks):
        rows = slice(c * chunk, (c + 1) * chunk)
        vch = vab[rows]
        s = jnp.zeros((chunk, A_WIDTH), F32)
        for g in range(A_GROUPS):
            s = jnp.where(lane_group == g, _dot(ws_ref[g], vch), s)
        ya_ref[rows, :] = (u[rows] * (s + bs_ref[...])).astype(BF16)

    qb_ref[...] = (proj(P_BQ, B_WIDTH) * (1.0 / math.sqrt(B_HEAD_DIM))).astype(BF16)
    zk = proj(P_BK, B_WIDTH)
    zv = proj(P_BV, B_WIDTH)
    if sample:
        kv_refs[0][...] = zk
        kv_refs[1][...] = zv
        kv_refs[2][...] = zk.astype(BF16)
        kv_refs[3][...] = zv.astype(BF16)
    else:
        kv_refs[0][...] = zk.T
        kv_refs[1][...] = zv.T

    qc_ref[...] = proj(P_CQ, C_WIDTH_PAD).astype(BF16)
    zk = proj(P_CK, C_WIDTH_PAD) * (C_HEAD_DIM ** -0.5)
    kc_ref[...] = zk.astype(BF16)
    kct_ref[...] = zk.T.astype(BF16)
    vc_ref[...] = proj(P_CV, C_WIDTH_PAD).astype(BF16)
    oc_ref[...] = proj(P_CO, C_WIDTH_PAD)
    g = proj(P_CG, LANES) + bg_ref[...]
    lane = lax.broadcasted_iota(jnp.int32, g.shape, 1)
    gates = jnp.where(lane < C_HEADS, g, _log_sigmoid(g))
    gt_ref[...] = gates
    gtt_ref[...] = gates.T[:GATE_ROWS, :]


def _proj(x, gmix, wp, bg, lng, lnb, ws, bs, tm, chunk, sample):
    n = x.shape[0]
    row = lambda w: pl.BlockSpec((tm, w), lambda i: (i, 0))
    full = lambda a: pl.BlockSpec(a.shape, lambda i: (0,) * a.ndim)
    col = lambda h: pl.BlockSpec((h, tm), lambda i: (0, i))
    outs = [((n, A_WIDTH), BF16, row(A_WIDTH)), ((n, B_WIDTH), BF16, row(B_WIDTH)),
            ((n, C_WIDTH_PAD), BF16, row(C_WIDTH_PAD)), ((n, C_WIDTH_PAD), BF16, row(C_WIDTH_PAD)),
            ((C_WIDTH_PAD, n), BF16, col(C_WIDTH_PAD)), ((n, C_WIDTH_PAD), BF16, row(C_WIDTH_PAD)),
            ((n, C_WIDTH_PAD), F32, row(C_WIDTH_PAD)), ((n, LANES), F32, row(LANES)),
            ((GATE_ROWS, n), F32, col(GATE_ROWS))]
    out_shape = [jax.ShapeDtypeStruct(s, dt) for s, dt, _ in outs]
    out_specs = [spec for _, _, spec in outs]
    if sample:
        extra = [(B_WIDTH, F32), (B_WIDTH, F32), (B_WIDTH, BF16), (B_WIDTH, BF16), (A_WIDTH, F32)]
        out_shape += [jax.ShapeDtypeStruct((n, w), dt) for w, dt in extra]
        out_specs += [row(w) for w, _ in extra]
    else:
        out_shape += [jax.ShapeDtypeStruct((B_WIDTH, n), F32)] * 2
        out_specs += [pl.BlockSpec((B_WIDTH, tm), lambda i: (0, i))] * 2
    return pl.pallas_call(
        functools.partial(_proj_kernel, tm // chunk, chunk, sample),
        out_shape=out_shape,
        grid=(n // tm,),
        in_specs=[row(D_MODEL), full(gmix), full(wp), full(bg), full(lng), full(lnb), full(ws), full(bs)],
        out_specs=out_specs,
        compiler_params=_cparams(("parallel",)),
        name="proj",
    )(x, gmix, wp, bg, lng, lnb, ws, bs)


def _sb_step(qh, kblk, vblk, carry, acc, tri, mask, transposed):
    z = _dot(qh, kblk) if transposed else _dot_nt(qh, kblk)
    tk = z.shape[1]
    drop = jnp.maximum(z, 0.0) + jnp.log(1.0 + jnp.exp(-jnp.abs(z)))
    if mask is not None:
        drop = jnp.where(mask, drop, 0.0)
    hi, lo = _split_bf16(drop)
    cs = _dot(hi, tri) + _dot(lo, tri)
    if tk >= LANES:
        carry_b = jnp.concatenate([carry] * (tk // LANES), axis=1)
    else:
        carry_b = carry[:, :tk]
    a = jnp.exp(z - cs - carry_b)
    if mask is not None:
        a = jnp.where(mask, a, 0.0)
    a = a.astype(BF16)
    acc = acc + (_dot_nt(a, vblk) if transposed else _dot(a, vblk))
    return carry + jnp.broadcast_to(cs[:, :1], carry.shape), acc


def _tri(tk):
    j = lax.broadcasted_iota(jnp.int32, (tk, tk), 0)
    s = lax.broadcasted_iota(jnp.int32, (tk, tk), 1)
    return (j >= s).astype(BF16)


def _sb_finish(acc_s, gn_ref, o_ref, head0):
    out = jnp.where(head0, acc_s[0], acc_s[1])
    r = lax.broadcasted_iota(jnp.int32, (LANES, LANES), 0) // B_HEAD_DIM
    c = lax.broadcasted_iota(jnp.int32, (LANES, LANES), 1) // B_HEAD_DIM
    same_head = (r == c).astype(BF16)
    hi, lo = _split_bf16(out * out)
    ms = (_dot(hi, same_head) + _dot(lo, same_head)) * (1.0 / B_HEAD_DIM)
    return (out * lax.rsqrt(ms + EPS) * gn_ref[...]).astype(o_ref.dtype)


def _sb_walk(qh, load_kv, first_block, carry_s, acc_s, tri):
    def alive():
        return jnp.minimum(jnp.min(carry_s[0]), jnp.min(carry_s[1])) <= -SB_DEAD_LOG_WEIGHT

    def cond(st):
        j, live = st
        return jnp.logical_and(j >= 0, live)

    def body(st):
        j, _ = st
        kblk, vblk = load_kv(j)
        for h in range(2):
            carry, acc = _sb_step(qh[h], kblk, vblk, carry_s[h], acc_s[h], tri, None, True)
            carry_s[h] = carry
            acc_s[h] = acc
        return j - 1, alive()

    lax.while_loop(cond, body, (first_block, alive()))


def _sb_prompt_kernel(q_ref, k_ref, v_ref, gn_ref, o_ref, carry_s, acc_s):
    tq = SB_BLOCK
    n_q = q_ref.shape[0] // tq
    head0 = lax.broadcasted_iota(jnp.int32, (tq, LANES), 1) < B_HEAD_DIM
    tri = _tri(tq)
    t = lax.broadcasted_iota(jnp.int32, (tq, tq), 0)
    s = lax.broadcasted_iota(jnp.int32, (tq, tq), 1)
    causal = s < t
    zeros = jnp.zeros((tq, LANES), F32)

    def load_kv(j):
        cols = pl.ds(pl.multiple_of(j * tq, tq), tq)
        return k_ref[:, cols].astype(BF16), v_ref[:, cols].astype(BF16)

    first = pl.program_id(1) * n_q
    kv = [load_kv(jnp.maximum(first + b - 1, 0)) for b in range(n_q + 1)]
    qhs = []
    for b in range(n_q):
        q = q_ref[b * tq:(b + 1) * tq, :]
        qh = [jnp.where(head0, q, 0), jnp.where(head0, 0, q)]
        qhs.append(qh)
        has_prev = t >= jnp.where(first + b >= 1, 0, tq)
        for h in range(2):
            carry, acc = _sb_step(qh[h], kv[b + 1][0], kv[b + 1][1], zeros, zeros, tri, causal, True)
            carry, acc = _sb_step(qh[h], kv[b][0], kv[b][1], carry, acc, tri, has_prev, True)
            carry_s[b, h] = carry
            acc_s[b, h] = acc
    for b in range(n_q):
        _sb_walk(qhs[b], load_kv, first + b - 2, carry_s.at[b], acc_s.at[b], tri)
        o_ref[b * tq:(b + 1) * tq, :] = _sb_finish(acc_s.at[b], gn_ref, o_ref, head0)


def _sb_prompt(q, k, v, gn):
    n = q.shape[0]
    n_q = SB_BLOCKS_PER_STEP
    tq = SB_BLOCK * n_q
    blk = pl.BlockSpec((tq, LANES), lambda p, i: (i, p))
    seq = pl.BlockSpec((LANES, n), lambda p, i: (p, 0))
    state = pltpu.VMEM((n_q, 2, SB_BLOCK, LANES), F32)
    return pl.pallas_call(
        _sb_prompt_kernel,
        out_shape=jax.ShapeDtypeStruct((n, B_WIDTH), BF16),
        grid=(B_PAIRS, n // tq),
        in_specs=[blk, seq, seq, pl.BlockSpec((1, LANES), lambda p, i: (0, p))],
        out_specs=blk,
        scratch_shapes=[state, state],
        compiler_params=_cparams(("parallel", "parallel")),
        name="sb_prompt",
    )(q, k, v, gn)


def _sb_sample_kernel(q_ref, kn_ref, vn_ref, kc_ref, vc_ref, gn_ref, o_ref, carry_s, acc_s):
    tq = q_ref.shape[1]
    tk = SB_BLOCK
    head0 = lax.broadcasted_iota(jnp.int32, (tq, LANES), 1) < B_HEAD_DIM
    q = q_ref[0]
    zero = jnp.zeros_like(q)
    qh = [jnp.where(head0, q, zero), jnp.where(head0, zero, q)]
    t = lax.broadcasted_iota(jnp.int32, (tq, tq), 0)
    s = lax.broadcasted_iota(jnp.int32, (tq, tq), 1)
    causal = s < t
    zeros = jnp.zeros((tq, LANES), F32)
    def load_kv(j):
        cols = pl.ds(pl.multiple_of(j * tk, tk), tk)
        return kc_ref[0, :, cols].astype(BF16), vc_ref[0, :, cols].astype(BF16)

    last = kc_ref.shape[2] // tk - 1
    kp, vp = load_kv(last)
    tri = _tri(tk)
    for h in range(2):
        carry, acc = _sb_step(qh[h], kn_ref[0], vn_ref[0], zeros, zeros, _tri(tq), causal, False)
        carry, acc = _sb_step(qh[h], kp, vp, carry, acc, tri, None, True)
        carry_s[h] = carry
        acc_s[h] = acc
    _sb_walk(qh, load_kv, last - 1, carry_s, acc_s, tri)
    o_ref[0] = _sb_finish(acc_s, gn_ref, o_ref, head0)


def _sb_sample(q, kn, vn, kc, vc, gn, layer):
    nb, tq, _ = q.shape
    past = kc.shape[3]
    new = pl.BlockSpec((1, tq, LANES), lambda b, p: (b, 0, p))
    old = pl.BlockSpec((None, 1, LANES, past), lambda b, p: (layer, b, p, 0))
    return pl.pallas_call(
        _sb_sample_kernel,
        out_shape=jax.ShapeDtypeStruct((nb, tq, B_WIDTH), BF16),
        grid=(nb, B_PAIRS),
        in_specs=[new, new, new, old, old, pl.BlockSpec((1, LANES), lambda b, p: (0, p))],
        out_specs=new,
        scratch_shapes=[pltpu.VMEM((2, tq, LANES), F32), pltpu.VMEM((2, tq, LANES), F32)],
        compiler_params=_cparams(("parallel", "parallel")),
        name="sb_sample",
    )(q, kn, vn, kc, vc, gn)


def _split3(x):
    h1 = x.astype(BF16)
    r1 = x - h1.astype(F32)
    h2 = r1.astype(BF16)
    return h1, h2, (r1 - h2.astype(F32)).astype(BF16)


def _dot3(x, rhs01):
    return _dot(jnp.concatenate(_split3(x), axis=1), jnp.concatenate([rhs01] * 3, axis=0))


def _dot3_left(lhs01, x):
    return _dot(jnp.concatenate([lhs01] * 3, axis=1), jnp.concatenate(_split3(x), axis=0))


def _mlstm_kernel(n_sub, q_ref, k_ref, kt_ref, v_ref, o_ref, gt_ref, gtt_ref, gn_ref, c0_ref, n0_ref, m0_ref,
                  yc_ref, c_out, n_out, m_out, c_s, n_s, m_s):
    L = MLSTM_BLOCK
    t_blk = pl.program_id(1)

    @pl.when(t_blk == 0)
    def _():
        c_s[...] = c0_ref[0]
        n_s[...] = n0_ref[0]
        m_s[...] = m0_ref[0]

    r = lax.broadcasted_iota(jnp.int32, (L, L), 0)
    c = lax.broadcasted_iota(jnp.int32, (L, L), 1)
    causal = c <= r
    upper = (r <= c).astype(BF16)
    ones_sq = jnp.ones((L, LANES), BF16)
    sel_r = lax.broadcasted_iota(jnp.int32, (LANES, C_WIDTH_PAD), 0)
    sel_c = lax.broadcasted_iota(jnp.int32, (LANES, C_WIDTH_PAD), 1) // C_HEAD_PAD
    sel_p = (sel_r == sel_c).astype(BF16)
    sel_b = (sel_r == sel_c + C_HEADS).astype(BF16)
    lane = c

    state = [(c_s[h], n_s[h], m_s[h]) for h in range(C_HEADS)]
    for sub in range(n_sub):
        rows = slice(sub * L, (sub + 1) * L)
        gtt = gtt_ref[0, :, rows]
        bct = _dot3(gtt, upper)
        a_rows = gtt[:C_HEADS] - bct[C_HEADS:]
        gt = gt_ref[0, rows, :]
        bc = _dot3_left(causal.astype(BF16), gt)
        pmax = gt - pltpu.roll(bc, LANES - C_HEADS, axis=1)
        for sh in (1, 2, 4, 8, 16, 32, 64):
            pmax = jnp.maximum(pmax, jnp.where(r >= sh, pltpu.roll(pmax, sh, axis=0), -jnp.inf))
        pb_cols = jnp.where(lane < C_HEADS, pmax, bc)
        p_all = _dot3(pb_cols, sel_p)
        b_all = _dot3(pb_cols, sel_b)
        for h in range(C_HEADS):
            lanes = slice(h * C_HEAD_PAD, (h + 1) * C_HEAD_PAD)
            q = q_ref[0, rows, lanes]
            v1 = jnp.concatenate([v_ref[0, rows, lanes], ones_sq], axis=1)
            p_rep = p_all[:, lanes]
            b_rep = b_all[:, lanes]
            a_row = a_rows[h:h + 1, :]
            p_last = p_rep[L - 1:L, :]
            c_prev, n_prev, m_prev = state[h]

            w = jnp.exp(jnp.where(causal, a_row - p_rep, -jnp.inf)) * _dot_nt(q, k_ref[0, rows, lanes])
            here_sums = _dot(w.astype(BF16), v1)
            past_sums = _dot(q, jnp.concatenate([c_prev, n_prev], axis=1).astype(BF16))
            top = jnp.maximum(m_prev, p_rep)
            past = jnp.exp(m_prev - top)
            here = jnp.exp(p_rep - top)
            mix = jnp.concatenate([past, past], axis=1) * past_sums + jnp.concatenate([here, here], axis=1) * here_sums
            den = jnp.maximum(jnp.abs(mix[:, LANES:]), jnp.exp(-(b_rep + top)))
            hh = mix[:, :LANES] / den

            kwt = (kt_ref[0, lanes, rows].astype(F32) * jnp.exp(a_row - p_last)).astype(BF16)
            fresh = _dot(kwt, v1)
            top_last = jnp.maximum(m_prev, p_last)
            decay = jnp.exp(m_prev - top_last)
            gain = jnp.exp(p_last - top_last)
            state[h] = (decay * c_prev + gain * fresh[:, :LANES], decay * n_prev + gain * fresh[:, LANES:],
                        b_rep[L - 1:L, :] + top_last)

            ms = _dot(jnp.concatenate(_split_bf16(hh * hh), axis=1),
                      jnp.concatenate([ones_sq, ones_sq], axis=0)) * (1.0 / C_HEAD_DIM)
            hn = hh * lax.rsqrt(ms + EPS) * gn_ref[:, lanes]
            yc_ref[0, rows, lanes] = (hn * _sigmoid(o_ref[0, rows, lanes])).astype(BF16)

    for h in range(C_HEADS):
        c_s[h], n_s[h], m_s[h] = state[h]

    @pl.when(t_blk == pl.num_programs(1) - 1)
    def _():
        c_out[0] = c_s[...]
        n_out[0] = n_s[...]
        m_out[0] = m_s[...]


def _mlstm(q, k, kt, v, o, gt, gtt, gn, c0, n0, m0, n_sub):
    nb, n, _ = q.shape
    tb = n_sub * MLSTM_BLOCK
    seq = lambda w: pl.BlockSpec((1, tb, w), lambda b, t: (b, t, 0))
    seq_t = lambda h: pl.BlockSpec((1, h, tb), lambda b, t: (b, 0, t))
    st = lambda a: pl.BlockSpec((1,) + a.shape[1:], lambda b, t: (b,) + (0,) * (a.ndim - 1))
    return pl.pallas_call(
        functools.partial(_mlstm_kernel, n_sub),
        out_shape=[jax.ShapeDtypeStruct((nb, n, C_WIDTH_PAD), BF16),
                   jax.ShapeDtypeStruct(c0.shape, F32),
                   jax.ShapeDtypeStruct(n0.shape, F32),
                   jax.ShapeDtypeStruct(m0.shape, F32)],
        grid=(nb, n // tb),
        in_specs=[seq(C_WIDTH_PAD), seq(C_WIDTH_PAD), seq_t(C_WIDTH_PAD), seq(C_WIDTH_PAD), seq(C_WIDTH_PAD),
                  seq(LANES), seq_t(GATE_ROWS), pl.BlockSpec((1, C_WIDTH_PAD), lambda b, t: (0, 0)),
                  st(c0), st(n0), st(m0)],
        out_specs=[seq(C_WIDTH_PAD), st(c0), st(n0), st(m0)],
        scratch_shapes=[pltpu.VMEM(c0.shape[1:], F32), pltpu.VMEM(n0.shape[1:], F32),
                        pltpu.VMEM(m0.shape[1:], F32)],
        compiler_params=_cparams(("parallel", "arbitrary")),
        name="mlstm",
    )(q, k, kt, v, o, gt, gtt, gn, c0, n0, m0)


def _mixer_out(x_ref, ya_ref, yb_ref, yc_ref, wa_ref, wb_ref, wc_ref, g_ref, xmid_ref):
    y = _dot(ya_ref[...], wa_ref[...]) + _dot(yb_ref[...], wb_ref[...]) + _dot(yc_ref[...], wc_ref[...])
    x = x_ref[...] + y
    xmid_ref[...] = x
    return x * lax.rsqrt(jnp.mean(x * x, axis=-1, keepdims=True) + EPS) * g_ref[...]


def _merge_dense_kernel(x_ref, ya_ref, yb_ref, yc_ref, wa_ref, wb_ref, wc_ref, g_ref, xmid_ref, xn_ref):
    xn_ref[...] = _mixer_out(x_ref, ya_ref, yb_ref, yc_ref, wa_ref, wb_ref, wc_ref, g_ref,
                             xmid_ref).astype(BF16)


def _merge_moe_kernel(x_ref, ya_ref, yb_ref, yc_ref, wa_ref, wb_ref, wc_ref, g_ref, wrh_ref, wrl_ref,
                      earlier_ref, xmid_ref, xrow_ref, ri_ref, rf_ref, cnt_ref, run_s):
    tm = x_ref.shape[0]

    @pl.when(pl.program_id(0) == 0)
    def _():
        run_s[...] = jnp.zeros_like(run_s)

    xn = _mixer_out(x_ref, ya_ref, yb_ref, yc_ref, wa_ref, wb_ref, wc_ref, g_ref, xmid_ref)
    for s in range(D_MODEL // LANES):
        xrow_ref[pl.ds(s, tm, stride=ROW_TILE), :] = xn[:, s * LANES:(s + 1) * LANES]

    hi, lo = _split_bf16(xn)
    logits = _dot(hi, wrh_ref[...]) + _dot(hi, wrl_ref[...]) + _dot(lo, wrh_ref[...])
    lane = lax.broadcasted_iota(jnp.int32, logits.shape, 1)
    lg = jnp.where(lane < N_EXPERTS, logits, -jnp.inf)
    m1 = jnp.max(lg, axis=-1, keepdims=True)
    i1 = jnp.min(jnp.where(lg == m1, lane, LANES), axis=-1, keepdims=True)
    lg2 = jnp.where(lane == i1, -jnp.inf, lg)
    m2 = jnp.max(lg2, axis=-1, keepdims=True)
    i2 = jnp.min(jnp.where(lg2 == m2, lane, LANES), axis=-1, keepdims=True)
    e2 = jnp.exp(m2 - m1)
    g1 = 1.0 / (1.0 + e2)
    g2 = e2 * g1

    sel1 = lane == i1
    sel2 = lane == i2
    onehot = jnp.logical_or(sel1, sel2)
    before = _dot(earlier_ref[...], onehot.astype(BF16)) + run_s[...]
    rank1 = jnp.sum(jnp.where(sel1, before, 0.0), axis=-1, keepdims=True).astype(jnp.int32)
    rank2 = jnp.sum(jnp.where(sel2, before, 0.0), axis=-1, keepdims=True).astype(jnp.int32)
    run_s[...] += jnp.sum(onehot.astype(F32), axis=0, keepdims=True)
    cnt_ref[...] = run_s[...].astype(jnp.int32)
    ri_ref[...] = jnp.where(lane == 0, i1, jnp.where(lane == 1, i2,
                            jnp.where(lane == 2, rank1, jnp.where(lane == 3, rank2, 0))))
    rf_ref[...] = jnp.where(lane == 0, g1, jnp.where(lane == 1, g2, 0.0))


def _merge(x, ya, yb, yc, wa, wb, wc, g, wr, tm):
    n = x.shape[0]
    row = lambda w: pl.BlockSpec((tm, w), lambda i: (i, 0))
    full = lambda a: pl.BlockSpec(a.shape, lambda i: (0,) * a.ndim)
    ins = [x, ya, yb, yc, wa, wb, wc, g]
    in_specs = [row(D_MODEL), row(A_WIDTH), row(B_WIDTH), row(C_WIDTH_PAD), full(wa), full(wb), full(wc), full(g)]
    if wr is None:
        return pl.pallas_call(
            _merge_dense_kernel,
            out_shape=[jax.ShapeDtypeStruct((n, D_MODEL), F32), jax.ShapeDtypeStruct((n, D_MODEL), BF16)],
            grid=(n // tm,), in_specs=in_specs, out_specs=[row(D_MODEL), row(D_MODEL)],
            compiler_params=_cparams(("parallel",)), name="merge_dense",
        )(*ins)
    wrh, wrl = wr
    earlier = (jnp.arange(tm)[None, :] < jnp.arange(tm)[:, None]).astype(BF16)
    return pl.pallas_call(
        _merge_moe_kernel,
        out_shape=[jax.ShapeDtypeStruct((n, D_MODEL), F32),
                   jax.ShapeDtypeStruct((n * ROW_TILE, LANES), F32),
                   jax.ShapeDtypeStruct((n, LANES), jnp.int32),
                   jax.ShapeDtypeStruct((n, LANES), F32),
                   jax.ShapeDtypeStruct((1, LANES), jnp.int32)],
        grid=(n // tm,), in_specs=in_specs + [full(wrh), full(wrl), full(earlier)],
        out_specs=[row(D_MODEL), pl.BlockSpec((tm * ROW_TILE, LANES), lambda i: (i, 0)),
                   row(LANES), row(LANES), pl.BlockSpec((1, LANES), lambda i: (0, 0))],
        scratch_shapes=[pltpu.VMEM((1, LANES), F32)],
        compiler_params=_cparams(("arbitrary",)), name="merge_moe",
    )(*ins, wrh, wrl, earlier)


def _dispatch_kernel(dest_ref, src_ref, xs_in_ref, xs_ref, sem):
    del xs_in_ref
    tt = dest_ref.shape[2] // 2

    def issue(t, carry):
        rows = pl.ds(pl.multiple_of(t * ROW_TILE, ROW_TILE), ROW_TILE)
        for k in range(2):
            pltpu.make_async_copy(src_ref.at[rows], xs_ref.at[dest_ref[0, 0, 2 * t + k]], sem).start(priority=k)
        return carry

    lax.fori_loop(0, tt, issue, 0, unroll=8)

    def drain(t, carry):
        for k in range(2):
            pltpu.make_async_copy(src_ref.at[pl.ds(0, ROW_TILE)], xs_ref.at[0], sem).wait()
        return carry

    lax.fori_loop(0, tt, drain, 0, unroll=8)


def _dispatch(dest, src, xs, tt):
    n = src.shape[0] // ROW_TILE
    dest3 = dest.reshape(n // tt, 1, 2 * tt)
    return pl.pallas_call(
        _dispatch_kernel,
        out_shape=jax.ShapeDtypeStruct(xs.shape, xs.dtype),
        grid=(n // tt,),
        in_specs=[pl.BlockSpec((1, 1, 2 * tt), lambda i: (i, 0, 0), memory_space=pltpu.SMEM),
                  pl.BlockSpec((tt * ROW_TILE, LANES), lambda i: (i, 0)), pl.BlockSpec(memory_space=pl.ANY)],
        out_specs=pl.BlockSpec(memory_space=pl.ANY),
        scratch_shapes=[pltpu.SemaphoreType.DMA(())],
        input_output_aliases={2: 0},
        compiler_params=_cparams(("arbitrary",)), name="moe_dispatch",
    )(dest3, src, xs)


def _combine_kernel(final, dest_ref, ys_ref, rf_ref, xmid_ref, gf_ref, o_ref, buf_s, sem):
    tt = xmid_ref.shape[0]

    def issue(t, carry):
        for k in range(2):
            rows = pl.ds(pl.multiple_of(t * ROW_TILE, ROW_TILE), ROW_TILE)
            pltpu.make_async_copy(ys_ref.at[dest_ref[0, 0, 2 * t + k]], buf_s.at[k, rows], sem).start(priority=k)
        return carry

    lax.fori_loop(0, tt, issue, 0, unroll=8)

    def drain(t, carry):
        for k in range(2):
            pltpu.make_async_copy(ys_ref.at[0], buf_s.at[0, pl.ds(0, ROW_TILE)], sem).wait()
        return carry

    lax.fori_loop(0, tt, drain, 0, unroll=8)

    g1 = rf_ref[:, 0:1]
    g2 = rf_ref[:, 1:2]
    parts = []
    for s in range(D_MODEL // LANES):
        lanes = slice(s * LANES, (s + 1) * LANES)
        sub = pl.ds(s, tt, stride=ROW_TILE)
        parts.append(xmid_ref[:, lanes] + g1 * buf_s[0, sub, :] + g2 * buf_s[1, sub, :])
    y = jnp.concatenate(parts, axis=1)
    if final:
        y = y * lax.rsqrt(jnp.mean(y * y, axis=-1, keepdims=True) + EPS) * gf_ref[...]
    o_ref[...] = y


def _combine(dest, ys, rf, xmid, gf, tt, final):
    n = xmid.shape[0]
    dest3 = dest.reshape(n // tt, 1, 2 * tt)
    row = lambda w: pl.BlockSpec((tt, w), lambda i: (i, 0))
    return pl.pallas_call(
        functools.partial(_combine_kernel, final),
        out_shape=jax.ShapeDtypeStruct((n, D_MODEL), F32),
        grid=(n // tt,),
        in_specs=[pl.BlockSpec((1, 1, 2 * tt), lambda i: (i, 0, 0), memory_space=pltpu.SMEM),
                  pl.BlockSpec(memory_space=pl.ANY), row(LANES), row(D_MODEL),
                  pl.BlockSpec((1, D_MODEL), lambda i: (0, 0))],
        out_specs=row(D_MODEL),
        scratch_shapes=[pltpu.VMEM((2, tt * ROW_TILE, LANES), F32), pltpu.SemaphoreType.DMA(())],
        compiler_params=_cparams(("arbitrary",)), name="moe_combine",
    )(dest3, ys, rf, xmid, gf)


def _swiglu_acc(xn, wg_ref, wu_ref, wd_ref, acc_s):
    g = _dot(xn, wg_ref[0])
    u = _dot(xn, wu_ref[0])
    acc_s[...] += _dot((g * _sigmoid(g) * u).astype(BF16), wd_ref[0])


def _ffn_dense_kernel(final, xn_ref, wg_ref, wu_ref, wd_ref, xmid_ref, gf_ref, o_ref, acc_s):
    f = pl.program_id(1)

    @pl.when(f == 0)
    def _():
        acc_s[...] = jnp.zeros_like(acc_s)

    _swiglu_acc(xn_ref[...], wg_ref, wu_ref, wd_ref, acc_s)

    @pl.when(f == pl.num_programs(1) - 1)
    def _():
        y = xmid_ref[...] + acc_s[...]
        if final:
            y = y * lax.rsqrt(jnp.mean(y * y, axis=-1, keepdims=True) + EPS) * gf_ref[...]
        o_ref[...] = y


def _ffn_dense(xn, wg, wu, wd, xmid, gf, tm, final):
    n = xn.shape[0]
    tf = FFN_F_TILE
    row = lambda w: pl.BlockSpec((tm, w), lambda i, f: (i, 0))
    return pl.pallas_call(
        functools.partial(_ffn_dense_kernel, final),
        out_shape=jax.ShapeDtypeStruct((n, D_MODEL), F32),
        grid=(n // tm, D_FF // tf),
        in_specs=[row(D_MODEL),
                  pl.BlockSpec((1, D_MODEL, tf), lambda i, f: (0, 0, f)),
                  pl.BlockSpec((1, D_MODEL, tf), lambda i, f: (0, 0, f)),
                  pl.BlockSpec((1, tf, D_MODEL), lambda i, f: (0, f, 0)),
                  row(D_MODEL), pl.BlockSpec((1, D_MODEL), lambda i, f: (0, 0))],
        out_specs=row(D_MODEL),
        scratch_shapes=[pltpu.VMEM((tm, D_MODEL), F32)],
        compiler_params=_cparams(("parallel", "arbitrary")),
        name="ffn_dense",
    )(xn, wg, wu, wd, xmid, gf)


def _ffn_routed_kernel(te_ref, nu_ref, xs_ref, wg_ref, wu_ref, wd_ref, ys_ref, xb_s, acc_s):
    del te_ref
    tm = xb_s.shape[0]
    i = pl.program_id(0)
    f = pl.program_id(1)

    @pl.when(i < nu_ref[0])
    def _():
        @pl.when(f == 0)
        def _():
            acc_s[...] = jnp.zeros_like(acc_s)
            for s in range(D_MODEL // LANES):
                xb_s[:, s * LANES:(s + 1) * LANES] = xs_ref[pl.ds(s, tm, stride=ROW_TILE), :].astype(BF16)

        _swiglu_acc(xb_s[...], wg_ref, wu_ref, wd_ref, acc_s)

        @pl.when(f == pl.num_programs(1) - 1)
        def _():
            for s in range(D_MODEL // LANES):
                ys_ref[pl.ds(s, tm, stride=ROW_TILE), :] = acc_s[:, s * LANES:(s + 1) * LANES]

    @pl.when(jnp.logical_and(i >= nu_ref[0], f == 0))
    def _():
        ys_ref[...] = jnp.zeros_like(ys_ref)


def _ffn_routed(tile_expert, n_used, xs, wg, wu, wd):
    n_tiles = tile_expert.shape[0]
    tm, tf = MOE_TILE, FFN_F_TILE
    n_f = D_FF // tf
    last = lambda i, nu: jnp.minimum(i, nu[0] - 1)
    fcol = lambda i, f, nu: jnp.where(i < nu[0], f, n_f - 1)
    return pl.pallas_call(
        _ffn_routed_kernel,
        out_shape=jax.ShapeDtypeStruct(xs.shape, F32),
        grid_spec=pltpu.PrefetchScalarGridSpec(
            num_scalar_prefetch=2,
            grid=(n_tiles, n_f),
            in_specs=[pl.BlockSpec((tm * ROW_TILE, LANES), lambda i, f, te, nu: (last(i, nu), 0)),
                      pl.BlockSpec((1, D_MODEL, tf), lambda i, f, te, nu: (te[last(i, nu)], 0, fcol(i, f, nu))),
                      pl.BlockSpec((1, D_MODEL, tf), lambda i, f, te, nu: (te[last(i, nu)], 0, fcol(i, f, nu))),
                      pl.BlockSpec((1, tf, D_MODEL), lambda i, f, te, nu: (te[last(i, nu)], fcol(i, f, nu), 0))],
            out_specs=pl.BlockSpec((tm * ROW_TILE, LANES), lambda i, f, te, nu: (i, 0)),
            scratch_shapes=[pltpu.VMEM((tm, D_MODEL), BF16), pltpu.VMEM((tm, D_MODEL), F32)]),
        compiler_params=_cparams(("arbitrary", "arbitrary")),
        name="ffn_routed",
    )(tile_expert, n_used, xs, wg, wu, wd)


def _route_plan(ri_p, cnt_p, ri_s, cnt_s, n_tiles):
    cnt_p, cnt_s = cnt_p[0, :N_EXPERTS], cnt_s[0, :N_EXPERTS]
    tiles = (cnt_p + cnt_s + MOE_TILE - 1) // MOE_TILE
    ends = jnp.cumsum(tiles)
    start = (ends - tiles) * MOE_TILE
    lookup = lambda table, idx: jnp.sum(
        jnp.where(idx[..., None] == jnp.arange(N_EXPERTS, dtype=jnp.int32), table, 0), axis=-1)
    dest_p = lookup(start, ri_p[:, 0:2]) + ri_p[:, 2:4]
    dest_s = lookup(start + cnt_p, ri_s[:, 0:2]) + ri_s[:, 2:4]
    tile_expert = jnp.minimum(jnp.sum(jnp.arange(n_tiles, dtype=jnp.int32)[:, None] >= ends[None, :], axis=-1),
                              N_EXPERTS - 1).astype(jnp.int32)
    return dest_p, dest_s, tile_expert, ends[-1:].astype(jnp.int32)


def _pad_heads_cols(w):
    w = w.reshape(w.shape[0], C_HEADS, C_HEAD_DIM)
    return jnp.pad(w, ((0, 0), (0, 0), (0, C_HEAD_PAD - C_HEAD_DIM))).reshape(w.shape[0], C_WIDTH_PAD)


def _layer_params(l, w_in, b_gate, w_s, b_s, gn_c, w_out):
    w = w_in[l]
    gates = jnp.pad(w[:, OFF_CG:OFF_CG + 2 * C_HEADS], ((0, 0), (0, LANES - 2 * C_HEADS)))
    wp = jnp.concatenate(
        [w[:, :OFF_CQ]] + [_pad_heads_cols(w[:, o:o + C_WIDTH]) for o in (OFF_CQ, OFF_CK, OFF_CV, OFF_CO)]
        + [gates], axis=1).astype(BF16)
    bg = jnp.pad(b_gate[l], (0, LANES - 2 * C_HEADS))[None, :]
    pos = jnp.arange(GMLP_CHUNK)
    mask = (pos[None, :] // CHUNK) <= (pos[:, None] // CHUNK)
    wm = jnp.where(mask[None], w_s[l], 0.0)
    bs = jnp.repeat(b_s[l].T, A_GROUP_DIM, axis=1)
    wo = w_out[l]
    wc = wo[A_WIDTH + B_WIDTH:].reshape(C_HEADS, C_HEAD_DIM, D_MODEL)
    wc = jnp.pad(wc, ((0, 0), (0, C_HEAD_PAD - C_HEAD_DIM), (0, 0))).reshape(C_WIDTH_PAD, D_MODEL)
    gnc = _pad_heads_cols(gn_c[l][None, :])
    return dict(wp=wp, bg=bg, wm=wm, bs=bs, wa=wo[:A_WIDTH].astype(BF16),
                wb=wo[A_WIDTH:A_WIDTH + B_WIDTH].astype(BF16), wc=wc.astype(BF16), gnc=gnc)


def _pad_state(c, n, m):
    p = C_HEAD_PAD - C_HEAD_DIM
    c = jnp.pad(c, ((0, 0), (0, 0), (0, p), (0, p)))
    n = jnp.pad(n, ((0, 0), (0, 0), (0, p)))
    n = jnp.broadcast_to(n[:, :, :, None], n.shape + (LANES,))
    m = jnp.broadcast_to(m[:, :, None, None], m.shape + (1, LANES))
    return c, n, m


def _unpad_state(c, n, m):
    return c[:, :, :C_HEAD_DIM, :C_HEAD_DIM], n[:, :, :C_HEAD_DIM, 0], m[:, :, 0, 0]


def kernel(x_prompt, x_sample, cache_k_b, cache_v_b, state_c_mlstm, state_n_mlstm, state_m_mlstm,
           g_mix, w_in, b_gate, ln_a_g, ln_a_b, w_s, b_s, gn_b, gn_c, w_out,
           g_ffn, w_gate_d, w_up_d, w_down_d, w_router, w_gate_e, w_up_e, w_down_e, g_final):
    n_seq = x_prompt.shape[1]
    n_dec, n_new = x_sample.shape[0], x_sample.shape[1]
    past = cache_k_b.shape[2]
    n_samp = n_dec * n_new

    xp = x_prompt.reshape(n_seq, D_MODEL)
    xs = x_sample.reshape(n_samp, D_MODEL)
    gfin = g_final[None, :]
    keys_last = lambda a: jnp.transpose(a, (0, 1, 3, 4, 2)).reshape(DEPTH, n_dec, B_WIDTH, past)
    cache_kt, cache_vt = keys_last(cache_k_b), keys_last(cache_v_b)

    outs = {k: [] for k in ("kbp", "vbp", "cp", "np", "mp", "kbs", "vbs", "cs", "ns", "ms", "vas")}
    for l in range(DEPTH):
        p = _layer_params(l, w_in, b_gate, w_s, b_s, gn_c, w_out)
        gmix = g_mix[l][None, :]
        lng, lnb = ln_a_g[l][None, :], ln_a_b[l][None, :]
        gnb = gn_b[l][None, :]
        gffn = g_ffn[l][None, :]
        moe = l % 2 == 1
        j = l // 2
        if moe:
            wg, wu, wd = w_gate_e[j].astype(BF16), w_up_e[j].astype(BF16), w_down_e[j].astype(BF16)
            wr = _split_bf16(jnp.pad(w_router[j], ((0, 0), (0, LANES - N_EXPERTS))))
        else:
            wg, wu, wd = (w_gate_d[j][None].astype(BF16), w_up_d[j][None].astype(BF16),
                          w_down_d[j][None].astype(BF16))
            wr = None
        final = l == DEPTH - 1

        wm_s = jnp.kron(jnp.eye(n_dec, dtype=F32), p["wm"][:, :n_new, :n_new])
        bs_s = jnp.tile(p["bs"][:n_new], (n_dec, 1))

        (ya, qb, qc, kc, kct, vc, oc, gt, gtt, kbt, vbt) = _proj(
            xp, gmix, p["wp"], p["bg"], lng, lnb, p["wm"].astype(BF16), p["bs"], ROW_BLOCK, GMLP_CHUNK, False)
        yb = _sb_prompt(qb, kbt, vbt, gnb)
        c0, n0, m0 = _pad_state(jnp.zeros((1, C_HEADS, C_HEAD_DIM, C_HEAD_DIM), F32),
                                jnp.zeros((1, C_HEADS, C_HEAD_DIM), F32), jnp.zeros((1, C_HEADS), F32))
        yc, c_f, n_f, m_f = _mlstm(qc[None], kc[None], kct[None], vc[None], oc[None], gt[None], gtt[None],
                                   p["gnc"], c0, n0, m0, MLSTM_SUB_BLOCKS)
        merged_p = _merge(xp, ya, yb, yc[0], p["wa"], p["wb"], p["wc"], gffn, wr, ROW_BLOCK)
        c_f, n_f, m_f = _unpad_state(c_f, n_f, m_f)
        heads_last = lambda a: jnp.transpose(a.reshape(B_HEADS, B_HEAD_DIM, n_seq), (2, 0, 1))[None]
        outs["kbp"].append(heads_last(kbt))
        outs["vbp"].append(heads_last(vbt))
        outs["cp"].append(c_f)
        outs["np"].append(n_f)
        outs["mp"].append(m_f)

        (ya, qb, qc, kc, kct, vc, oc, gt, gtt, kbf, vbf, kbh, vbh, va) = _proj(
            xs, gmix, p["wp"], p["bg"], lng, lnb, wm_s.astype(BF16), bs_s, n_samp, n_samp, True)
        r3 = lambda a: a.reshape(n_dec, n_new, a.shape[-1])
        yb = _sb_sample(r3(qb), r3(kbh), r3(vbh), cache_kt, cache_vt, gnb, l)
        n_pad = MLSTM_BLOCK - n_new
        padr = lambda a: jnp.pad(r3(a), ((0, 0), (0, n_pad), (0, 0)))
        per_stream = lambda a: jnp.transpose(a.reshape(a.shape[0], n_dec, n_new), (1, 0, 2))
        kct_s = jnp.pad(per_stream(kct), ((0, 0), (0, 0), (0, n_pad)))
        gate_row = jnp.arange(GATE_ROWS)[None, :, None]
        gtt_pad = jnp.broadcast_to(jnp.where(gate_row < C_HEADS, NEG_BIG, 0.0).astype(F32),
                                   (n_dec, GATE_ROWS, n_pad))
        gtt_s = jnp.concatenate([per_stream(gtt), gtt_pad], axis=2)
        gt_s = jnp.concatenate([r3(gt), jnp.broadcast_to(
            jnp.where(jnp.arange(LANES) < C_HEADS, NEG_BIG, 0.0).astype(F32), (n_dec, n_pad, LANES))], axis=1)
        c0, n0, m0 = _pad_state(state_c_mlstm[l], state_n_mlstm[l], state_m_mlstm[l])
        yc, c_u, n_u, m_u = _mlstm(padr(qc), padr(kc), kct_s, padr(vc), padr(oc), gt_s, gtt_s, p["gnc"],
                                   c0, n0, m0, 1)
        yc = yc[:, :n_new].reshape(n_samp, C_WIDTH_PAD)
        merged_s = _merge(xs, ya, yb.reshape(n_samp, B_WIDTH), yc, p["wa"], p["wb"], p["wc"],
                          gffn, wr, n_samp)
        c_u, n_u, m_u = _unpad_state(c_u, n_u, m_u)

        if moe:
            xmid_p, xrow_p, ri_p, rf_p, cnt_p = merged_p
            xmid_s, xrow_s, ri_s, rf_s, cnt_s = merged_s
            n_tiles = 2 * (n_seq + n_samp) // MOE_TILE + N_EXPERTS
            dest_p, dest_s, tile_expert, n_used = _route_plan(ri_p, cnt_p, ri_s, cnt_s, n_tiles)
            as_tiles = lambda a: a.reshape(a.shape[0] // ROW_TILE, ROW_TILE, LANES)
            xsort = jnp.zeros((n_tiles * MOE_TILE, ROW_TILE, LANES), F32)
            xsort = _dispatch(dest_p, xrow_p, xsort, ROW_BLOCK)
            xsort = _dispatch(dest_s, xrow_s, xsort, n_samp)
            ysort = _ffn_routed(tile_expert, n_used, xsort.reshape(-1, LANES), wg, wu, wd)
            xp = _combine(dest_p, as_tiles(ysort), rf_p, xmid_p, gfin, MOE_COMBINE_BLOCK, final)
            xs = _combine(dest_s, as_tiles(ysort), rf_s, xmid_s, gfin, n_samp, final)
        else:
            xp = _ffn_dense(merged_p[1], wg, wu, wd, merged_p[0], gfin, ROW_BLOCK, final)
            xs = _ffn_dense(merged_s[1], wg, wu, wd, merged_s[0], gfin, n_samp, final)

        outs["kbs"].append(kbf.reshape(n_dec, n_new, B_HEADS, B_HEAD_DIM))
        outs["vbs"].append(vbf.reshape(n_dec, n_new, B_HEADS, B_HEAD_DIM))
        outs["cs"].append(c_u)
        outs["ns"].append(n_u)
        outs["ms"].append(m_u)
        outs["vas"].append(va.reshape(n_dec, n_new, A_WIDTH))

    st = lambda k: jnp.stack(outs[k])
    return (xp.reshape(1, n_seq, D_MODEL), xs.reshape(n_dec, n_new, D_MODEL),
            st("kbp"), st("vbp"), st("cp"), st("np"), st("mp"),
            st("kbs"), st("vbs"), st("cs"), st("ns"), st("ms"), st("vas"))
```

```python
import functools
import math

import jax
import jax.numpy as jnp
from jax import lax
from jax.experimental import pallas as pl
from jax.experimental.pallas import tpu as pltpu

F32 = jnp.float32
BF16 = jnp.bfloat16

D_MODEL = 1024
DEPTH = 2
EPS = 1e-6
CHUNK = 64
A_WIDTH = 256
A_GROUPS = 4
A_GROUP_DIM = 64
GMLP_CHUNK = 128
B_HEAD_DIM = 64
B_WIDTH = 384
B_HEADS = 6
B_PAIRS = 3
C_HEADS = 4
C_HEAD_DIM = 96
C_WIDTH = 384
D_FF = 2816
N_EXPERTS = 8

LANES = 128
C_HEAD_PAD = LANES
C_WIDTH_PAD = C_HEADS * C_HEAD_PAD

OFF_AU, OFF_AV, OFF_BQ, OFF_BK, OFF_BV = 0, 256, 512, 896, 1280
OFF_CQ, OFF_CK, OFF_CV, OFF_CO, OFF_CG = 1664, 2048, 2432, 2816, 3200
P_A = 0
P_BQ = 512
P_BK = P_BQ + B_WIDTH
P_BV = P_BK + B_WIDTH
P_CQ = P_BV + B_WIDTH
P_CK = P_CQ + C_WIDTH_PAD
P_CV = P_CK + C_WIDTH_PAD
P_CO = P_CV + C_WIDTH_PAD
P_CG = P_CO + C_WIDTH_PAD
P_DIM = P_CG + LANES

SB_DEAD_LOG_WEIGHT = -110.0
SB_BLOCK = 256
SB_BLOCKS_PER_STEP = 4
MLSTM_BLOCK = 128
MLSTM_SUB_BLOCKS = 4
GATE_ROWS = 2 * C_HEADS
NEG_BIG = -1e30
ROW_TILE = 8
MOE_TILE = 512
FFN_F_TILE = 1408
ROW_BLOCK = 512
MOE_COMBINE_BLOCK = 256

VMEM_LIMIT = 56 * 1024 * 1024


def _cparams(sem):
    return pltpu.CompilerParams(dimension_semantics=sem, vmem_limit_bytes=VMEM_LIMIT)


def _gelu(x):
    return 0.5 * x * (1.0 + lax.erf(x * (1.0 / math.sqrt(2.0))))


def _log_sigmoid(x):
    return jnp.minimum(x, 0.0) - jnp.log(1.0 + jnp.exp(-jnp.abs(x)))


def _sigmoid(x):
    return 1.0 / (1.0 + jnp.exp(-x))


def _split_bf16(x):
    hi = x.astype(BF16)
    lo = (x - hi.astype(F32)).astype(BF16)
    return hi, lo


def _to_token_tiles(x):
    return pltpu.einshape("t(sl)->tsl", x, s=ROW_TILE)


def _from_token_tiles(x):
    return pltpu.einshape("tsl->t(sl)", x)


def _dot(a, b):
    return jnp.dot(a, b, preferred_element_type=F32)


def _dot_nt(a, b):
    return lax.dot_general(a, b, (((1,), (1,)), ((), ())), preferred_element_type=F32)


def _dot_tn(a, b):
    return lax.dot_general(a, b, (((0,), (0,)), ((), ())), preferred_element_type=F32)


def _proj_kernel(n_chunks, chunk, sample, x_ref, gmix_ref, w_ref, bg_ref, lng_ref, lnb_ref, ws_ref, bs_ref,
                 *refs):
    n_out = 14 if sample else 11
    ya_ref, qb_ref, qc_ref, kc_ref, kct_ref, vc_ref, oc_ref, gt_ref, gtt_ref, *kv_refs = refs[len(refs) - n_out:]
    x = x_ref[...]
    xn = (x * lax.rsqrt(jnp.mean(x * x, axis=-1, keepdims=True) + EPS) * gmix_ref[...]).astype(BF16)

    def proj(off, width):
        return _dot(xn, w_ref[:, off:off + width])

    za = proj(P_A, 2 * A_WIDTH)
    u = _gelu(za[:, :A_WIDTH])
    gv = _gelu(za[:, A_WIDTH:])
    xc = gv - jnp.mean(gv, axis=-1, keepdims=True)
    va = xc * lax.rsqrt(jnp.mean(xc * xc, axis=-1, keepdims=True) + EPS) * lng_ref[...] + lnb_ref[...]
    if sample:
        kv_refs[4][...] = va
    vab = va.astype(BF16)
    lane_group = lax.broadcasted_iota(jnp.int32, (chunk, A_WIDTH), 1) // A_GROUP_DIM
    for c in range(n_chunks):
        rows = slice(c * chunk, (c + 1) * chunk)
        vch = vab[rows]
        s = jnp.zeros((chunk, A_WIDTH), F32)
        for g in range(A_GROUPS):
            s = jnp.where(lane_group == g, _dot(ws_ref[g], vch), s)
        ya_ref[rows, :] = (u[rows] * (s + bs_ref[...])).astype(BF16)

    qb_ref[...] = (proj(P_BQ, B_WIDTH) * (1.0 / math.sqrt(B_HEAD_DIM))).astype(BF16)
    zk = proj(P_BK, B_WIDTH)
    zv = proj(P_BV, B_WIDTH)
    if sample:
        kv_refs[0][...] = zk
        kv_refs[1][...] = zv
        kv_refs[2][...] = zk.astype(BF16)
        kv_refs[3][...] = zv.astype(BF16)
    else:
        kv_refs[0][0] = zk.T
        kv_refs[1][0] = zv.T
        for other in range(1, kv_refs[0].shape[0]):
            kv_refs[0][other] = jnp.zeros(kv_refs[0].shape[1:], F32)
            kv_refs[1][other] = jnp.zeros(kv_refs[1].shape[1:], F32)

    qc_ref[...] = proj(P_CQ, C_WIDTH_PAD).astype(BF16)
    zk = proj(P_CK, C_WIDTH_PAD) * (C_HEAD_DIM ** -0.5)
    kc_ref[...] = zk.astype(BF16)
    kct_ref[...] = zk.T.astype(BF16)
    vc_ref[...] = proj(P_CV, C_WIDTH_PAD).astype(BF16)
    oc_ref[...] = proj(P_CO, C_WIDTH_PAD)
    g = proj(P_CG, LANES) + bg_ref[...]
    lane = lax.broadcasted_iota(jnp.int32, g.shape, 1)
    gates = jnp.where(lane < C_HEADS, g, _log_sigmoid(g))
    gt_ref[...] = gates
    gtt_ref[...] = gates.T[:GATE_ROWS, :]


def _proj(x, gmix, wp, bg, lng, lnb, ws, bs, tm, chunk, sample, layer=0, kv_all=None):
    n = x.shape[0]
    row = lambda w: pl.BlockSpec((tm, w), lambda i: (i, 0))
    full = lambda a: pl.BlockSpec(a.shape, lambda i: (0,) * a.ndim)
    col = lambda h: pl.BlockSpec((h, tm), lambda i: (0, i))
    outs = [((n, A_WIDTH), BF16, row(A_WIDTH)), ((n, B_WIDTH), BF16, row(B_WIDTH)),
            ((n, C_WIDTH_PAD), BF16, row(C_WIDTH_PAD)), ((n, C_WIDTH_PAD), BF16, row(C_WIDTH_PAD)),
            ((C_WIDTH_PAD, n), BF16, col(C_WIDTH_PAD)), ((n, C_WIDTH_PAD), BF16, row(C_WIDTH_PAD)),
            ((n, C_WIDTH_PAD), F32, row(C_WIDTH_PAD)), ((n, LANES), F32, row(LANES)),
            ((GATE_ROWS, n), F32, col(GATE_ROWS))]
    out_shape = [jax.ShapeDtypeStruct(s, dt) for s, dt, _ in outs]
    out_specs = [spec for _, _, spec in outs]
    if sample:
        extra = [(B_WIDTH, F32), (B_WIDTH, F32), (B_WIDTH, BF16), (B_WIDTH, BF16), (A_WIDTH, F32)]
        out_shape += [jax.ShapeDtypeStruct((n, w), dt) for w, dt in extra]
        out_specs += [row(w) for w, _ in extra]
    ins = [x, gmix, wp, bg, lng, lnb, ws, bs]
    in_specs = [row(D_MODEL), full(gmix), full(wp), full(bg), full(lng), full(lnb), full(ws), full(bs)]
    aliases = {}
    if not sample:
        out_shape += [jax.ShapeDtypeStruct((DEPTH, B_WIDTH, n), F32)] * 2
        if kv_all is None:
            out_specs += [pl.BlockSpec((DEPTH, B_WIDTH, tm), lambda i: (0, 0, i))] * 2
        else:
            out_specs += [pl.BlockSpec((1, B_WIDTH, tm), lambda i: (layer, 0, i))] * 2
            aliases = {len(ins): len(out_shape) - 2, len(ins) + 1: len(out_shape) - 1}
            ins += list(kv_all)
            in_specs += [pl.BlockSpec(memory_space=pl.ANY)] * 2
    return pl.pallas_call(
        functools.partial(_proj_kernel, tm // chunk, chunk, sample),
        out_shape=out_shape,
        grid=(n // tm,),
        in_specs=in_specs,
        out_specs=out_specs,
        input_output_aliases=aliases,
        compiler_params=_cparams(("parallel",)),
        name="proj",
    )(*ins)


def _sb_step(qh, kblk, vblk, carry, acc, tri, mask, transposed):
    z = _dot(qh, kblk) if transposed else _dot_nt(qh, kblk)
    tk = z.shape[1]
    drop = jnp.maximum(z, 0.0) + jnp.log(1.0 + jnp.exp(-jnp.abs(z)))
    if mask is not None:
        drop = jnp.where(mask, drop, 0.0)
    hi, lo = _split_bf16(drop)
    cs = _dot(hi, tri) + _dot(lo, tri)
    if tk >= LANES:
        carry_b = jnp.concatenate([carry] * (tk // LANES), axis=1)
    else:
        carry_b = carry[:, :tk]
    a = jnp.exp(z - cs - carry_b)
    if mask is not None:
        a = jnp.where(mask, a, 0.0)
    a = a.astype(BF16)
    acc = acc + (_dot_nt(a, vblk) if transposed else _dot(a, vblk))
    return carry + jnp.broadcast_to(cs[:, :1], carry.shape), acc


def _tri(tk):
    j = lax.broadcasted_iota(jnp.int32, (tk, tk), 0)
    s = lax.broadcasted_iota(jnp.int32, (tk, tk), 1)
    return (j >= s).astype(BF16)


def _sb_finish(acc_s, gn_ref, o_ref, head0):
    out = jnp.where(head0, acc_s[0], acc_s[1])
    r = lax.broadcasted_iota(jnp.int32, (LANES, LANES), 0) // B_HEAD_DIM
    c = lax.broadcasted_iota(jnp.int32, (LANES, LANES), 1) // B_HEAD_DIM
    same_head = (r == c).astype(BF16)
    hi, lo = _split_bf16(out * out)
    ms = (_dot(hi, same_head) + _dot(lo, same_head)) * (1.0 / B_HEAD_DIM)
    return (out * lax.rsqrt(ms + EPS) * gn_ref[...]).astype(o_ref.dtype)


def _sb_walk(qh, load_kv, first_block, carry_s, acc_s, tri):
    def alive():
        return jnp.minimum(jnp.min(carry_s[0]), jnp.min(carry_s[1])) <= -SB_DEAD_LOG_WEIGHT

    def cond(st):
        j, live = st
        return jnp.logical_and(j >= 0, live)

    def body(st):
        j, _ = st
        kblk, vblk = load_kv(j)
        for h in range(2):
            carry, acc = _sb_step(qh[h], kblk, vblk, carry_s[h], acc_s[h], tri, None, True)
            carry_s[h] = carry
            acc_s[h] = acc
        return j - 1, alive()

    lax.while_loop(cond, body, (first_block, alive()))


def _sb_prompt_kernel(q_ref, k_ref, v_ref, gn_ref, o_ref, carry_s, acc_s):
    tq = SB_BLOCK
    n_q = q_ref.shape[0] // tq
    head0 = lax.broadcasted_iota(jnp.int32, (tq, LANES), 1) < B_HEAD_DIM
    tri = _tri(tq)
    t = lax.broadcasted_iota(jnp.int32, (tq, tq), 0)
    s = lax.broadcasted_iota(jnp.int32, (tq, tq), 1)
    causal = s < t
    zeros = jnp.zeros((tq, LANES), F32)

    def load_kv(j):
        cols = pl.ds(pl.multiple_of(j * tq, tq), tq)
        return k_ref[:, cols].astype(BF16), v_ref[:, cols].astype(BF16)

    first = pl.program_id(1) * n_q
    kv = [load_kv(jnp.maximum(first + b - 1, 0)) for b in range(n_q + 1)]
    qhs = []
    for b in range(n_q):
        q = q_ref[b * tq:(b + 1) * tq, :]
        qh = [jnp.where(head0, q, 0), jnp.where(head0, 0, q)]
        qhs.append(qh)
        has_prev = t >= jnp.where(first + b >= 1, 0, tq)
        for h in range(2):
            carry, acc = _sb_step(qh[h], kv[b + 1][0], kv[b + 1][1], zeros, zeros, tri, causal, True)
            carry, acc = _sb_step(qh[h], kv[b][0], kv[b][1], carry, acc, tri, has_prev, True)
            carry_s[b, h] = carry
            acc_s[b, h] = acc
    for b in range(n_q):
        _sb_walk(qhs[b], load_kv, first + b - 2, carry_s.at[b], acc_s.at[b], tri)
        o_ref[b * tq:(b + 1) * tq, :] = _sb_finish(acc_s.at[b], gn_ref, o_ref, head0)


def _sb_prompt(q, k, v, gn, layer):
    n = q.shape[0]
    n_q = SB_BLOCKS_PER_STEP
    tq = SB_BLOCK * n_q
    blk = pl.BlockSpec((tq, LANES), lambda p, i: (i, p))
    seq = pl.BlockSpec((None, LANES, n), lambda p, i: (layer, p, 0))
    state = pltpu.VMEM((n_q, 2, SB_BLOCK, LANES), F32)
    return pl.pallas_call(
        _sb_prompt_kernel,
        out_shape=jax.ShapeDtypeStruct((n, B_WIDTH), BF16),
        grid=(B_PAIRS, n // tq),
        in_specs=[blk, seq, seq, pl.BlockSpec((1, LANES), lambda p, i: (0, p))],
        out_specs=blk,
        scratch_shapes=[state, state],
        compiler_params=_cparams(("parallel", "parallel")),
        name="sb_prompt",
    )(q, k, v, gn)


def _sb_sample_kernel(q_ref, kn_ref, vn_ref, kc_ref, vc_ref, gn_ref, o_ref, carry_s, acc_s):
    tq = q_ref.shape[1]
    tk = SB_BLOCK
    head0 = lax.broadcasted_iota(jnp.int32, (tq, LANES), 1) < B_HEAD_DIM
    q = q_ref[0]
    zero = jnp.zeros_like(q)
    qh = [jnp.where(head0, q, zero), jnp.where(head0, zero, q)]
    t = lax.broadcasted_iota(jnp.int32, (tq, tq), 0)
    s = lax.broadcasted_iota(jnp.int32, (tq, tq), 1)
    causal = s < t
    zeros = jnp.zeros((tq, LANES), F32)
    def load_kv(j):
        cols = pl.ds(pl.multiple_of(j * tk, tk), tk)
        return kc_ref[0, :, cols].astype(BF16), vc_ref[0, :, cols].astype(BF16)

    last = kc_ref.shape[2] // tk - 1
    kp, vp = load_kv(last)
    tri = _tri(tk)
    for h in range(2):
        carry, acc = _sb_step(qh[h], kn_ref[0], vn_ref[0], zeros, zeros, _tri(tq), causal, False)
        carry, acc = _sb_step(qh[h], kp, vp, carry, acc, tri, None, True)
        carry_s[h] = carry
        acc_s[h] = acc
    _sb_walk(qh, load_kv, last - 1, carry_s, acc_s, tri)
    o_ref[0] = _sb_finish(acc_s, gn_ref, o_ref, head0)


def _sb_sample(q, kn, vn, kc, vc, gn, layer):
    nb, tq, _ = q.shape
    past = kc.shape[3]
    new = pl.BlockSpec((1, tq, LANES), lambda b, p: (b, 0, p))
    old = pl.BlockSpec((None, 1, LANES, past), lambda b, p: (layer, b, p, 0))
    return pl.pallas_call(
        _sb_sample_kernel,
        out_shape=jax.ShapeDtypeStruct((nb, tq, B_WIDTH), BF16),
        grid=(nb, B_PAIRS),
        in_specs=[new, new, new, old, old, pl.BlockSpec((1, LANES), lambda b, p: (0, p))],
        out_specs=new,
        scratch_shapes=[pltpu.VMEM((2, tq, LANES), F32), pltpu.VMEM((2, tq, LANES), F32)],
        compiler_params=_cparams(("parallel", "parallel")),
        name="sb_sample",
    )(q, kn, vn, kc, vc, gn)


def _split3(x):
    h1 = x.astype(BF16)
    r1 = x - h1.astype(F32)
    h2 = r1.astype(BF16)
    return h1, h2, (r1 - h2.astype(F32)).astype(BF16)


def _dot3(x, rhs01):
    return _dot(jnp.concatenate(_split3(x), axis=1), jnp.concatenate([rhs01] * 3, axis=0))


def _dot3_left(lhs01, x):
    return _dot(jnp.concatenate([lhs01] * 3, axis=1), jnp.concatenate(_split3(x), axis=0))


def _mlstm_kernel(n_sub, q_ref, k_ref, kt_ref, v_ref, o_ref, gt_ref, gtt_ref, gn_ref, c0_ref, n0_ref, m0_ref,
                  yc_ref, c_out, n_out, m_out, c_s, n_s, m_s):
    L = MLSTM_BLOCK
    t_blk = pl.program_id(1)

    @pl.when(t_blk == 0)
    def _():
        c_s[...] = c0_ref[0]
        n_s[...] = n0_ref[0]
        m_s[...] = m0_ref[0]

    r = lax.broadcasted_iota(jnp.int32, (L, L), 0)
    c = lax.broadcasted_iota(jnp.int32, (L, L), 1)
    causal = c <= r
    upper = (r <= c).astype(BF16)
    ones_sq = jnp.ones((L, LANES), BF16)
    sel_r = lax.broadcasted_iota(jnp.int32, (LANES, C_WIDTH_PAD), 0)
    sel_c = lax.broadcasted_iota(jnp.int32, (LANES, C_WIDTH_PAD), 1) // C_HEAD_PAD
    sel_p = (sel_r == sel_c).astype(BF16)
    sel_b = (sel_r == sel_c + C_HEADS).astype(BF16)
    lane = c

    a_rows_all, pb_cols_all = [], []
    for sub in range(n_sub):
        rows = slice(sub * L, (sub + 1) * L)
        gtt = gtt_ref[0, :, rows]
        bct = _dot3(gtt, upper)
        a_rows_all.append(gtt[:C_HEADS] - bct[C_HEADS:])
        gt = gt_ref[0, rows, :]
        bc = _dot3_left(causal.astype(BF16), gt)
        pmax = gt - pltpu.roll(bc, LANES - C_HEADS, axis=1)
        for sh in (1, 2, 4, 8, 16, 32, 64):
            pmax = jnp.maximum(pmax, jnp.where(r >= sh, pltpu.roll(pmax, sh, axis=0), -jnp.inf))
        pb_cols_all.append(jnp.where(lane < C_HEADS, pmax, bc))
    pb_cols_all = jnp.concatenate(pb_cols_all, axis=0)
    p_all = _dot3(pb_cols_all, sel_p)
    b_all = _dot3(pb_cols_all, sel_b)

    state = [(c_s[h], n_s[h], m_s[h]) for h in range(C_HEADS)]
    for sub in range(n_sub):
        rows = slice(sub * L, (sub + 1) * L)
        a_rows = a_rows_all[sub]
        heads = []
        for h in range(C_HEADS):
            lanes = slice(h * C_HEAD_PAD, (h + 1) * C_HEAD_PAD)
            q = q_ref[0, rows, lanes]
            v1 = jnp.concatenate([v_ref[0, rows, lanes], ones_sq], axis=1)
            p_rep = p_all[rows, lanes]
            b_rep = b_all[rows, lanes]
            a_row = a_rows[h:h + 1, :]
            p_last = p_rep[L - 1:L, :]
            c_prev, n_prev, m_prev = state[h]

            w = jnp.exp(jnp.where(causal, a_row - p_rep, -jnp.inf)) * _dot_nt(q, k_ref[0, rows, lanes])
            kwt = (kt_ref[0, lanes, rows].astype(F32) * jnp.exp(a_row - p_last)).astype(BF16)
            both = _dot(jnp.concatenate([w.astype(BF16), kwt], axis=0), v1)
            here_sums = both[:L]
            fresh = both[L:]
            past_sums = _dot(q, jnp.concatenate([c_prev, n_prev], axis=1).astype(BF16))
            top = jnp.maximum(m_prev, p_rep)
            past = jnp.exp(m_prev - top)
            here = jnp.exp(p_rep - top)
            mix = jnp.concatenate([past, past], axis=1) * past_sums + jnp.concatenate([here, here], axis=1) * here_sums
            den = jnp.maximum(jnp.abs(mix[:, LANES:]), jnp.exp(-(b_rep + top)))
            heads.append(mix[:, :LANES] / den)

            top_last = jnp.maximum(m_prev, p_last)
            decay = jnp.exp(m_prev - top_last)
            gain = jnp.exp(p_last - top_last)
            state[h] = (decay * c_prev + gain * fresh[:, :LANES], decay * n_prev + gain * fresh[:, LANES:],
                        b_rep[L - 1:L, :] + top_last)

        hh = jnp.concatenate(heads, axis=0)
        ms = _dot(jnp.concatenate(_split_bf16(hh * hh), axis=1),
                  jnp.concatenate([ones_sq, ones_sq], axis=0)) * (1.0 / C_HEAD_DIM)
        hn = hh * lax.rsqrt(ms + EPS)
        for h in range(C_HEADS):
            lanes = slice(h * C_HEAD_PAD, (h + 1) * C_HEAD_PAD)
            yc_ref[0, rows, lanes] = (hn[h * L:(h + 1) * L] * gn_ref[:, lanes]
                                      * _sigmoid(o_ref[0, rows, lanes])).astype(BF16)

    for h in range(C_HEADS):
        c_s[h], n_s[h], m_s[h] = state[h]

    @pl.when(t_blk == pl.num_programs(1) - 1)
    def _():
        c_out[0] = c_s[...]
        n_out[0] = n_s[...]
        m_out[0] = m_s[...]


def _mlstm(q, k, kt, v, o, gt, gtt, gn, c0, n0, m0, n_sub):
    nb, n, _ = q.shape
    tb = n_sub * MLSTM_BLOCK
    seq = lambda w: pl.BlockSpec((1, tb, w), lambda b, t: (b, t, 0))
    seq_t = lambda h: pl.BlockSpec((1, h, tb), lambda b, t: (b, 0, t))
    st = lambda a: pl.BlockSpec((1,) + a.shape[1:], lambda b, t: (b,) + (0,) * (a.ndim - 1))
    return pl.pallas_call(
        functools.partial(_mlstm_kernel, n_sub),
        out_shape=[jax.ShapeDtypeStruct((nb, n, C_WIDTH_PAD), BF16),
                   jax.ShapeDtypeStruct(c0.shape, F32),
                   jax.ShapeDtypeStruct(n0.shape, F32),
                   jax.ShapeDtypeStruct(m0.shape, F32)],
        grid=(nb, n // tb),
        in_specs=[seq(C_WIDTH_PAD), seq(C_WIDTH_PAD), seq_t(C_WIDTH_PAD), seq(C_WIDTH_PAD), seq(C_WIDTH_PAD),
                  seq(LANES), seq_t(GATE_ROWS), pl.BlockSpec((1, C_WIDTH_PAD), lambda b, t: (0, 0)),
                  st(c0), st(n0), st(m0)],
        out_specs=[seq(C_WIDTH_PAD), st(c0), st(n0), st(m0)],
        scratch_shapes=[pltpu.VMEM(c0.shape[1:], F32), pltpu.VMEM(n0.shape[1:], F32),
                        pltpu.VMEM(m0.shape[1:], F32)],
        compiler_params=_cparams(("parallel", "arbitrary")),
        name="mlstm",
    )(q, k, kt, v, o, gt, gtt, gn, c0, n0, m0)


def _mixer_out(x_ref, ya_ref, yb_ref, yc_ref, wa_ref, wb_ref, wc_ref, g_ref, xmid_ref):
    y = _dot(ya_ref[...], wa_ref[...]) + _dot(yb_ref[...], wb_ref[...]) + _dot(yc_ref[...], wc_ref[...])
    x = x_ref[...] + y
    xmid_ref[...] = x
    return x * lax.rsqrt(jnp.mean(x * x, axis=-1, keepdims=True) + EPS) * g_ref[...]


def _merge_dense_kernel(x_ref, ya_ref, yb_ref, yc_ref, wa_ref, wb_ref, wc_ref, g_ref, xmid_ref, xn_ref):
    xn_ref[...] = _mixer_out(x_ref, ya_ref, yb_ref, yc_ref, wa_ref, wb_ref, wc_ref, g_ref,
                             xmid_ref).astype(BF16)


def _merge_moe_kernel(x_ref, ya_ref, yb_ref, yc_ref, wa_ref, wb_ref, wc_ref, g_ref, wrh_ref, wrl_ref,
                      earlier_ref, xmid_ref, xrow_ref, ri_ref, rf_ref, cnt_ref, run_s):
    tm = x_ref.shape[0]

    @pl.when(pl.program_id(0) == 0)
    def _():
        run_s[...] = jnp.zeros_like(run_s)

    xn = _mixer_out(x_ref, ya_ref, yb_ref, yc_ref, wa_ref, wb_ref, wc_ref, g_ref, xmid_ref)
    xrow_ref[...] = _to_token_tiles(xn)

    hi, lo = _split_bf16(xn)
    logits = _dot(hi, wrh_ref[...]) + _dot(hi, wrl_ref[...]) + _dot(lo, wrh_ref[...])
    lane = lax.broadcasted_iota(jnp.int32, logits.shape, 1)
    lg = jnp.where(lane < N_EXPERTS, logits, -jnp.inf)
    m1 = jnp.max(lg, axis=-1, keepdims=True)
    i1 = jnp.min(jnp.where(lg == m1, lane, LANES), axis=-1, keepdims=True)
    lg2 = jnp.where(lane == i1, -jnp.inf, lg)
    m2 = jnp.max(lg2, axis=-1, keepdims=True)
    i2 = jnp.min(jnp.where(lg2 == m2, lane, LANES), axis=-1, keepdims=True)
    e2 = jnp.exp(m2 - m1)
    g1 = 1.0 / (1.0 + e2)
    g2 = e2 * g1

    sel1 = lane == i1
    sel2 = lane == i2
    onehot = jnp.logical_or(sel1, sel2)
    before = _dot(earlier_ref[...], onehot.astype(BF16)) + run_s[...]
    rank1 = jnp.sum(jnp.where(sel1, before, 0.0), axis=-1, keepdims=True).astype(jnp.int32)
    rank2 = jnp.sum(jnp.where(sel2, before, 0.0), axis=-1, keepdims=True).astype(jnp.int32)
    run_s[...] += jnp.sum(onehot.astype(F32), axis=0, keepdims=True)
    cnt_ref[...] = run_s[...].astype(jnp.int32)
    ri_ref[...] = jnp.where(lane == 0, i1, jnp.where(lane == 1, i2,
                            jnp.where(lane == 2, rank1, jnp.where(lane == 3, rank2, 0))))
    rf_ref[...] = jnp.where(lane == 0, g1, jnp.where(lane == 1, g2, 0.0))


def _merge(x, ya, yb, yc, wa, wb, wc, g, wr, tm):
    n = x.shape[0]
    row = lambda w: pl.BlockSpec((tm, w), lambda i: (i, 0))
    full = lambda a: pl.BlockSpec(a.shape, lambda i: (0,) * a.ndim)
    ins = [x, ya, yb, yc, wa, wb, wc, g]
    in_specs = [row(D_MODEL), row(A_WIDTH), row(B_WIDTH), row(C_WIDTH_PAD), full(wa), full(wb), full(wc), full(g)]
    if wr is None:
        return pl.pallas_call(
            _merge_dense_kernel,
            out_shape=[jax.ShapeDtypeStruct((n, D_MODEL), F32), jax.ShapeDtypeStruct((n, D_MODEL), BF16)],
            grid=(n // tm,), in_specs=in_specs, out_specs=[row(D_MODEL), row(D_MODEL)],
            compiler_params=_cparams(("parallel",)), name="merge_dense",
        )(*ins)
    wrh, wrl = wr
    earlier = (jnp.arange(tm)[None, :] < jnp.arange(tm)[:, None]).astype(BF16)
    return pl.pallas_call(
        _merge_moe_kernel,
        out_shape=[jax.ShapeDtypeStruct((n, D_MODEL), F32),
                   jax.ShapeDtypeStruct((n, ROW_TILE, LANES), F32),
                   jax.ShapeDtypeStruct((n, LANES), jnp.int32),
                   jax.ShapeDtypeStruct((n, LANES), F32),
                   jax.ShapeDtypeStruct((1, LANES), jnp.int32)],
        grid=(n // tm,), in_specs=in_specs + [full(wrh), full(wrl), full(earlier)],
        out_specs=[row(D_MODEL), pl.BlockSpec((tm, ROW_TILE, LANES), lambda i: (i, 0, 0)),
                   row(LANES), row(LANES), pl.BlockSpec((1, LANES), lambda i: (0, 0))],
        scratch_shapes=[pltpu.VMEM((1, LANES), F32)],
        compiler_params=_cparams(("arbitrary",)), name="merge_moe",
    )(*ins, wrh, wrl, earlier)


def _dispatch_kernel(dest_ref, src_ref, xs_in_ref, xs_ref, sem):
    del xs_in_ref
    tt = dest_ref.shape[2] // 2

    def issue(t, carry):
        for k in range(2):
            pltpu.make_async_copy(src_ref.at[t], xs_ref.at[dest_ref[0, 0, 2 * t + k]], sem).start(priority=k)
        return carry

    lax.fori_loop(0, tt, issue, 0, unroll=8)

    def drain(t, carry):
        for k in range(2):
            pltpu.make_async_copy(src_ref.at[0], xs_ref.at[0], sem).wait()
        return carry

    lax.fori_loop(0, tt, drain, 0, unroll=8)


def _dispatch(dest, src, xs, tt):
    n = src.shape[0]
    dest3 = dest.reshape(n // tt, 1, 2 * tt)
    return pl.pallas_call(
        _dispatch_kernel,
        out_shape=jax.ShapeDtypeStruct(xs.shape, xs.dtype),
        grid=(n // tt,),
        in_specs=[pl.BlockSpec((1, 1, 2 * tt), lambda i: (i, 0, 0), memory_space=pltpu.SMEM),
                  pl.BlockSpec((tt, ROW_TILE, LANES), lambda i: (i, 0, 0)), pl.BlockSpec(memory_space=pl.ANY)],
        out_specs=pl.BlockSpec(memory_space=pl.ANY),
        scratch_shapes=[pltpu.SemaphoreType.DMA(())],
        input_output_aliases={2: 0},
        compiler_params=_cparams(("arbitrary",)), name="moe_dispatch",
    )(dest3, src, xs)


def _combine_kernel(final, dest_ref, next_ref, ys_ref, rf_ref, xmid_ref, gf_ref, o_ref, buf_s, sems):
    tt = xmid_ref.shape[0]
    i = pl.program_id(0)
    n_steps = pl.num_programs(0)

    def gather(idx_ref, slot):
        def issue(t, carry):
            for k in range(2):
                pltpu.make_async_copy(ys_ref.at[idx_ref[0, 0, 2 * t + k]], buf_s.at[slot, k, t],
                                      sems.at[slot]).start(priority=k)
            return carry

        lax.fori_loop(0, tt, issue, 0, unroll=8)

    slot = i % 2

    @pl.when(i == 0)
    def _():
        gather(dest_ref, 0)

    @pl.when(i + 1 < n_steps)
    def _():
        gather(next_ref, 1 - slot)

    def drain(t, carry):
        for k in range(2):
            pltpu.make_async_copy(ys_ref.at[0], buf_s.at[slot, 0, 0], sems.at[slot]).wait()
        return carry

    lax.fori_loop(0, tt, drain, 0, unroll=8)

    y = (xmid_ref[...] + rf_ref[:, 0:1] * _from_token_tiles(buf_s[slot, 0])
         + rf_ref[:, 1:2] * _from_token_tiles(buf_s[slot, 1]))
    if final:
        y = y * lax.rsqrt(jnp.mean(y * y, axis=-1, keepdims=True) + EPS) * gf_ref[...]
    o_ref[...] = y


def _combine(dest, ys, rf, xmid, gf, tt, final):
    n = xmid.shape[0]
    n_steps = n // tt
    dest3 = dest.reshape(n_steps, 1, 2 * tt)
    row = lambda w: pl.BlockSpec((tt, w), lambda i: (i, 0))
    return pl.pallas_call(
        functools.partial(_combine_kernel, final),
        out_shape=jax.ShapeDtypeStruct((n, D_MODEL), F32),
        grid=(n_steps,),
        in_specs=[pl.BlockSpec((1, 1, 2 * tt), lambda i: (i, 0, 0), memory_space=pltpu.SMEM),
                  pl.BlockSpec((1, 1, 2 * tt), lambda i: (jnp.minimum(i + 1, n_steps - 1), 0, 0),
                               memory_space=pltpu.SMEM),
                  pl.BlockSpec(memory_space=pl.ANY), row(LANES), row(D_MODEL),
                  pl.BlockSpec((1, D_MODEL), lambda i: (0, 0))],
        out_specs=row(D_MODEL),
        scratch_shapes=[pltpu.VMEM((2, 2, tt, ROW_TILE, LANES), F32), pltpu.SemaphoreType.DMA((2,))],
        compiler_params=_cparams(("arbitrary",)), name="moe_combine",
    )(dest3, dest3, ys, rf, xmid, gf)


def _swiglu_acc(xn, wg_ref, wu_ref, wd_ref, acc_s):
    g = _dot(xn, wg_ref[0])
    u = _dot(xn, wu_ref[0])
    acc_s[...] += _dot((g * _sigmoid(g) * u).astype(BF16), wd_ref[0])


def _ffn_dense_kernel(final, xn_ref, wg_ref, wu_ref, wd_ref, xmid_ref, gf_ref, o_ref, acc_s):
    f = pl.program_id(1)

    @pl.when(f == 0)
    def _():
        acc_s[...] = jnp.zeros_like(acc_s)

    _swiglu_acc(xn_ref[...], wg_ref, wu_ref, wd_ref, acc_s)

    @pl.when(f == pl.num_programs(1) - 1)
    def _():
        y = xmid_ref[...] + acc_s[...]
        if final:
            y = y * lax.rsqrt(jnp.mean(y * y, axis=-1, keepdims=True) + EPS) * gf_ref[...]
        o_ref[...] = y


def _ffn_dense(xn, wg, wu, wd, xmid, gf, tm, final):
    n = xn.shape[0]
    tf = FFN_F_TILE
    row = lambda w: pl.BlockSpec((tm, w), lambda i, f: (i, 0))
    return pl.pallas_call(
        functools.partial(_ffn_dense_kernel, final),
        out_shape=jax.ShapeDtypeStruct((n, D_MODEL), F32),
        grid=(n // tm, D_FF // tf),
        in_specs=[row(D_MODEL),
                  pl.BlockSpec((1, D_MODEL, tf), lambda i, f: (0, 0, f)),
                  pl.BlockSpec((1, D_MODEL, tf), lambda i, f: (0, 0, f)),
                  pl.BlockSpec((1, tf, D_MODEL), lambda i, f: (0, f, 0)),
                  row(D_MODEL), pl.BlockSpec((1, D_MODEL), lambda i, f: (0, 0))],
        out_specs=row(D_MODEL),
        scratch_shapes=[pltpu.VMEM((tm, D_MODEL), F32)],
        compiler_params=_cparams(("parallel", "arbitrary")),
        name="ffn_dense",
    )(xn, wg, wu, wd, xmid, gf)


def _ffn_routed_kernel(te_ref, nu_ref, xs_ref, wg_ref, wu_ref, wd_ref, ys_ref, xb_s, acc_s):
    del te_ref
    i = pl.program_id(0)
    f = pl.program_id(1)

    @pl.when(i < nu_ref[0])
    def _():
        @pl.when(f == 0)
        def _():
            acc_s[...] = jnp.zeros_like(acc_s)
            xb_s[...] = _from_token_tiles(xs_ref[...]).astype(BF16)

        _swiglu_acc(xb_s[...], wg_ref, wu_ref, wd_ref, acc_s)

        @pl.when(f == pl.num_programs(1) - 1)
        def _():
            ys_ref[...] = _to_token_tiles(acc_s[...])

    @pl.when(jnp.logical_and(i >= nu_ref[0], f == 0))
    def _():
        ys_ref[...] = jnp.zeros_like(ys_ref)


def _ffn_routed(tile_expert, n_used, xs, wg, wu, wd):
    n_tiles = tile_expert.shape[0]
    tm, tf = MOE_TILE, FFN_F_TILE
    n_f = D_FF // tf
    last = lambda i, nu: jnp.minimum(i, nu[0] - 1)
    fcol = lambda i, f, nu: jnp.where(i < nu[0], f, n_f - 1)
    return pl.pallas_call(
        _ffn_routed_kernel,
        out_shape=jax.ShapeDtypeStruct(xs.shape, F32),
        grid_spec=pltpu.PrefetchScalarGridSpec(
            num_scalar_prefetch=2,
            grid=(n_tiles, n_f),
            in_specs=[pl.BlockSpec((tm, ROW_TILE, LANES), lambda i, f, te, nu: (last(i, nu), 0, 0)),
                      pl.BlockSpec((1, D_MODEL, tf), lambda i, f, te, nu: (te[last(i, nu)], 0, fcol(i, f, nu))),
                      pl.BlockSpec((1, D_MODEL, tf), lambda i, f, te, nu: (te[last(i, nu)], 0, fcol(i, f, nu))),
                      pl.BlockSpec((1, tf, D_MODEL), lambda i, f, te, nu: (te[last(i, nu)], fcol(i, f, nu), 0))],
            out_specs=pl.BlockSpec((tm, ROW_TILE, LANES), lambda i, f, te, nu: (i, 0, 0)),
            scratch_shapes=[pltpu.VMEM((tm, D_MODEL), BF16), pltpu.VMEM((tm, D_MODEL), F32)]),
        compiler_params=_cparams(("arbitrary", "arbitrary")),
        name="ffn_routed",
    )(tile_expert, n_used, xs, wg, wu, wd)


def _route_plan(ri_p, cnt_p, ri_s, cnt_s, n_tiles):
    cnt_p, cnt_s = cnt_p[0, :N_EXPERTS], cnt_s[0, :N_EXPERTS]
    tiles = (cnt_p + cnt_s + MOE_TILE - 1) // MOE_TILE
    ends = jnp.cumsum(tiles)
    start = (ends - tiles) * MOE_TILE
    lookup = lambda table, idx: jnp.sum(
        jnp.where(idx[..., None] == jnp.arange(N_EXPERTS, dtype=jnp.int32), table, 0), axis=-1)
    dest_p = lookup(start, ri_p[:, 0:2]) + ri_p[:, 2:4]
    dest_s = lookup(start + cnt_p, ri_s[:, 0:2]) + ri_s[:, 2:4]
    tile_expert = jnp.minimum(jnp.sum(jnp.arange(n_tiles, dtype=jnp.int32)[:, None] >= ends[None, :], axis=-1),
                              N_EXPERTS - 1).astype(jnp.int32)
    return dest_p, dest_s, tile_expert, ends[-1:].astype(jnp.int32)


def _pad_heads_cols(w):
    w = w.reshape(w.shape[0], C_HEADS, C_HEAD_DIM)
    return jnp.pad(w, ((0, 0), (0, 0), (0, C_HEAD_PAD - C_HEAD_DIM))).reshape(w.shape[0], C_WIDTH_PAD)


def _layer_params(l, w_in, b_gate, w_s, b_s, gn_c, w_out):
    w = w_in[l]
    gates = jnp.pad(w[:, OFF_CG:OFF_CG + 2 * C_HEADS], ((0, 0), (0, LANES - 2 * C_HEADS)))
    wp = jnp.concatenate(
        [w[:, :OFF_CQ]] + [_pad_heads_cols(w[:, o:o + C_WIDTH]) for o in (OFF_CQ, OFF_CK, OFF_CV, OFF_CO)]
        + [gates], axis=1).astype(BF16)
    bg = jnp.pad(b_gate[l], (0, LANES - 2 * C_HEADS))[None, :]
    pos = jnp.arange(GMLP_CHUNK)
    mask = (pos[None, :] // CHUNK) <= (pos[:, None] // CHUNK)
    wm = jnp.where(mask[None], w_s[l], 0.0)
    bs = jnp.repeat(b_s[l].T, A_GROUP_DIM, axis=1)
    wo = w_out[l]
    wc = wo[A_WIDTH + B_WIDTH:].reshape(C_HEADS, C_HEAD_DIM, D_MODEL)
    wc = jnp.pad(wc, ((0, 0), (0, C_HEAD_PAD - C_HEAD_DIM), (0, 0))).reshape(C_WIDTH_PAD, D_MODEL)
    gnc = _pad_heads_cols(gn_c[l][None, :])
    return dict(wp=wp, bg=bg, wm=wm, bs=bs, wa=wo[:A_WIDTH].astype(BF16),
                wb=wo[A_WIDTH:A_WIDTH + B_WIDTH].astype(BF16), wc=wc.astype(BF16), gnc=gnc)


def _pad_state(c, n, m):
    p = C_HEAD_PAD - C_HEAD_DIM
    c = jnp.pad(c, ((0, 0), (0, 0), (0, p), (0, p)))
    n = jnp.pad(n, ((0, 0), (0, 0), (0, p)))
    n = jnp.broadcast_to(n[:, :, :, None], n.shape + (LANES,))
    m = jnp.broadcast_to(m[:, :, None, None], m.shape + (1, LANES))
    return c, n, m


def _unpad_state(c, n, m):
    return c[:, :, :C_HEAD_DIM, :C_HEAD_DIM], n[:, :, :C_HEAD_DIM, 0], m[:, :, 0, 0]


def kernel(x_prompt, x_sample, cache_k_b, cache_v_b, state_c_mlstm, state_n_mlstm, state_m_mlstm,
           g_mix, w_in, b_gate, ln_a_g, ln_a_b, w_s, b_s, gn_b, gn_c, w_out,
           g_ffn, w_gate_d, w_up_d, w_down_d, w_router, w_gate_e, w_up_e, w_down_e, g_final):
    n_seq = x_prompt.shape[1]
    n_dec, n_new = x_sample.shape[0], x_sample.shape[1]
    past = cache_k_b.shape[2]
    n_samp = n_dec * n_new

    xp = x_prompt.reshape(n_seq, D_MODEL)
    xs = x_sample.reshape(n_samp, D_MODEL)
    gfin = g_final[None, :]
    keys_last = lambda a: jnp.transpose(a, (0, 1, 3, 4, 2)).reshape(DEPTH, n_dec, B_WIDTH, past)
    cache_kt, cache_vt = keys_last(cache_k_b), keys_last(cache_v_b)

    outs = {k: [] for k in ("kbp", "vbp", "cp", "np", "mp", "kbs", "vbs", "cs", "ns", "ms", "vas")}
    for l in range(DEPTH):
        p = _layer_params(l, w_in, b_gate, w_s, b_s, gn_c, w_out)
        gmix = g_mix[l][None, :]
        lng, lnb = ln_a_g[l][None, :], ln_a_b[l][None, :]
        gnb = gn_b[l][None, :]
        gffn = g_ffn[l][None, :]
        moe = l % 2 == 1
        j = l // 2
        if moe:
            wg, wu, wd = w_gate_e[j].astype(BF16), w_up_e[j].astype(BF16), w_down_e[j].astype(BF16)
            wr = _split_bf16(jnp.pad(w_router[j], ((0, 0), (0, LANES - N_EXPERTS))))
        else:
            wg, wu, wd = (w_gate_d[j][None].astype(BF16), w_up_d[j][None].astype(BF16),
                          w_down_d[j][None].astype(BF16))
            wr = None
        final = l == DEPTH - 1

        wm_s = jnp.kron(jnp.eye(n_dec, dtype=F32), p["wm"][:, :n_new, :n_new])
        bs_s = jnp.tile(p["bs"][:n_new], (n_dec, 1))

        (ya, qb, qc, kc, kct, vc, oc, gt, gtt, kbt, vbt) = _proj(
            xp, gmix, p["wp"], p["bg"], lng, lnb, p["wm"].astype(BF16), p["bs"], ROW_BLOCK, GMLP_CHUNK, False,
            l, None if l == 0 else (kbt, vbt))
        yb = _sb_prompt(qb, kbt, vbt, gnb, l)
        c0, n0, m0 = _pad_state(jnp.zeros((1, C_HEADS, C_HEAD_DIM, C_HEAD_DIM), F32),
                                jnp.zeros((1, C_HEADS, C_HEAD_DIM), F32), jnp.zeros((1, C_HEADS), F32))
        yc, c_f, n_f, m_f = _mlstm(qc[None], kc[None], kct[None], vc[None], oc[None], gt[None], gtt[None],
                                   p["gnc"], c0, n0, m0, MLSTM_SUB_BLOCKS)
        merged_p = _merge(xp, ya, yb, yc[0], p["wa"], p["wb"], p["wc"], gffn, wr, ROW_BLOCK)
        c_f, n_f, m_f = _unpad_state(c_f, n_f, m_f)
        outs["cp"].append(c_f)
        outs["np"].append(n_f)
        outs["mp"].append(m_f)

        (ya, qb, qc, kc, kct, vc, oc, gt, gtt, kbf, vbf, kbh, vbh, va) = _proj(
            xs, gmix, p["wp"], p["bg"], lng, lnb, wm_s.astype(BF16), bs_s, n_samp, n_samp, True)
        r3 = lambda a: a.reshape(n_dec, n_new, a.shape[-1])
        yb = _sb_sample(r3(qb), r3(kbh), r3(vbh), cache_kt, cache_vt, gnb, l)
        n_pad = MLSTM_BLOCK - n_new
        padr = lambda a: jnp.pad(r3(a), ((0, 0), (0, n_pad), (0, 0)))
        per_stream = lambda a: jnp.transpose(a.reshape(a.shape[0], n_dec, n_new), (1, 0, 2))
        kct_s = jnp.pad(per_stream(kct), ((0, 0), (0, 0), (0, n_pad)))
        gate_row = jnp.arange(GATE_ROWS)[None, :, None]
        gtt_pad = jnp.broadcast_to(jnp.where(gate_row < C_HEADS, NEG_BIG, 0.0).astype(F32),
                                   (n_dec, GATE_ROWS, n_pad))
        gtt_s = jnp.concatenate([per_stream(gtt), gtt_pad], axis=2)
        gt_s = jnp.concatenate([r3(gt), jnp.broadcast_to(
            jnp.where(jnp.arange(LANES) < C_HEADS, NEG_BIG, 0.0).astype(F32), (n_dec, n_pad, LANES))], axis=1)
        c0, n0, m0 = _pad_state(state_c_mlstm[l], state_n_mlstm[l], state_m_mlstm[l])
        yc, c_u, n_u, m_u = _mlstm(padr(qc), padr(kc), kct_s, padr(vc), padr(oc), gt_s, gtt_s, p["gnc"],
                                   c0, n0, m0, 1)
        yc = yc[:, :n_new].reshape(n_samp, C_WIDTH_PAD)
        merged_s = _merge(xs, ya, yb.reshape(n_samp, B_WIDTH), yc, p["wa"], p["wb"], p["wc"],
                          gffn, wr, n_samp)
        c_u, n_u, m_u = _unpad_state(c_u, n_u, m_u)

        if moe:
            xmid_p, xrow_p, ri_p, rf_p, cnt_p = merged_p
            xmid_s, xrow_s, ri_s, rf_s, cnt_s = merged_s
            n_tiles = 2 * (n_seq + n_samp) // MOE_TILE + N_EXPERTS
            dest_p, dest_s, tile_expert, n_used = _route_plan(ri_p, cnt_p, ri_s, cnt_s, n_tiles)
            xsort = jnp.zeros((n_tiles * MOE_TILE, ROW_TILE, LANES), F32)
            xsort = _dispatch(dest_p, xrow_p, xsort, ROW_BLOCK)
            xsort = _dispatch(dest_s, xrow_s, xsort, n_samp)
            ysort = _ffn_routed(tile_expert, n_used, xsort, wg, wu, wd)
            xp = _combine(dest_p, ysort, rf_p, xmid_p, gfin, MOE_COMBINE_BLOCK, final)
            xs = _combine(dest_s, ysort, rf_s, xmid_s, gfin, n_samp, final)
        else:
            xp = _ffn_dense(merged_p[1], wg, wu, wd, merged_p[0], gfin, ROW_BLOCK, final)
            xs = _ffn_dense(merged_s[1], wg, wu, wd, merged_s[0], gfin, n_samp, final)

        outs["kbs"].append(kbf.reshape(n_dec, n_new, B_HEADS, B_HEAD_DIM))
        outs["vbs"].append(vbf.reshape(n_dec, n_new, B_HEADS, B_HEAD_DIM))
        outs["cs"].append(c_u)
        outs["ns"].append(n_u)
        outs["ms"].append(m_u)
        outs["vas"].append(va.reshape(n_dec, n_new, A_WIDTH))

    st = lambda k: jnp.stack(outs[k])
    heads_last = lambda a: jnp.transpose(a.reshape(DEPTH, 1, B_HEADS, B_HEAD_DIM, n_seq), (0, 1, 4, 2, 3))
    return (xp.reshape(1, n_seq, D_MODEL), xs.reshape(n_dec, n_new, D_MODEL),
            heads_last(kbt), heads_last(vbt), st("cp"), st("np"), st("mp"),
            st("kbs"), st("vbs"), st("cs"), st("ns"), st("ms"), st("vas"))
```

```python
import functools
import math

import jax
import jax.numpy as jnp
from jax import lax
from jax.experimental import pallas as pl
from jax.experimental.pallas import tpu as pltpu

F32 = jnp.float32
BF16 = jnp.bfloat16

D_MODEL = 1024
DEPTH = 2
EPS = 1e-6
CHUNK = 64
A_WIDTH = 256
A_GROUPS = 4
A_GROUP_DIM = 64
GMLP_CHUNK = 128
B_HEAD_DIM = 64
B_WIDTH = 384
B_HEADS = 6
B_PAIRS = 3
C_HEADS = 4
C_HEAD_DIM = 96
C_WIDTH = 384
D_FF = 2816
N_EXPERTS = 8

LANES = 128
C_HEAD_PAD = LANES
C_WIDTH_PAD = C_HEADS * C_HEAD_PAD

OFF_AU, OFF_AV, OFF_BQ, OFF_BK, OFF_BV = 0, 256, 512, 896, 1280
OFF_CQ, OFF_CK, OFF_CV, OFF_CO, OFF_CG = 1664, 2048, 2432, 2816, 3200
P_A = 0
P_BQ = 512
P_BK = P_BQ + B_WIDTH
P_BV = P_BK + B_WIDTH
P_CQ = P_BV + B_WIDTH
P_CK = P_CQ + C_WIDTH_PAD
P_CV = P_CK + C_WIDTH_PAD
P_CO = P_CV + C_WIDTH_PAD
P_CG = P_CO + C_WIDTH_PAD
P_DIM = P_CG + LANES

SB_DEAD_LOG_WEIGHT = -110.0
SB_BLOCK = 256
SB_BLOCKS_PER_STEP = 4
MLSTM_BLOCK = 128
MLSTM_SUB_BLOCKS = 4
GATE_ROWS = 2 * C_HEADS
NEG_BIG = -1e30
ROW_TILE = 8
MOE_TILE = 512
FFN_F_TILE = 1408
ROW_BLOCK = 512
MOE_COMBINE_BLOCK = 256

VMEM_LIMIT = 56 * 1024 * 1024


def _cparams(sem):
    return pltpu.CompilerParams(dimension_semantics=sem, vmem_limit_bytes=VMEM_LIMIT)


def _gelu(x):
    return 0.5 * x * (1.0 + lax.erf(x * (1.0 / math.sqrt(2.0))))


def _log_sigmoid(x):
    return jnp.minimum(x, 0.0) - jnp.log(1.0 + jnp.exp(-jnp.abs(x)))


def _sigmoid(x):
    return 1.0 / (1.0 + jnp.exp(-x))


def _split_bf16(x):
    hi = x.astype(BF16)
    lo = (x - hi.astype(F32)).astype(BF16)
    return hi, lo


def _store_token_tiles(ref, x):
    t = x.shape[0]
    for s in range(D_MODEL // LANES):
        ref[pl.ds(s, t, stride=ROW_TILE), :] = x[:, s * LANES:(s + 1) * LANES]


def _load_token_tiles(ref):
    t = ref.shape[0] // ROW_TILE
    return jnp.concatenate([ref[pl.ds(s, t, stride=ROW_TILE), :] for s in range(D_MODEL // LANES)], axis=1)


def _dot(a, b):
    return jnp.dot(a, b, preferred_element_type=F32)


def _dot_nt(a, b):
    return lax.dot_general(a, b, (((1,), (1,)), ((), ())), preferred_element_type=F32)


def _dot_tn(a, b):
    return lax.dot_general(a, b, (((0,), (0,)), ((), ())), preferred_element_type=F32)


def _proj_kernel(n_chunks, chunk, sample, x_ref, gmix_ref, w_ref, bg_ref, lng_ref, lnb_ref, ws_ref, bs_ref,
                 *refs):
    n_out = 14 if sample else 11
    ya_ref, qb_ref, qc_ref, kc_ref, kct_ref, vc_ref, oc_ref, gt_ref, gtt_ref, *kv_refs = refs[len(refs) - n_out:]
    x = x_ref[...]
    xn = (x * lax.rsqrt(jnp.mean(x * x, axis=-1, keepdims=True) + EPS) * gmix_ref[...]).astype(BF16)

    def proj(off, width):
        return _dot(xn, w_ref[:, off:off + width])

    za = proj(P_A, 2 * A_WIDTH)
    u = _gelu(za[:, :A_WIDTH])
    gv = _gelu(za[:, A_WIDTH:])
    xc = gv - jnp.mean(gv, axis=-1, keepdims=True)
    va = xc * lax.rsqrt(jnp.mean(xc * xc, axis=-1, keepdims=True) + EPS) * lng_ref[...] + lnb_ref[...]
    if sample:
        kv_refs[4][...] = va
    vab = va.astype(BF16)
    lane_group = lax.broadcasted_iota(jnp.int32, (chunk, A_WIDTH), 1) // A_GROUP_DIM
    for c in range(n_chunks):
        rows = slice(c * chunk, (c + 1) * chunk)
        vch = vab[rows]
        s = jnp.zeros((chunk, A_WIDTH), F32)
        for g in range(A_GROUPS):
            s = jnp.where(lane_group == g, _dot(ws_ref[g], vch), s)
        ya_ref[rows, :] = (u[rows] * (s + bs_ref[...])).astype(BF16)

    qb_ref[...] = (proj(P_BQ, B_WIDTH) * (1.0 / math.sqrt(B_HEAD_DIM))).astype(BF16)
    zk = proj(P_BK, B_WIDTH)
    zv = proj(P_BV, B_WIDTH)
    if sample:
        kv_refs[0][...] = zk
        kv_refs[1][...] = zv
        kv_refs[2][...] = zk.astype(BF16)
        kv_refs[3][...] = zv.astype(BF16)
    else:
        kv_refs[0][0] = zk.T
        kv_refs[1][0] = zv.T
        for other in range(1, kv_refs[0].shape[0]):
            kv_refs[0][other] = jnp.zeros(kv_refs[0].shape[1:], F32)
            kv_refs[1][other] = jnp.zeros(kv_refs[1].shape[1:], F32)

    qc_ref[...] = proj(P_CQ, C_WIDTH_PAD).astype(BF16)
    zk = proj(P_CK, C_WIDTH_PAD) * (C_HEAD_DIM ** -0.5)
    kc_ref[...] = zk.astype(BF16)
    kct_ref[...] = zk.T.astype(BF16)
    vc_ref[...] = proj(P_CV, C_WIDTH_PAD).astype(BF16)
    oc_ref[...] = proj(P_CO, C_WIDTH_PAD)
    g = proj(P_CG, LANES) + bg_ref[...]
    lane = lax.broadcasted_iota(jnp.int32, g.shape, 1)
    gates = jnp.where(lane < C_HEADS, g, _log_sigmoid(g))
    gt_ref[...] = gates
    gtt_ref[...] = gates.T[:GATE_ROWS, :]


def _proj(x, gmix, wp, bg, lng, lnb, ws, bs, tm, chunk, sample, layer=0, kv_all=None):
    n = x.shape[0]
    row = lambda w: pl.BlockSpec((tm, w), lambda i: (i, 0))
    full = lambda a: pl.BlockSpec(a.shape, lambda i: (0,) * a.ndim)
    col = lambda h: pl.BlockSpec((h, tm), lambda i: (0, i))
    outs = [((n, A_WIDTH), BF16, row(A_WIDTH)), ((n, B_WIDTH), BF16, row(B_WIDTH)),
            ((n, C_WIDTH_PAD), BF16, row(C_WIDTH_PAD)), ((n, C_WIDTH_PAD), BF16, row(C_WIDTH_PAD)),
            ((C_WIDTH_PAD, n), BF16, col(C_WIDTH_PAD)), ((n, C_WIDTH_PAD), BF16, row(C_WIDTH_PAD)),
            ((n, C_WIDTH_PAD), F32, row(C_WIDTH_PAD)), ((n, LANES), F32, row(LANES)),
            ((GATE_ROWS, n), F32, col(GATE_ROWS))]
    out_shape = [jax.ShapeDtypeStruct(s, dt) for s, dt, _ in outs]
    out_specs = [spec for _, _, spec in outs]
    if sample:
        extra = [(B_WIDTH, F32), (B_WIDTH, F32), (B_WIDTH, BF16), (B_WIDTH, BF16), (A_WIDTH, F32)]
        out_shape += [jax.ShapeDtypeStruct((n, w), dt) for w, dt in extra]
        out_specs += [row(w) for w, _ in extra]
    ins = [x, gmix, wp, bg, lng, lnb, ws, bs]
    in_specs = [row(D_MODEL), full(gmix), full(wp), full(bg), full(lng), full(lnb), full(ws), full(bs)]
    aliases = {}
    if not sample:
        out_shape += [jax.ShapeDtypeStruct((DEPTH, B_WIDTH, n), F32)] * 2
        if kv_all is None:
            out_specs += [pl.BlockSpec((DEPTH, B_WIDTH, tm), lambda i: (0, 0, i))] * 2
        else:
            out_specs += [pl.BlockSpec((1, B_WIDTH, tm), lambda i: (layer, 0, i))] * 2
            aliases = {len(ins): len(out_shape) - 2, len(ins) + 1: len(out_shape) - 1}
            ins += list(kv_all)
            in_specs += [pl.BlockSpec(memory_space=pl.ANY)] * 2
    return pl.pallas_call(
        functools.partial(_proj_kernel, tm // chunk, chunk, sample),
        out_shape=out_shape,
        grid=(n // tm,),
        in_specs=in_specs,
        out_specs=out_specs,
        input_output_aliases=aliases,
        compiler_params=_cparams(("parallel",)),
        name="proj",
    )(*ins)


def _sb_step(qh, kblk, vblk, carry, acc, tri, mask, transposed):
    z = _dot(qh, kblk) if transposed else _dot_nt(qh, kblk)
    tk = z.shape[1]
    drop = jnp.maximum(z, 0.0) + jnp.log(1.0 + jnp.exp(-jnp.abs(z)))
    if mask is not None:
        drop = jnp.where(mask, drop, 0.0)
    hi, lo = _split_bf16(drop)
    cs = _dot(hi, tri) + _dot(lo, tri)
    if tk >= LANES:
        carry_b = jnp.concatenate([carry] * (tk // LANES), axis=1)
    else:
        carry_b = carry[:, :tk]
    a = jnp.exp(z - cs - carry_b)
    if mask is not None:
        a = jnp.where(mask, a, 0.0)
    a = a.astype(BF16)
    acc = acc + (_dot_nt(a, vblk) if transposed else _dot(a, vblk))
    return carry + jnp.broadcast_to(cs[:, :1], carry.shape), acc


def _tri(tk):
    j = lax.broadcasted_iota(jnp.int32, (tk, tk), 0)
    s = lax.broadcasted_iota(jnp.int32, (tk, tk), 1)
    return (j >= s).astype(BF16)


def _sb_finish(acc_s, gn_ref, o_ref, head0):
    out = jnp.where(head0, acc_s[0], acc_s[1])
    r = lax.broadcasted_iota(jnp.int32, (LANES, LANES), 0) // B_HEAD_DIM
    c = lax.broadcasted_iota(jnp.int32, (LANES, LANES), 1) // B_HEAD_DIM
    same_head = (r == c).astype(BF16)
    hi, lo = _split_bf16(out * out)
    ms = (_dot(hi, same_head) + _dot(lo, same_head)) * (1.0 / B_HEAD_DIM)
    return (out * lax.rsqrt(ms + EPS) * gn_ref[...]).astype(o_ref.dtype)


def _sb_walk(qh, load_kv, first_block, carry_s, acc_s, tri):
    def alive():
        return jnp.minimum(jnp.min(carry_s[0]), jnp.min(carry_s[1])) <= -SB_DEAD_LOG_WEIGHT

    def cond(st):
        j, live = st
        return jnp.logical_and(j >= 0, live)

    def body(st):
        j, _ = st
        kblk, vblk = load_kv(j)
        for h in range(2):
            carry, acc = _sb_step(qh[h], kblk, vblk, carry_s[h], acc_s[h], tri, None, True)
            carry_s[h] = carry
            acc_s[h] = acc
        return j - 1, alive()

    lax.while_loop(cond, body, (first_block, alive()))


def _sb_prompt_kernel(q_ref, k_ref, v_ref, gn_ref, o_ref, carry_s, acc_s):
    tq = SB_BLOCK
    n_q = q_ref.shape[0] // tq
    head0 = lax.broadcasted_iota(jnp.int32, (tq, LANES), 1) < B_HEAD_DIM
    tri = _tri(tq)
    t = lax.broadcasted_iota(jnp.int32, (tq, tq), 0)
    s = lax.broadcasted_iota(jnp.int32, (tq, tq), 1)
    causal = s < t
    zeros = jnp.zeros((tq, LANES), F32)

    def load_kv(j):
        cols = pl.ds(pl.multiple_of(j * tq, tq), tq)
        return k_ref[:, cols].astype(BF16), v_ref[:, cols].astype(BF16)

    first = pl.program_id(1) * n_q
    kv = [load_kv(jnp.maximum(first + b - 1, 0)) for b in range(n_q + 1)]
    qhs = []
    for b in range(n_q):
        q = q_ref[b * tq:(b + 1) * tq, :]
        qh = [jnp.where(head0, q, 0), jnp.where(head0, 0, q)]
        qhs.append(qh)
        has_prev = t >= jnp.where(first + b >= 1, 0, tq)
        for h in range(2):
            carry, acc = _sb_step(qh[h], kv[b + 1][0], kv[b + 1][1], zeros, zeros, tri, causal, True)
            carry, acc = _sb_step(qh[h], kv[b][0], kv[b][1], carry, acc, tri, has_prev, True)
            carry_s[b, h] = carry
            acc_s[b, h] = acc
    for b in range(n_q):
        _sb_walk(qhs[b], load_kv, first + b - 2, carry_s.at[b], acc_s.at[b], tri)
        o_ref[b * tq:(b + 1) * tq, :] = _sb_finish(acc_s.at[b], gn_ref, o_ref, head0)


def _sb_prompt(q, k, v, gn, layer):
    n = q.shape[0]
    n_q = SB_BLOCKS_PER_STEP
    tq = SB_BLOCK * n_q
    blk = pl.BlockSpec((tq, LANES), lambda p, i: (i, p))
    seq = pl.BlockSpec((None, LANES, n), lambda p, i: (layer, p, 0))
    state = pltpu.VMEM((n_q, 2, SB_BLOCK, LANES), F32)
    return pl.pallas_call(
        _sb_prompt_kernel,
        out_shape=jax.ShapeDtypeStruct((n, B_WIDTH), BF16),
        grid=(B_PAIRS, n // tq),
        in_specs=[blk, seq, seq, pl.BlockSpec((1, LANES), lambda p, i: (0, p))],
        out_specs=blk,
        scratch_shapes=[state, state],
        compiler_params=_cparams(("parallel", "parallel")),
        name="sb_prompt",
    )(q, k, v, gn)


def _sb_sample_kernel(q_ref, kn_ref, vn_ref, kc_ref, vc_ref, gn_ref, o_ref, carry_s, acc_s):
    tq = q_ref.shape[1]
    tk = SB_BLOCK
    head0 = lax.broadcasted_iota(jnp.int32, (tq, LANES), 1) < B_HEAD_DIM
    q = q_ref[0]
    zero = jnp.zeros_like(q)
    qh = [jnp.where(head0, q, zero), jnp.where(head0, zero, q)]
    t = lax.broadcasted_iota(jnp.int32, (tq, tq), 0)
    s = lax.broadcasted_iota(jnp.int32, (tq, tq), 1)
    causal = s < t
    zeros = jnp.zeros((tq, LANES), F32)
    def load_kv(j):
        cols = pl.ds(pl.multiple_of(j * tk, tk), tk)
        return kc_ref[0, :, cols].astype(BF16), vc_ref[0, :, cols].astype(BF16)

    last = kc_ref.shape[2] // tk - 1
    kp, vp = load_kv(last)
    tri = _tri(tk)
    for h in range(2):
        carry, acc = _sb_step(qh[h], kn_ref[0], vn_ref[0], zeros, zeros, _tri(tq), causal, False)
        carry, acc = _sb_step(qh[h], kp, vp, carry, acc, tri, None, True)
        carry_s[h] = carry
        acc_s[h] = acc
    _sb_walk(qh, load_kv, last - 1, carry_s, acc_s, tri)
    o_ref[0] = _sb_finish(acc_s, gn_ref, o_ref, head0)


def _sb_sample(q, kn, vn, kc, vc, gn, layer):
    nb, tq, _ = q.shape
    past = kc.shape[3]
    new = pl.BlockSpec((1, tq, LANES), lambda b, p: (b, 0, p))
    old = pl.BlockSpec((None, 1, LANES, past), lambda b, p: (layer, b, p, 0))
    return pl.pallas_call(
        _sb_sample_kernel,
        out_shape=jax.ShapeDtypeStruct((nb, tq, B_WIDTH), BF16),
        grid=(nb, B_PAIRS),
        in_specs=[new, new, new, old, old, pl.BlockSpec((1, LANES), lambda b, p: (0, p))],
        out_specs=new,
        scratch_shapes=[pltpu.VMEM((2, tq, LANES), F32), pltpu.VMEM((2, tq, LANES), F32)],
        compiler_params=_cparams(("parallel", "parallel")),
        name="sb_sample",
    )(q, kn, vn, kc, vc, gn)


def _split3(x):
    h1 = x.astype(BF16)
    r1 = x - h1.astype(F32)
    h2 = r1.astype(BF16)
    return h1, h2, (r1 - h2.astype(F32)).astype(BF16)


def _dot3(x, rhs01):
    return _dot(jnp.concatenate(_split3(x), axis=1), jnp.concatenate([rhs01] * 3, axis=0))


def _dot3_left(lhs01, x):
    return _dot(jnp.concatenate([lhs01] * 3, axis=1), jnp.concatenate(_split3(x), axis=0))


def _mlstm_kernel(n_sub, q_ref, k_ref, kt_ref, v_ref, o_ref, gt_ref, gtt_ref, gn_ref, c0_ref, n0_ref, m0_ref,
                  yc_ref, c_out, n_out, m_out, c_s, n_s, m_s):
    L = MLSTM_BLOCK
    t_blk = pl.program_id(1)

    @pl.when(t_blk == 0)
    def _():
        c_s[...] = c0_ref[0]
        n_s[...] = n0_ref[0]
        m_s[...] = m0_ref[0]

    r = lax.broadcasted_iota(jnp.int32, (L, L), 0)
    c = lax.broadcasted_iota(jnp.int32, (L, L), 1)
    causal = c <= r
    upper = (r <= c).astype(BF16)
    ones_sq = jnp.ones((L, LANES), BF16)
    sel_r = lax.broadcasted_iota(jnp.int32, (LANES, C_WIDTH_PAD), 0)
    sel_c = lax.broadcasted_iota(jnp.int32, (LANES, C_WIDTH_PAD), 1) // C_HEAD_PAD
    sel_p = (sel_r == sel_c).astype(BF16)
    sel_b = (sel_r == sel_c + C_HEADS).astype(BF16)
    lane = c

    a_rows_all, pb_cols_all = [], []
    for sub in range(n_sub):
        rows = slice(sub * L, (sub + 1) * L)
        gtt = gtt_ref[0, :, rows]
        bct = _dot3(gtt, upper)
        a_rows_all.append(gtt[:C_HEADS] - bct[C_HEADS:])
        gt = gt_ref[0, rows, :]
        bc = _dot3_left(causal.astype(BF16), gt)
        pmax = gt - pltpu.roll(bc, LANES - C_HEADS, axis=1)
        for sh in (1, 2, 4, 8, 16, 32, 64):
            pmax = jnp.maximum(pmax, jnp.where(r >= sh, pltpu.roll(pmax, sh, axis=0), -jnp.inf))
        pb_cols_all.append(jnp.where(lane < C_HEADS, pmax, bc))
    pb_cols_all = jnp.concatenate(pb_cols_all, axis=0)
    p_all = _dot3(pb_cols_all, sel_p)
    b_all = _dot3(pb_cols_all, sel_b)

    state = [(c_s[h], n_s[h], m_s[h]) for h in range(C_HEADS)]
    for sub in range(n_sub):
        rows = slice(sub * L, (sub + 1) * L)
        a_rows = a_rows_all[sub]
        heads = []
        for h in range(C_HEADS):
            lanes = slice(h * C_HEAD_PAD, (h + 1) * C_HEAD_PAD)
            q = q_ref[0, rows, lanes]
            v1 = jnp.concatenate([v_ref[0, rows, lanes], ones_sq], axis=1)
            p_rep = p_all[rows, lanes]
            b_rep = b_all[rows, lanes]
            a_row = a_rows[h:h + 1, :]
            p_last = p_rep[L - 1:L, :]
            c_prev, n_prev, m_prev = state[h]

            w = jnp.exp(jnp.where(causal, a_row - p_rep, -jnp.inf)) * _dot_nt(q, k_ref[0, rows, lanes])
            kwt = (kt_ref[0, lanes, rows].astype(F32) * jnp.exp(a_row - p_last)).astype(BF16)
            both = _dot(jnp.concatenate([w.astype(BF16), kwt], axis=0), v1)
            here_sums = both[:L]
            fresh = both[L:]
            past_sums = _dot(q, jnp.concatenate([c_prev, n_prev], axis=1).astype(BF16))
            top = jnp.maximum(m_prev, p_rep)
            past = jnp.exp(m_prev - top)
            here = jnp.exp(p_rep - top)
            mix = jnp.concatenate([past, past], axis=1) * past_sums + jnp.concatenate([here, here], axis=1) * here_sums
            den = jnp.maximum(jnp.abs(mix[:, LANES:]), jnp.exp(-(b_rep + top)))
            heads.append(mix[:, :LANES] / den)

            top_last = jnp.maximum(m_prev, p_last)
            decay = jnp.exp(m_prev - top_last)
            gain = jnp.exp(p_last - top_last)
            state[h] = (decay * c_prev + gain * fresh[:, :LANES], decay * n_prev + gain * fresh[:, LANES:],
                        b_rep[L - 1:L, :] + top_last)

        hh = jnp.concatenate(heads, axis=0)
        ms = _dot(jnp.concatenate(_split_bf16(hh * hh), axis=1),
                  jnp.concatenate([ones_sq, ones_sq], axis=0)) * (1.0 / C_HEAD_DIM)
        hn = hh * lax.rsqrt(ms + EPS)
        for h in range(C_HEADS):
            lanes = slice(h * C_HEAD_PAD, (h + 1) * C_HEAD_PAD)
            yc_ref[0, rows, lanes] = (hn[h * L:(h + 1) * L] * gn_ref[:, lanes]
                                      * _sigmoid(o_ref[0, rows, lanes])).astype(BF16)

    for h in range(C_HEADS):
        c_s[h], n_s[h], m_s[h] = state[h]

    @pl.when(t_blk == pl.num_programs(1) - 1)
    def _():
        c_out[0] = c_s[...]
        n_out[0] = n_s[...]
        m_out[0] = m_s[...]


def _mlstm(q, k, kt, v, o, gt, gtt, gn, c0, n0, m0, n_sub):
    nb, n, _ = q.shape
    tb = n_sub * MLSTM_BLOCK
    seq = lambda w: pl.BlockSpec((1, tb, w), lambda b, t: (b, t, 0))
    seq_t = lambda h: pl.BlockSpec((1, h, tb), lambda b, t: (b, 0, t))
    st = lambda a: pl.BlockSpec((1,) + a.shape[1:], lambda b, t: (b,) + (0,) * (a.ndim - 1))
    return pl.pallas_call(
        functools.partial(_mlstm_kernel, n_sub),
        out_shape=[jax.ShapeDtypeStruct((nb, n, C_WIDTH_PAD), BF16),
                   jax.ShapeDtypeStruct(c0.shape, F32),
                   jax.ShapeDtypeStruct(n0.shape, F32),
                   jax.ShapeDtypeStruct(m0.shape, F32)],
        grid=(nb, n // tb),
        in_specs=[seq(C_WIDTH_PAD), seq(C_WIDTH_PAD), seq_t(C_WIDTH_PAD), seq(C_WIDTH_PAD), seq(C_WIDTH_PAD),
                  seq(LANES), seq_t(GATE_ROWS), pl.BlockSpec((1, C_WIDTH_PAD), lambda b, t: (0, 0)),
                  st(c0), st(n0), st(m0)],
        out_specs=[seq(C_WIDTH_PAD), st(c0), st(n0), st(m0)],
        scratch_shapes=[pltpu.VMEM(c0.shape[1:], F32), pltpu.VMEM(n0.shape[1:], F32),
                        pltpu.VMEM(m0.shape[1:], F32)],
        compiler_params=_cparams(("parallel", "arbitrary")),
        name="mlstm",
    )(q, k, kt, v, o, gt, gtt, gn, c0, n0, m0)


def _mixer_out(x_ref, ya_ref, yb_ref, yc_ref, wa_ref, wb_ref, wc_ref, g_ref, xmid_ref):
    y = _dot(ya_ref[...], wa_ref[...]) + _dot(yb_ref[...], wb_ref[...]) + _dot(yc_ref[...], wc_ref[...])
    x = x_ref[...] + y
    xmid_ref[...] = x
    return x * lax.rsqrt(jnp.mean(x * x, axis=-1, keepdims=True) + EPS) * g_ref[...]


def _merge_dense_kernel(x_ref, ya_ref, yb_ref, yc_ref, wa_ref, wb_ref, wc_ref, g_ref, xmid_ref, xn_ref):
    xn_ref[...] = _mixer_out(x_ref, ya_ref, yb_ref, yc_ref, wa_ref, wb_ref, wc_ref, g_ref,
                             xmid_ref).astype(BF16)


def _merge_moe_kernel(x_ref, ya_ref, yb_ref, yc_ref, wa_ref, wb_ref, wc_ref, g_ref, wrt_ref,
                      earlier_ref, xmid_ref, xrow_ref, ri_ref, rf_ref, cnt_ref, run_s):
    tm = x_ref.shape[0]

    @pl.when(pl.program_id(0) == 0)
    def _():
        run_s[...] = jnp.zeros_like(run_s)

    xn = _mixer_out(x_ref, ya_ref, yb_ref, yc_ref, wa_ref, wb_ref, wc_ref, g_ref, xmid_ref)
    _store_token_tiles(xrow_ref, xn)

    hi, lo = _split_bf16(xn)
    part = _dot_nt(wrt_ref[...], hi)
    lg = part[:N_EXPERTS] + part[N_EXPERTS:] + _dot_nt(wrt_ref[:N_EXPERTS, :], lo)
    expert = lax.broadcasted_iota(jnp.int32, lg.shape, 0)
    m1 = jnp.max(lg, axis=0, keepdims=True)
    i1 = jnp.min(jnp.where(lg == m1, expert, N_EXPERTS), axis=0, keepdims=True)
    lg2 = jnp.where(expert == i1, -jnp.inf, lg)
    m2 = jnp.max(lg2, axis=0, keepdims=True)
    i2 = jnp.min(jnp.where(lg2 == m2, expert, N_EXPERTS), axis=0, keepdims=True)
    e2 = jnp.exp(m2 - m1)
    g1 = 1.0 / (1.0 + e2)
    g2 = e2 * g1

    sel1 = expert == i1
    sel2 = expert == i2
    onehot = jnp.logical_or(sel1, sel2).astype(BF16)
    before = _dot(onehot, earlier_ref[...]) + jnp.concatenate([run_s[...]] * (tm // LANES), axis=1)
    rank1 = jnp.sum(jnp.where(sel1, before, 0.0), axis=0, keepdims=True).astype(jnp.int32)
    rank2 = jnp.sum(jnp.where(sel2, before, 0.0), axis=0, keepdims=True).astype(jnp.int32)
    run_s[...] += _dot(onehot, jnp.ones((tm, LANES), BF16))
    cnt_ref[...] = run_s[...].astype(jnp.int32)
    ri_ref[...] = jnp.where(expert == 0, i1, jnp.where(expert == 1, i2,
                            jnp.where(expert == 2, rank1, jnp.where(expert == 3, rank2, 0))))
    rf_ref[...] = jnp.where(expert == 0, g1, jnp.where(expert == 1, g2, 0.0))


def _merge(x, ya, yb, yc, wa, wb, wc, g, wr, tm):
    n = x.shape[0]
    row = lambda w: pl.BlockSpec((tm, w), lambda i: (i, 0))
    full = lambda a: pl.BlockSpec(a.shape, lambda i: (0,) * a.ndim)
    ins = [x, ya, yb, yc, wa, wb, wc, g]
    in_specs = [row(D_MODEL), row(A_WIDTH), row(B_WIDTH), row(C_WIDTH_PAD), full(wa), full(wb), full(wc), full(g)]
    if wr is None:
        return pl.pallas_call(
            _merge_dense_kernel,
            out_shape=[jax.ShapeDtypeStruct((n, D_MODEL), F32), jax.ShapeDtypeStruct((n, D_MODEL), BF16)],
            grid=(n // tm,), in_specs=in_specs, out_specs=[row(D_MODEL), row(D_MODEL)],
            compiler_params=_cparams(("parallel",)), name="merge_dense",
        )(*ins)
    earlier = (jnp.arange(tm)[:, None] < jnp.arange(tm)[None, :]).astype(BF16)
    col = pl.BlockSpec((N_EXPERTS, tm), lambda i: (0, i))
    return pl.pallas_call(
        _merge_moe_kernel,
        out_shape=[jax.ShapeDtypeStruct((n, D_MODEL), F32),
                   jax.ShapeDtypeStruct((n * ROW_TILE, LANES), F32),
                   jax.ShapeDtypeStruct((N_EXPERTS, n), jnp.int32),
                   jax.ShapeDtypeStruct((N_EXPERTS, n), F32),
                   jax.ShapeDtypeStruct((N_EXPERTS, LANES), jnp.int32)],
        grid=(n // tm,), in_specs=in_specs + [full(wr), full(earlier)],
        out_specs=[row(D_MODEL), pl.BlockSpec((tm * ROW_TILE, LANES), lambda i: (i, 0)),
                   col, col, pl.BlockSpec((N_EXPERTS, LANES), lambda i: (0, 0))],
        scratch_shapes=[pltpu.VMEM((N_EXPERTS, LANES), F32)],
        compiler_params=_cparams(("arbitrary",)), name="merge_moe",
    )(*ins, wr, earlier)


def _dispatch_kernel(dest_ref, src_ref, xs_in_ref, xs_ref, sem):
    del xs_in_ref
    tt = dest_ref.shape[2] // 2

    def issue(t, carry):
        rows = pl.ds(pl.multiple_of(t * ROW_TILE, ROW_TILE), ROW_TILE)
        for k in range(2):
            pltpu.make_async_copy(src_ref.at[rows], xs_ref.at[dest_ref[0, 0, 2 * t + k]], sem).start(priority=k)
        return carry

    lax.fori_loop(0, tt, issue, 0, unroll=8)

    def drain(t, carry):
        for k in range(2):
            pltpu.make_async_copy(src_ref.at[pl.ds(0, ROW_TILE)], xs_ref.at[0], sem).wait()
        return carry

    lax.fori_loop(0, tt, drain, 0, unroll=8)


def _dispatch(dest, src, xs, tt):
    n = src.shape[0] // ROW_TILE
    dest3 = dest.reshape(n // tt, 1, 2 * tt)
    return pl.pallas_call(
        _dispatch_kernel,
        out_shape=jax.ShapeDtypeStruct(xs.shape, xs.dtype),
        grid=(n // tt,),
        in_specs=[pl.BlockSpec((1, 1, 2 * tt), lambda i: (i, 0, 0), memory_space=pltpu.SMEM),
                  pl.BlockSpec((tt * ROW_TILE, LANES), lambda i: (i, 0)), pl.BlockSpec(memory_space=pl.ANY)],
        out_specs=pl.BlockSpec(memory_space=pl.ANY),
        scratch_shapes=[pltpu.SemaphoreType.DMA(())],
        input_output_aliases={2: 0},
        compiler_params=_cparams(("arbitrary",)), name="moe_dispatch",
    )(dest3, src, xs)


def _combine_kernel(final, dest_ref, next_ref, ys_ref, rf_ref, xmid_ref, gf_ref, o_ref, buf_s, sems):
    tt = xmid_ref.shape[0]
    i = pl.program_id(0)
    n_steps = pl.num_programs(0)

    def gather(idx_ref, slot):
        def issue(t, carry):
            rows = pl.ds(pl.multiple_of(t * ROW_TILE, ROW_TILE), ROW_TILE)
            for k in range(2):
                pltpu.make_async_copy(ys_ref.at[idx_ref[0, 0, 2 * t + k]], buf_s.at[slot, k, rows],
                                      sems.at[slot]).start(priority=k)
            return carry

        lax.fori_loop(0, tt, issue, 0, unroll=8)

    slot = i % 2

    @pl.when(i == 0)
    def _():
        gather(dest_ref, 0)

    @pl.when(i + 1 < n_steps)
    def _():
        gather(next_ref, 1 - slot)

    def drain(t, carry):
        for k in range(2):
            pltpu.make_async_copy(ys_ref.at[0], buf_s.at[slot, 0, pl.ds(0, ROW_TILE)], sems.at[slot]).wait()
        return carry

    lax.fori_loop(0, tt, drain, 0, unroll=8)

    y = (xmid_ref[...] + rf_ref[:, 0:1] * _load_token_tiles(buf_s.at[slot, 0])
         + rf_ref[:, 1:2] * _load_token_tiles(buf_s.at[slot, 1]))
    if final:
        y = y * lax.rsqrt(jnp.mean(y * y, axis=-1, keepdims=True) + EPS) * gf_ref[...]
    o_ref[...] = y


def _combine(dest, ys, rf, xmid, gf, tt, final):
    n = xmid.shape[0]
    n_steps = n // tt
    dest3 = dest.reshape(n_steps, 1, 2 * tt)
    row = lambda w: pl.BlockSpec((tt, w), lambda i: (i, 0))
    return pl.pallas_call(
        functools.partial(_combine_kernel, final),
        out_shape=jax.ShapeDtypeStruct((n, D_MODEL), F32),
        grid=(n_steps,),
        in_specs=[pl.BlockSpec((1, 1, 2 * tt), lambda i: (i, 0, 0), memory_space=pltpu.SMEM),
                  pl.BlockSpec((1, 1, 2 * tt), lambda i: (jnp.minimum(i + 1, n_steps - 1), 0, 0),
                               memory_space=pltpu.SMEM),
                  pl.BlockSpec(memory_space=pl.ANY), row(LANES), row(D_MODEL),
                  pl.BlockSpec((1, D_MODEL), lambda i: (0, 0))],
        out_specs=row(D_MODEL),
        scratch_shapes=[pltpu.VMEM((2, 2, tt * ROW_TILE, LANES), F32), pltpu.SemaphoreType.DMA((2,))],
        compiler_params=_cparams(("arbitrary",)), name="moe_combine",
    )(dest3, dest3, ys, rf, xmid, gf)


def _swiglu_acc(xn, wg_ref, wu_ref, wd_ref, acc_s):
    g = _dot(xn, wg_ref[0])
    u = _dot(xn, wu_ref[0])
    acc_s[...] += _dot((g * _sigmoid(g) * u).astype(BF16), wd_ref[0])


def _ffn_dense_kernel(final, xn_ref, wg_ref, wu_ref, wd_ref, xmid_ref, gf_ref, o_ref, acc_s):
    f = pl.program_id(1)

    @pl.when(f == 0)
    def _():
        acc_s[...] = jnp.zeros_like(acc_s)

    _swiglu_acc(xn_ref[...], wg_ref, wu_ref, wd_ref, acc_s)

    @pl.when(f == pl.num_programs(1) - 1)
    def _():
        y = xmid_ref[...] + acc_s[...]
        if final:
            y = y * lax.rsqrt(jnp.mean(y * y, axis=-1, keepdims=True) + EPS) * gf_ref[...]
        o_ref[...] = y


def _ffn_dense(xn, wg, wu, wd, xmid, gf, tm, final):
    n = xn.shape[0]
    tf = FFN_F_TILE
    row = lambda w: pl.BlockSpec((tm, w), lambda i, f: (i, 0))
    return pl.pallas_call(
        functools.partial(_ffn_dense_kernel, final),
        out_shape=jax.ShapeDtypeStruct((n, D_MODEL), F32),
        grid=(n // tm, D_FF // tf),
        in_specs=[row(D_MODEL),
                  pl.BlockSpec((1, D_MODEL, tf), lambda i, f: (0, 0, f)),
                  pl.BlockSpec((1, D_MODEL, tf), lambda i, f: (0, 0, f)),
                  pl.BlockSpec((1, tf, D_MODEL), lambda i, f: (0, f, 0)),
                  row(D_MODEL), pl.BlockSpec((1, D_MODEL), lambda i, f: (0, 0))],
        out_specs=row(D_MODEL),
        scratch_shapes=[pltpu.VMEM((tm, D_MODEL), F32)],
        compiler_params=_cparams(("parallel", "arbitrary")),
        name="ffn_dense",
    )(xn, wg, wu, wd, xmid, gf)


def _ffn_routed_kernel(te_ref, nu_ref, xs_ref, wg_ref, wu_ref, wd_ref, ys_ref, xb_s, acc_s):
    del te_ref
    i = pl.program_id(0)
    f = pl.program_id(1)

    @pl.when(i < nu_ref[0])
    def _():
        @pl.when(f == 0)
        def _():
            acc_s[...] = jnp.zeros_like(acc_s)
            xb_s[...] = _load_token_tiles(xs_ref).astype(BF16)

        _swiglu_acc(xb_s[...], wg_ref, wu_ref, wd_ref, acc_s)

        @pl.when(f == pl.num_programs(1) - 1)
        def _():
            _store_token_tiles(ys_ref, acc_s[...])

    @pl.when(jnp.logical_and(i >= nu_ref[0], f == 0))
    def _():
        ys_ref[...] = jnp.zeros_like(ys_ref)


def _ffn_routed(tile_expert, n_used, xs, wg, wu, wd):
    n_tiles = tile_expert.shape[0]
    tm, tf = MOE_TILE, FFN_F_TILE
    n_f = D_FF // tf
    last = lambda i, nu: jnp.minimum(i, nu[0] - 1)
    fcol = lambda i, f, nu: jnp.where(i < nu[0], f, n_f - 1)
    return pl.pallas_call(
        _ffn_routed_kernel,
        out_shape=jax.ShapeDtypeStruct(xs.shape, F32),
        grid_spec=pltpu.PrefetchScalarGridSpec(
            num_scalar_prefetch=2,
            grid=(n_tiles, n_f),
            in_specs=[pl.BlockSpec((tm * ROW_TILE, LANES), lambda i, f, te, nu: (last(i, nu), 0)),
                      pl.BlockSpec((1, D_MODEL, tf), lambda i, f, te, nu: (te[last(i, nu)], 0, fcol(i, f, nu))),
                      pl.BlockSpec((1, D_MODEL, tf), lambda i, f, te, nu: (te[last(i, nu)], 0, fcol(i, f, nu))),
                      pl.BlockSpec((1, tf, D_MODEL), lambda i, f, te, nu: (te[last(i, nu)], fcol(i, f, nu), 0))],
            out_specs=pl.BlockSpec((tm * ROW_TILE, LANES), lambda i, f, te, nu: (i, 0)),
            scratch_shapes=[pltpu.VMEM((tm, D_MODEL), BF16), pltpu.VMEM((tm, D_MODEL), F32)]),
        compiler_params=_cparams(("arbitrary", "arbitrary")),
        name="ffn_routed",
    )(tile_expert, n_used, xs, wg, wu, wd)


def _route_plan(ri_p, cnt_p, ri_s, cnt_s, n_tiles):
    cnt_p, cnt_s = cnt_p[:, 0], cnt_s[:, 0]
    tiles = (cnt_p + cnt_s + MOE_TILE - 1) // MOE_TILE
    ends = jnp.cumsum(tiles)
    start = (ends - tiles) * MOE_TILE
    lookup = lambda table, idx: jnp.sum(
        jnp.where(idx[..., None] == jnp.arange(N_EXPERTS, dtype=jnp.int32), table, 0), axis=-1)
    dest_p = (lookup(start, ri_p[0:2]) + ri_p[2:4]).T
    dest_s = (lookup(start + cnt_p, ri_s[0:2]) + ri_s[2:4]).T
    tile_expert = jnp.minimum(jnp.sum(jnp.arange(n_tiles, dtype=jnp.int32)[:, None] >= ends[None, :], axis=-1),
                              N_EXPERTS - 1).astype(jnp.int32)
    return dest_p, dest_s, tile_expert, ends[-1:].astype(jnp.int32)


def _pad_heads_cols(w):
    w = w.reshape(w.shape[0], C_HEADS, C_HEAD_DIM)
    return jnp.pad(w, ((0, 0), (0, 0), (0, C_HEAD_PAD - C_HEAD_DIM))).reshape(w.shape[0], C_WIDTH_PAD)


def _layer_params(l, w_in, b_gate, w_s, b_s, gn_c, w_out):
    w = w_in[l]
    gates = jnp.pad(w[:, OFF_CG:OFF_CG + 2 * C_HEADS], ((0, 0), (0, LANES - 2 * C_HEADS)))
    wp = jnp.concatenate(
        [w[:, :OFF_CQ]] + [_pad_heads_cols(w[:, o:o + C_WIDTH]) for o in (OFF_CQ, OFF_CK, OFF_CV, OFF_CO)]
        + [gates], axis=1).astype(BF16)
    bg = jnp.pad(b_gate[l], (0, LANES - 2 * C_HEADS))[None, :]
    pos = jnp.arange(GMLP_CHUNK)
    mask = (pos[None, :] // CHUNK) <= (pos[:, None] // CHUNK)
    wm = jnp.where(mask[None], w_s[l], 0.0)
    bs = jnp.repeat(b_s[l].T, A_GROUP_DIM, axis=1)
    wo = w_out[l]
    wc = wo[A_WIDTH + B_WIDTH:].reshape(C_HEADS, C_HEAD_DIM, D_MODEL)
    wc = jnp.pad(wc, ((0, 0), (0, C_HEAD_PAD - C_HEAD_DIM), (0, 0))).reshape(C_WIDTH_PAD, D_MODEL)
    gnc = _pad_heads_cols(gn_c[l][None, :])
    return dict(wp=wp, bg=bg, wm=wm, bs=bs, wa=wo[:A_WIDTH].astype(BF16),
                wb=wo[A_WIDTH:A_WIDTH + B_WIDTH].astype(BF16), wc=wc.astype(BF16), gnc=gnc)


def _pad_state(c, n, m):
    p = C_HEAD_PAD - C_HEAD_DIM
    c = jnp.pad(c, ((0, 0), (0, 0), (0, p), (0, p)))
    n = jnp.pad(n, ((0, 0), (0, 0), (0, p)))
    n = jnp.broadcast_to(n[:, :, :, None], n.shape + (LANES,))
    m = jnp.broadcast_to(m[:, :, None, None], m.shape + (1, LANES))
    return c, n, m


def _unpad_state(c, n, m):
    return c[:, :, :C_HEAD_DIM, :C_HEAD_DIM], n[:, :, :C_HEAD_DIM, 0], m[:, :, 0, 0]


def kernel(x_prompt, x_sample, cache_k_b, cache_v_b, state_c_mlstm, state_n_mlstm, state_m_mlstm,
           g_mix, w_in, b_gate, ln_a_g, ln_a_b, w_s, b_s, gn_b, gn_c, w_out,
           g_ffn, w_gate_d, w_up_d, w_down_d, w_router, w_gate_e, w_up_e, w_down_e, g_final):
    n_seq = x_prompt.shape[1]
    n_dec, n_new = x_sample.shape[0], x_sample.shape[1]
    past = cache_k_b.shape[2]
    n_samp = n_dec * n_new

    xp = x_prompt.reshape(n_seq, D_MODEL)
    xs = x_sample.reshape(n_samp, D_MODEL)
    gfin = g_final[None, :]
    keys_last = lambda a: jnp.transpose(a, (0, 1, 3, 4, 2)).reshape(DEPTH, n_dec, B_WIDTH, past)
    cache_kt, cache_vt = keys_last(cache_k_b), keys_last(cache_v_b)

    outs = {k: [] for k in ("kbp", "vbp", "cp", "np", "mp", "kbs", "vbs", "cs", "ns", "ms", "vas")}
    for l in range(DEPTH):
        p = _layer_params(l, w_in, b_gate, w_s, b_s, gn_c, w_out)
        gmix = g_mix[l][None, :]
        lng, lnb = ln_a_g[l][None, :], ln_a_b[l][None, :]
        gnb = gn_b[l][None, :]
        gffn = g_ffn[l][None, :]
        moe = l % 2 == 1
        j = l // 2
        if moe:
            wg, wu, wd = w_gate_e[j].astype(BF16), w_up_e[j].astype(BF16), w_down_e[j].astype(BF16)
            wr = jnp.concatenate(_split_bf16(w_router[j].T), axis=0)
        else:
            wg, wu, wd = (w_gate_d[j][None].astype(BF16), w_up_d[j][None].astype(BF16),
                          w_down_d[j][None].astype(BF16))
            wr = None
        final = l == DEPTH - 1

        wm_s = jnp.kron(jnp.eye(n_dec, dtype=F32), p["wm"][:, :n_new, :n_new])
        bs_s = jnp.tile(p["bs"][:n_new], (n_dec, 1))

        (ya, qb, qc, kc, kct, vc, oc, gt, gtt, kbt, vbt) = _proj(
            xp, gmix, p["wp"], p["bg"], lng, lnb, p["wm"].astype(BF16), p["bs"], ROW_BLOCK, GMLP_CHUNK, False,
            l, None if l == 0 else (kbt, vbt))
        yb = _sb_prompt(qb, kbt, vbt, gnb, l)
        c0, n0, m0 = _pad_state(jnp.zeros((1, C_HEADS, C_HEAD_DIM, C_HEAD_DIM), F32),
                                jnp.zeros((1, C_HEADS, C_HEAD_DIM), F32), jnp.zeros((1, C_HEADS), F32))
        yc, c_f, n_f, m_f = _mlstm(qc[None], kc[None], kct[None], vc[None], oc[None], gt[None], gtt[None],
                                   p["gnc"], c0, n0, m0, MLSTM_SUB_BLOCKS)
        merged_p = _merge(xp, ya, yb, yc[0], p["wa"], p["wb"], p["wc"], gffn, wr, ROW_BLOCK)
        c_f, n_f, m_f = _unpad_state(c_f, n_f, m_f)
        outs["cp"].append(c_f)
        outs["np"].append(n_f)
        outs["mp"].append(m_f)

        (ya, qb, qc, kc, kct, vc, oc, gt, gtt, kbf, vbf, kbh, vbh, va) = _proj(
            xs, gmix, p["wp"], p["bg"], lng, lnb, wm_s.astype(BF16), bs_s, n_samp, n_samp, True)
        r3 = lambda a: a.reshape(n_dec, n_new, a.shape[-1])
        yb = _sb_sample(r3(qb), r3(kbh), r3(vbh), cache_kt, cache_vt, gnb, l)
        n_pad = MLSTM_BLOCK - n_new
        padr = lambda a: jnp.pad(r3(a), ((0, 0), (0, n_pad), (0, 0)))
        per_stream = lambda a: jnp.transpose(a.reshape(a.shape[0], n_dec, n_new), (1, 0, 2))
        kct_s = jnp.pad(per_stream(kct), ((0, 0), (0, 0), (0, n_pad)))
        gate_row = jnp.arange(GATE_ROWS)[None, :, None]
        gtt_pad = jnp.broadcast_to(jnp.where(gate_row < C_HEADS, NEG_BIG, 0.0).astype(F32),
                                   (n_dec, GATE_ROWS, n_pad))
        gtt_s = jnp.concatenate([per_stream(gtt), gtt_pad], axis=2)
        gt_s = jnp.concatenate([r3(gt), jnp.broadcast_to(
            jnp.where(jnp.arange(LANES) < C_HEADS, NEG_BIG, 0.0).astype(F32), (n_dec, n_pad, LANES))], axis=1)
        c0, n0, m0 = _pad_state(state_c_mlstm[l], state_n_mlstm[l], state_m_mlstm[l])
        yc, c_u, n_u, m_u = _mlstm(padr(qc), padr(kc), kct_s, padr(vc), padr(oc), gt_s, gtt_s, p["gnc"],
                                   c0, n0, m0, 1)
        yc = yc[:, :n_new].reshape(n_samp, C_WIDTH_PAD)
        merged_s = _merge(xs, ya, yb.reshape(n_samp, B_WIDTH), yc, p["wa"], p["wb"], p["wc"],
                          gffn, wr, n_samp)
        c_u, n_u, m_u = _unpad_state(c_u, n_u, m_u)

        if moe:
            xmid_p, xrow_p, ri_p, rf_p, cnt_p = merged_p
            xmid_s, xrow_s, ri_s, rf_s, cnt_s = merged_s
            n_tiles = 2 * (n_seq + n_samp) // MOE_TILE + N_EXPERTS
            dest_p, dest_s, tile_expert, n_used = _route_plan(ri_p, cnt_p, ri_s, cnt_s, n_tiles)
            as_tiles = lambda a: a.reshape(a.shape[0] // ROW_TILE, ROW_TILE, LANES)
            gates = lambda rf: jnp.pad(rf[:2].T, ((0, 0), (0, LANES - 2)))
            xsort = jnp.zeros((n_tiles * MOE_TILE, ROW_TILE, LANES), F32)
            xsort = _dispatch(dest_p, xrow_p, xsort, ROW_BLOCK)
            xsort = _dispatch(dest_s, xrow_s, xsort, n_samp)
            ysort = _ffn_routed(tile_expert, n_used, xsort.reshape(-1, LANES), wg, wu, wd)
            xp = _combine(dest_p, as_tiles(ysort), gates(rf_p), xmid_p, gfin, MOE_COMBINE_BLOCK, final)
            xs = _combine(dest_s, as_tiles(ysort), gates(rf_s), xmid_s, gfin, n_samp, final)
        else:
            xp = _ffn_dense(merged_p[1], wg, wu, wd, merged_p[0], gfin, ROW_BLOCK, final)
            xs = _ffn_dense(merged_s[1], wg, wu, wd, merged_s[0], gfin, n_samp, final)

        outs["kbs"].append(kbf.reshape(n_dec, n_new, B_HEADS, B_HEAD_DIM))
        outs["vbs"].append(vbf.reshape(n_dec, n_new, B_HEADS, B_HEAD_DIM))
        outs["cs"].append(c_u)
        outs["ns"].append(n_u)
        outs["ms"].append(m_u)
        outs["vas"].append(va.reshape(n_dec, n_new, A_WIDTH))

    st = lambda k: jnp.stack(outs[k])
    heads_last = lambda a: jnp.transpose(a.reshape(DEPTH, 1, B_HEADS, B_HEAD_DIM, n_seq), (0, 1, 4, 2, 3))
    return (xp.reshape(1, n_seq, D_MODEL), xs.reshape(n_dec, n_new, D_MODEL),
            heads_last(kbt), heads_last(vbt), st("cp"), st("np"), st("mp"),
            st("kbs"), st("vbs"), st("cs"), st("ns"), st("ms"), st("vas"))
```

```python
import functools
import math

import jax
import jax.numpy as jnp
from jax import lax
from jax.experimental import pallas as pl
from jax.experimental.pallas import tpu as pltpu

F32 = jnp.float32
BF16 = jnp.bfloat16

D_MODEL = 1024
DEPTH = 2
EPS = 1e-6
CHUNK = 64
A_WIDTH = 256
A_GROUPS = 4
A_GROUP_DIM = 64
GMLP_CHUNK = 128
B_HEAD_DIM = 64
B_WIDTH = 384
B_HEADS = 6
B_PAIRS = 3
C_HEADS = 4
C_HEAD_DIM = 96
C_WIDTH = 384
D_FF = 2816
N_EXPERTS = 8

LANES = 128
C_HEAD_PAD = LANES
C_WIDTH_PAD = C_HEADS * C_HEAD_PAD

OFF_AU, OFF_AV, OFF_BQ, OFF_BK, OFF_BV = 0, 256, 512, 896, 1280
OFF_CQ, OFF_CK, OFF_CV, OFF_CO, OFF_CG = 1664, 2048, 2432, 2816, 3200
P_A = 0
P_BQ = 512
P_BK = P_BQ + B_WIDTH
P_BV = P_BK + B_WIDTH
P_CQ = P_BV + B_WIDTH
P_CK = P_CQ + C_WIDTH_PAD
P_CV = P_CK + C_WIDTH_PAD
P_CO = P_CV + C_WIDTH_PAD
P_CG = P_CO + C_WIDTH_PAD
P_DIM = P_CG + LANES

SB_DEAD_LOG_WEIGHT = -110.0
SB_BLOCK = 256
SB_BLOCKS_PER_STEP = 8
MLSTM_BLOCK = 128
MLSTM_SUB_BLOCKS = 8
GATE_ROWS = 2 * C_HEADS
NEG_BIG = -1e30
ROW_TILE = 8
MOE_TILE = 512
FFN_F_TILE = 1408
ROW_BLOCK = 512
MOE_COMBINE_BLOCK = 256

VMEM_LIMIT = 56 * 1024 * 1024


def _cparams(sem):
    return pltpu.CompilerParams(dimension_semantics=sem, vmem_limit_bytes=VMEM_LIMIT)


def _gelu(x):
    return 0.5 * x * (1.0 + lax.erf(x * (1.0 / math.sqrt(2.0))))


def _log_sigmoid(x):
    return jnp.minimum(x, 0.0) - jnp.log(1.0 + jnp.exp(-jnp.abs(x)))


def _sigmoid(x):
    return 1.0 / (1.0 + jnp.exp(-x))


def _split_bf16(x):
    hi = x.astype(BF16)
    lo = (x - hi.astype(F32)).astype(BF16)
    return hi, lo


def _store_token_tiles(ref, x):
    t = x.shape[0]
    for s in range(D_MODEL // LANES):
        ref[pl.ds(s, t, stride=ROW_TILE), :] = x[:, s * LANES:(s + 1) * LANES]


def _load_token_tiles(ref):
    t = ref.shape[0] // ROW_TILE
    return jnp.concatenate([ref[pl.ds(s, t, stride=ROW_TILE), :] for s in range(D_MODEL // LANES)], axis=1)


def _dot(a, b):
    return jnp.dot(a, b, preferred_element_type=F32)


def _dot_nt(a, b):
    return lax.dot_general(a, b, (((1,), (1,)), ((), ())), preferred_element_type=F32)


def _dot_tn(a, b):
    return lax.dot_general(a, b, (((0,), (0,)), ((), ())), preferred_element_type=F32)


def _proj_kernel(n_chunks, chunk, sample, x_ref, gmix_ref, w_ref, bg_ref, lng_ref, lnb_ref, ws_ref, bs_ref,
                 *refs):
    n_out = 14 if sample else 11
    ya_ref, qb_ref, qc_ref, kc_ref, kct_ref, vc_ref, oc_ref, gt_ref, gtt_ref, *kv_refs = refs[len(refs) - n_out:]
    x = x_ref[...]
    xn = (x * lax.rsqrt(jnp.mean(x * x, axis=-1, keepdims=True) + EPS) * gmix_ref[...]).astype(BF16)

    def proj(off, width):
        return _dot(xn, w_ref[:, off:off + width])

    za = proj(P_A, 2 * A_WIDTH)
    u = _gelu(za[:, :A_WIDTH])
    gv = _gelu(za[:, A_WIDTH:])
    xc = gv - jnp.mean(gv, axis=-1, keepdims=True)
    va = xc * lax.rsqrt(jnp.mean(xc * xc, axis=-1, keepdims=True) + EPS) * lng_ref[...] + lnb_ref[...]
    if sample:
        kv_refs[4][...] = va
    vab = va.astype(BF16)
    lane_group = lax.broadcasted_iota(jnp.int32, (chunk, A_WIDTH), 1) // A_GROUP_DIM
    for c in range(n_chunks):
        rows = slice(c * chunk, (c + 1) * chunk)
        vch = vab[rows]
        s = jnp.zeros((chunk, A_WIDTH), F32)
        for g in range(A_GROUPS):
            s = jnp.where(lane_group == g, _dot(ws_ref[g], vch), s)
        ya_ref[rows, :] = (u[rows] * (s + bs_ref[...])).astype(BF16)

    qb_ref[...] = (proj(P_BQ, B_WIDTH) * (1.0 / math.sqrt(B_HEAD_DIM))).astype(BF16)
    zk = proj(P_BK, B_WIDTH)
    zv = proj(P_BV, B_WIDTH)
    if sample:
        kv_refs[0][...] = zk
        kv_refs[1][...] = zv
        kv_refs[2][...] = zk.astype(BF16)
        kv_refs[3][...] = zv.astype(BF16)
    else:
        kv_refs[0][0] = zk.T
        kv_refs[1][0] = zv.T
        for other in range(1, kv_refs[0].shape[0]):
            kv_refs[0][other] = jnp.zeros(kv_refs[0].shape[1:], F32)
            kv_refs[1][other] = jnp.zeros(kv_refs[1].shape[1:], F32)

    qc_ref[...] = proj(P_CQ, C_WIDTH_PAD).astype(BF16)
    zk = proj(P_CK, C_WIDTH_PAD) * (C_HEAD_DIM ** -0.5)
    kc_ref[...] = zk.astype(BF16)
    kct_ref[...] = zk.T.astype(BF16)
    vc_ref[...] = proj(P_CV, C_WIDTH_PAD).astype(BF16)
    oc_ref[...] = proj(P_CO, C_WIDTH_PAD)
    g = proj(P_CG, LANES) + bg_ref[...]
    lane = lax.broadcasted_iota(jnp.int32, g.shape, 1)
    gates = jnp.where(lane < C_HEADS, g, _log_sigmoid(g))
    gt_ref[...] = gates
    gtt_ref[...] = gates.T[:GATE_ROWS, :]


def _proj(x, gmix, wp, bg, lng, lnb, ws, bs, tm, chunk, sample, layer=0, kv_all=None):
    n = x.shape[0]
    row = lambda w: pl.BlockSpec((tm, w), lambda i: (i, 0))
    full = lambda a: pl.BlockSpec(a.shape, lambda i: (0,) * a.ndim)
    col = lambda h: pl.BlockSpec((h, tm), lambda i: (0, i))
    outs = [((n, A_WIDTH), BF16, row(A_WIDTH)), ((n, B_WIDTH), BF16, row(B_WIDTH)),
            ((n, C_WIDTH_PAD), BF16, row(C_WIDTH_PAD)), ((n, C_WIDTH_PAD), BF16, row(C_WIDTH_PAD)),
            ((C_WIDTH_PAD, n), BF16, col(C_WIDTH_PAD)), ((n, C_WIDTH_PAD), BF16, row(C_WIDTH_PAD)),
            ((n, C_WIDTH_PAD), F32, row(C_WIDTH_PAD)), ((n, LANES), F32, row(LANES)),
            ((GATE_ROWS, n), F32, col(GATE_ROWS))]
    out_shape = [jax.ShapeDtypeStruct(s, dt) for s, dt, _ in outs]
    out_specs = [spec for _, _, spec in outs]
    if sample:
        extra = [(B_WIDTH, F32), (B_WIDTH, F32), (B_WIDTH, BF16), (B_WIDTH, BF16), (A_WIDTH, F32)]
        out_shape += [jax.ShapeDtypeStruct((n, w), dt) for w, dt in extra]
        out_specs += [row(w) for w, _ in extra]
    ins = [x, gmix, wp, bg, lng, lnb, ws, bs]
    in_specs = [row(D_MODEL), full(gmix), full(wp), full(bg), full(lng), full(lnb), full(ws), full(bs)]
    aliases = {}
    if not sample:
        out_shape += [jax.ShapeDtypeStruct((DEPTH, B_WIDTH, n), F32)] * 2
        if kv_all is None:
            out_specs += [pl.BlockSpec((DEPTH, B_WIDTH, tm), lambda i: (0, 0, i))] * 2
        else:
            out_specs += [pl.BlockSpec((1, B_WIDTH, tm), lambda i: (layer, 0, i))] * 2
            aliases = {len(ins): len(out_shape) - 2, len(ins) + 1: len(out_shape) - 1}
            ins += list(kv_all)
            in_specs += [pl.BlockSpec(memory_space=pl.ANY)] * 2
    return pl.pallas_call(
        functools.partial(_proj_kernel, tm // chunk, chunk, sample),
        out_shape=out_shape,
        grid=(n // tm,),
        in_specs=in_specs,
        out_specs=out_specs,
        input_output_aliases=aliases,
        compiler_params=_cparams(("parallel",)),
        name="proj",
    )(*ins)


def _sb_step(qh, kblk, vblk, carry, acc, tri, mask, transposed):
    z = _dot(qh, kblk) if transposed else _dot_nt(qh, kblk)
    tk = z.shape[1]
    drop = jnp.maximum(z, 0.0) + jnp.log(1.0 + jnp.exp(-jnp.abs(z)))
    if mask is not None:
        drop = jnp.where(mask, drop, 0.0)
    hi, lo = _split_bf16(drop)
    cs = _dot(hi, tri) + _dot(lo, tri)
    if tk >= LANES:
        carry_b = jnp.concatenate([carry] * (tk // LANES), axis=1)
    else:
        carry_b = carry[:, :tk]
    a = jnp.exp(z - cs - carry_b)
    if mask is not None:
        a = jnp.where(mask, a, 0.0)
    a = a.astype(BF16)
    acc = acc + (_dot_nt(a, vblk) if transposed else _dot(a, vblk))
    return carry + jnp.broadcast_to(cs[:, :1], carry.shape), acc


def _tri(tk):
    j = lax.broadcasted_iota(jnp.int32, (tk, tk), 0)
    s = lax.broadcasted_iota(jnp.int32, (tk, tk), 1)
    return (j >= s).astype(BF16)


def _sb_finish(acc_s, gn_ref, o_ref, head0):
    out = jnp.where(head0, acc_s[0], acc_s[1])
    r = lax.broadcasted_iota(jnp.int32, (LANES, LANES), 0) // B_HEAD_DIM
    c = lax.broadcasted_iota(jnp.int32, (LANES, LANES), 1) // B_HEAD_DIM
    same_head = (r == c).astype(BF16)
    hi, lo = _split_bf16(out * out)
    ms = (_dot(hi, same_head) + _dot(lo, same_head)) * (1.0 / B_HEAD_DIM)
    return (out * lax.rsqrt(ms + EPS) * gn_ref[...]).astype(o_ref.dtype)


def _sb_walk(qh, load_kv, first_block, carry_s, acc_s, tri):
    def alive():
        return jnp.minimum(jnp.min(carry_s[0]), jnp.min(carry_s[1])) <= -SB_DEAD_LOG_WEIGHT

    def cond(st):
        j, live = st
        return jnp.logical_and(j >= 0, live)

    def body(st):
        j, _ = st
        kblk, vblk = load_kv(j)
        for h in range(2):
            carry, acc = _sb_step(qh[h], kblk, vblk, carry_s[h], acc_s[h], tri, None, True)
            carry_s[h] = carry
            acc_s[h] = acc
        return j - 1, alive()

    lax.while_loop(cond, body, (first_block, alive()))


def _sb_prompt_kernel(q_ref, k_ref, v_ref, gn_ref, o_ref, carry_s, acc_s):
    tq = SB_BLOCK
    n_q = q_ref.shape[0] // tq
    head0 = lax.broadcasted_iota(jnp.int32, (tq, LANES), 1) < B_HEAD_DIM
    tri = _tri(tq)
    t = lax.broadcasted_iota(jnp.int32, (tq, tq), 0)
    s = lax.broadcasted_iota(jnp.int32, (tq, tq), 1)
    causal = s < t
    zeros = jnp.zeros((tq, LANES), F32)

    def load_kv(j):
        cols = pl.ds(pl.multiple_of(j * tq, tq), tq)
        return k_ref[:, cols].astype(BF16), v_ref[:, cols].astype(BF16)

    first = pl.program_id(1) * n_q
    kv = [load_kv(jnp.maximum(first + b - 1, 0)) for b in range(n_q + 1)]
    qhs = []
    for b in range(n_q):
        q = q_ref[b * tq:(b + 1) * tq, :]
        qh = [jnp.where(head0, q, 0), jnp.where(head0, 0, q)]
        qhs.append(qh)
        has_prev = t >= jnp.where(first + b >= 1, 0, tq)
        for h in range(2):
            carry, acc = _sb_step(qh[h], kv[b + 1][0], kv[b + 1][1], zeros, zeros, tri, causal, True)
            carry, acc = _sb_step(qh[h], kv[b][0], kv[b][1], carry, acc, tri, has_prev, True)
            carry_s[b, h] = carry
            acc_s[b, h] = acc
    for b in range(n_q):
        _sb_walk(qhs[b], load_kv, first + b - 2, carry_s.at[b], acc_s.at[b], tri)
        o_ref[b * tq:(b + 1) * tq, :] = _sb_finish(acc_s.at[b], gn_ref, o_ref, head0)


def _sb_prompt(q, k, v, gn, layer):
    n = q.shape[0]
    n_q = SB_BLOCKS_PER_STEP
    tq = SB_BLOCK * n_q
    blk = pl.BlockSpec((tq, LANES), lambda p, i: (i, p))
    seq = pl.BlockSpec((None, LANES, n), lambda p, i: (layer, p, 0))
    state = pltpu.VMEM((n_q, 2, SB_BLOCK, LANES), F32)
    return pl.pallas_call(
        _sb_prompt_kernel,
        out_shape=jax.ShapeDtypeStruct((n, B_WIDTH), BF16),
        grid=(B_PAIRS, n // tq),
        in_specs=[blk, seq, seq, pl.BlockSpec((1, LANES), lambda p, i: (0, p))],
        out_specs=blk,
        scratch_shapes=[state, state],
        compiler_params=_cparams(("parallel", "parallel")),
        name="sb_prompt",
    )(q, k, v, gn)


def _sb_sample_kernel(q_ref, kn_ref, vn_ref, kc_ref, vc_ref, gn_ref, o_ref, carry_s, acc_s):
    tq = q_ref.shape[1]
    tk = SB_BLOCK
    head0 = lax.broadcasted_iota(jnp.int32, (tq, LANES), 1) < B_HEAD_DIM
    q = q_ref[0]
    zero = jnp.zeros_like(q)
    qh = [jnp.where(head0, q, zero), jnp.where(head0, zero, q)]
    t = lax.broadcasted_iota(jnp.int32, (tq, tq), 0)
    s = lax.broadcasted_iota(jnp.int32, (tq, tq), 1)
    causal = s < t
    zeros = jnp.zeros((tq, LANES), F32)
    def load_kv(j):
        cols = pl.ds(pl.multiple_of(j * tk, tk), tk)
        return kc_ref[0, :, cols].astype(BF16), vc_ref[0, :, cols].astype(BF16)

    last = kc_ref.shape[2] // tk - 1
    kp, vp = load_kv(last)
    tri = _tri(tk)
    for h in range(2):
        carry, acc = _sb_step(qh[h], kn_ref[0], vn_ref[0], zeros, zeros, _tri(tq), causal, False)
        carry, acc = _sb_step(qh[h], kp, vp, carry, acc, tri, None, True)
        carry_s[h] = carry
        acc_s[h] = acc
    _sb_walk(qh, load_kv, last - 1, carry_s, acc_s, tri)
    o_ref[0] = _sb_finish(acc_s, gn_ref, o_ref, head0)


def _sb_sample(q, kn, vn, kc, vc, gn, layer):
    nb, tq, _ = q.shape
    past = kc.shape[3]
    new = pl.BlockSpec((1, tq, LANES), lambda b, p: (b, 0, p))
    old = pl.BlockSpec((None, 1, LANES, past), lambda b, p: (layer, b, p, 0))
    return pl.pallas_call(
        _sb_sample_kernel,
        out_shape=jax.ShapeDtypeStruct((nb, tq, B_WIDTH), BF16),
        grid=(nb, B_PAIRS),
        in_specs=[new, new, new, old, old, pl.BlockSpec((1, LANES), lambda b, p: (0, p))],
        out_specs=new,
        scratch_shapes=[pltpu.VMEM((2, tq, LANES), F32), pltpu.VMEM((2, tq, LANES), F32)],
        compiler_params=_cparams(("parallel", "parallel")),
        name="sb_sample",
    )(q, kn, vn, kc, vc, gn)


def _split3(x):
    h1 = x.astype(BF16)
    r1 = x - h1.astype(F32)
    h2 = r1.astype(BF16)
    return h1, h2, (r1 - h2.astype(F32)).astype(BF16)


def _dot3(x, rhs01):
    return _dot(jnp.concatenate(_split3(x), axis=1), jnp.concatenate([rhs01] * 3, axis=0))


def _dot3_left(lhs01, x):
    return _dot(jnp.concatenate([lhs01] * 3, axis=1), jnp.concatenate(_split3(x), axis=0))


def _mlstm_kernel(n_sub, q_ref, k_ref, kt_ref, v_ref, o_ref, gt_ref, gtt_ref, gn_ref, c0_ref, n0_ref, m0_ref,
                  yc_ref, c_out, n_out, m_out, c_s, n_s, m_s):
    L = MLSTM_BLOCK
    t_blk = pl.program_id(1)

    @pl.when(t_blk == 0)
    def _():
        c_s[...] = c0_ref[0]
        n_s[...] = n0_ref[0]
        m_s[...] = m0_ref[0]

    r = lax.broadcasted_iota(jnp.int32, (L, L), 0)
    c = lax.broadcasted_iota(jnp.int32, (L, L), 1)
    causal = c <= r
    upper = (r <= c).astype(BF16)
    ones_sq = jnp.ones((L, LANES), BF16)
    sel_r = lax.broadcasted_iota(jnp.int32, (LANES, C_WIDTH_PAD), 0)
    sel_c = lax.broadcasted_iota(jnp.int32, (LANES, C_WIDTH_PAD), 1) // C_HEAD_PAD
    sel_p = (sel_r == sel_c).astype(BF16)
    sel_b = (sel_r == sel_c + C_HEADS).astype(BF16)
    lane = c

    a_rows_all, pb_cols_all = [], []
    for sub in range(n_sub):
        rows = slice(sub * L, (sub + 1) * L)
        gtt = gtt_ref[0, :, rows]
        bct = _dot3(gtt, upper)
        a_rows_all.append(gtt[:C_HEADS] - bct[C_HEADS:])
        gt = gt_ref[0, rows, :]
        bc = _dot3_left(causal.astype(BF16), gt)
        pmax = gt - pltpu.roll(bc, LANES - C_HEADS, axis=1)
        for sh in (1, 2, 4, 8, 16, 32, 64):
            pmax = jnp.maximum(pmax, jnp.where(r >= sh, pltpu.roll(pmax, sh, axis=0), -jnp.inf))
        pb_cols_all.append(jnp.where(lane < C_HEADS, pmax, bc))
    pb_cols_all = jnp.concatenate(pb_cols_all, axis=0)
    p_all = _dot3(pb_cols_all, sel_p)
    b_all = _dot3(pb_cols_all, sel_b)

    state = [(c_s[h], n_s[h], m_s[h]) for h in range(C_HEADS)]
    for sub in range(n_sub):
        rows = slice(sub * L, (sub + 1) * L)
        a_rows = a_rows_all[sub]
        heads = []
        for h in range(C_HEADS):
            lanes = slice(h * C_HEAD_PAD, (h + 1) * C_HEAD_PAD)
            q = q_ref[0, rows, lanes]
            v1 = jnp.concatenate([v_ref[0, rows, lanes], ones_sq], axis=1)
            p_rep = p_all[rows, lanes]
            b_rep = b_all[rows, lanes]
            a_row = a_rows[h:h + 1, :]
            p_last = p_rep[L - 1:L, :]
            c_prev, n_prev, m_prev = state[h]

            w = jnp.exp(jnp.where(causal, a_row - p_rep, -jnp.inf)) * _dot_nt(q, k_ref[0, rows, lanes])
            kwt = (kt_ref[0, lanes, rows].astype(F32) * jnp.exp(a_row - p_last)).astype(BF16)
            both = _dot(jnp.concatenate([w.astype(BF16), kwt], axis=0), v1)
            here_sums = both[:L]
            fresh = both[L:]
            past_sums = _dot(q, jnp.concatenate([c_prev, n_prev], axis=1).astype(BF16))
            top = jnp.maximum(m_prev, p_rep)
            past = jnp.exp(m_prev - top)
            here = jnp.exp(p_rep - top)
            mix = jnp.concatenate([past, past], axis=1) * past_sums + jnp.concatenate([here, here], axis=1) * here_sums
            den = jnp.maximum(jnp.abs(mix[:, LANES:]), jnp.exp(-(b_rep + top)))
            heads.append(mix[:, :LANES] / den)

            top_last = jnp.maximum(m_prev, p_last)
            decay = jnp.exp(m_prev - top_last)
            gain = jnp.exp(p_last - top_last)
            state[h] = (decay * c_prev + gain * fresh[:, :LANES], decay * n_prev + gain * fresh[:, LANES:],
                        b_rep[L - 1:L, :] + top_last)

        hh = jnp.concatenate(heads, axis=0)
        ms = _dot(jnp.concatenate(_split_bf16(hh * hh), axis=1),
                  jnp.concatenate([ones_sq, ones_sq], axis=0)) * (1.0 / C_HEAD_DIM)
        hn = hh * lax.rsqrt(ms + EPS)
        for h in range(C_HEADS):
            lanes = slice(h * C_HEAD_PAD, (h + 1) * C_HEAD_PAD)
            yc_ref[0, rows, lanes] = (hn[h * L:(h + 1) * L] * gn_ref[:, lanes]
                                      * _sigmoid(o_ref[0, rows, lanes])).astype(BF16)

    for h in range(C_HEADS):
        c_s[h], n_s[h], m_s[h] = state[h]

    @pl.when(t_blk == pl.num_programs(1) - 1)
    def _():
        c_out[0] = c_s[...]
        n_out[0] = n_s[...]
        m_out[0] = m_s[...]


def _mlstm(q, k, kt, v, o, gt, gtt, gn, c0, n0, m0, n_sub):
    nb, n, _ = q.shape
    tb = n_sub * MLSTM_BLOCK
    seq = lambda w: pl.BlockSpec((1, tb, w), lambda b, t: (b, t, 0))
    seq_t = lambda h: pl.BlockSpec((1, h, tb), lambda b, t: (b, 0, t))
    st = lambda a: pl.BlockSpec((1,) + a.shape[1:], lambda b, t: (b,) + (0,) * (a.ndim - 1))
    return pl.pallas_call(
        functools.partial(_mlstm_kernel, n_sub),
        out_shape=[jax.ShapeDtypeStruct((nb, n, C_WIDTH_PAD), BF16),
                   jax.ShapeDtypeStruct(c0.shape, F32),
                   jax.ShapeDtypeStruct(n0.shape, F32),
                   jax.ShapeDtypeStruct(m0.shape, F32)],
        grid=(nb, n // tb),
        in_specs=[seq(C_WIDTH_PAD), seq(C_WIDTH_PAD), seq_t(C_WIDTH_PAD), seq(C_WIDTH_PAD), seq(C_WIDTH_PAD),
                  seq(LANES), seq_t(GATE_ROWS), pl.BlockSpec((1, C_WIDTH_PAD), lambda b, t: (0, 0)),
                  st(c0), st(n0), st(m0)],
        out_specs=[seq(C_WIDTH_PAD), st(c0), st(n0), st(m0)],
        scratch_shapes=[pltpu.VMEM(c0.shape[1:], F32), pltpu.VMEM(n0.shape[1:], F32),
                        pltpu.VMEM(m0.shape[1:], F32)],
        compiler_params=_cparams(("parallel", "arbitrary")),
        name="mlstm",
    )(q, k, kt, v, o, gt, gtt, gn, c0, n0, m0)


def _mixer_out(x_ref, ya_ref, yb_ref, yc_ref, wa_ref, wb_ref, wc_ref, g_ref, xmid_ref):
    y = _dot(ya_ref[...], wa_ref[...]) + _dot(yb_ref[...], wb_ref[...]) + _dot(yc_ref[...], wc_ref[...])
    x = x_ref[...] + y
    xmid_ref[...] = x
    return x * lax.rsqrt(jnp.mean(x * x, axis=-1, keepdims=True) + EPS) * g_ref[...]


def _merge_dense_kernel(x_ref, ya_ref, yb_ref, yc_ref, wa_ref, wb_ref, wc_ref, g_ref, xmid_ref, xn_ref):
    xn_ref[...] = _mixer_out(x_ref, ya_ref, yb_ref, yc_ref, wa_ref, wb_ref, wc_ref, g_ref,
                             xmid_ref).astype(BF16)


def _merge_moe_kernel(x_ref, ya_ref, yb_ref, yc_ref, wa_ref, wb_ref, wc_ref, g_ref, wrt_ref,
                      earlier_ref, xmid_ref, xrow_ref, ri_ref, rf_ref, cnt_ref, run_s):
    tm = x_ref.shape[0]

    @pl.when(pl.program_id(0) == 0)
    def _():
        run_s[...] = jnp.zeros_like(run_s)

    xn = _mixer_out(x_ref, ya_ref, yb_ref, yc_ref, wa_ref, wb_ref, wc_ref, g_ref, xmid_ref)
    _store_token_tiles(xrow_ref, xn)

    hi, lo = _split_bf16(xn)
    part = _dot_nt(wrt_ref[...], hi)
    lg = part[:N_EXPERTS] + part[N_EXPERTS:] + _dot_nt(wrt_ref[:N_EXPERTS, :], lo)
    expert = lax.broadcasted_iota(jnp.int32, lg.shape, 0)
    m1 = jnp.max(lg, axis=0, keepdims=True)
    i1 = jnp.min(jnp.where(lg == m1, expert, N_EXPERTS), axis=0, keepdims=True)
    lg2 = jnp.where(expert == i1, -jnp.inf, lg)
    m2 = jnp.max(lg2, axis=0, keepdims=True)
    i2 = jnp.min(jnp.where(lg2 == m2, expert, N_EXPERTS), axis=0, keepdims=True)
    e2 = jnp.exp(m2 - m1)
    g1 = 1.0 / (1.0 + e2)
    g2 = e2 * g1

    sel1 = expert == i1
    sel2 = expert == i2
    onehot = jnp.logical_or(sel1, sel2).astype(BF16)
    before = _dot(onehot, earlier_ref[...]) + jnp.concatenate([run_s[...]] * (tm // LANES), axis=1)
    rank1 = jnp.sum(jnp.where(sel1, before, 0.0), axis=0, keepdims=True).astype(jnp.int32)
    rank2 = jnp.sum(jnp.where(sel2, before, 0.0), axis=0, keepdims=True).astype(jnp.int32)
    run_s[...] += _dot(onehot, jnp.ones((tm, LANES), BF16))
    cnt_ref[...] = run_s[...].astype(jnp.int32)
    ri_ref[...] = jnp.where(expert == 0, i1, jnp.where(expert == 1, i2,
                            jnp.where(expert == 2, rank1, jnp.where(expert == 3, rank2, 0))))
    rf_ref[...] = jnp.where(expert == 0, g1, jnp.where(expert == 1, g2, 0.0))


def _merge(x, ya, yb, yc, wa, wb, wc, g, wr, tm):
    n = x.shape[0]
    row = lambda w: pl.BlockSpec((tm, w), lambda i: (i, 0))
    full = lambda a: pl.BlockSpec(a.shape, lambda i: (0,) * a.ndim)
    ins = [x, ya, yb, yc, wa, wb, wc, g]
    in_specs = [row(D_MODEL), row(A_WIDTH), row(B_WIDTH), row(C_WIDTH_PAD), full(wa), full(wb), full(wc), full(g)]
    if wr is None:
        return pl.pallas_call(
            _merge_dense_kernel,
            out_shape=[jax.ShapeDtypeStruct((n, D_MODEL), F32), jax.ShapeDtypeStruct((n, D_MODEL), BF16)],
            grid=(n // tm,), in_specs=in_specs, out_specs=[row(D_MODEL), row(D_MODEL)],
            compiler_params=_cparams(("parallel",)), name="merge_dense",
        )(*ins)
    earlier = (jnp.arange(tm)[:, None] < jnp.arange(tm)[None, :]).astype(BF16)
    col = pl.BlockSpec((N_EXPERTS, tm), lambda i: (0, i))
    return pl.pallas_call(
        _merge_moe_kernel,
        out_shape=[jax.ShapeDtypeStruct((n, D_MODEL), F32),
                   jax.ShapeDtypeStruct((n * ROW_TILE, LANES), F32),
                   jax.ShapeDtypeStruct((N_EXPERTS, n), jnp.int32),
                   jax.ShapeDtypeStruct((N_EXPERTS, n), F32),
                   jax.ShapeDtypeStruct((N_EXPERTS, LANES), jnp.int32)],
        grid=(n // tm,), in_specs=in_specs + [full(wr), full(earlier)],
        out_specs=[row(D_MODEL), pl.BlockSpec((tm * ROW_TILE, LANES), lambda i: (i, 0)),
                   col, col, pl.BlockSpec((N_EXPERTS, LANES), lambda i: (0, 0))],
        scratch_shapes=[pltpu.VMEM((N_EXPERTS, LANES), F32)],
        compiler_params=_cparams(("arbitrary",)), name="merge_moe",
    )(*ins, wr, earlier)


def _dispatch_kernel(dest_ref, src_ref, xs_in_ref, xs_ref, sem):
    del xs_in_ref
    tt = dest_ref.shape[2] // 2

    def issue(t, carry):
        rows = pl.ds(pl.multiple_of(t * ROW_TILE, ROW_TILE), ROW_TILE)
        for k in range(2):
            pltpu.make_async_copy(src_ref.at[rows], xs_ref.at[dest_ref[0, 0, 2 * t + k]], sem).start(priority=k)
        return carry

    lax.fori_loop(0, tt, issue, 0, unroll=8)

    def drain(t, carry):
        for k in range(2):
            pltpu.make_async_copy(src_ref.at[pl.ds(0, ROW_TILE)], xs_ref.at[0], sem).wait()
        return carry

    lax.fori_loop(0, tt, drain, 0, unroll=8)


def _dispatch(dest, src, xs, tt):
    n = src.shape[0] // ROW_TILE
    dest3 = dest.reshape(n // tt, 1, 2 * tt)
    return pl.pallas_call(
        _dispatch_kernel,
        out_shape=jax.ShapeDtypeStruct(xs.shape, xs.dtype),
        grid=(n // tt,),
        in_specs=[pl.BlockSpec((1, 1, 2 * tt), lambda i: (i, 0, 0), memory_space=pltpu.SMEM),
                  pl.BlockSpec((tt * ROW_TILE, LANES), lambda i: (i, 0)), pl.BlockSpec(memory_space=pl.ANY)],
        out_specs=pl.BlockSpec(memory_space=pl.ANY),
        scratch_shapes=[pltpu.SemaphoreType.DMA(())],
        input_output_aliases={2: 0},
        compiler_params=_cparams(("arbitrary",)), name="moe_dispatch",
    )(dest3, src, xs)


def _combine_kernel(final, dest_ref, next_ref, ys_ref, rf_ref, xmid_ref, gf_ref, o_ref, buf_s, sems):
    tt = xmid_ref.shape[0]
    i = pl.program_id(0)
    n_steps = pl.num_programs(0)

    def gather(idx_ref, slot):
        def issue(t, carry):
            rows = pl.ds(pl.multiple_of(t * ROW_TILE, ROW_TILE), ROW_TILE)
            for k in range(2):
                pltpu.make_async_copy(ys_ref.at[idx_ref[0, 0, 2 * t + k]], buf_s.at[slot, k, rows],
                                      sems.at[slot]).start(priority=k)
            return carry

        lax.fori_loop(0, tt, issue, 0, unroll=8)

    slot = i % 2

    @pl.when(i == 0)
    def _():
        gather(dest_ref, 0)

    @pl.when(i + 1 < n_steps)
    def _():
        gather(next_ref, 1 - slot)

    def drain(t, carry):
        for k in range(2):
            pltpu.make_async_copy(ys_ref.at[0], buf_s.at[slot, 0, pl.ds(0, ROW_TILE)], sems.at[slot]).wait()
        return carry

    lax.fori_loop(0, tt, drain, 0, unroll=8)

    y = (xmid_ref[...] + rf_ref[:, 0:1] * _load_token_tiles(buf_s.at[slot, 0])
         + rf_ref[:, 1:2] * _load_token_tiles(buf_s.at[slot, 1]))
    if final:
        y = y * lax.rsqrt(jnp.mean(y * y, axis=-1, keepdims=True) + EPS) * gf_ref[...]
    o_ref[...] = y


def _combine(dest, ys, rf, xmid, gf, tt, final):
    n = xmid.shape[0]
    n_steps = n // tt
    dest3 = dest.reshape(n_steps, 1, 2 * tt)
    row = lambda w: pl.BlockSpec((tt, w), lambda i: (i, 0))
    return pl.pallas_call(
        functools.partial(_combine_kernel, final),
        out_shape=jax.ShapeDtypeStruct((n, D_MODEL), F32),
        grid=(n_steps,),
        in_specs=[pl.BlockSpec((1, 1, 2 * tt), lambda i: (i, 0, 0), memory_space=pltpu.SMEM),
                  pl.BlockSpec((1, 1, 2 * tt), lambda i: (jnp.minimum(i + 1, n_steps - 1), 0, 0),
                               memory_space=pltpu.SMEM),
                  pl.BlockSpec(memory_space=pl.ANY), row(LANES), row(D_MODEL),
                  pl.BlockSpec((1, D_MODEL), lambda i: (0, 0))],
        out_specs=row(D_MODEL),
        scratch_shapes=[pltpu.VMEM((2, 2, tt * ROW_TILE, LANES), F32), pltpu.SemaphoreType.DMA((2,))],
        compiler_params=_cparams(("arbitrary",)), name="moe_combine",
    )(dest3, dest3, ys, rf, xmid, gf)


def _swiglu_acc(xn, wg_ref, wu_ref, wd_ref, acc_s):
    g = _dot(xn, wg_ref[0])
    u = _dot(xn, wu_ref[0])
    acc_s[...] += _dot((g * _sigmoid(g) * u).astype(BF16), wd_ref[0])


def _ffn_dense_kernel(final, n_cast, xn_ref, wg_ref, wu_ref, wd_ref, xmid_ref, gf_ref, *refs):
    casts_in, o_ref, casts_out, acc_s = refs[:n_cast], refs[n_cast], refs[n_cast + 1:-1], refs[-1]
    f = pl.program_id(1)

    @pl.when(f == 0)
    def _():
        acc_s[...] = jnp.zeros_like(acc_s)

    _swiglu_acc(xn_ref[...], wg_ref, wu_ref, wd_ref, acc_s)
    for src, dst in zip(casts_in, casts_out):
        dst[...] = src[...].astype(BF16)

    @pl.when(f == pl.num_programs(1) - 1)
    def _():
        y = xmid_ref[...] + acc_s[...]
        if final:
            y = y * lax.rsqrt(jnp.mean(y * y, axis=-1, keepdims=True) + EPS) * gf_ref[...]
        o_ref[...] = y


def _ffn_dense(xn, wg, wu, wd, xmid, gf, tm, final, cast=()):
    n = xn.shape[0]
    tf = FFN_F_TILE
    n_f = D_FF // tf
    steps = (n // tm) * n_f
    row = lambda w: pl.BlockSpec((tm, w), lambda i, f: (i, 0))
    slabs = [a.reshape(steps, a.size // (steps * a.shape[-1]), a.shape[-1]) for a in cast]
    slab_spec = lambda a: pl.BlockSpec((1,) + a.shape[1:], lambda i, f: (i * n_f + f, 0, 0))
    outs = pl.pallas_call(
        functools.partial(_ffn_dense_kernel, final, len(cast)),
        out_shape=[jax.ShapeDtypeStruct((n, D_MODEL), F32)] + [jax.ShapeDtypeStruct(a.shape, BF16) for a in slabs],
        grid=(n // tm, n_f),
        in_specs=[row(D_MODEL),
                  pl.BlockSpec((1, D_MODEL, tf), lambda i, f: (0, 0, f)),
                  pl.BlockSpec((1, D_MODEL, tf), lambda i, f: (0, 0, f)),
                  pl.BlockSpec((1, tf, D_MODEL), lambda i, f: (0, f, 0)),
                  row(D_MODEL), pl.BlockSpec((1, D_MODEL), lambda i, f: (0, 0))] + [slab_spec(a) for a in slabs],
        out_specs=[row(D_MODEL)] + [slab_spec(a) for a in slabs],
        scratch_shapes=[pltpu.VMEM((tm, D_MODEL), F32)],
        compiler_params=_cparams(("parallel", "arbitrary")),
        name="ffn_dense",
    )(xn, wg, wu, wd, xmid, gf, *slabs)
    return outs[0], [o.reshape(a.shape) for o, a in zip(outs[1:], cast)]


def _ffn_routed_kernel(te_ref, nu_ref, xs_ref, wg_ref, wu_ref, wd_ref, ys_ref, xb_s, acc_s):
    del te_ref
    i = pl.program_id(0)
    f = pl.program_id(1)

    @pl.when(i < nu_ref[0])
    def _():
        @pl.when(f == 0)
        def _():
            acc_s[...] = jnp.zeros_like(acc_s)
            xb_s[...] = _load_token_tiles(xs_ref).astype(BF16)

        _swiglu_acc(xb_s[...], wg_ref, wu_ref, wd_ref, acc_s)

        @pl.when(f == pl.num_programs(1) - 1)
        def _():
            _store_token_tiles(ys_ref, acc_s[...])

    @pl.when(jnp.logical_and(i >= nu_ref[0], f == 0))
    def _():
        ys_ref[...] = jnp.zeros_like(ys_ref)


def _ffn_routed(tile_expert, n_used, xs, wg, wu, wd):
    n_tiles = tile_expert.shape[0]
    tm, tf = MOE_TILE, FFN_F_TILE
    n_f = D_FF // tf
    last = lambda i, nu: jnp.minimum(i, nu[0] - 1)
    fcol = lambda i, f, nu: jnp.where(i < nu[0], f, n_f - 1)
    return pl.pallas_call(
        _ffn_routed_kernel,
        out_shape=jax.ShapeDtypeStruct(xs.shape, F32),
        grid_spec=pltpu.PrefetchScalarGridSpec(
            num_scalar_prefetch=2,
            grid=(n_tiles, n_f),
            in_specs=[pl.BlockSpec((tm * ROW_TILE, LANES), lambda i, f, te, nu: (last(i, nu), 0)),
                      pl.BlockSpec((1, D_MODEL, tf), lambda i, f, te, nu: (te[last(i, nu)], 0, fcol(i, f, nu))),
                      pl.BlockSpec((1, D_MODEL, tf), lambda i, f, te, nu: (te[last(i, nu)], 0, fcol(i, f, nu))),
                      pl.BlockSpec((1, tf, D_MODEL), lambda i, f, te, nu: (te[last(i, nu)], fcol(i, f, nu), 0))],
            out_specs=pl.BlockSpec((tm * ROW_TILE, LANES), lambda i, f, te, nu: (i, 0)),
            scratch_shapes=[pltpu.VMEM((tm, D_MODEL), BF16), pltpu.VMEM((tm, D_MODEL), F32)]),
        compiler_params=_cparams(("arbitrary", "arbitrary")),
        name="ffn_routed",
    )(tile_expert, n_used, xs, wg, wu, wd)


def _route_plan(ri_p, cnt_p, ri_s, cnt_s, n_tiles):
    cnt_p, cnt_s = cnt_p[:, 0], cnt_s[:, 0]
    tiles = (cnt_p + cnt_s + MOE_TILE - 1) // MOE_TILE
    ends = jnp.cumsum(tiles)
    start = (ends - tiles) * MOE_TILE
    lookup = lambda table, idx: jnp.sum(
        jnp.where(idx[..., None] == jnp.arange(N_EXPERTS, dtype=jnp.int32), table, 0), axis=-1)
    dest_p = (lookup(start, ri_p[0:2]) + ri_p[2:4]).T
    dest_s = (lookup(start + cnt_p, ri_s[0:2]) + ri_s[2:4]).T
    tile_expert = jnp.minimum(jnp.sum(jnp.arange(n_tiles, dtype=jnp.int32)[:, None] >= ends[None, :], axis=-1),
                              N_EXPERTS - 1).astype(jnp.int32)
    return dest_p, dest_s, tile_expert, ends[-1:].astype(jnp.int32)


def _pad_heads_cols(w):
    w = w.reshape(w.shape[0], C_HEADS, C_HEAD_DIM)
    return jnp.pad(w, ((0, 0), (0, 0), (0, C_HEAD_PAD - C_HEAD_DIM))).reshape(w.shape[0], C_WIDTH_PAD)


def _layer_params(l, w_in, b_gate, w_s, b_s, gn_c, w_out):
    w = w_in[l]
    gates = jnp.pad(w[:, OFF_CG:OFF_CG + 2 * C_HEADS], ((0, 0), (0, LANES - 2 * C_HEADS)))
    c_part = w[:, OFF_CQ:OFF_CG].reshape(D_MODEL, 4 * C_HEADS, C_HEAD_DIM)
    c_part = jnp.pad(c_part, ((0, 0), (0, 0), (0, C_HEAD_PAD - C_HEAD_DIM))).reshape(D_MODEL, 4 * C_WIDTH_PAD)
    wp = jnp.concatenate([w[:, :OFF_CQ], c_part, gates], axis=1).astype(BF16)
    bg = jnp.pad(b_gate[l], (0, LANES - 2 * C_HEADS))[None, :]
    pos = jnp.arange(GMLP_CHUNK)
    mask = (pos[None, :] // CHUNK) <= (pos[:, None] // CHUNK)
    wm = jnp.where(mask[None], w_s[l], 0.0)
    bs = jnp.repeat(b_s[l].T, A_GROUP_DIM, axis=1)
    wo = w_out[l]
    wc = wo[A_WIDTH + B_WIDTH:].reshape(C_HEADS, C_HEAD_DIM, D_MODEL)
    wc = jnp.pad(wc, ((0, 0), (0, C_HEAD_PAD - C_HEAD_DIM), (0, 0))).reshape(C_WIDTH_PAD, D_MODEL)
    gnc = _pad_heads_cols(gn_c[l][None, :])
    return dict(wp=wp, bg=bg, wm=wm, bs=bs, wa=wo[:A_WIDTH].astype(BF16),
                wb=wo[A_WIDTH:A_WIDTH + B_WIDTH].astype(BF16), wc=wc.astype(BF16), gnc=gnc)


def _pad_state(c, n, m):
    p = C_HEAD_PAD - C_HEAD_DIM
    c = jnp.pad(c, ((0, 0), (0, 0), (0, p), (0, p)))
    n = jnp.pad(n, ((0, 0), (0, 0), (0, p)))
    n = jnp.broadcast_to(n[:, :, :, None], n.shape + (LANES,))
    m = jnp.broadcast_to(m[:, :, None, None], m.shape + (1, LANES))
    return c, n, m


def _unpad_state(c, n, m):
    return c[:, :, :C_HEAD_DIM, :C_HEAD_DIM], n[:, :, :C_HEAD_DIM, 0], m[:, :, 0, 0]


def kernel(x_prompt, x_sample, cache_k_b, cache_v_b, state_c_mlstm, state_n_mlstm, state_m_mlstm,
           g_mix, w_in, b_gate, ln_a_g, ln_a_b, w_s, b_s, gn_b, gn_c, w_out,
           g_ffn, w_gate_d, w_up_d, w_down_d, w_router, w_gate_e, w_up_e, w_down_e, g_final):
    n_seq = x_prompt.shape[1]
    n_dec, n_new = x_sample.shape[0], x_sample.shape[1]
    past = cache_k_b.shape[2]
    n_samp = n_dec * n_new

    xp = x_prompt.reshape(n_seq, D_MODEL)
    xs = x_sample.reshape(n_samp, D_MODEL)
    gfin = g_final[None, :]
    keys_last = lambda a: jnp.transpose(a, (0, 1, 3, 4, 2)).reshape(DEPTH, n_dec, B_WIDTH, past)
    cache_kt, cache_vt = keys_last(cache_k_b), keys_last(cache_v_b)

    outs = {k: [] for k in ("kbp", "vbp", "cp", "np", "mp", "kbs", "vbs", "cs", "ns", "ms", "vas")}
    for l in range(DEPTH):
        p = _layer_params(l, w_in, b_gate, w_s, b_s, gn_c, w_out)
        gmix = g_mix[l][None, :]
        lng, lnb = ln_a_g[l][None, :], ln_a_b[l][None, :]
        gnb = gn_b[l][None, :]
        gffn = g_ffn[l][None, :]
        moe = l % 2 == 1
        j = l // 2
        if moe:
            wg, wu, wd = expert_bf16
            wr = jnp.concatenate(_split_bf16(w_router[j].T), axis=0)
        else:
            wg, wu, wd = (w_gate_d[j][None].astype(BF16), w_up_d[j][None].astype(BF16),
                          w_down_d[j][None].astype(BF16))
            wr = None
        final = l == DEPTH - 1

        wm_s = jnp.kron(jnp.eye(n_dec, dtype=F32), p["wm"][:, :n_new, :n_new])
        bs_s = jnp.tile(p["bs"][:n_new], (n_dec, 1))

        (ya, qb, qc, kc, kct, vc, oc, gt, gtt, kbt, vbt) = _proj(
            xp, gmix, p["wp"], p["bg"], lng, lnb, p["wm"].astype(BF16), p["bs"], ROW_BLOCK, GMLP_CHUNK, False,
            l, None if l == 0 else (kbt, vbt))
        yb = _sb_prompt(qb, kbt, vbt, gnb, l)
        c0, n0, m0 = _pad_state(jnp.zeros((1, C_HEADS, C_HEAD_DIM, C_HEAD_DIM), F32),
                                jnp.zeros((1, C_HEADS, C_HEAD_DIM), F32), jnp.zeros((1, C_HEADS), F32))
        yc, c_f, n_f, m_f = _mlstm(qc[None], kc[None], kct[None], vc[None], oc[None], gt[None], gtt[None],
                                   p["gnc"], c0, n0, m0, MLSTM_SUB_BLOCKS)
        merged_p = _merge(xp, ya, yb, yc[0], p["wa"], p["wb"], p["wc"], gffn, wr, ROW_BLOCK)
        c_f, n_f, m_f = _unpad_state(c_f, n_f, m_f)
        outs["cp"].append(c_f)
        outs["np"].append(n_f)
        outs["mp"].append(m_f)

        (ya, qb, qc, kc, kct, vc, oc, gt, gtt, kbf, vbf, kbh, vbh, va) = _proj(
            xs, gmix, p["wp"], p["bg"], lng, lnb, wm_s.astype(BF16), bs_s, n_samp, n_samp, True)
        r3 = lambda a: a.reshape(n_dec, n_new, a.shape[-1])
        yb = _sb_sample(r3(qb), r3(kbh), r3(vbh), cache_kt, cache_vt, gnb, l)
        n_pad = MLSTM_BLOCK - n_new
        padr = lambda a: jnp.pad(r3(a), ((0, 0), (0, n_pad), (0, 0)))
        per_stream = lambda a: jnp.transpose(a.reshape(a.shape[0], n_dec, n_new), (1, 0, 2))
        kct_s = jnp.pad(per_stream(kct), ((0, 0), (0, 0), (0, n_pad)))
        gate_row = jnp.arange(GATE_ROWS)[None, :, None]
        gtt_pad = jnp.broadcast_to(jnp.where(gate_row < C_HEADS, NEG_BIG, 0.0).astype(F32),
                                   (n_dec, GATE_ROWS, n_pad))
        gtt_s = jnp.concatenate([per_stream(gtt), gtt_pad], axis=2)
        gt_s = jnp.concatenate([r3(gt), jnp.broadcast_to(
            jnp.where(jnp.arange(LANES) < C_HEADS, NEG_BIG, 0.0).astype(F32), (n_dec, n_pad, LANES))], axis=1)
        c0, n0, m0 = _pad_state(state_c_mlstm[l], state_n_mlstm[l], state_m_mlstm[l])
        yc, c_u, n_u, m_u = _mlstm(padr(qc), padr(kc), kct_s, padr(vc), padr(oc), gt_s, gtt_s, p["gnc"],
                                   c0, n0, m0, 1)
        yc = yc[:, :n_new].reshape(n_samp, C_WIDTH_PAD)
        merged_s = _merge(xs, ya, yb.reshape(n_samp, B_WIDTH), yc, p["wa"], p["wb"], p["wc"],
                          gffn, wr, n_samp)
        c_u, n_u, m_u = _unpad_state(c_u, n_u, m_u)

        if moe:
            xmid_p, xrow_p, ri_p, rf_p, cnt_p = merged_p
            xmid_s, xrow_s, ri_s, rf_s, cnt_s = merged_s
            n_tiles = 2 * (n_seq + n_samp) // MOE_TILE + N_EXPERTS
            dest_p, dest_s, tile_expert, n_used = _route_plan(ri_p, cnt_p, ri_s, cnt_s, n_tiles)
            as_tiles = lambda a: a.reshape(a.shape[0] // ROW_TILE, ROW_TILE, LANES)
            gates = lambda rf: jnp.pad(rf[:2].T, ((0, 0), (0, LANES - 2)))
            xsort = jnp.zeros((n_tiles * MOE_TILE, ROW_TILE, LANES), F32)
            xsort = _dispatch(dest_p, xrow_p, xsort, ROW_BLOCK)
            xsort = _dispatch(dest_s, xrow_s, xsort, n_samp)
            ysort = _ffn_routed(tile_expert, n_used, xsort.reshape(-1, LANES), wg, wu, wd)
            xp = _combine(dest_p, as_tiles(ysort), gates(rf_p), xmid_p, gfin, MOE_COMBINE_BLOCK, final)
            xs = _combine(dest_s, as_tiles(ysort), gates(rf_s), xmid_s, gfin, n_samp, final)
        else:
            nxt = (l + 1) // 2
            to_cast = (w_gate_e[nxt], w_up_e[nxt], w_down_e[nxt]) if l + 1 < DEPTH else ()
            xp, expert_bf16 = _ffn_dense(merged_p[1], wg, wu, wd, merged_p[0], gfin, ROW_BLOCK, final, to_cast)
            xs, _ = _ffn_dense(merged_s[1], wg, wu, wd, merged_s[0], gfin, n_samp, final)

        outs["kbs"].append(kbf.reshape(n_dec, n_new, B_HEADS, B_HEAD_DIM))
        outs["vbs"].append(vbf.reshape(n_dec, n_new, B_HEADS, B_HEAD_DIM))
        outs["cs"].append(c_u)
        outs["ns"].append(n_u)
        outs["ms"].append(m_u)
        outs["vas"].append(va.reshape(n_dec, n_new, A_WIDTH))

    st = lambda k: jnp.stack(outs[k])
    heads_last = lambda a: jnp.transpose(a.reshape(DEPTH, 1, B_HEADS, B_HEAD_DIM, n_seq), (0, 1, 4, 2, 3))
    return (xp.reshape(1, n_seq, D_MODEL), xs.reshape(n_dec, n_new, D_MODEL),
            heads_last(kbt), heads_last(vbt), st("cp"), st("np"), st("mp"),
            st("kbs"), st("vbs"), st("cs"), st("ns"), st("ms"), st("vas"))
```

```python
import functools
import math

import jax
import jax.numpy as jnp
from jax import lax
from jax.experimental import pallas as pl
from jax.experimental.pallas import tpu as pltpu

F32 = jnp.float32
BF16 = jnp.bfloat16

D_MODEL = 1024
DEPTH = 2
EPS = 1e-6
CHUNK = 64
A_WIDTH = 256
A_GROUPS = 4
A_GROUP_DIM = 64
GMLP_CHUNK = 128
B_HEAD_DIM = 64
B_WIDTH = 384
B_HEADS = 6
B_PAIRS = 3
C_HEADS = 4
C_HEAD_DIM = 96
C_WIDTH = 384
D_FF = 2816
N_EXPERTS = 8

LANES = 128
C_HEAD_PAD = LANES
C_WIDTH_PAD = C_HEADS * C_HEAD_PAD

OFF_AU, OFF_AV, OFF_BQ, OFF_BK, OFF_BV = 0, 256, 512, 896, 1280
OFF_CQ, OFF_CK, OFF_CV, OFF_CO, OFF_CG = 1664, 2048, 2432, 2816, 3200
P_A = 0
P_BQ = 512
P_BK = P_BQ + B_WIDTH
P_BV = P_BK + B_WIDTH
P_CQ = P_BV + B_WIDTH
P_CK = P_CQ + C_WIDTH_PAD
P_CV = P_CK + C_WIDTH_PAD
P_CO = P_CV + C_WIDTH_PAD
P_CG = P_CO + C_WIDTH_PAD
P_DIM = P_CG + LANES

SB_DEAD_LOG_WEIGHT = -110.0
SB_BLOCK = 256
SB_BLOCKS_PER_STEP = 8
MLSTM_BLOCK = 128
MLSTM_SUB_BLOCKS = 8
GATE_ROWS = 2 * C_HEADS
NEG_BIG = -1e30
ROW_TILE = 8
MOE_TILE = 512
FFN_F_TILE = 1408
ROW_BLOCK = 512
MOE_COMBINE_BLOCK = 256

VMEM_LIMIT = 56 * 1024 * 1024


def _cparams(sem):
    return pltpu.CompilerParams(dimension_semantics=sem, vmem_limit_bytes=VMEM_LIMIT)


def _gelu(x):
    return 0.5 * x * (1.0 + lax.erf(x * (1.0 / math.sqrt(2.0))))


def _log_sigmoid(x):
    return jnp.minimum(x, 0.0) - jnp.log(1.0 + jnp.exp(-jnp.abs(x)))


def _sigmoid(x):
    return 1.0 / (1.0 + jnp.exp(-x))


def _split_bf16(x):
    hi = x.astype(BF16)
    lo = (x - hi.astype(F32)).astype(BF16)
    return hi, lo


def _store_token_tiles(ref, x):
    t = x.shape[0]
    for s in range(D_MODEL // LANES):
        ref[pl.ds(s, t, stride=ROW_TILE), :] = x[:, s * LANES:(s + 1) * LANES]


def _load_token_tiles(ref):
    t = ref.shape[0] // ROW_TILE
    return jnp.concatenate([ref[pl.ds(s, t, stride=ROW_TILE), :] for s in range(D_MODEL // LANES)], axis=1)


def _dot(a, b):
    return jnp.dot(a, b, preferred_element_type=F32)


def _dot_nt(a, b):
    return lax.dot_general(a, b, (((1,), (1,)), ((), ())), preferred_element_type=F32)


def _dot_tn(a, b):
    return lax.dot_general(a, b, (((0,), (0,)), ((), ())), preferred_element_type=F32)


def _proj_kernel(n_chunks, chunk, sample, x_ref, gmix_ref, w_ref, bg_ref, lng_ref, lnb_ref, ws_ref, bs_ref,
                 *refs):
    n_out = 14 if sample else 11
    ya_ref, qb_ref, qc_ref, kc_ref, kct_ref, vc_ref, oc_ref, gt_ref, gtt_ref, *kv_refs = refs[len(refs) - n_out:]
    x = x_ref[...]
    xn = (x * lax.rsqrt(jnp.mean(x * x, axis=-1, keepdims=True) + EPS) * gmix_ref[...]).astype(BF16)

    def proj(off, width):
        return _dot(xn, w_ref[:, off:off + width])

    za = proj(P_A, 2 * A_WIDTH)
    u = _gelu(za[:, :A_WIDTH])
    gv = _gelu(za[:, A_WIDTH:])
    xc = gv - jnp.mean(gv, axis=-1, keepdims=True)
    va = xc * lax.rsqrt(jnp.mean(xc * xc, axis=-1, keepdims=True) + EPS) * lng_ref[...] + lnb_ref[...]
    if sample:
        kv_refs[4][...] = va
    vab = va.astype(BF16)
    lane_group = lax.broadcasted_iota(jnp.int32, (chunk, A_WIDTH), 1) // A_GROUP_DIM
    for c in range(n_chunks):
        rows = slice(c * chunk, (c + 1) * chunk)
        vch = vab[rows]
        s = jnp.zeros((chunk, A_WIDTH), F32)
        for g in range(A_GROUPS):
            s = jnp.where(lane_group == g, _dot(ws_ref[g], vch), s)
        ya_ref[rows, :] = (u[rows] * (s + bs_ref[...])).astype(BF16)

    qb_ref[...] = (proj(P_BQ, B_WIDTH) * (1.0 / math.sqrt(B_HEAD_DIM))).astype(BF16)
    zk = proj(P_BK, B_WIDTH)
    zv = proj(P_BV, B_WIDTH)
    if sample:
        kv_refs[0][...] = zk
        kv_refs[1][...] = zv
        kv_refs[2][...] = zk.astype(BF16)
        kv_refs[3][...] = zv.astype(BF16)
    else:
        kv_refs[0][0] = zk.T
        kv_refs[1][0] = zv.T
        for other in range(1, kv_refs[0].shape[0]):
            kv_refs[0][other] = jnp.zeros(kv_refs[0].shape[1:], F32)
            kv_refs[1][other] = jnp.zeros(kv_refs[1].shape[1:], F32)

    qc_ref[...] = proj(P_CQ, C_WIDTH_PAD).astype(BF16)
    zk = proj(P_CK, C_WIDTH_PAD) * (C_HEAD_DIM ** -0.5)
    kc_ref[...] = zk.astype(BF16)
    kct_ref[...] = zk.T.astype(BF16)
    vc_ref[...] = proj(P_CV, C_WIDTH_PAD).astype(BF16)
    oc_ref[...] = proj(P_CO, C_WIDTH_PAD)
    g = proj(P_CG, LANES) + bg_ref[...]
    lane = lax.broadcasted_iota(jnp.int32, g.shape, 1)
    gates = jnp.where(lane < C_HEADS, g, _log_sigmoid(g))
    gt_ref[...] = gates
    gtt_ref[...] = gates.T[:GATE_ROWS, :]


def _proj(x, gmix, wp, bg, lng, lnb, ws, bs, tm, chunk, sample, layer=0, kv_all=None):
    n = x.shape[0]
    row = lambda w: pl.BlockSpec((tm, w), lambda i: (i, 0))
    full = lambda a: pl.BlockSpec(a.shape, lambda i: (0,) * a.ndim)
    col = lambda h: pl.BlockSpec((h, tm), lambda i: (0, i))
    outs = [((n, A_WIDTH), BF16, row(A_WIDTH)), ((n, B_WIDTH), BF16, row(B_WIDTH)),
            ((n, C_WIDTH_PAD), BF16, row(C_WIDTH_PAD)), ((n, C_WIDTH_PAD), BF16, row(C_WIDTH_PAD)),
            ((C_WIDTH_PAD, n), BF16, col(C_WIDTH_PAD)), ((n, C_WIDTH_PAD), BF16, row(C_WIDTH_PAD)),
            ((n, C_WIDTH_PAD), F32, row(C_WIDTH_PAD)), ((n, LANES), F32, row(LANES)),
            ((GATE_ROWS, n), F32, col(GATE_ROWS))]
    out_shape = [jax.ShapeDtypeStruct(s, dt) for s, dt, _ in outs]
    out_specs = [spec for _, _, spec in outs]
    if sample:
        extra = [(B_WIDTH, F32), (B_WIDTH, F32), (B_WIDTH, BF16), (B_WIDTH, BF16), (A_WIDTH, F32)]
        out_shape += [jax.ShapeDtypeStruct((n, w), dt) for w, dt in extra]
        out_specs += [row(w) for w, _ in extra]
    ins = [x, gmix, wp, bg, lng, lnb, ws, bs]
    in_specs = [row(D_MODEL), full(gmix), full(wp), full(bg), full(lng), full(lnb), full(ws), full(bs)]
    aliases = {}
    if not sample:
        out_shape += [jax.ShapeDtypeStruct((DEPTH, B_WIDTH, n), F32)] * 2
        if kv_all is None:
            out_specs += [pl.BlockSpec((DEPTH, B_WIDTH, tm), lambda i: (0, 0, i))] * 2
        else:
            out_specs += [pl.BlockSpec((1, B_WIDTH, tm), lambda i: (layer, 0, i))] * 2
            aliases = {len(ins): len(out_shape) - 2, len(ins) + 1: len(out_shape) - 1}
            ins += list(kv_all)
            in_specs += [pl.BlockSpec(memory_space=pl.ANY)] * 2
    return pl.pallas_call(
        functools.partial(_proj_kernel, tm // chunk, chunk, sample),
        out_shape=out_shape,
        grid=(n // tm,),
        in_specs=in_specs,
        out_specs=out_specs,
        input_output_aliases=aliases,
        compiler_params=_cparams(("parallel",)),
        name="proj",
    )(*ins)


def _sb_weights(qh, kblk, carry, tri2, mask, transposed, shift=None):
    z = _dot(qh, kblk) if transposed else _dot_nt(qh, kblk)
    if shift is not None:
        z = z + shift
    tk = z.shape[1]
    drop = jnp.maximum(z, 0.0) + jnp.log(1.0 + jnp.exp(-jnp.abs(z)))
    if mask is not None:
        drop = jnp.where(mask, drop, 0.0)
    cs = _dot(jnp.concatenate(_split_bf16(drop), axis=1), tri2)
    if tk >= LANES:
        carry_b = jnp.concatenate([carry] * (tk // LANES), axis=1)
    else:
        carry_b = carry[:, :tk]
    a = jnp.exp(z - cs - carry_b)
    if mask is not None:
        a = jnp.where(mask, a, 0.0)
    return a.astype(BF16), carry + jnp.broadcast_to(cs[:, :1], carry.shape)


def _sb_step(qh, kblk, vblk, carry, acc, tri2, mask, transposed):
    a, carry = _sb_weights(qh, kblk, carry, tri2, mask, transposed)
    return carry, acc + (_dot_nt(a, vblk) if transposed else _dot(a, vblk))


def _tri2(tk):
    j = lax.broadcasted_iota(jnp.int32, (2 * tk, tk), 0)
    s = lax.broadcasted_iota(jnp.int32, (2 * tk, tk), 1)
    return (jnp.where(j >= tk, j - tk, j) >= s).astype(BF16)


def _sb_finish(acc_s, gn_ref, o_ref, head0):
    out = jnp.where(head0, acc_s[0], acc_s[1])
    r = lax.broadcasted_iota(jnp.int32, (LANES, LANES), 0) // B_HEAD_DIM
    c = lax.broadcasted_iota(jnp.int32, (LANES, LANES), 1) // B_HEAD_DIM
    same_head = (r == c).astype(BF16)
    hi, lo = _split_bf16(out * out)
    ms = (_dot(hi, same_head) + _dot(lo, same_head)) * (1.0 / B_HEAD_DIM)
    return (out * lax.rsqrt(ms + EPS) * gn_ref[...]).astype(o_ref.dtype)


def _sb_walk(qh, load_kv, first_block, carry_s, acc_s, tri):
    def alive():
        return jnp.minimum(jnp.min(carry_s[0]), jnp.min(carry_s[1])) <= -SB_DEAD_LOG_WEIGHT

    def cond(st):
        j, live = st
        return jnp.logical_and(j >= 0, live)

    def body(st):
        j, _ = st
        kblk, vblk = load_kv(j)
        for h in range(2):
            carry, acc = _sb_step(qh[h], kblk, vblk, carry_s[h], acc_s[h], tri, None, True)
            carry_s[h] = carry
            acc_s[h] = acc
        return j - 1, alive()

    lax.while_loop(cond, body, (first_block, alive()))


def _sb_prompt_kernel(q_ref, k_ref, v_ref, gn_ref, o_ref, carry_s, acc_s):
    tq = SB_BLOCK
    n_q = q_ref.shape[0] // tq
    head0 = lax.broadcasted_iota(jnp.int32, (tq, LANES), 1) < B_HEAD_DIM
    tri = _tri2(tq)
    t = lax.broadcasted_iota(jnp.int32, (tq, tq), 0)
    s = lax.broadcasted_iota(jnp.int32, (tq, tq), 1)
    causal = s < t
    zeros = jnp.zeros((tq, LANES), F32)

    def load_kv(j):
        cols = pl.ds(pl.multiple_of(j * tq, tq), tq)
        return k_ref[:, cols].astype(BF16), v_ref[:, cols].astype(BF16)

    first = pl.program_id(1) * n_q
    kv = [load_kv(jnp.maximum(first + b - 1, 0)) for b in range(n_q + 1)]
    qhs = []
    for b in range(n_q):
        q = q_ref[b * tq:(b + 1) * tq, :]
        qh = [jnp.where(head0, q, 0), jnp.where(head0, 0, q)]
        qhs.append(qh)
        no_prev = jnp.where(first + b >= 1, 0.0, NEG_BIG)
        v_both = jnp.concatenate([kv[b + 1][1], kv[b][1]], axis=1)
        for h in range(2):
            a_diag, carry = _sb_weights(qh[h], kv[b + 1][0], zeros, tri, causal, True)
            a_prev, carry = _sb_weights(qh[h], kv[b][0], carry, tri, None, True, no_prev)
            carry_s[b, h] = carry
            acc_s[b, h] = _dot_nt(jnp.concatenate([a_diag, a_prev], axis=1), v_both)
    for b in range(n_q):
        _sb_walk(qhs[b], load_kv, first + b - 2, carry_s.at[b], acc_s.at[b], tri)
        o_ref[b * tq:(b + 1) * tq, :] = _sb_finish(acc_s.at[b], gn_ref, o_ref, head0)


def _sb_prompt(q, k, v, gn, layer):
    n = q.shape[0]
    n_q = SB_BLOCKS_PER_STEP
    tq = SB_BLOCK * n_q
    blk = pl.BlockSpec((tq, LANES), lambda p, i: (i, p))
    seq = pl.BlockSpec((None, LANES, n), lambda p, i: (layer, p, 0))
    state = pltpu.VMEM((n_q, 2, SB_BLOCK, LANES), F32)
    return pl.pallas_call(
        _sb_prompt_kernel,
        out_shape=jax.ShapeDtypeStruct((n, B_WIDTH), BF16),
        grid=(B_PAIRS, n // tq),
        in_specs=[blk, seq, seq, pl.BlockSpec((1, LANES), lambda p, i: (0, p))],
        out_specs=blk,
        scratch_shapes=[state, state],
        compiler_params=_cparams(("parallel", "parallel")),
        name="sb_prompt",
    )(q, k, v, gn)


def _sb_sample_kernel(q_ref, kn_ref, vn_ref, kc_ref, vc_ref, gn_ref, o_ref, carry_s, acc_s):
    tq = q_ref.shape[1]
    tk = SB_BLOCK
    head0 = lax.broadcasted_iota(jnp.int32, (tq, LANES), 1) < B_HEAD_DIM
    q = q_ref[0]
    zero = jnp.zeros_like(q)
    qh = [jnp.where(head0, q, zero), jnp.where(head0, zero, q)]
    t = lax.broadcasted_iota(jnp.int32, (tq, tq), 0)
    s = lax.broadcasted_iota(jnp.int32, (tq, tq), 1)
    causal = s < t
    zeros = jnp.zeros((tq, LANES), F32)
    def load_kv(j):
        cols = pl.ds(pl.multiple_of(j * tk, tk), tk)
        return kc_ref[0, :, cols].astype(BF16), vc_ref[0, :, cols].astype(BF16)

    last = kc_ref.shape[2] // tk - 1
    kp, vp = load_kv(last)
    tri = _tri2(tk)
    for h in range(2):
        carry, acc = _sb_step(qh[h], kn_ref[0], vn_ref[0], zeros, zeros, _tri2(tq), causal, False)
        carry, acc = _sb_step(qh[h], kp, vp, carry, acc, tri, None, True)
        carry_s[h] = carry
        acc_s[h] = acc
    _sb_walk(qh, load_kv, last - 1, carry_s, acc_s, tri)
    o_ref[0] = _sb_finish(acc_s, gn_ref, o_ref, head0)


def _sb_sample(q, kn, vn, kc, vc, gn, layer):
    nb, tq, _ = q.shape
    past = kc.shape[3]
    new = pl.BlockSpec((1, tq, LANES), lambda b, p: (b, 0, p))
    old = pl.BlockSpec((None, 1, LANES, past), lambda b, p: (layer, b, p, 0))
    return pl.pallas_call(
        _sb_sample_kernel,
        out_shape=jax.ShapeDtypeStruct((nb, tq, B_WIDTH), BF16),
        grid=(nb, B_PAIRS),
        in_specs=[new, new, new, old, old, pl.BlockSpec((1, LANES), lambda b, p: (0, p))],
        out_specs=new,
        scratch_shapes=[pltpu.VMEM((2, tq, LANES), F32), pltpu.VMEM((2, tq, LANES), F32)],
        compiler_params=_cparams(("parallel", "parallel")),
        name="sb_sample",
    )(q, kn, vn, kc, vc, gn)


def _split3(x):
    h1 = x.astype(BF16)
    r1 = x - h1.astype(F32)
    h2 = r1.astype(BF16)
    return h1, h2, (r1 - h2.astype(F32)).astype(BF16)


def _dot3(x, rhs01):
    return _dot(jnp.concatenate(_split3(x), axis=1), jnp.concatenate([rhs01] * 3, axis=0))


def _dot3_left(lhs01, x):
    return _dot(jnp.concatenate([lhs01] * 3, axis=1), jnp.concatenate(_split3(x), axis=0))


def _mlstm_kernel(n_sub, q_ref, k_ref, kt_ref, v_ref, o_ref, gt_ref, gtt_ref, gn_ref, c0_ref, n0_ref, m0_ref,
                  yc_ref, c_out, n_out, m_out, c_s, n_s, m_s):
    L = MLSTM_BLOCK
    t_blk = pl.program_id(1)

    @pl.when(t_blk == 0)
    def _():
        c_s[...] = c0_ref[0]
        n_s[...] = n0_ref[0]
        m_s[...] = m0_ref[0]

    r = lax.broadcasted_iota(jnp.int32, (L, L), 0)
    c = lax.broadcasted_iota(jnp.int32, (L, L), 1)
    causal = c <= r
    upper = (r <= c).astype(BF16)
    ones_sq = jnp.ones((L, LANES), BF16)
    sel_r = lax.broadcasted_iota(jnp.int32, (LANES, C_WIDTH_PAD), 0)
    sel_c = lax.broadcasted_iota(jnp.int32, (LANES, C_WIDTH_PAD), 1) // C_HEAD_PAD
    sel_p = (sel_r == sel_c).astype(BF16)
    sel_b = (sel_r == sel_c + C_HEADS).astype(BF16)
    lane = c

    a_rows_all, pb_cols_all = [], []
    for sub in range(n_sub):
        rows = slice(sub * L, (sub + 1) * L)
        gtt = gtt_ref[0, :, rows]
        bct = _dot3(gtt, upper)
        a_rows_all.append(gtt[:C_HEADS] - bct[C_HEADS:])
        gt = gt_ref[0, rows, :]
        bc = _dot3_left(causal.astype(BF16), gt)
        pmax = gt - pltpu.roll(bc, LANES - C_HEADS, axis=1)
        for sh in (1, 2, 4, 8, 16, 32, 64):
            pmax = jnp.maximum(pmax, jnp.where(r >= sh, pltpu.roll(pmax, sh, axis=0), -jnp.inf))
        pb_cols_all.append(jnp.where(lane < C_HEADS, pmax, bc))
    pb_cols_all = jnp.concatenate(pb_cols_all, axis=0)
    p_all = _dot3(pb_cols_all, sel_p)
    b_all = _dot3(pb_cols_all, sel_b)

    state = [(c_s[h], n_s[h], m_s[h]) for h in range(C_HEADS)]
    for sub in range(n_sub):
        rows = slice(sub * L, (sub + 1) * L)
        a_rows = a_rows_all[sub]
        heads = []
        for h in range(C_HEADS):
            lanes = slice(h * C_HEAD_PAD, (h + 1) * C_HEAD_PAD)
            q = q_ref[0, rows, lanes]
            v1 = jnp.concatenate([v_ref[0, rows, lanes], ones_sq], axis=1)
            p_rep = p_all[rows, lanes]
            b_rep = b_all[rows, lanes]
            a_row = a_rows[h:h + 1, :]
            p_last = p_rep[L - 1:L, :]
            c_prev, n_prev, m_prev = state[h]

            w = jnp.exp(jnp.where(causal, a_row - p_rep, -jnp.inf)) * _dot_nt(q, k_ref[0, rows, lanes])
            kwt = (kt_ref[0, lanes, rows].astype(F32) * jnp.exp(a_row - p_last)).astype(BF16)
            both = _dot(jnp.concatenate([w.astype(BF16), kwt], axis=0), v1)
            here_sums = both[:L]
            fresh = both[L:]
            past_sums = _dot(q, jnp.concatenate([c_prev, n_prev], axis=1).astype(BF16))
            top = jnp.maximum(m_prev, p_rep)
            past = jnp.exp(m_prev - top)
            here = jnp.exp(p_rep - top)
            mix = jnp.concatenate([past, past], axis=1) * past_sums + jnp.concatenate([here, here], axis=1) * here_sums
            den = jnp.maximum(jnp.abs(mix[:, LANES:]), jnp.exp(-(b_rep + top)))
            heads.append(mix[:, :LANES] / den)

            top_last = jnp.maximum(m_prev, p_last)
            decay = jnp.exp(m_prev - top_last)
            gain = jnp.exp(p_last - top_last)
            state[h] = (decay * c_prev + gain * fresh[:, :LANES], decay * n_prev + gain * fresh[:, LANES:],
                        b_rep[L - 1:L, :] + top_last)

        hh = jnp.concatenate(heads, axis=0)
        ms = _dot(jnp.concatenate(_split_bf16(hh * hh), axis=1),
                  jnp.concatenate([ones_sq, ones_sq], axis=0)) * (1.0 / C_HEAD_DIM)
        hn = hh * lax.rsqrt(ms + EPS)
        for h in range(C_HEADS):
            lanes = slice(h * C_HEAD_PAD, (h + 1) * C_HEAD_PAD)
            yc_ref[0, rows, lanes] = (hn[h * L:(h + 1) * L] * gn_ref[:, lanes]
                                      * _sigmoid(o_ref[0, rows, lanes])).astype(BF16)

    for h in range(C_HEADS):
        c_s[h], n_s[h], m_s[h] = state[h]

    @pl.when(t_blk == pl.num_programs(1) - 1)
    def _():
        c_out[0] = c_s[...]
        n_out[0] = n_s[...]
        m_out[0] = m_s[...]


def _mlstm(q, k, kt, v, o, gt, gtt, gn, c0, n0, m0, n_sub):
    nb, n, _ = q.shape
    tb = n_sub * MLSTM_BLOCK
    seq = lambda w: pl.BlockSpec((1, tb, w), lambda b, t: (b, t, 0))
    seq_t = lambda h: pl.BlockSpec((1, h, tb), lambda b, t: (b, 0, t))
    st = lambda a: pl.BlockSpec((1,) + a.shape[1:], lambda b, t: (b,) + (0,) * (a.ndim - 1))
    return pl.pallas_call(
        functools.partial(_mlstm_kernel, n_sub),
        out_shape=[jax.ShapeDtypeStruct((nb, n, C_WIDTH_PAD), BF16),
                   jax.ShapeDtypeStruct(c0.shape, F32),
                   jax.ShapeDtypeStruct(n0.shape, F32),
                   jax.ShapeDtypeStruct(m0.shape, F32)],
        grid=(nb, n // tb),
        in_specs=[seq(C_WIDTH_PAD), seq(C_WIDTH_PAD), seq_t(C_WIDTH_PAD), seq(C_WIDTH_PAD), seq(C_WIDTH_PAD),
                  seq(LANES), seq_t(GATE_ROWS), pl.BlockSpec((1, C_WIDTH_PAD), lambda b, t: (0, 0)),
                  st(c0), st(n0), st(m0)],
        out_specs=[seq(C_WIDTH_PAD), st(c0), st(n0), st(m0)],
        scratch_shapes=[pltpu.VMEM(c0.shape[1:], F32), pltpu.VMEM(n0.shape[1:], F32),
                        pltpu.VMEM(m0.shape[1:], F32)],
        compiler_params=_cparams(("parallel", "arbitrary")),
        name="mlstm",
    )(q, k, kt, v, o, gt, gtt, gn, c0, n0, m0)


def _mixer_out(x_ref, ya_ref, yb_ref, yc_ref, wa_ref, wb_ref, wc_ref, g_ref, xmid_ref):
    y = _dot(ya_ref[...], wa_ref[...]) + _dot(yb_ref[...], wb_ref[...]) + _dot(yc_ref[...], wc_ref[...])
    x = x_ref[...] + y
    xmid_ref[...] = x
    return x * lax.rsqrt(jnp.mean(x * x, axis=-1, keepdims=True) + EPS) * g_ref[...]


def _merge_dense_kernel(x_ref, ya_ref, yb_ref, yc_ref, wa_ref, wb_ref, wc_ref, g_ref, xmid_ref, xn_ref):
    xn_ref[...] = _mixer_out(x_ref, ya_ref, yb_ref, yc_ref, wa_ref, wb_ref, wc_ref, g_ref,
                             xmid_ref).astype(BF16)


def _merge_moe_kernel(x_ref, ya_ref, yb_ref, yc_ref, wa_ref, wb_ref, wc_ref, g_ref, wrt_ref,
                      earlier_ref, xmid_ref, xrow_ref, ri_ref, rf_ref, cnt_ref, run_s):
    tm = x_ref.shape[0]

    @pl.when(pl.program_id(0) == 0)
    def _():
        run_s[...] = jnp.zeros_like(run_s)

    xn = _mixer_out(x_ref, ya_ref, yb_ref, yc_ref, wa_ref, wb_ref, wc_ref, g_ref, xmid_ref)
    _store_token_tiles(xrow_ref, xn)

    hi, lo = _split_bf16(xn)
    part = _dot_nt(wrt_ref[...], hi)
    lg = part[:N_EXPERTS] + part[N_EXPERTS:] + _dot_nt(wrt_ref[:N_EXPERTS, :], lo)
    expert = lax.broadcasted_iota(jnp.int32, lg.shape, 0)
    m1 = jnp.max(lg, axis=0, keepdims=True)
    i1 = jnp.min(jnp.where(lg == m1, expert, N_EXPERTS), axis=0, keepdims=True)
    lg2 = jnp.where(expert == i1, -jnp.inf, lg)
    m2 = jnp.max(lg2, axis=0, keepdims=True)
    i2 = jnp.min(jnp.where(lg2 == m2, expert, N_EXPERTS), axis=0, keepdims=True)
    e2 = jnp.exp(m2 - m1)
    g1 = 1.0 / (1.0 + e2)
    g2 = e2 * g1

    sel1 = expert == i1
    sel2 = expert == i2
    onehot = jnp.logical_or(sel1, sel2).astype(BF16)
    before = _dot(onehot, earlier_ref[...]) + jnp.concatenate([run_s[...]] * (tm // LANES), axis=1)
    rank1 = jnp.sum(jnp.where(sel1, before, 0.0), axis=0, keepdims=True).astype(jnp.int32)
    rank2 = jnp.sum(jnp.where(sel2, before, 0.0), axis=0, keepdims=True).astype(jnp.int32)
    run_s[...] += _dot(onehot, jnp.ones((tm, LANES), BF16))
    cnt_ref[...] = run_s[...].astype(jnp.int32)
    ri_ref[...] = jnp.where(expert == 0, i1, jnp.where(expert == 1, i2,
                            jnp.where(expert == 2, rank1, jnp.where(expert == 3, rank2, 0))))
    rf_ref[...] = jnp.where(expert == 0, g1, jnp.where(expert == 1, g2, 0.0))


def _merge(x, ya, yb, yc, wa, wb, wc, g, wr, tm):
    n = x.shape[0]
    row = lambda w: pl.BlockSpec((tm, w), lambda i: (i, 0))
    full = lambda a: pl.BlockSpec(a.shape, lambda i: (0,) * a.ndim)
    ins = [x, ya, yb, yc, wa, wb, wc, g]
    in_specs = [row(D_MODEL), row(A_WIDTH), row(B_WIDTH), row(C_WIDTH_PAD), full(wa), full(wb), full(wc), full(g)]
    if wr is None:
        return pl.pallas_call(
            _merge_dense_kernel,
            out_shape=[jax.ShapeDtypeStruct((n, D_MODEL), F32), jax.ShapeDtypeStruct((n, D_MODEL), BF16)],
            grid=(n // tm,), in_specs=in_specs, out_specs=[row(D_MODEL), row(D_MODEL)],
            compiler_params=_cparams(("parallel",)), name="merge_dense",
        )(*ins)
    earlier = (jnp.arange(tm)[:, None] < jnp.arange(tm)[None, :]).astype(BF16)
    col = pl.BlockSpec((N_EXPERTS, tm), lambda i: (0, i))
    return pl.pallas_call(
        _merge_moe_kernel,
        out_shape=[jax.ShapeDtypeStruct((n, D_MODEL), F32),
                   jax.ShapeDtypeStruct((n * ROW_TILE, LANES), F32),
                   jax.ShapeDtypeStruct((N_EXPERTS, n), jnp.int32),
                   jax.ShapeDtypeStruct((N_EXPERTS, n), F32),
                   jax.ShapeDtypeStruct((N_EXPERTS, LANES), jnp.int32)],
        grid=(n // tm,), in_specs=in_specs + [full(wr), full(earlier)],
        out_specs=[row(D_MODEL), pl.BlockSpec((tm * ROW_TILE, LANES), lambda i: (i, 0)),
                   col, col, pl.BlockSpec((N_EXPERTS, LANES), lambda i: (0, 0))],
        scratch_shapes=[pltpu.VMEM((N_EXPERTS, LANES), F32)],
        compiler_params=_cparams(("arbitrary",)), name="merge_moe",
    )(*ins, wr, earlier)


def _dispatch_kernel(dest_ref, src_ref, xs_in_ref, xs_ref, sem):
    del xs_in_ref
    tt = dest_ref.shape[2] // 2

    def issue(t, carry):
        rows = pl.ds(pl.multiple_of(t * ROW_TILE, ROW_TILE), ROW_TILE)
        for k in range(2):
            pltpu.make_async_copy(src_ref.at[rows], xs_ref.at[dest_ref[0, 0, 2 * t + k]], sem).start(priority=k)
        return carry

    lax.fori_loop(0, tt, issue, 0, unroll=8)

    def drain(t, carry):
        for k in range(2):
            pltpu.make_async_copy(src_ref.at[pl.ds(0, ROW_TILE)], xs_ref.at[0], sem).wait()
        return carry

    lax.fori_loop(0, tt, drain, 0, unroll=8)


def _dispatch(dest, src, xs, tt):
    n = src.shape[0] // ROW_TILE
    dest3 = dest.reshape(n // tt, 1, 2 * tt)
    return pl.pallas_call(
        _dispatch_kernel,
        out_shape=jax.ShapeDtypeStruct(xs.shape, xs.dtype),
        grid=(n // tt,),
        in_specs=[pl.BlockSpec((1, 1, 2 * tt), lambda i: (i, 0, 0), memory_space=pltpu.SMEM),
                  pl.BlockSpec((tt * ROW_TILE, LANES), lambda i: (i, 0)), pl.BlockSpec(memory_space=pl.ANY)],
        out_specs=pl.BlockSpec(memory_space=pl.ANY),
        scratch_shapes=[pltpu.SemaphoreType.DMA(())],
        input_output_aliases={2: 0},
        compiler_params=_cparams(("arbitrary",)), name="moe_dispatch",
    )(dest3, src, xs)


def _combine_kernel(final, dest_ref, next_ref, ys_ref, rf_ref, xmid_ref, gf_ref, o_ref, buf_s, sems):
    tt = xmid_ref.shape[0]
    i = pl.program_id(0)
    n_steps = pl.num_programs(0)

    def gather(idx_ref, slot):
        def issue(t, carry):
            rows = pl.ds(pl.multiple_of(t * ROW_TILE, ROW_TILE), ROW_TILE)
            for k in range(2):
                pltpu.make_async_copy(ys_ref.at[idx_ref[0, 0, 2 * t + k]], buf_s.at[slot, k, rows],
                                      sems.at[slot]).start(priority=k)
            return carry

        lax.fori_loop(0, tt, issue, 0, unroll=8)

    slot = i % 2

    @pl.when(i == 0)
    def _():
        gather(dest_ref, 0)

    @pl.when(i + 1 < n_steps)
    def _():
        gather(next_ref, 1 - slot)

    def drain(t, carry):
        for k in range(2):
            pltpu.make_async_copy(ys_ref.at[0], buf_s.at[slot, 0, pl.ds(0, ROW_TILE)], sems.at[slot]).wait()
        return carry

    lax.fori_loop(0, tt, drain, 0, unroll=8)

    y = (xmid_ref[...] + rf_ref[:, 0:1] * _load_token_tiles(buf_s.at[slot, 0])
         + rf_ref[:, 1:2] * _load_token_tiles(buf_s.at[slot, 1]))
    if final:
        y = y * lax.rsqrt(jnp.mean(y * y, axis=-1, keepdims=True) + EPS) * gf_ref[...]
    o_ref[...] = y


def _combine(dest, ys, rf, xmid, gf, tt, final):
    n = xmid.shape[0]
    n_steps = n // tt
    dest3 = dest.reshape(n_steps, 1, 2 * tt)
    row = lambda w: pl.BlockSpec((tt, w), lambda i: (i, 0))
    return pl.pallas_call(
        functools.partial(_combine_kernel, final),
        out_shape=jax.ShapeDtypeStruct((n, D_MODEL), F32),
        grid=(n_steps,),
        in_specs=[pl.BlockSpec((1, 1, 2 * tt), lambda i: (i, 0, 0), memory_space=pltpu.SMEM),
                  pl.BlockSpec((1, 1, 2 * tt), lambda i: (jnp.minimum(i + 1, n_steps - 1), 0, 0),
                               memory_space=pltpu.SMEM),
                  pl.BlockSpec(memory_space=pl.ANY), row(LANES), row(D_MODEL),
                  pl.BlockSpec((1, D_MODEL), lambda i: (0, 0))],
        out_specs=row(D_MODEL),
        scratch_shapes=[pltpu.VMEM((2, 2, tt * ROW_TILE, LANES), F32), pltpu.SemaphoreType.DMA((2,))],
        compiler_params=_cparams(("arbitrary",)), name="moe_combine",
    )(dest3, dest3, ys, rf, xmid, gf)


def _swiglu_acc(xn, wg_ref, wu_ref, wd_ref, acc_s):
    g = _dot(xn, wg_ref[0])
    u = _dot(xn, wu_ref[0])
    acc_s[...] += _dot((g * _sigmoid(g) * u).astype(BF16), wd_ref[0])


def _ffn_dense_kernel(final, n_cast, xn_ref, wg_ref, wu_ref, wd_ref, xmid_ref, gf_ref, *refs):
    casts_in, o_ref, casts_out, acc_s = refs[:n_cast], refs[n_cast], refs[n_cast + 1:-1], refs[-1]
    f = pl.program_id(1)

    @pl.when(f == 0)
    def _():
        acc_s[...] = jnp.zeros_like(acc_s)

    _swiglu_acc(xn_ref[...], wg_ref, wu_ref, wd_ref, acc_s)
    for src, dst in zip(casts_in, casts_out):
        dst[...] = src[...].astype(BF16)

    @pl.when(f == pl.num_programs(1) - 1)
    def _():
        y = xmid_ref[...] + acc_s[...]
        if final:
            y = y * lax.rsqrt(jnp.mean(y * y, axis=-1, keepdims=True) + EPS) * gf_ref[...]
        o_ref[...] = y


def _ffn_dense(xn, wg, wu, wd, xmid, gf, tm, final, cast=()):
    n = xn.shape[0]
    tf = FFN_F_TILE
    n_f = D_FF // tf
    steps = (n // tm) * n_f
    row = lambda w: pl.BlockSpec((tm, w), lambda i, f: (i, 0))
    slabs = [a.reshape(steps, a.size // (steps * a.shape[-1]), a.shape[-1]) for a in cast]
    slab_spec = lambda a: pl.BlockSpec((1,) + a.shape[1:], lambda i, f: (i * n_f + f, 0, 0))
    outs = pl.pallas_call(
        functools.partial(_ffn_dense_kernel, final, len(cast)),
        out_shape=[jax.ShapeDtypeStruct((n, D_MODEL), F32)] + [jax.ShapeDtypeStruct(a.shape, BF16) for a in slabs],
        grid=(n // tm, n_f),
        in_specs=[row(D_MODEL),
                  pl.BlockSpec((1, D_MODEL, tf), lambda i, f: (0, 0, f)),
                  pl.BlockSpec((1, D_MODEL, tf), lambda i, f: (0, 0, f)),
                  pl.BlockSpec((1, tf, D_MODEL), lambda i, f: (0, f, 0)),
                  row(D_MODEL), pl.BlockSpec((1, D_MODEL), lambda i, f: (0, 0))] + [slab_spec(a) for a in slabs],
        out_specs=[row(D_MODEL)] + [slab_spec(a) for a in slabs],
        scratch_shapes=[pltpu.VMEM((tm, D_MODEL), F32)],
        compiler_params=_cparams(("parallel", "arbitrary")),
        name="ffn_dense",
    )(xn, wg, wu, wd, xmid, gf, *slabs)
    return outs[0], [o.reshape(a.shape) for o, a in zip(outs[1:], cast)]


def _ffn_routed_kernel(te_ref, nu_ref, xs_ref, wg_ref, wu_ref, wd_ref, ys_ref, xb_s, acc_s):
    del te_ref
    i = pl.program_id(0)
    f = pl.program_id(1)

    @pl.when(i < nu_ref[0])
    def _():
        @pl.when(f == 0)
        def _():
            acc_s[...] = jnp.zeros_like(acc_s)
            xb_s[...] = _load_token_tiles(xs_ref).astype(BF16)

        _swiglu_acc(xb_s[...], wg_ref, wu_ref, wd_ref, acc_s)

        @pl.when(f == pl.num_programs(1) - 1)
        def _():
            _store_token_tiles(ys_ref, acc_s[...])

    @pl.when(jnp.logical_and(i >= nu_ref[0], f == 0))
    def _():
        ys_ref[...] = jnp.zeros_like(ys_ref)


def _ffn_routed(tile_expert, n_used, xs, wg, wu, wd):
    n_tiles = tile_expert.shape[0]
    tm, tf = MOE_TILE, FFN_F_TILE
    n_f = D_FF // tf
    last = lambda i, nu: jnp.minimum(i, nu[0] - 1)
    fcol = lambda i, f, nu: jnp.where(i < nu[0], f, n_f - 1)
    return pl.pallas_call(
        _ffn_routed_kernel,
        out_shape=jax.ShapeDtypeStruct(xs.shape, F32),
        grid_spec=pltpu.PrefetchScalarGridSpec(
            num_scalar_prefetch=2,
            grid=(n_tiles, n_f),
            in_specs=[pl.BlockSpec((tm * ROW_TILE, LANES), lambda i, f, te, nu: (last(i, nu), 0)),
                      pl.BlockSpec((1, D_MODEL, tf), lambda i, f, te, nu: (te[last(i, nu)], 0, fcol(i, f, nu))),
                      pl.BlockSpec((1, D_MODEL, tf), lambda i, f, te, nu: (te[last(i, nu)], 0, fcol(i, f, nu))),
                      pl.BlockSpec((1, tf, D_MODEL), lambda i, f, te, nu: (te[last(i, nu)], fcol(i, f, nu), 0))],
            out_specs=pl.BlockSpec((tm * ROW_TILE, LANES), lambda i, f, te, nu: (i, 0)),
            scratch_shapes=[pltpu.VMEM((tm, D_MODEL), BF16), pltpu.VMEM((tm, D_MODEL), F32)]),
        compiler_params=_cparams(("arbitrary", "arbitrary")),
        name="ffn_routed",
    )(tile_expert, n_used, xs, wg, wu, wd)


def _route_plan(ri_p, cnt_p, ri_s, cnt_s, n_tiles):
    cnt_p, cnt_s = cnt_p[:, 0], cnt_s[:, 0]
    tiles = (cnt_p + cnt_s + MOE_TILE - 1) // MOE_TILE
    ends = jnp.cumsum(tiles)
    start = (ends - tiles) * MOE_TILE
    lookup = lambda table, idx: jnp.sum(
        jnp.where(idx[..., None] == jnp.arange(N_EXPERTS, dtype=jnp.int32), table, 0), axis=-1)
    dest_p = (lookup(start, ri_p[0:2]) + ri_p[2:4]).T
    dest_s = (lookup(start + cnt_p, ri_s[0:2]) + ri_s[2:4]).T
    tile_expert = jnp.minimum(jnp.sum(jnp.arange(n_tiles, dtype=jnp.int32)[:, None] >= ends[None, :], axis=-1),
                              N_EXPERTS - 1).astype(jnp.int32)
    return dest_p, dest_s, tile_expert, ends[-1:].astype(jnp.int32)


def _pad_heads_cols(w):
    w = w.reshape(w.shape[0], C_HEADS, C_HEAD_DIM)
    return jnp.pad(w, ((0, 0), (0, 0), (0, C_HEAD_PAD - C_HEAD_DIM))).reshape(w.shape[0], C_WIDTH_PAD)


def _layer_params(l, w_in, b_gate, w_s, b_s, gn_c, w_out):
    w = w_in[l]
    gates = jnp.pad(w[:, OFF_CG:OFF_CG + 2 * C_HEADS], ((0, 0), (0, LANES - 2 * C_HEADS)))
    c_part = w[:, OFF_CQ:OFF_CG].reshape(D_MODEL, 4 * C_HEADS, C_HEAD_DIM)
    c_part = jnp.pad(c_part, ((0, 0), (0, 0), (0, C_HEAD_PAD - C_HEAD_DIM))).reshape(D_MODEL, 4 * C_WIDTH_PAD)
    wp = jnp.concatenate([w[:, :OFF_CQ], c_part, gates], axis=1).astype(BF16)
    bg = jnp.pad(b_gate[l], (0, LANES - 2 * C_HEADS))[None, :]
    pos = jnp.arange(GMLP_CHUNK)
    mask = (pos[None, :] // CHUNK) <= (pos[:, None] // CHUNK)
    wm = jnp.where(mask[None], w_s[l], 0.0)
    bs = jnp.repeat(b_s[l].T, A_GROUP_DIM, axis=1)
    wo = w_out[l]
    wc = wo[A_WIDTH + B_WIDTH:].reshape(C_HEADS, C_HEAD_DIM, D_MODEL)
    wc = jnp.pad(wc, ((0, 0), (0, C_HEAD_PAD - C_HEAD_DIM), (0, 0))).reshape(C_WIDTH_PAD, D_MODEL)
    gnc = _pad_heads_cols(gn_c[l][None, :])
    return dict(wp=wp, bg=bg, wm=wm, bs=bs, wa=wo[:A_WIDTH].astype(BF16),
                wb=wo[A_WIDTH:A_WIDTH + B_WIDTH].astype(BF16), wc=wc.astype(BF16), gnc=gnc)


def _pad_state(c, n, m):
    p = C_HEAD_PAD - C_HEAD_DIM
    c = jnp.pad(c, ((0, 0), (0, 0), (0, p), (0, p)))
    n = jnp.pad(n, ((0, 0), (0, 0), (0, p)))
    n = jnp.broadcast_to(n[:, :, :, None], n.shape + (LANES,))
    m = jnp.broadcast_to(m[:, :, None, None], m.shape + (1, LANES))
    return c, n, m


def _unpad_state(c, n, m):
    return c[:, :, :C_HEAD_DIM, :C_HEAD_DIM], n[:, :, :C_HEAD_DIM, 0], m[:, :, 0, 0]


def kernel(x_prompt, x_sample, cache_k_b, cache_v_b, state_c_mlstm, state_n_mlstm, state_m_mlstm,
           g_mix, w_in, b_gate, ln_a_g, ln_a_b, w_s, b_s, gn_b, gn_c, w_out,
           g_ffn, w_gate_d, w_up_d, w_down_d, w_router, w_gate_e, w_up_e, w_down_e, g_final):
    n_seq = x_prompt.shape[1]
    n_dec, n_new = x_sample.shape[0], x_sample.shape[1]
    past = cache_k_b.shape[2]
    n_samp = n_dec * n_new

    xp = x_prompt.reshape(n_seq, D_MODEL)
    xs = x_sample.reshape(n_samp, D_MODEL)
    gfin = g_final[None, :]
    keys_last = lambda a: jnp.transpose(a, (0, 1, 3, 4, 2)).reshape(DEPTH, n_dec, B_WIDTH, past)
    cache_kt, cache_vt = keys_last(cache_k_b), keys_last(cache_v_b)

    outs = {k: [] for k in ("kbp", "vbp", "cp", "np", "mp", "kbs", "vbs", "cs", "ns", "ms", "vas")}
    for l in range(DEPTH):
        p = _layer_params(l, w_in, b_gate, w_s, b_s, gn_c, w_out)
        gmix = g_mix[l][None, :]
        lng, lnb = ln_a_g[l][None, :], ln_a_b[l][None, :]
        gnb = gn_b[l][None, :]
        gffn = g_ffn[l][None, :]
        moe = l % 2 == 1
        j = l // 2
        if moe:
            wg, wu, wd = expert_bf16
            wr = jnp.concatenate(_split_bf16(w_router[j].T), axis=0)
        else:
            wg, wu, wd = (w_gate_d[j][None].astype(BF16), w_up_d[j][None].astype(BF16),
                          w_down_d[j][None].astype(BF16))
            wr = None
        final = l == DEPTH - 1

        wm_s = jnp.kron(jnp.eye(n_dec, dtype=F32), p["wm"][:, :n_new, :n_new])
        bs_s = jnp.tile(p["bs"][:n_new], (n_dec, 1))

        (ya, qb, qc, kc, kct, vc, oc, gt, gtt, kbt, vbt) = _proj(
            xp, gmix, p["wp"], p["bg"], lng, lnb, p["wm"].astype(BF16), p["bs"], ROW_BLOCK, GMLP_CHUNK, False,
            l, None if l == 0 else (kbt, vbt))
        yb = _sb_prompt(qb, kbt, vbt, gnb, l)
        c0, n0, m0 = _pad_state(jnp.zeros((1, C_HEADS, C_HEAD_DIM, C_HEAD_DIM), F32),
                                jnp.zeros((1, C_HEADS, C_HEAD_DIM), F32), jnp.zeros((1, C_HEADS), F32))
        yc, c_f, n_f, m_f = _mlstm(qc[None], kc[None], kct[None], vc[None], oc[None], gt[None], gtt[None],
                                   p["gnc"], c0, n0, m0, MLSTM_SUB_BLOCKS)
        merged_p = _merge(xp, ya, yb, yc[0], p["wa"], p["wb"], p["wc"], gffn, wr, ROW_BLOCK)
        c_f, n_f, m_f = _unpad_state(c_f, n_f, m_f)
        outs["cp"].append(c_f)
        outs["np"].append(n_f)
        outs["mp"].append(m_f)

        (ya, qb, qc, kc, kct, vc, oc, gt, gtt, kbf, vbf, kbh, vbh, va) = _proj(
            xs, gmix, p["wp"], p["bg"], lng, lnb, wm_s.astype(BF16), bs_s, n_samp, n_samp, True)
        r3 = lambda a: a.reshape(n_dec, n_new, a.shape[-1])
        yb = _sb_sample(r3(qb), r3(kbh), r3(vbh), cache_kt, cache_vt, gnb, l)
        n_pad = MLSTM_BLOCK - n_new
        padr = lambda a: jnp.pad(r3(a), ((0, 0), (0, n_pad), (0, 0)))
        per_stream = lambda a: jnp.transpose(a.reshape(a.shape[0], n_dec, n_new), (1, 0, 2))
        kct_s = jnp.pad(per_stream(kct), ((0, 0), (0, 0), (0, n_pad)))
        gate_row = jnp.arange(GATE_ROWS)[None, :, None]
        gtt_pad = jnp.broadcast_to(jnp.where(gate_row < C_HEADS, NEG_BIG, 0.0).astype(F32),
                                   (n_dec, GATE_ROWS, n_pad))
        gtt_s = jnp.concatenate([per_stream(gtt), gtt_pad], axis=2)
        gt_s = jnp.concatenate([r3(gt), jnp.broadcast_to(
            jnp.where(jnp.arange(LANES) < C_HEADS, NEG_BIG, 0.0).astype(F32), (n_dec, n_pad, LANES))], axis=1)
        c0, n0, m0 = _pad_state(state_c_mlstm[l], state_n_mlstm[l], state_m_mlstm[l])
        yc, c_u, n_u, m_u = _mlstm(padr(qc), padr(kc), kct_s, padr(vc), padr(oc), gt_s, gtt_s, p["gnc"],
                                   c0, n0, m0, 1)
        yc = yc[:, :n_new].reshape(n_samp, C_WIDTH_PAD)
        merged_s = _merge(xs, ya, yb.reshape(n_samp, B_WIDTH), yc, p["wa"], p["wb"], p["wc"],
                          gffn, wr, n_samp)
        c_u, n_u, m_u = _unpad_state(c_u, n_u, m_u)

        if moe:
            xmid_p, xrow_p, ri_p, rf_p, cnt_p = merged_p
            xmid_s, xrow_s, ri_s, rf_s, cnt_s = merged_s
            n_tiles = 2 * (n_seq + n_samp) // MOE_TILE + N_EXPERTS
            dest_p, dest_s, tile_expert, n_used = _route_plan(ri_p, cnt_p, ri_s, cnt_s, n_tiles)
            as_tiles = lambda a: a.reshape(a.shape[0] // ROW_TILE, ROW_TILE, LANES)
            gates = lambda rf: jnp.pad(rf[:2].T, ((0, 0), (0, LANES - 2)))
            xsort = jnp.zeros((n_tiles * MOE_TILE, ROW_TILE, LANES), F32)
            xsort = _dispatch(dest_p, xrow_p, xsort, ROW_BLOCK)
            xsort = _dispatch(dest_s, xrow_s, xsort, n_samp)
            ysort = _ffn_routed(tile_expert, n_used, xsort.reshape(-1, LANES), wg, wu, wd)
            xp = _combine(dest_p, as_tiles(ysort), gates(rf_p), xmid_p, gfin, MOE_COMBINE_BLOCK, final)
            xs = _combine(dest_s, as_tiles(ysort), gates(rf_s), xmid_s, gfin, n_samp, final)
        else:
            nxt = (l + 1) // 2
            to_cast = (w_gate_e[nxt], w_up_e[nxt], w_down_e[nxt]) if l + 1 < DEPTH else ()
            xp, expert_bf16 = _ffn_dense(merged_p[1], wg, wu, wd, merged_p[0], gfin, ROW_BLOCK, final, to_cast)
            xs, _ = _ffn_dense(merged_s[1], wg, wu, wd, merged_s[0], gfin, n_samp, final)

        outs["kbs"].append(kbf.reshape(n_dec, n_new, B_HEADS, B_HEAD_DIM))
        outs["vbs"].append(vbf.reshape(n_dec, n_new, B_HEADS, B_HEAD_DIM))
        outs["cs"].append(c_u)
        outs["ns"].append(n_u)
        outs["ms"].append(m_u)
        outs["vas"].append(va.reshape(n_dec, n_new, A_WIDTH))

    st = lambda k: jnp.stack(outs[k])
    heads_last = lambda a: jnp.transpose(a.reshape(DEPTH, 1, B_HEADS, B_HEAD_DIM, n_seq), (0, 1, 4, 2, 3))
    return (xp.reshape(1, n_seq, D_MODEL), xs.reshape(n_dec, n_new, D_MODEL),
            heads_last(kbt), heads_last(vbt), st("cp"), st("np"), st("mp"),
            st("kbs"), st("vbs"), st("cs"), st("ns"), st("ms"), st("vas"))
```

```python
import functools
import math

import jax
import jax.numpy as jnp
from jax import lax
from jax.experimental import pallas as pl
from jax.experimental.pallas import tpu as pltpu

F32 = jnp.float32
BF16 = jnp.bfloat16

D_MODEL = 1024
DEPTH = 2
EPS = 1e-6
CHUNK = 64
A_WIDTH = 256
A_GROUPS = 4
A_GROUP_DIM = 64
GMLP_CHUNK = 128
B_HEAD_DIM = 64
B_WIDTH = 384
B_HEADS = 6
B_PAIRS = 3
C_HEADS = 4
C_HEAD_DIM = 96
C_WIDTH = 384
D_FF = 2816
N_EXPERTS = 8

LANES = 128
C_HEAD_PAD = LANES
C_WIDTH_PAD = C_HEADS * C_HEAD_PAD

OFF_AU, OFF_AV, OFF_BQ, OFF_BK, OFF_BV = 0, 256, 512, 896, 1280
OFF_CQ, OFF_CK, OFF_CV, OFF_CO, OFF_CG = 1664, 2048, 2432, 2816, 3200
P_A = 0
P_BQ = 512
P_BK = P_BQ + B_WIDTH
P_BV = P_BK + B_WIDTH
P_CQ = P_BV + B_WIDTH
P_CK = P_CQ + C_WIDTH_PAD
P_CV = P_CK + C_WIDTH_PAD
P_CO = P_CV + C_WIDTH_PAD
P_CG = P_CO + C_WIDTH_PAD
P_DIM = P_CG + LANES

SB_DEAD_LOG_WEIGHT = -110.0
SB_BLOCK = 256
SB_BLOCKS_PER_STEP = 8
MLSTM_BLOCK = 128
MLSTM_SUB_BLOCKS = 8
GATE_ROWS = 2 * C_HEADS
NEG_BIG = -1e30
ROW_TILE = 8
MOE_TILE = 512
FFN_F_TILE = 1408
ROW_BLOCK = 512
MOE_COMBINE_BLOCK = 256
CAST_SLAB_ROWS = 128

VMEM_LIMIT = 56 * 1024 * 1024


def _cparams(sem):
    return pltpu.CompilerParams(dimension_semantics=sem, vmem_limit_bytes=VMEM_LIMIT)


def _gelu(x):
    return 0.5 * x * (1.0 + lax.erf(x * (1.0 / math.sqrt(2.0))))


def _log_sigmoid(x):
    return jnp.minimum(x, 0.0) - jnp.log(1.0 + jnp.exp(-jnp.abs(x)))


def _sigmoid(x):
    return 1.0 / (1.0 + jnp.exp(-x))


def _split_bf16(x):
    hi = x.astype(BF16)
    lo = (x - hi.astype(F32)).astype(BF16)
    return hi, lo


def _store_token_tiles(ref, x):
    t = x.shape[0]
    for s in range(D_MODEL // LANES):
        ref[pl.ds(s, t, stride=ROW_TILE), :] = x[:, s * LANES:(s + 1) * LANES]


def _load_token_tiles(ref):
    t = ref.shape[0] // ROW_TILE
    return jnp.concatenate([ref[pl.ds(s, t, stride=ROW_TILE), :] for s in range(D_MODEL // LANES)], axis=1)


def _dot(a, b):
    return jnp.dot(a, b, preferred_element_type=F32)


def _dot_nt(a, b):
    return lax.dot_general(a, b, (((1,), (1,)), ((), ())), preferred_element_type=F32)


def _dot_tn(a, b):
    return lax.dot_general(a, b, (((0,), (0,)), ((), ())), preferred_element_type=F32)


def _proj_kernel(n_chunks, chunk, sample, x_ref, gmix_ref, w_ref, bg_ref, lng_ref, lnb_ref, ws_ref, bs_ref,
                 *refs):
    n_out = 14 if sample else 11
    ya_ref, qb_ref, qc_ref, kc_ref, kct_ref, vc_ref, oc_ref, gt_ref, gtt_ref, *kv_refs = refs[len(refs) - n_out:]
    x = x_ref[...]
    xn = (x * lax.rsqrt(jnp.mean(x * x, axis=-1, keepdims=True) + EPS) * gmix_ref[...]).astype(BF16)

    def proj(off, width):
        return _dot(xn, w_ref[:, off:off + width])

    za = proj(P_A, 2 * A_WIDTH)
    u = _gelu(za[:, :A_WIDTH])
    gv = _gelu(za[:, A_WIDTH:])
    xc = gv - jnp.mean(gv, axis=-1, keepdims=True)
    va = xc * lax.rsqrt(jnp.mean(xc * xc, axis=-1, keepdims=True) + EPS) * lng_ref[...] + lnb_ref[...]
    if sample:
        kv_refs[4][...] = va
    vab = va.astype(BF16)
    lane_group = lax.broadcasted_iota(jnp.int32, (chunk, A_WIDTH), 1) // A_GROUP_DIM
    for c in range(n_chunks):
        rows = slice(c * chunk, (c + 1) * chunk)
        vch = vab[rows]
        s = jnp.zeros((chunk, A_WIDTH), F32)
        for g in range(A_GROUPS):
            s = jnp.where(lane_group == g, _dot(ws_ref[g], vch), s)
        ya_ref[rows, :] = (u[rows] * (s + bs_ref[...])).astype(BF16)

    qb_ref[...] = (proj(P_BQ, B_WIDTH) * (1.0 / math.sqrt(B_HEAD_DIM))).astype(BF16)
    zk = proj(P_BK, B_WIDTH)
    zv = proj(P_BV, B_WIDTH)
    if sample:
        kv_refs[0][...] = zk
        kv_refs[1][...] = zv
        kv_refs[2][...] = zk.astype(BF16)
        kv_refs[3][...] = zv.astype(BF16)
    else:
        kv_refs[0][0] = zk.T
        kv_refs[1][0] = zv.T
        for other in range(1, kv_refs[0].shape[0]):
            kv_refs[0][other] = jnp.zeros(kv_refs[0].shape[1:], F32)
            kv_refs[1][other] = jnp.zeros(kv_refs[1].shape[1:], F32)

    qc_ref[...] = proj(P_CQ, C_WIDTH_PAD).astype(BF16)
    zk = proj(P_CK, C_WIDTH_PAD) * (C_HEAD_DIM ** -0.5)
    kc_ref[...] = zk.astype(BF16)
    kct_ref[...] = zk.T.astype(BF16)
    vc_ref[...] = proj(P_CV, C_WIDTH_PAD).astype(BF16)
    oc_ref[...] = proj(P_CO, C_WIDTH_PAD)
    g = proj(P_CG, LANES) + bg_ref[...]
    lane = lax.broadcasted_iota(jnp.int32, g.shape, 1)
    gates = jnp.where(lane < C_HEADS, g, _log_sigmoid(g))
    gt_ref[...] = gates
    gtt_ref[...] = gates.T[:GATE_ROWS, :]


def _proj(x, gmix, wp, bg, lng, lnb, ws, bs, tm, chunk, sample, layer=0, kv_all=None):
    n = x.shape[0]
    row = lambda w: pl.BlockSpec((tm, w), lambda i: (i, 0))
    full = lambda a: pl.BlockSpec(a.shape, lambda i: (0,) * a.ndim)
    col = lambda h: pl.BlockSpec((h, tm), lambda i: (0, i))
    outs = [((n, A_WIDTH), BF16, row(A_WIDTH)), ((n, B_WIDTH), BF16, row(B_WIDTH)),
            ((n, C_WIDTH_PAD), BF16, row(C_WIDTH_PAD)), ((n, C_WIDTH_PAD), BF16, row(C_WIDTH_PAD)),
            ((C_WIDTH_PAD, n), BF16, col(C_WIDTH_PAD)), ((n, C_WIDTH_PAD), BF16, row(C_WIDTH_PAD)),
            ((n, C_WIDTH_PAD), F32, row(C_WIDTH_PAD)), ((n, LANES), F32, row(LANES)),
            ((GATE_ROWS, n), F32, col(GATE_ROWS))]
    out_shape = [jax.ShapeDtypeStruct(s, dt) for s, dt, _ in outs]
    out_specs = [spec for _, _, spec in outs]
    if sample:
        extra = [(B_WIDTH, F32), (B_WIDTH, F32), (B_WIDTH, BF16), (B_WIDTH, BF16), (A_WIDTH, F32)]
        out_shape += [jax.ShapeDtypeStruct((n, w), dt) for w, dt in extra]
        out_specs += [row(w) for w, _ in extra]
    ins = [x, gmix, wp, bg, lng, lnb, ws, bs]
    in_specs = [row(D_MODEL), full(gmix), full(wp), full(bg), full(lng), full(lnb), full(ws), full(bs)]
    aliases = {}
    if not sample:
        out_shape += [jax.ShapeDtypeStruct((DEPTH, B_WIDTH, n), F32)] * 2
        if kv_all is None:
            out_specs += [pl.BlockSpec((DEPTH, B_WIDTH, tm), lambda i: (0, 0, i))] * 2
        else:
            out_specs += [pl.BlockSpec((1, B_WIDTH, tm), lambda i: (layer, 0, i))] * 2
            aliases = {len(ins): len(out_shape) - 2, len(ins) + 1: len(out_shape) - 1}
            ins += list(kv_all)
            in_specs += [pl.BlockSpec(memory_space=pl.ANY)] * 2
    return pl.pallas_call(
        functools.partial(_proj_kernel, tm // chunk, chunk, sample),
        out_shape=out_shape,
        grid=(n // tm,),
        in_specs=in_specs,
        out_specs=out_specs,
        input_output_aliases=aliases,
        compiler_params=_cparams(("parallel",)),
        name="proj",
    )(*ins)


def _sb_weights(qh, kblk, carry, tri2, mask, transposed, shift=None):
    z = _dot(qh, kblk) if transposed else _dot_nt(qh, kblk)
    if shift is not None:
        z = z + shift
    tk = z.shape[1]
    drop = jnp.maximum(z, 0.0) + jnp.log(1.0 + jnp.exp(-jnp.abs(z)))
    if mask is not None:
        drop = jnp.where(mask, drop, 0.0)
    cs = _dot(jnp.concatenate(_split_bf16(drop), axis=1), tri2)
    if tk >= LANES:
        carry_b = jnp.concatenate([carry] * (tk // LANES), axis=1)
    else:
        carry_b = carry[:, :tk]
    a = jnp.exp(z - cs - carry_b)
    if mask is not None:
        a = jnp.where(mask, a, 0.0)
    return a.astype(BF16), carry + jnp.broadcast_to(cs[:, :1], carry.shape)


def _sb_step(qh, kblk, vblk, carry, acc, tri2, mask, transposed):
    a, carry = _sb_weights(qh, kblk, carry, tri2, mask, transposed)
    return carry, acc + (_dot_nt(a, vblk) if transposed else _dot(a, vblk))


def _tri2(tk):
    j = lax.broadcasted_iota(jnp.int32, (2 * tk, tk), 0)
    s = lax.broadcasted_iota(jnp.int32, (2 * tk, tk), 1)
    return (jnp.where(j >= tk, j - tk, j) >= s).astype(BF16)


def _sb_finish(acc_s, gn_ref, o_ref, head0):
    out = jnp.where(head0, acc_s[0], acc_s[1])
    r = lax.broadcasted_iota(jnp.int32, (LANES, LANES), 0) // B_HEAD_DIM
    c = lax.broadcasted_iota(jnp.int32, (LANES, LANES), 1) // B_HEAD_DIM
    same_head = (r == c).astype(BF16)
    hi, lo = _split_bf16(out * out)
    ms = (_dot(hi, same_head) + _dot(lo, same_head)) * (1.0 / B_HEAD_DIM)
    return (out * lax.rsqrt(ms + EPS) * gn_ref[...]).astype(o_ref.dtype)


def _sb_walk(qh, load_kv, first_block, carry_s, acc_s, tri):
    def alive():
        return jnp.minimum(jnp.min(carry_s[0]), jnp.min(carry_s[1])) <= -SB_DEAD_LOG_WEIGHT

    def cond(st):
        j, live = st
        return jnp.logical_and(j >= 0, live)

    def body(st):
        j, _ = st
        kblk, vblk = load_kv(j)
        for h in range(2):
            carry, acc = _sb_step(qh[h], kblk, vblk, carry_s[h], acc_s[h], tri, None, True)
            carry_s[h] = carry
            acc_s[h] = acc
        return j - 1, alive()

    lax.while_loop(cond, body, (first_block, alive()))


def _sb_prompt_kernel(q_ref, k_ref, v_ref, gn_ref, o_ref, carry_s, acc_s):
    tq = SB_BLOCK
    n_q = q_ref.shape[0] // tq
    head0 = lax.broadcasted_iota(jnp.int32, (tq, LANES), 1) < B_HEAD_DIM
    tri = _tri2(tq)
    t = lax.broadcasted_iota(jnp.int32, (tq, tq), 0)
    s = lax.broadcasted_iota(jnp.int32, (tq, tq), 1)
    causal = s < t
    zeros = jnp.zeros((tq, LANES), F32)

    def load_kv(j):
        cols = pl.ds(pl.multiple_of(j * tq, tq), tq)
        return k_ref[:, cols].astype(BF16), v_ref[:, cols].astype(BF16)

    first = pl.program_id(1) * n_q
    kv = [load_kv(jnp.maximum(first + b - 1, 0)) for b in range(n_q + 1)]
    qhs = []
    for b in range(n_q):
        q = q_ref[b * tq:(b + 1) * tq, :]
        qh = [jnp.where(head0, q, 0), jnp.where(head0, 0, q)]
        qhs.append(qh)
        no_prev = jnp.where(first + b >= 1, 0.0, NEG_BIG)
        v_both = jnp.concatenate([kv[b + 1][1], kv[b][1]], axis=1)
        for h in range(2):
            a_diag, carry = _sb_weights(qh[h], kv[b + 1][0], zeros, tri, causal, True)
            a_prev, carry = _sb_weights(qh[h], kv[b][0], carry, tri, None, True, no_prev)
            carry_s[b, h] = carry
            acc_s[b, h] = _dot_nt(jnp.concatenate([a_diag, a_prev], axis=1), v_both)
    for b in range(n_q):
        _sb_walk(qhs[b], load_kv, first + b - 2, carry_s.at[b], acc_s.at[b], tri)
        o_ref[b * tq:(b + 1) * tq, :] = _sb_finish(acc_s.at[b], gn_ref, o_ref, head0)


def _sb_prompt(q, k, v, gn, layer):
    n = q.shape[0]
    n_q = SB_BLOCKS_PER_STEP
    tq = SB_BLOCK * n_q
    blk = pl.BlockSpec((tq, LANES), lambda p, i: (i, p))
    seq = pl.BlockSpec((None, LANES, n), lambda p, i: (layer, p, 0))
    state = pltpu.VMEM((n_q, 2, SB_BLOCK, LANES), F32)
    return pl.pallas_call(
        _sb_prompt_kernel,
        out_shape=jax.ShapeDtypeStruct((n, B_WIDTH), BF16),
        grid=(B_PAIRS, n // tq),
        in_specs=[blk, seq, seq, pl.BlockSpec((1, LANES), lambda p, i: (0, p))],
        out_specs=blk,
        scratch_shapes=[state, state],
        compiler_params=_cparams(("parallel", "parallel")),
        name="sb_prompt",
    )(q, k, v, gn)


def _sb_sample_kernel(q_ref, kn_ref, vn_ref, kc_ref, vc_ref, gn_ref, o_ref, carry_s, acc_s):
    tq = q_ref.shape[1]
    tk = SB_BLOCK
    head0 = lax.broadcasted_iota(jnp.int32, (tq, LANES), 1) < B_HEAD_DIM
    q = q_ref[0]
    zero = jnp.zeros_like(q)
    qh = [jnp.where(head0, q, zero), jnp.where(head0, zero, q)]
    t = lax.broadcasted_iota(jnp.int32, (tq, tq), 0)
    s = lax.broadcasted_iota(jnp.int32, (tq, tq), 1)
    causal = s < t
    zeros = jnp.zeros((tq, LANES), F32)
    def load_kv(j):
        cols = pl.ds(pl.multiple_of(j * tk, tk), tk)
        return kc_ref[0, :, cols].astype(BF16), vc_ref[0, :, cols].astype(BF16)

    last = kc_ref.shape[2] // tk - 1
    kp, vp = load_kv(last)
    tri = _tri2(tk)
    for h in range(2):
        carry, acc = _sb_step(qh[h], kn_ref[0], vn_ref[0], zeros, zeros, _tri2(tq), causal, False)
        carry, acc = _sb_step(qh[h], kp, vp, carry, acc, tri, None, True)
        carry_s[h] = carry
        acc_s[h] = acc
    _sb_walk(qh, load_kv, last - 1, carry_s, acc_s, tri)
    o_ref[0] = _sb_finish(acc_s, gn_ref, o_ref, head0)


def _sb_sample(q, kn, vn, kc, vc, gn, layer):
    nb, tq, _ = q.shape
    past = kc.shape[3]
    new = pl.BlockSpec((1, tq, LANES), lambda b, p: (b, 0, p))
    old = pl.BlockSpec((None, 1, LANES, past), lambda b, p: (layer, b, p, 0))
    return pl.pallas_call(
        _sb_sample_kernel,
        out_shape=jax.ShapeDtypeStruct((nb, tq, B_WIDTH), BF16),
        grid=(nb, B_PAIRS),
        in_specs=[new, new, new, old, old, pl.BlockSpec((1, LANES), lambda b, p: (0, p))],
        out_specs=new,
        scratch_shapes=[pltpu.VMEM((2, tq, LANES), F32), pltpu.VMEM((2, tq, LANES), F32)],
        compiler_params=_cparams(("parallel", "parallel")),
        name="sb_sample",
    )(q, kn, vn, kc, vc, gn)


def _split3(x):
    h1 = x.astype(BF16)
    r1 = x - h1.astype(F32)
    h2 = r1.astype(BF16)
    return h1, h2, (r1 - h2.astype(F32)).astype(BF16)


def _dot3(x, rhs01):
    return _dot(jnp.concatenate(_split3(x), axis=1), jnp.concatenate([rhs01] * 3, axis=0))


def _dot3_left(lhs01, x):
    return _dot(jnp.concatenate([lhs01] * 3, axis=1), jnp.concatenate(_split3(x), axis=0))


def _mlstm_kernel(n_sub, q_ref, k_ref, kt_ref, v_ref, o_ref, gt_ref, gtt_ref, gn_ref, c0_ref, n0_ref, m0_ref,
                  yc_ref, c_out, n_out, m_out, c_s, n_s, m_s):
    L = MLSTM_BLOCK
    t_blk = pl.program_id(1)

    @pl.when(t_blk == 0)
    def _():
        c_s[...] = c0_ref[0]
        n_s[...] = n0_ref[0]
        m_s[...] = m0_ref[0]

    r = lax.broadcasted_iota(jnp.int32, (L, L), 0)
    c = lax.broadcasted_iota(jnp.int32, (L, L), 1)
    causal = c <= r
    upper = (r <= c).astype(BF16)
    ones_sq = jnp.ones((L, LANES), BF16)
    sel_r = lax.broadcasted_iota(jnp.int32, (LANES, C_WIDTH_PAD), 0)
    sel_c = lax.broadcasted_iota(jnp.int32, (LANES, C_WIDTH_PAD), 1) // C_HEAD_PAD
    sel_p = (sel_r == sel_c).astype(BF16)
    sel_b = (sel_r == sel_c + C_HEADS).astype(BF16)
    lane = c

    a_rows_all, pb_cols_all = [], []
    for sub in range(n_sub):
        rows = slice(sub * L, (sub + 1) * L)
        gtt = gtt_ref[0, :, rows]
        bct = _dot3(gtt, upper)
        a_rows_all.append(gtt[:C_HEADS] - bct[C_HEADS:])
        gt = gt_ref[0, rows, :]
        bc = _dot3_left(causal.astype(BF16), gt)
        pmax = gt - pltpu.roll(bc, LANES - C_HEADS, axis=1)
        for sh in (1, 2, 4, 8, 16, 32, 64):
            pmax = jnp.maximum(pmax, jnp.where(r >= sh, pltpu.roll(pmax, sh, axis=0), -jnp.inf))
        pb_cols_all.append(jnp.where(lane < C_HEADS, pmax, bc))
    pb_cols_all = jnp.concatenate(pb_cols_all, axis=0)
    p_all = _dot3(pb_cols_all, sel_p)
    b_all = _dot3(pb_cols_all, sel_b)

    state = [(c_s[h], n_s[h], m_s[h]) for h in range(C_HEADS)]
    for sub in range(n_sub):
        rows = slice(sub * L, (sub + 1) * L)
        a_rows = a_rows_all[sub]
        heads = []
        for h in range(C_HEADS):
            lanes = slice(h * C_HEAD_PAD, (h + 1) * C_HEAD_PAD)
            q = q_ref[0, rows, lanes]
            v1 = jnp.concatenate([v_ref[0, rows, lanes], ones_sq], axis=1)
            p_rep = p_all[rows, lanes]
            b_rep = b_all[rows, lanes]
            a_row = a_rows[h:h + 1, :]
            p_last = p_rep[L - 1:L, :]
            c_prev, n_prev, m_prev = state[h]

            w = jnp.exp(jnp.where(causal, a_row - p_rep, -jnp.inf)) * _dot_nt(q, k_ref[0, rows, lanes])
            kwt = (kt_ref[0, lanes, rows].astype(F32) * jnp.exp(a_row - p_last)).astype(BF16)
            both = _dot(jnp.concatenate([w.astype(BF16), kwt], axis=0), v1)
            here_sums = both[:L]
            fresh = both[L:]
            past_sums = _dot(q, jnp.concatenate([c_prev, n_prev], axis=1).astype(BF16))
            top = jnp.maximum(m_prev, p_rep)
            past = jnp.exp(m_prev - top)
            here = jnp.exp(p_rep - top)
            mix = jnp.concatenate([past, past], axis=1) * past_sums + jnp.concatenate([here, here], axis=1) * here_sums
            den = jnp.maximum(jnp.abs(mix[:, LANES:]), jnp.exp(-(b_rep + top)))
            heads.append(mix[:, :LANES] / den)

            top_last = jnp.maximum(m_prev, p_last)
            decay = jnp.exp(m_prev - top_last)
            gain = jnp.exp(p_last - top_last)
            state[h] = (decay * c_prev + gain * fresh[:, :LANES], decay * n_prev + gain * fresh[:, LANES:],
                        b_rep[L - 1:L, :] + top_last)

        hh = jnp.concatenate(heads, axis=0)
        ms = _dot(jnp.concatenate(_split_bf16(hh * hh), axis=1),
                  jnp.concatenate([ones_sq, ones_sq], axis=0)) * (1.0 / C_HEAD_DIM)
        hn = hh * lax.rsqrt(ms + EPS)
        for h in range(C_HEADS):
            lanes = slice(h * C_HEAD_PAD, (h + 1) * C_HEAD_PAD)
            yc_ref[0, rows, lanes] = (hn[h * L:(h + 1) * L] * gn_ref[:, lanes]
                                      * _sigmoid(o_ref[0, rows, lanes])).astype(BF16)

    for h in range(C_HEADS):
        c_s[h], n_s[h], m_s[h] = state[h]

    @pl.when(t_blk == pl.num_programs(1) - 1)
    def _():
        c_out[0] = c_s[...]
        n_out[0] = n_s[...]
        m_out[0] = m_s[...]


def _mlstm(q, k, kt, v, o, gt, gtt, gn, c0, n0, m0, n_sub):
    nb, n, _ = q.shape
    tb = n_sub * MLSTM_BLOCK
    seq = lambda w: pl.BlockSpec((1, tb, w), lambda b, t: (b, t, 0))
    seq_t = lambda h: pl.BlockSpec((1, h, tb), lambda b, t: (b, 0, t))
    st = lambda a: pl.BlockSpec((1,) + a.shape[1:], lambda b, t: (b,) + (0,) * (a.ndim - 1))
    return pl.pallas_call(
        functools.partial(_mlstm_kernel, n_sub),
        out_shape=[jax.ShapeDtypeStruct((nb, n, C_WIDTH_PAD), BF16),
                   jax.ShapeDtypeStruct(c0.shape, F32),
                   jax.ShapeDtypeStruct(n0.shape, F32),
                   jax.ShapeDtypeStruct(m0.shape, F32)],
        grid=(nb, n // tb),
        in_specs=[seq(C_WIDTH_PAD), seq(C_WIDTH_PAD), seq_t(C_WIDTH_PAD), seq(C_WIDTH_PAD), seq(C_WIDTH_PAD),
                  seq(LANES), seq_t(GATE_ROWS), pl.BlockSpec((1, C_WIDTH_PAD), lambda b, t: (0, 0)),
                  st(c0), st(n0), st(m0)],
        out_specs=[seq(C_WIDTH_PAD), st(c0), st(n0), st(m0)],
        scratch_shapes=[pltpu.VMEM(c0.shape[1:], F32), pltpu.VMEM(n0.shape[1:], F32),
                        pltpu.VMEM(m0.shape[1:], F32)],
        compiler_params=_cparams(("parallel", "arbitrary")),
        name="mlstm",
    )(q, k, kt, v, o, gt, gtt, gn, c0, n0, m0)


def _mixer_out(x_ref, ya_ref, yb_ref, yc_ref, wa_ref, wb_ref, wc_ref, g_ref, xmid_ref):
    y = _dot(ya_ref[...], wa_ref[...]) + _dot(yb_ref[...], wb_ref[...]) + _dot(yc_ref[...], wc_ref[...])
    x = x_ref[...] + y
    xmid_ref[...] = x
    return x * lax.rsqrt(jnp.mean(x * x, axis=-1, keepdims=True) + EPS) * g_ref[...]


def _merge_dense_kernel(n_cast, x_ref, ya_ref, yb_ref, yc_ref, wa_ref, wb_ref, wc_ref, g_ref, *refs):
    casts_in, xmid_ref, xn_ref, casts_out = refs[:n_cast], refs[n_cast], refs[n_cast + 1], refs[n_cast + 2:]
    xn_ref[...] = _mixer_out(x_ref, ya_ref, yb_ref, yc_ref, wa_ref, wb_ref, wc_ref, g_ref,
                             xmid_ref).astype(BF16)
    for src, dst in zip(casts_in, casts_out):
        dst[...] = src[...].astype(BF16)


def _merge_moe_kernel(x_ref, ya_ref, yb_ref, yc_ref, wa_ref, wb_ref, wc_ref, g_ref, wrt_ref,
                      earlier_ref, xmid_ref, xrow_ref, ri_ref, rf_ref, cnt_ref, *rest):
    tm = x_ref.shape[0]
    run_s = rest[-1]
    if len(rest) == 2:
        rest[0][...] = jnp.zeros(rest[0].shape, F32)

    @pl.when(pl.program_id(0) == 0)
    def _():
        run_s[...] = jnp.zeros_like(run_s)

    xn = _mixer_out(x_ref, ya_ref, yb_ref, yc_ref, wa_ref, wb_ref, wc_ref, g_ref, xmid_ref)
    _store_token_tiles(xrow_ref, xn)

    hi, lo = _split_bf16(xn)
    part = _dot_nt(wrt_ref[...], hi)
    lg = part[:N_EXPERTS] + part[N_EXPERTS:] + _dot_nt(wrt_ref[:N_EXPERTS, :], lo)
    expert = lax.broadcasted_iota(jnp.int32, lg.shape, 0)
    m1 = jnp.max(lg, axis=0, keepdims=True)
    i1 = jnp.min(jnp.where(lg == m1, expert, N_EXPERTS), axis=0, keepdims=True)
    lg2 = jnp.where(expert == i1, -jnp.inf, lg)
    m2 = jnp.max(lg2, axis=0, keepdims=True)
    i2 = jnp.min(jnp.where(lg2 == m2, expert, N_EXPERTS), axis=0, keepdims=True)
    e2 = jnp.exp(m2 - m1)
    g1 = 1.0 / (1.0 + e2)
    g2 = e2 * g1

    sel1 = expert == i1
    sel2 = expert == i2
    onehot = jnp.logical_or(sel1, sel2).astype(BF16)
    before = _dot(onehot, earlier_ref[...]) + jnp.concatenate([run_s[...]] * (tm // LANES), axis=1)
    rank1 = jnp.sum(jnp.where(sel1, before, 0.0), axis=0, keepdims=True).astype(jnp.int32)
    rank2 = jnp.sum(jnp.where(sel2, before, 0.0), axis=0, keepdims=True).astype(jnp.int32)
    run_s[...] += _dot(onehot, jnp.ones((tm, LANES), BF16))
    cnt_ref[...] = run_s[...].astype(jnp.int32)
    ri_ref[...] = jnp.where(expert == 0, i1, jnp.where(expert == 1, i2,
                            jnp.where(expert == 2, rank1, jnp.where(expert == 3, rank2, 0))))
    rf_ref[...] = jnp.where(expert == 0, g1, jnp.where(expert == 1, g2, 0.0))


def _merge(x, ya, yb, yc, wa, wb, wc, g, wr, tm, cast=(), zero_rows=0):
    n = x.shape[0]
    assert zero_rows % (n // tm) == 0
    assert all(a.shape[0] // CAST_SLAB_ROWS <= n // tm for a in cast)
    row = lambda w: pl.BlockSpec((tm, w), lambda i: (i, 0))
    full = lambda a: pl.BlockSpec(a.shape, lambda i: (0,) * a.ndim)
    ins = [x, ya, yb, yc, wa, wb, wc, g]
    in_specs = [row(D_MODEL), row(A_WIDTH), row(B_WIDTH), row(C_WIDTH_PAD), full(wa), full(wb), full(wc), full(g)]
    if wr is None:
        slab = lambda a: pl.BlockSpec((CAST_SLAB_ROWS, a.shape[1]),
                                      lambda i: (jnp.minimum(i, a.shape[0] // CAST_SLAB_ROWS - 1), 0))
        outs = pl.pallas_call(
            functools.partial(_merge_dense_kernel, len(cast)),
            out_shape=[jax.ShapeDtypeStruct((n, D_MODEL), F32), jax.ShapeDtypeStruct((n, D_MODEL), BF16)]
            + [jax.ShapeDtypeStruct(a.shape, BF16) for a in cast],
            grid=(n // tm,), in_specs=in_specs + [slab(a) for a in cast],
            out_specs=[row(D_MODEL), row(D_MODEL)] + [slab(a) for a in cast],
            compiler_params=_cparams(("arbitrary",)), name="merge_dense",
        )(*ins, *cast)
        return outs[0], outs[1], outs[2:]
    earlier = (jnp.arange(tm)[:, None] < jnp.arange(tm)[None, :]).astype(BF16)
    col = pl.BlockSpec((N_EXPERTS, tm), lambda i: (0, i))
    out_shape = [jax.ShapeDtypeStruct((n, D_MODEL), F32),
                 jax.ShapeDtypeStruct((n * ROW_TILE, LANES), F32),
                 jax.ShapeDtypeStruct((N_EXPERTS, n), jnp.int32),
                 jax.ShapeDtypeStruct((N_EXPERTS, n), F32),
                 jax.ShapeDtypeStruct((N_EXPERTS, LANES), jnp.int32)]
    out_specs = [row(D_MODEL), pl.BlockSpec((tm * ROW_TILE, LANES), lambda i: (i, 0)),
                 col, col, pl.BlockSpec((N_EXPERTS, LANES), lambda i: (0, 0))]
    if zero_rows:
        steps = n // tm
        out_shape.append(jax.ShapeDtypeStruct((zero_rows, ROW_TILE, LANES), F32))
        out_specs.append(pl.BlockSpec((zero_rows // steps, ROW_TILE, LANES), lambda i: (i, 0, 0)))
    return pl.pallas_call(
        _merge_moe_kernel,
        out_shape=out_shape,
        grid=(n // tm,), in_specs=in_specs + [full(wr), full(earlier)],
        out_specs=out_specs,
        scratch_shapes=[pltpu.VMEM((N_EXPERTS, LANES), F32)],
        compiler_params=_cparams(("arbitrary",)), name="merge_moe",
    )(*ins, wr, earlier)


def _dest_blocks(dest, tt):
    return jnp.transpose(dest.reshape(2, dest.shape[1] // tt, tt), (1, 0, 2))


def _dispatch_kernel(dest_ref, src_ref, xs_in_ref, xs_ref, sem):
    del xs_in_ref
    tt = dest_ref.shape[2]

    def issue(t, carry):
        rows = pl.ds(pl.multiple_of(t * ROW_TILE, ROW_TILE), ROW_TILE)
        for k in range(2):
            pltpu.make_async_copy(src_ref.at[rows], xs_ref.at[dest_ref[0, k, t]], sem).start(priority=k)
        return carry

    lax.fori_loop(0, tt, issue, 0, unroll=8)

    def drain(t, carry):
        for k in range(2):
            pltpu.make_async_copy(src_ref.at[pl.ds(0, ROW_TILE)], xs_ref.at[0], sem).wait()
        return carry

    lax.fori_loop(0, tt, drain, 0, unroll=8)


def _dispatch(dest, src, xs, tt):
    n = src.shape[0] // ROW_TILE
    dest3 = _dest_blocks(dest, tt)
    return pl.pallas_call(
        _dispatch_kernel,
        out_shape=jax.ShapeDtypeStruct(xs.shape, xs.dtype),
        grid=(n // tt,),
        in_specs=[pl.BlockSpec((1, 2, tt), lambda i: (i, 0, 0), memory_space=pltpu.SMEM),
                  pl.BlockSpec((tt * ROW_TILE, LANES), lambda i: (i, 0)), pl.BlockSpec(memory_space=pl.ANY)],
        out_specs=pl.BlockSpec(memory_space=pl.ANY),
        scratch_shapes=[pltpu.SemaphoreType.DMA(())],
        input_output_aliases={2: 0},
        compiler_params=_cparams(("arbitrary",)), name="moe_dispatch",
    )(dest3, src, xs)


def _combine_kernel(final, dest_ref, next_ref, ys_ref, rf_ref, xmid_ref, gf_ref, o_ref, buf_s, sems):
    tt = xmid_ref.shape[0]
    i = pl.program_id(0)
    n_steps = pl.num_programs(0)

    def gather(idx_ref, slot):
        def issue(t, carry):
            rows = pl.ds(pl.multiple_of(t * ROW_TILE, ROW_TILE), ROW_TILE)
            for k in range(2):
                pltpu.make_async_copy(ys_ref.at[idx_ref[0, k, t]], buf_s.at[slot, k, rows],
                                      sems.at[slot]).start(priority=k)
            return carry

        lax.fori_loop(0, tt, issue, 0, unroll=8)

    slot = i % 2

    @pl.when(i == 0)
    def _():
        gather(dest_ref, 0)

    @pl.when(i + 1 < n_steps)
    def _():
        gather(next_ref, 1 - slot)

    def drain(t, carry):
        for k in range(2):
            pltpu.make_async_copy(ys_ref.at[0], buf_s.at[slot, 0, pl.ds(0, ROW_TILE)], sems.at[slot]).wait()
        return carry

    lax.fori_loop(0, tt, drain, 0, unroll=8)

    y = (xmid_ref[...] + rf_ref[:, 0:1] * _load_token_tiles(buf_s.at[slot, 0])
         + rf_ref[:, 1:2] * _load_token_tiles(buf_s.at[slot, 1]))
    if final:
        y = y * lax.rsqrt(jnp.mean(y * y, axis=-1, keepdims=True) + EPS) * gf_ref[...]
    o_ref[...] = y


def _combine(dest, ys, rf, xmid, gf, tt, final):
    n = xmid.shape[0]
    n_steps = n // tt
    dest3 = _dest_blocks(dest, tt)
    row = lambda w: pl.BlockSpec((tt, w), lambda i: (i, 0))
    return pl.pallas_call(
        functools.partial(_combine_kernel, final),
        out_shape=jax.ShapeDtypeStruct((n, D_MODEL), F32),
        grid=(n_steps,),
        in_specs=[pl.BlockSpec((1, 2, tt), lambda i: (i, 0, 0), memory_space=pltpu.SMEM),
                  pl.BlockSpec((1, 2, tt), lambda i: (jnp.minimum(i + 1, n_steps - 1), 0, 0),
                               memory_space=pltpu.SMEM),
                  pl.BlockSpec(memory_space=pl.ANY), row(LANES), row(D_MODEL),
                  pl.BlockSpec((1, D_MODEL), lambda i: (0, 0))],
        out_specs=row(D_MODEL),
        scratch_shapes=[pltpu.VMEM((2, 2, tt * ROW_TILE, LANES), F32), pltpu.SemaphoreType.DMA((2,))],
        compiler_params=_cparams(("arbitrary",)), name="moe_combine",
    )(dest3, dest3, ys, rf, xmid, gf)


def _swiglu_acc(xn, wg_ref, wu_ref, wd_ref, acc_s):
    g = _dot(xn, wg_ref[0])
    u = _dot(xn, wu_ref[0])
    acc_s[...] += _dot((g * _sigmoid(g) * u).astype(BF16), wd_ref[0])


def _ffn_dense_kernel(final, n_cast, xn_ref, wg_ref, wu_ref, wd_ref, xmid_ref, gf_ref, *refs):
    casts_in, o_ref, casts_out, acc_s = refs[:n_cast], refs[n_cast], refs[n_cast + 1:-1], refs[-1]
    f = pl.program_id(1)

    @pl.when(f == 0)
    def _():
        acc_s[...] = jnp.zeros_like(acc_s)

    _swiglu_acc(xn_ref[...], wg_ref, wu_ref, wd_ref, acc_s)
    for src, dst in zip(casts_in, casts_out):
        dst[...] = src[...].astype(BF16)

    @pl.when(f == pl.num_programs(1) - 1)
    def _():
        y = xmid_ref[...] + acc_s[...]
        if final:
            y = y * lax.rsqrt(jnp.mean(y * y, axis=-1, keepdims=True) + EPS) * gf_ref[...]
        o_ref[...] = y


def _ffn_dense(xn, wg, wu, wd, xmid, gf, tm, final, cast=()):
    n = xn.shape[0]
    tf = FFN_F_TILE
    n_f = D_FF // tf
    steps = (n // tm) * n_f
    row = lambda w: pl.BlockSpec((tm, w), lambda i, f: (i, 0))
    slabs = [a.reshape(steps, a.size // (steps * a.shape[-1]), a.shape[-1]) for a in cast]
    slab_spec = lambda a: pl.BlockSpec((1,) + a.shape[1:], lambda i, f: (i * n_f + f, 0, 0))
    outs = pl.pallas_call(
        functools.partial(_ffn_dense_kernel, final, len(cast)),
        out_shape=[jax.ShapeDtypeStruct((n, D_MODEL), F32)] + [jax.ShapeDtypeStruct(a.shape, BF16) for a in slabs],
        grid=(n // tm, n_f),
        in_specs=[row(D_MODEL),
                  pl.BlockSpec((1, D_MODEL, tf), lambda i, f: (0, 0, f)),
                  pl.BlockSpec((1, D_MODEL, tf), lambda i, f: (0, 0, f)),
                  pl.BlockSpec((1, tf, D_MODEL), lambda i, f: (0, f, 0)),
                  row(D_MODEL), pl.BlockSpec((1, D_MODEL), lambda i, f: (0, 0))] + [slab_spec(a) for a in slabs],
        out_specs=[row(D_MODEL)] + [slab_spec(a) for a in slabs],
        scratch_shapes=[pltpu.VMEM((tm, D_MODEL), F32)],
        compiler_params=_cparams(("parallel", "arbitrary")),
        name="ffn_dense",
    )(xn, wg, wu, wd, xmid, gf, *slabs)
    return outs[0], [o.reshape(a.shape) for o, a in zip(outs[1:], cast)]


def _ffn_routed_kernel(te_ref, nu_ref, xs_ref, wg_ref, wu_ref, wd_ref, ys_ref, xb_s, acc_s):
    del te_ref
    i = pl.program_id(0)
    f = pl.program_id(1)

    @pl.when(i < nu_ref[0])
    def _():
        @pl.when(f == 0)
        def _():
            acc_s[...] = jnp.zeros_like(acc_s)
            xb_s[...] = _load_token_tiles(xs_ref).astype(BF16)

        _swiglu_acc(xb_s[...], wg_ref, wu_ref, wd_ref, acc_s)

        @pl.when(f == pl.num_programs(1) - 1)
        def _():
            _store_token_tiles(ys_ref, acc_s[...])

    @pl.when(jnp.logical_and(i >= nu_ref[0], f == 0))
    def _():
        ys_ref[...] = jnp.zeros_like(ys_ref)


def _ffn_routed(tile_expert, n_used, xs, wg, wu, wd):
    n_tiles = tile_expert.shape[0]
    tm, tf = MOE_TILE, FFN_F_TILE
    n_f = D_FF // tf
    last = lambda i, nu: jnp.minimum(i, nu[0] - 1)
    fcol = lambda i, f, nu: jnp.where(i < nu[0], f, n_f - 1)
    return pl.pallas_call(
        _ffn_routed_kernel,
        out_shape=jax.ShapeDtypeStruct(xs.shape, F32),
        grid_spec=pltpu.PrefetchScalarGridSpec(
            num_scalar_prefetch=2,
            grid=(n_tiles, n_f),
            in_specs=[pl.BlockSpec((tm * ROW_TILE, LANES), lambda i, f, te, nu: (last(i, nu), 0)),
                      pl.BlockSpec((1, D_MODEL, tf), lambda i, f, te, nu: (te[last(i, nu)], 0, fcol(i, f, nu))),
                      pl.BlockSpec((1, D_MODEL, tf), lambda i, f, te, nu: (te[last(i, nu)], 0, fcol(i, f, nu))),
                      pl.BlockSpec((1, tf, D_MODEL), lambda i, f, te, nu: (te[last(i, nu)], fcol(i, f, nu), 0))],
            out_specs=pl.BlockSpec((tm * ROW_TILE, LANES), lambda i, f, te, nu: (i, 0)),
            scratch_shapes=[pltpu.VMEM((tm, D_MODEL), BF16), pltpu.VMEM((tm, D_MODEL), F32)]),
        compiler_params=_cparams(("arbitrary", "arbitrary")),
        name="ffn_routed",
    )(tile_expert, n_used, xs, wg, wu, wd)


def _route_plan(ri_p, cnt_p, ri_s, cnt_s, n_tiles):
    cnt_p, cnt_s = cnt_p[:, 0], cnt_s[:, 0]
    tiles = (cnt_p + cnt_s + MOE_TILE - 1) // MOE_TILE
    ends = jnp.cumsum(tiles)
    start = (ends - tiles) * MOE_TILE
    lookup = lambda table, idx: jnp.sum(
        jnp.where(idx[..., None] == jnp.arange(N_EXPERTS, dtype=jnp.int32), table, 0), axis=-1)
    dest_p = lookup(start, ri_p[0:2]) + ri_p[2:4]
    dest_s = lookup(start + cnt_p, ri_s[0:2]) + ri_s[2:4]
    tile_expert = jnp.minimum(jnp.sum(jnp.arange(n_tiles, dtype=jnp.int32)[:, None] >= ends[None, :], axis=-1),
                              N_EXPERTS - 1).astype(jnp.int32)
    return dest_p, dest_s, tile_expert, ends[-1:].astype(jnp.int32)


def _pad_heads_cols(w):
    w = w.reshape(w.shape[0], C_HEADS, C_HEAD_DIM)
    return jnp.pad(w, ((0, 0), (0, 0), (0, C_HEAD_PAD - C_HEAD_DIM))).reshape(w.shape[0], C_WIDTH_PAD)


def _layer_params(l, w_in, b_gate, w_s, b_s, gn_c, w_out):
    w = w_in[l]
    gates = jnp.pad(w[:, OFF_CG:OFF_CG + 2 * C_HEADS], ((0, 0), (0, LANES - 2 * C_HEADS)))
    c_part = w[:, OFF_CQ:OFF_CG].reshape(D_MODEL, 4 * C_HEADS, C_HEAD_DIM)
    c_part = jnp.pad(c_part, ((0, 0), (0, 0), (0, C_HEAD_PAD - C_HEAD_DIM))).reshape(D_MODEL, 4 * C_WIDTH_PAD)
    wp = jnp.concatenate([w[:, :OFF_CQ], c_part, gates], axis=1).astype(BF16)
    bg = jnp.pad(b_gate[l], (0, LANES - 2 * C_HEADS))[None, :]
    pos = jnp.arange(GMLP_CHUNK)
    mask = (pos[None, :] // CHUNK) <= (pos[:, None] // CHUNK)
    wm = jnp.where(mask[None], w_s[l], 0.0)
    bs = jnp.repeat(b_s[l].T, A_GROUP_DIM, axis=1)
    wo = w_out[l]
    wc = wo[A_WIDTH + B_WIDTH:].reshape(C_HEADS, C_HEAD_DIM, D_MODEL)
    wc = jnp.pad(wc, ((0, 0), (0, C_HEAD_PAD - C_HEAD_DIM), (0, 0))).reshape(C_WIDTH_PAD, D_MODEL)
    gnc = _pad_heads_cols(gn_c[l][None, :])
    return dict(wp=wp, bg=bg, wm=wm, bs=bs, wa=wo[:A_WIDTH].astype(BF16),
                wb=wo[A_WIDTH:A_WIDTH + B_WIDTH].astype(BF16), wc=wc.astype(BF16), gnc=gnc)


def _pad_state(c, n, m):
    p = C_HEAD_PAD - C_HEAD_DIM
    c = jnp.pad(c, ((0, 0), (0, 0), (0, p), (0, p)))
    n = jnp.pad(n, ((0, 0), (0, 0), (0, p)))
    n = jnp.broadcast_to(n[:, :, :, None], n.shape + (LANES,))
    m = jnp.broadcast_to(m[:, :, None, None], m.shape + (1, LANES))
    return c, n, m


def _unpad_state(c, n, m):
    return c[:, :, :C_HEAD_DIM, :C_HEAD_DIM], n[:, :, :C_HEAD_DIM, 0], m[:, :, 0, 0]


def kernel(x_prompt, x_sample, cache_k_b, cache_v_b, state_c_mlstm, state_n_mlstm, state_m_mlstm,
           g_mix, w_in, b_gate, ln_a_g, ln_a_b, w_s, b_s, gn_b, gn_c, w_out,
           g_ffn, w_gate_d, w_up_d, w_down_d, w_router, w_gate_e, w_up_e, w_down_e, g_final):
    n_seq = x_prompt.shape[1]
    n_dec, n_new = x_sample.shape[0], x_sample.shape[1]
    past = cache_k_b.shape[2]
    n_samp = n_dec * n_new

    xp = x_prompt.reshape(n_seq, D_MODEL)
    xs = x_sample.reshape(n_samp, D_MODEL)
    gfin = g_final[None, :]
    keys_last = lambda a: jnp.transpose(a, (0, 1, 3, 4, 2)).reshape(DEPTH, n_dec, B_WIDTH, past)
    cache_kt, cache_vt = keys_last(cache_k_b), keys_last(cache_v_b)

    outs = {k: [] for k in ("kbp", "vbp", "cp", "np", "mp", "kbs", "vbs", "cs", "ns", "ms", "vas")}
    for l in range(DEPTH):
        p = _layer_params(l, w_in, b_gate, w_s, b_s, gn_c, w_out)
        gmix = g_mix[l][None, :]
        lng, lnb = ln_a_g[l][None, :], ln_a_b[l][None, :]
        gnb = gn_b[l][None, :]
        gffn = g_ffn[l][None, :]
        moe = l % 2 == 1
        j = l // 2
        if moe:
            wg, wu, wd = expert_bf16
            wr = jnp.concatenate(_split_bf16(w_router[j].T), axis=0)
        else:
            wr = None
        final = l == DEPTH - 1
        n_tiles = 2 * (n_seq + n_samp) // MOE_TILE + N_EXPERTS

        wm_s = jnp.kron(jnp.eye(n_dec, dtype=F32), p["wm"][:, :n_new, :n_new])
        bs_s = jnp.tile(p["bs"][:n_new], (n_dec, 1))

        (ya, qb, qc, kc, kct, vc, oc, gt, gtt, kbt, vbt) = _proj(
            xp, gmix, p["wp"], p["bg"], lng, lnb, p["wm"].astype(BF16), p["bs"], ROW_BLOCK, GMLP_CHUNK, False,
            l, None if l == 0 else (kbt, vbt))
        yb = _sb_prompt(qb, kbt, vbt, gnb, l)
        c0, n0, m0 = _pad_state(jnp.zeros((1, C_HEADS, C_HEAD_DIM, C_HEAD_DIM), F32),
                                jnp.zeros((1, C_HEADS, C_HEAD_DIM), F32), jnp.zeros((1, C_HEADS), F32))
        yc, c_f, n_f, m_f = _mlstm(qc[None], kc[None], kct[None], vc[None], oc[None], gt[None], gtt[None],
                                   p["gnc"], c0, n0, m0, MLSTM_SUB_BLOCKS)
        merged_p = _merge(xp, ya, yb, yc[0], p["wa"], p["wb"], p["wc"], gffn, wr, ROW_BLOCK,
                          () if moe else (w_gate_d[j], w_up_d[j], w_down_d[j]), n_tiles * MOE_TILE if moe else 0)
        c_f, n_f, m_f = _unpad_state(c_f, n_f, m_f)
        outs["cp"].append(c_f)
        outs["np"].append(n_f)
        outs["mp"].append(m_f)

        (ya, qb, qc, kc, kct, vc, oc, gt, gtt, kbf, vbf, kbh, vbh, va) = _proj(
            xs, gmix, p["wp"], p["bg"], lng, lnb, wm_s.astype(BF16), bs_s, n_samp, n_samp, True)
        r3 = lambda a: a.reshape(n_dec, n_new, a.shape[-1])
        yb = _sb_sample(r3(qb), r3(kbh), r3(vbh), cache_kt, cache_vt, gnb, l)
        n_pad = MLSTM_BLOCK - n_new
        padr = lambda a: jnp.pad(r3(a), ((0, 0), (0, n_pad), (0, 0)))
        per_stream = lambda a: jnp.transpose(a.reshape(a.shape[0], n_dec, n_new), (1, 0, 2))
        kct_s = jnp.pad(per_stream(kct), ((0, 0), (0, 0), (0, n_pad)))
        gate_row = jnp.arange(GATE_ROWS)[None, :, None]
        gtt_pad = jnp.broadcast_to(jnp.where(gate_row < C_HEADS, NEG_BIG, 0.0).astype(F32),
                                   (n_dec, GATE_ROWS, n_pad))
        gtt_s = jnp.concatenate([per_stream(gtt), gtt_pad], axis=2)
        gt_s = jnp.concatenate([r3(gt), jnp.broadcast_to(
            jnp.where(jnp.arange(LANES) < C_HEADS, NEG_BIG, 0.0).astype(F32), (n_dec, n_pad, LANES))], axis=1)
        c0, n0, m0 = _pad_state(state_c_mlstm[l], state_n_mlstm[l], state_m_mlstm[l])
        yc, c_u, n_u, m_u = _mlstm(padr(qc), padr(kc), kct_s, padr(vc), padr(oc), gt_s, gtt_s, p["gnc"],
                                   c0, n0, m0, 1)
        yc = yc[:, :n_new].reshape(n_samp, C_WIDTH_PAD)
        merged_s = _merge(xs, ya, yb.reshape(n_samp, B_WIDTH), yc, p["wa"], p["wb"], p["wc"],
                          gffn, wr, n_samp)
        c_u, n_u, m_u = _unpad_state(c_u, n_u, m_u)

        if moe:
            xmid_p, xrow_p, ri_p, rf_p, cnt_p, xsort = merged_p
            xmid_s, xrow_s, ri_s, rf_s, cnt_s = merged_s
            dest_p, dest_s, tile_expert, n_used = _route_plan(ri_p, cnt_p, ri_s, cnt_s, n_tiles)
            as_tiles = lambda a: a.reshape(a.shape[0] // ROW_TILE, ROW_TILE, LANES)
            gates = lambda rf: jnp.pad(rf[:2].T, ((0, 0), (0, LANES - 2)))
            xsort = _dispatch(dest_p, xrow_p, xsort, ROW_BLOCK)
            xsort = _dispatch(dest_s, xrow_s, xsort, n_samp)
            ysort = _ffn_routed(tile_expert, n_used, xsort.reshape(-1, LANES), wg, wu, wd)
            xp = _combine(dest_p, as_tiles(ysort), gates(rf_p), xmid_p, gfin, MOE_COMBINE_BLOCK, final)
            xs = _combine(dest_s, as_tiles(ysort), gates(rf_s), xmid_s, gfin, n_samp, final)
        else:
            nxt = (l + 1) // 2
            to_cast = (w_gate_e[nxt], w_up_e[nxt], w_down_e[nxt]) if l + 1 < DEPTH else ()
            wg, wu, wd = (w[None] for w in merged_p[2])
            xp, expert_bf16 = _ffn_dense(merged_p[1], wg, wu, wd, merged_p[0], gfin, ROW_BLOCK, final, to_cast)
            xs, _ = _ffn_dense(merged_s[1], wg, wu, wd, merged_s[0], gfin, n_samp, final)

        outs["kbs"].append(kbf.reshape(n_dec, n_new, B_HEADS, B_HEAD_DIM))
        outs["vbs"].append(vbf.reshape(n_dec, n_new, B_HEADS, B_HEAD_DIM))
        outs["cs"].append(c_u)
        outs["ns"].append(n_u)
        outs["ms"].append(m_u)
        outs["vas"].append(va.reshape(n_dec, n_new, A_WIDTH))

    st = lambda k: jnp.stack(outs[k])
    heads_last = lambda a: jnp.transpose(a.reshape(DEPTH, 1, B_HEADS, B_HEAD_DIM, n_seq), (0, 1, 4, 2, 3))
    return (xp.reshape(1, n_seq, D_MODEL), xs.reshape(n_dec, n_new, D_MODEL),
            heads_last(kbt), heads_last(vbt), st("cp"), st("np"), st("mp"),
            st("kbs"), st("vbs"), st("cs"), st("ns"), st("ms"), st("vas"))
```

```python
import functools
import math

import jax
import jax.numpy as jnp
from jax import lax
from jax.experimental import pallas as pl
from jax.experimental.pallas import tpu as pltpu

F32 = jnp.float32
BF16 = jnp.bfloat16

D_MODEL = 1024
DEPTH = 2
EPS = 1e-6
CHUNK = 64
A_WIDTH = 256
A_GROUPS = 4
A_GROUP_DIM = 64
GMLP_CHUNK = 128
B_HEAD_DIM = 64
B_WIDTH = 384
B_HEADS = 6
B_PAIRS = 3
C_HEADS = 4
C_HEAD_DIM = 96
C_WIDTH = 384
D_FF = 2816
N_EXPERTS = 8

LANES = 128
C_HEAD_PAD = LANES
C_WIDTH_PAD = C_HEADS * C_HEAD_PAD

OFF_AU, OFF_AV, OFF_BQ, OFF_BK, OFF_BV = 0, 256, 512, 896, 1280
OFF_CQ, OFF_CK, OFF_CV, OFF_CO, OFF_CG = 1664, 2048, 2432, 2816, 3200
P_A = 0
P_BQ = 512
P_BK = P_BQ + B_WIDTH
P_BV = P_BK + B_WIDTH
P_CQ = P_BV + B_WIDTH
P_CK = P_CQ + C_WIDTH_PAD
P_CV = P_CK + C_WIDTH_PAD
P_CO = P_CV + C_WIDTH_PAD
P_CG = P_CO + C_WIDTH_PAD

SB_DEAD_LOG_WEIGHT = -110.0
SB_BLOCK = 256
SB_BLOCKS_PER_STEP = 8
MLSTM_BLOCK = 128
MLSTM_SUB_BLOCKS = 8
GATE_ROWS = 2 * C_HEADS
NEG_BIG = -1e30
ROW_TILE = 8
MOE_TILE = 512
FFN_F_TILE = 1408
ROW_BLOCK = 512
MOE_DISPATCH_BLOCK = 1024
MOE_COMBINE_BLOCK = 512
CAST_SLAB_ROWS = 128

VMEM_LIMIT = 56 * 1024 * 1024


def _cparams(sem):
    return pltpu.CompilerParams(dimension_semantics=sem, vmem_limit_bytes=VMEM_LIMIT)


def _gelu(x):
    return 0.5 * x * (1.0 + lax.erf(x * (1.0 / math.sqrt(2.0))))


def _log_sigmoid(x):
    return jnp.minimum(x, 0.0) - jnp.log(1.0 + jnp.exp(-jnp.abs(x)))


def _sigmoid(x):
    return 1.0 / (1.0 + jnp.exp(-x))


def _split_bf16(x):
    hi = x.astype(BF16)
    lo = (x - hi.astype(F32)).astype(BF16)
    return hi, lo


def _store_token_tiles(ref, x):
    t = x.shape[0]
    for s in range(D_MODEL // LANES):
        ref[pl.ds(s, t, stride=ROW_TILE), :] = x[:, s * LANES:(s + 1) * LANES]


def _load_token_tiles(ref):
    t = ref.shape[0] // ROW_TILE
    return jnp.concatenate([ref[pl.ds(s, t, stride=ROW_TILE), :] for s in range(D_MODEL // LANES)], axis=1)


def _dot(a, b):
    return jnp.dot(a, b, preferred_element_type=F32)


def _dot_nt(a, b):
    return lax.dot_general(a, b, (((1,), (1,)), ((), ())), preferred_element_type=F32)


def _proj_kernel(n_chunks, chunk, sample, x_ref, gmix_ref, w_ref, bg_ref, lng_ref, lnb_ref, ws_ref, bs_ref,
                 *refs):
    n_out = 14 if sample else 11
    ya_ref, qb_ref, qc_ref, kc_ref, kct_ref, vc_ref, oc_ref, gt_ref, gtt_ref, *kv_refs = refs[len(refs) - n_out:]
    x = x_ref[...]
    xn = (x * lax.rsqrt(jnp.mean(x * x, axis=-1, keepdims=True) + EPS) * gmix_ref[...]).astype(BF16)

    def proj(off, width):
        return _dot(xn, w_ref[:, off:off + width])

    za = proj(P_A, 2 * A_WIDTH)
    u = _gelu(za[:, :A_WIDTH])
    gv = _gelu(za[:, A_WIDTH:])
    xc = gv - jnp.mean(gv, axis=-1, keepdims=True)
    va = xc * lax.rsqrt(jnp.mean(xc * xc, axis=-1, keepdims=True) + EPS) * lng_ref[...] + lnb_ref[...]
    if sample:
        kv_refs[4][...] = va
    vab = va.astype(BF16)
    lane_group = lax.broadcasted_iota(jnp.int32, (chunk, A_WIDTH), 1) // A_GROUP_DIM
    for c in range(n_chunks):
        rows = slice(c * chunk, (c + 1) * chunk)
        vch = vab[rows]
        s = jnp.zeros((chunk, A_WIDTH), F32)
        for g in range(A_GROUPS):
            s = jnp.where(lane_group == g, _dot(ws_ref[g], vch), s)
        ya_ref[rows, :] = (u[rows] * (s + bs_ref[...])).astype(BF16)

    qb_ref[...] = (proj(P_BQ, B_WIDTH) * (1.0 / math.sqrt(B_HEAD_DIM))).astype(BF16)
    zk = proj(P_BK, B_WIDTH)
    zv = proj(P_BV, B_WIDTH)
    if sample:
        kv_refs[0][...] = zk
        kv_refs[1][...] = zv
        kv_refs[2][...] = zk.astype(BF16)
        kv_refs[3][...] = zv.astype(BF16)
    else:
        kv_refs[0][0] = zk.T
        kv_refs[1][0] = zv.T
        for other in range(1, kv_refs[0].shape[0]):
            kv_refs[0][other] = jnp.zeros(kv_refs[0].shape[1:], F32)
            kv_refs[1][other] = jnp.zeros(kv_refs[1].shape[1:], F32)

    qc_ref[...] = proj(P_CQ, C_WIDTH_PAD).astype(BF16)
    zk = proj(P_CK, C_WIDTH_PAD) * (C_HEAD_DIM ** -0.5)
    kc_ref[...] = zk.astype(BF16)
    kct_ref[...] = zk.T.astype(BF16)
    vc_ref[...] = proj(P_CV, C_WIDTH_PAD).astype(BF16)
    oc_ref[...] = proj(P_CO, C_WIDTH_PAD)
    g = proj(P_CG, LANES) + bg_ref[...]
    lane = lax.broadcasted_iota(jnp.int32, g.shape, 1)
    gates = jnp.where(lane < C_HEADS, g, _log_sigmoid(g))
    gt_ref[...] = gates
    gtt_ref[...] = gates.T[:GATE_ROWS, :]


def _proj(x, gmix, wp, bg, lng, lnb, ws, bs, tm, chunk, sample, layer=0, kv_all=None):
    n = x.shape[0]
    row = lambda w: pl.BlockSpec((tm, w), lambda i: (i, 0))
    full = lambda a: pl.BlockSpec(a.shape, lambda i: (0,) * a.ndim)
    col = lambda h: pl.BlockSpec((h, tm), lambda i: (0, i))
    outs = [((n, A_WIDTH), BF16, row(A_WIDTH)), ((n, B_WIDTH), BF16, row(B_WIDTH)),
            ((n, C_WIDTH_PAD), BF16, row(C_WIDTH_PAD)), ((n, C_WIDTH_PAD), BF16, row(C_WIDTH_PAD)),
            ((C_WIDTH_PAD, n), BF16, col(C_WIDTH_PAD)), ((n, C_WIDTH_PAD), BF16, row(C_WIDTH_PAD)),
            ((n, C_WIDTH_PAD), F32, row(C_WIDTH_PAD)), ((n, LANES), F32, row(LANES)),
            ((GATE_ROWS, n), F32, col(GATE_ROWS))]
    out_shape = [jax.ShapeDtypeStruct(s, dt) for s, dt, _ in outs]
    out_specs = [spec for _, _, spec in outs]
    if sample:
        extra = [(B_WIDTH, F32), (B_WIDTH, F32), (B_WIDTH, BF16), (B_WIDTH, BF16), (A_WIDTH, F32)]
        out_shape += [jax.ShapeDtypeStruct((n, w), dt) for w, dt in extra]
        out_specs += [row(w) for w, _ in extra]
    ins = [x, gmix, wp, bg, lng, lnb, ws, bs]
    in_specs = [row(D_MODEL), full(gmix), full(wp), full(bg), full(lng), full(lnb), full(ws), full(bs)]
    aliases = {}
    if not sample:
        out_shape += [jax.ShapeDtypeStruct((DEPTH, B_WIDTH, n), F32)] * 2
        if kv_all is None:
            out_specs += [pl.BlockSpec((DEPTH, B_WIDTH, tm), lambda i: (0, 0, i))] * 2
        else:
            out_specs += [pl.BlockSpec((1, B_WIDTH, tm), lambda i: (layer, 0, i))] * 2
            aliases = {len(ins): len(out_shape) - 2, len(ins) + 1: len(out_shape) - 1}
            ins += list(kv_all)
            in_specs += [pl.BlockSpec(memory_space=pl.ANY)] * 2
    return pl.pallas_call(
        functools.partial(_proj_kernel, tm // chunk, chunk, sample),
        out_shape=out_shape,
        grid=(n // tm,),
        in_specs=in_specs,
        out_specs=out_specs,
        input_output_aliases=aliases,
        compiler_params=_cparams(("parallel",)),
        name="proj",
    )(*ins)


def _sb_weights(qh, kblk, carry, tri2, mask, transposed, shift=None):
    z = _dot(qh, kblk) if transposed else _dot_nt(qh, kblk)
    if shift is not None:
        z = z + shift
    tk = z.shape[1]
    drop = jnp.maximum(z, 0.0) + jnp.log(1.0 + jnp.exp(-jnp.abs(z)))
    if mask is not None:
        drop = jnp.where(mask, drop, 0.0)
    cs = _dot(jnp.concatenate(_split_bf16(drop), axis=1), tri2)
    if tk >= LANES:
        carry_b = jnp.concatenate([carry] * (tk // LANES), axis=1)
    else:
        carry_b = carry[:, :tk]
    a = jnp.exp(z - cs - carry_b)
    if mask is not None:
        a = jnp.where(mask, a, 0.0)
    return a.astype(BF16), carry + jnp.broadcast_to(cs[:, :1], carry.shape)


def _sb_step(qh, kblk, vblk, carry, acc, tri2, mask, transposed):
    a, carry = _sb_weights(qh, kblk, carry, tri2, mask, transposed)
    return carry, acc + (_dot_nt(a, vblk) if transposed else _dot(a, vblk))


def _tri2(tk):
    j = lax.broadcasted_iota(jnp.int32, (2 * tk, tk), 0)
    s = lax.broadcasted_iota(jnp.int32, (2 * tk, tk), 1)
    return (jnp.where(j >= tk, j - tk, j) >= s).astype(BF16)


def _sb_finish(acc_s, gn_ref, o_ref, head0):
    out = jnp.where(head0, acc_s[0], acc_s[1])
    r = lax.broadcasted_iota(jnp.int32, (LANES, LANES), 0) // B_HEAD_DIM
    c = lax.broadcasted_iota(jnp.int32, (LANES, LANES), 1) // B_HEAD_DIM
    same_head = (r == c).astype(BF16)
    hi, lo = _split_bf16(out * out)
    ms = (_dot(hi, same_head) + _dot(lo, same_head)) * (1.0 / B_HEAD_DIM)
    return (out * lax.rsqrt(ms + EPS) * gn_ref[...]).astype(o_ref.dtype)


def _sb_walk(qh, load_kv, first_block, carry_s, acc_s, tri):
    def alive():
        return jnp.minimum(jnp.min(carry_s[0]), jnp.min(carry_s[1])) <= -SB_DEAD_LOG_WEIGHT

    def cond(st):
        j, live = st
        return jnp.logical_and(j >= 0, live)

    def body(st):
        j, _ = st
        kblk, vblk = load_kv(j)
        for h in range(2):
            carry, acc = _sb_step(qh[h], kblk, vblk, carry_s[h], acc_s[h], tri, None, True)
            carry_s[h] = carry
            acc_s[h] = acc
        return j - 1, alive()

    lax.while_loop(cond, body, (first_block, alive()))


def _sb_prompt_kernel(q_ref, k_ref, v_ref, gn_ref, o_ref, carry_s, acc_s):
    tq = SB_BLOCK
    n_q = q_ref.shape[0] // tq
    head0 = lax.broadcasted_iota(jnp.int32, (tq, LANES), 1) < B_HEAD_DIM
    tri = _tri2(tq)
    t = lax.broadcasted_iota(jnp.int32, (tq, tq), 0)
    s = lax.broadcasted_iota(jnp.int32, (tq, tq), 1)
    causal = s < t
    zeros = jnp.zeros((tq, LANES), F32)

    def load_kv(j):
        cols = pl.ds(pl.multiple_of(j * tq, tq), tq)
        return k_ref[:, cols].astype(BF16), v_ref[:, cols].astype(BF16)

    first = pl.program_id(1) * n_q
    kv = [load_kv(jnp.maximum(first + b - 1, 0)) for b in range(n_q + 1)]
    qhs = []
    for b in range(n_q):
        q = q_ref[b * tq:(b + 1) * tq, :]
        qh = [jnp.where(head0, q, 0), jnp.where(head0, 0, q)]
        qhs.append(qh)
        no_prev = jnp.where(first + b >= 1, 0.0, NEG_BIG)
        v_both = jnp.concatenate([kv[b + 1][1], kv[b][1]], axis=1)
        for h in range(2):
            a_diag, carry = _sb_weights(qh[h], kv[b + 1][0], zeros, tri, causal, True)
            a_prev, carry = _sb_weights(qh[h], kv[b][0], carry, tri, None, True, no_prev)
            carry_s[b, h] = carry
            acc_s[b, h] = _dot_nt(jnp.concatenate([a_diag, a_prev], axis=1), v_both)
    for b in range(n_q):
        _sb_walk(qhs[b], load_kv, first + b - 2, carry_s.at[b], acc_s.at[b], tri)
        o_ref[b * tq:(b + 1) * tq, :] = _sb_finish(acc_s.at[b], gn_ref, o_ref, head0)


def _sb_prompt(q, k, v, gn, layer):
    n = q.shape[0]
    n_q = SB_BLOCKS_PER_STEP
    tq = SB_BLOCK * n_q
    blk = pl.BlockSpec((tq, LANES), lambda p, i: (i, p))
    seq = pl.BlockSpec((None, LANES, n), lambda p, i: (layer, p, 0))
    state = pltpu.VMEM((n_q, 2, SB_BLOCK, LANES), F32)
    return pl.pallas_call(
        _sb_prompt_kernel,
        out_shape=jax.ShapeDtypeStruct((n, B_WIDTH), BF16),
        grid=(B_PAIRS, n // tq),
        in_specs=[blk, seq, seq, pl.BlockSpec((1, LANES), lambda p, i: (0, p))],
        out_specs=blk,
        scratch_shapes=[state, state],
        compiler_params=_cparams(("parallel", "parallel")),
        name="sb_prompt",
    )(q, k, v, gn)


def _sb_sample_kernel(q_ref, kn_ref, vn_ref, kc_ref, vc_ref, gn_ref, o_ref, carry_s, acc_s):
    tq = q_ref.shape[1]
    tk = SB_BLOCK
    head0 = lax.broadcasted_iota(jnp.int32, (tq, LANES), 1) < B_HEAD_DIM
    q = q_ref[0]
    zero = jnp.zeros_like(q)
    qh = [jnp.where(head0, q, zero), jnp.where(head0, zero, q)]
    t = lax.broadcasted_iota(jnp.int32, (tq, tq), 0)
    s = lax.broadcasted_iota(jnp.int32, (tq, tq), 1)
    causal = s < t
    zeros = jnp.zeros((tq, LANES), F32)
    def load_kv(j):
        cols = pl.ds(pl.multiple_of(j * tk, tk), tk)
        return kc_ref[0, :, cols].astype(BF16), vc_ref[0, :, cols].astype(BF16)

    last = kc_ref.shape[2] // tk - 1
    kp, vp = load_kv(last)
    tri = _tri2(tk)
    for h in range(2):
        carry, acc = _sb_step(qh[h], kn_ref[0], vn_ref[0], zeros, zeros, _tri2(tq), causal, False)
        carry, acc = _sb_step(qh[h], kp, vp, carry, acc, tri, None, True)
        carry_s[h] = carry
        acc_s[h] = acc
    _sb_walk(qh, load_kv, last - 1, carry_s, acc_s, tri)
    o_ref[0] = _sb_finish(acc_s, gn_ref, o_ref, head0)


def _sb_sample(q, kn, vn, kc, vc, gn, layer):
    nb, tq, _ = q.shape
    past = kc.shape[3]
    new = pl.BlockSpec((1, tq, LANES), lambda b, p: (b, 0, p))
    old = pl.BlockSpec((None, 1, LANES, past), lambda b, p: (layer, b, p, 0))
    return pl.pallas_call(
        _sb_sample_kernel,
        out_shape=jax.ShapeDtypeStruct((nb, tq, B_WIDTH), BF16),
        grid=(nb, B_PAIRS),
        in_specs=[new, new, new, old, old, pl.BlockSpec((1, LANES), lambda b, p: (0, p))],
        out_specs=new,
        scratch_shapes=[pltpu.VMEM((2, tq, LANES), F32), pltpu.VMEM((2, tq, LANES), F32)],
        compiler_params=_cparams(("parallel", "parallel")),
        name="sb_sample",
    )(q, kn, vn, kc, vc, gn)


def _split3(x):
    h1 = x.astype(BF16)
    r1 = x - h1.astype(F32)
    h2 = r1.astype(BF16)
    return h1, h2, (r1 - h2.astype(F32)).astype(BF16)


def _dot3(x, rhs01):
    return _dot(jnp.concatenate(_split3(x), axis=1), jnp.concatenate([rhs01] * 3, axis=0))


def _dot3_left(lhs01, x):
    return _dot(jnp.concatenate([lhs01] * 3, axis=1), jnp.concatenate(_split3(x), axis=0))


def _mlstm_kernel(n_sub, q_ref, k_ref, kt_ref, v_ref, o_ref, gt_ref, gtt_ref, gn_ref, c0_ref, n0_ref, m0_ref,
                  yc_ref, c_out, n_out, m_out, c_s, n_s, m_s):
    L = MLSTM_BLOCK
    t_blk = pl.program_id(1)

    @pl.when(t_blk == 0)
    def _():
        c_s[...] = c0_ref[0]
        n_s[...] = n0_ref[0]
        m_s[...] = m0_ref[0]

    r = lax.broadcasted_iota(jnp.int32, (L, L), 0)
    c = lax.broadcasted_iota(jnp.int32, (L, L), 1)
    causal = c <= r
    upper = (r <= c).astype(BF16)
    ones_sq = jnp.ones((L, LANES), BF16)
    sel_r = lax.broadcasted_iota(jnp.int32, (LANES, C_WIDTH_PAD), 0)
    sel_c = lax.broadcasted_iota(jnp.int32, (LANES, C_WIDTH_PAD), 1) // C_HEAD_PAD
    sel_p = (sel_r == sel_c).astype(BF16)
    sel_b = (sel_r == sel_c + C_HEADS).astype(BF16)
    lane = c

    a_rows_all, pb_cols_all = [], []
    for sub in range(n_sub):
        rows = slice(sub * L, (sub + 1) * L)
        gtt = gtt_ref[0, :, rows]
        bct = _dot3(gtt, upper)
        a_rows_all.append(gtt[:C_HEADS] - bct[C_HEADS:])
        gt = gt_ref[0, rows, :]
        bc = _dot3_left(causal.astype(BF16), gt)
        pmax = gt - pltpu.roll(bc, LANES - C_HEADS, axis=1)
        for sh in (1, 2, 4, 8, 16, 32, 64):
            pmax = jnp.maximum(pmax, jnp.where(r >= sh, pltpu.roll(pmax, sh, axis=0), -jnp.inf))
        pb_cols_all.append(jnp.where(lane < C_HEADS, pmax, bc))
    pb_cols_all = jnp.concatenate(pb_cols_all, axis=0)
    p_all = _dot3(pb_cols_all, sel_p)
    b_all = _dot3(pb_cols_all, sel_b)

    state = [(c_s[h], n_s[h], m_s[h]) for h in range(C_HEADS)]
    for sub in range(n_sub):
        rows = slice(sub * L, (sub + 1) * L)
        a_rows = a_rows_all[sub]
        heads = []
        for h in range(C_HEADS):
            lanes = slice(h * C_HEAD_PAD, (h + 1) * C_HEAD_PAD)
            q = q_ref[0, rows, lanes]
            v1 = jnp.concatenate([v_ref[0, rows, lanes], ones_sq], axis=1)
            p_rep = p_all[rows, lanes]
            b_rep = b_all[rows, lanes]
            a_row = a_rows[h:h + 1, :]
            p_last = p_rep[L - 1:L, :]
            c_prev, n_prev, m_prev = state[h]

            w = jnp.exp(jnp.where(causal, a_row - p_rep, -jnp.inf)) * _dot_nt(q, k_ref[0, rows, lanes])
            kwt = (kt_ref[0, lanes, rows].astype(F32) * jnp.exp(a_row - p_last)).astype(BF16)
            both = _dot(jnp.concatenate([w.astype(BF16), kwt], axis=0), v1)
            here_sums = both[:L]
            fresh = both[L:]
            past_sums = _dot(q, jnp.concatenate([c_prev, n_prev], axis=1).astype(BF16))
            top = jnp.maximum(m_prev, p_rep)
            past = jnp.exp(m_prev - top)
            here = jnp.exp(p_rep - top)
            mix = jnp.concatenate([past, past], axis=1) * past_sums + jnp.concatenate([here, here], axis=1) * here_sums
            den = jnp.maximum(jnp.abs(mix[:, LANES:]), jnp.exp(-(b_rep + top)))
            heads.append(mix[:, :LANES] / den)

            top_last = jnp.maximum(m_prev, p_last)
            decay = jnp.exp(m_prev - top_last)
            gain = jnp.exp(p_last - top_last)
            state[h] = (decay * c_prev + gain * fresh[:, :LANES], decay * n_prev + gain * fresh[:, LANES:],
                        b_rep[L - 1:L, :] + top_last)

        hh = jnp.concatenate(heads, axis=0)
        ms = _dot(jnp.concatenate(_split_bf16(hh * hh), axis=1),
                  jnp.concatenate([ones_sq, ones_sq], axis=0)) * (1.0 / C_HEAD_DIM)
        hn = hh * lax.rsqrt(ms + EPS)
        for h in range(C_HEADS):
            lanes = slice(h * C_HEAD_PAD, (h + 1) * C_HEAD_PAD)
            yc_ref[0, rows, lanes] = (hn[h * L:(h + 1) * L] * gn_ref[:, lanes]
                                      * _sigmoid(o_ref[0, rows, lanes])).astype(BF16)

    for h in range(C_HEADS):
        c_s[h], n_s[h], m_s[h] = state[h]

    @pl.when(t_blk == pl.num_programs(1) - 1)
    def _():
        c_out[0] = c_s[...]
        n_out[0] = n_s[...]
        m_out[0] = m_s[...]


def _mlstm(q, k, kt, v, o, gt, gtt, gn, c0, n0, m0, n_sub):
    nb, n, _ = q.shape
    tb = n_sub * MLSTM_BLOCK
    seq = lambda w: pl.BlockSpec((1, tb, w), lambda b, t: (b, t, 0))
    seq_t = lambda h: pl.BlockSpec((1, h, tb), lambda b, t: (b, 0, t))
    st = lambda a: pl.BlockSpec((1,) + a.shape[1:], lambda b, t: (b,) + (0,) * (a.ndim - 1))
    return pl.pallas_call(
        functools.partial(_mlstm_kernel, n_sub),
        out_shape=[jax.ShapeDtypeStruct((nb, n, C_WIDTH_PAD), BF16),
                   jax.ShapeDtypeStruct(c0.shape, F32),
                   jax.ShapeDtypeStruct(n0.shape, F32),
                   jax.ShapeDtypeStruct(m0.shape, F32)],
        grid=(nb, n // tb),
        in_specs=[seq(C_WIDTH_PAD), seq(C_WIDTH_PAD), seq_t(C_WIDTH_PAD), seq(C_WIDTH_PAD), seq(C_WIDTH_PAD),
                  seq(LANES), seq_t(GATE_ROWS), pl.BlockSpec((1, C_WIDTH_PAD), lambda b, t: (0, 0)),
                  st(c0), st(n0), st(m0)],
        out_specs=[seq(C_WIDTH_PAD), st(c0), st(n0), st(m0)],
        scratch_shapes=[pltpu.VMEM(c0.shape[1:], F32), pltpu.VMEM(n0.shape[1:], F32),
                        pltpu.VMEM(m0.shape[1:], F32)],
        compiler_params=_cparams(("parallel", "arbitrary")),
        name="mlstm",
    )(q, k, kt, v, o, gt, gtt, gn, c0, n0, m0)


def _mixer_out(x_ref, ya_ref, yb_ref, yc_ref, wa_ref, wb_ref, wc_ref, g_ref, xmid_ref):
    y = _dot(ya_ref[...], wa_ref[...]) + _dot(yb_ref[...], wb_ref[...]) + _dot(yc_ref[...], wc_ref[...])
    x = x_ref[...] + y
    xmid_ref[...] = x
    return x * lax.rsqrt(jnp.mean(x * x, axis=-1, keepdims=True) + EPS) * g_ref[...]


def _merge_dense_kernel(n_cast, x_ref, ya_ref, yb_ref, yc_ref, wa_ref, wb_ref, wc_ref, g_ref, *refs):
    casts_in, xmid_ref, xn_ref, casts_out = refs[:n_cast], refs[n_cast], refs[n_cast + 1], refs[n_cast + 2:]
    xn_ref[...] = _mixer_out(x_ref, ya_ref, yb_ref, yc_ref, wa_ref, wb_ref, wc_ref, g_ref,
                             xmid_ref).astype(BF16)
    for src, dst in zip(casts_in, casts_out):
        dst[...] = src[...].astype(BF16)


def _merge_moe_kernel(x_ref, ya_ref, yb_ref, yc_ref, wa_ref, wb_ref, wc_ref, g_ref, wrt_ref,
                      earlier_ref, xmid_ref, xrow_ref, ri_ref, rf_ref, cnt_ref, *rest):
    tm = x_ref.shape[0]
    run_s = rest[-1]
    if len(rest) == 2:
        rest[0][...] = jnp.zeros(rest[0].shape, F32)

    @pl.when(pl.program_id(0) == 0)
    def _():
        run_s[...] = jnp.zeros_like(run_s)

    xn = _mixer_out(x_ref, ya_ref, yb_ref, yc_ref, wa_ref, wb_ref, wc_ref, g_ref, xmid_ref)
    _store_token_tiles(xrow_ref, xn)

    hi, lo = _split_bf16(xn)
    part = _dot_nt(wrt_ref[...], hi)
    lg = part[:N_EXPERTS] + part[N_EXPERTS:] + _dot_nt(wrt_ref[:N_EXPERTS, :], lo)
    expert = lax.broadcasted_iota(jnp.int32, lg.shape, 0)
    m1 = jnp.max(lg, axis=0, keepdims=True)
    i1 = jnp.min(jnp.where(lg == m1, expert, N_EXPERTS), axis=0, keepdims=True)
    lg2 = jnp.where(expert == i1, -jnp.inf, lg)
    m2 = jnp.max(lg2, axis=0, keepdims=True)
    i2 = jnp.min(jnp.where(lg2 == m2, expert, N_EXPERTS), axis=0, keepdims=True)
    e2 = jnp.exp(m2 - m1)
    g1 = 1.0 / (1.0 + e2)
    g2 = e2 * g1

    sel1 = expert == i1
    sel2 = expert == i2
    onehot = jnp.logical_or(sel1, sel2).astype(BF16)
    before = _dot(onehot, earlier_ref[...]) + jnp.concatenate([run_s[...]] * (tm // LANES), axis=1)
    rank1 = jnp.sum(jnp.where(sel1, before, 0.0), axis=0, keepdims=True).astype(jnp.int32)
    rank2 = jnp.sum(jnp.where(sel2, before, 0.0), axis=0, keepdims=True).astype(jnp.int32)
    run_s[...] += _dot(onehot, jnp.ones((tm, LANES), BF16))
    cnt_ref[...] = run_s[...].astype(jnp.int32)
    ri_ref[...] = jnp.where(expert == 0, i1, jnp.where(expert == 1, i2,
                            jnp.where(expert == 2, rank1, jnp.where(expert == 3, rank2, 0))))
    rf_ref[...] = jnp.where(expert == 0, g1, jnp.where(expert == 1, g2, 0.0))


def _merge(x, ya, yb, yc, wa, wb, wc, g, wr, tm, cast=(), zero_rows=0):
    n = x.shape[0]
    assert zero_rows % (n // tm) == 0
    assert all(a.shape[0] // CAST_SLAB_ROWS <= n // tm for a in cast)
    row = lambda w: pl.BlockSpec((tm, w), lambda i: (i, 0))
    full = lambda a: pl.BlockSpec(a.shape, lambda i: (0,) * a.ndim)
    ins = [x, ya, yb, yc, wa, wb, wc, g]
    in_specs = [row(D_MODEL), row(A_WIDTH), row(B_WIDTH), row(C_WIDTH_PAD), full(wa), full(wb), full(wc), full(g)]
    if wr is None:
        slab = lambda a: pl.BlockSpec((CAST_SLAB_ROWS, a.shape[1]),
                                      lambda i: (jnp.minimum(i, a.shape[0] // CAST_SLAB_ROWS - 1), 0))
        outs = pl.pallas_call(
            functools.partial(_merge_dense_kernel, len(cast)),
            out_shape=[jax.ShapeDtypeStruct((n, D_MODEL), F32), jax.ShapeDtypeStruct((n, D_MODEL), BF16)]
            + [jax.ShapeDtypeStruct(a.shape, BF16) for a in cast],
            grid=(n // tm,), in_specs=in_specs + [slab(a) for a in cast],
            out_specs=[row(D_MODEL), row(D_MODEL)] + [slab(a) for a in cast],
            compiler_params=_cparams(("arbitrary",)), name="merge_dense",
        )(*ins, *cast)
        return outs[0], outs[1], outs[2:]
    earlier = (jnp.arange(tm)[:, None] < jnp.arange(tm)[None, :]).astype(BF16)
    col = pl.BlockSpec((N_EXPERTS, tm), lambda i: (0, i))
    out_shape = [jax.ShapeDtypeStruct((n, D_MODEL), F32),
                 jax.ShapeDtypeStruct((n * ROW_TILE, LANES), F32),
                 jax.ShapeDtypeStruct((N_EXPERTS, n), jnp.int32),
                 jax.ShapeDtypeStruct((N_EXPERTS, n), F32),
                 jax.ShapeDtypeStruct((N_EXPERTS, LANES), jnp.int32)]
    out_specs = [row(D_MODEL), pl.BlockSpec((tm * ROW_TILE, LANES), lambda i: (i, 0)),
                 col, col, pl.BlockSpec((N_EXPERTS, LANES), lambda i: (0, 0))]
    if zero_rows:
        steps = n // tm
        out_shape.append(jax.ShapeDtypeStruct((zero_rows, ROW_TILE, LANES), F32))
        out_specs.append(pl.BlockSpec((zero_rows // steps, ROW_TILE, LANES), lambda i: (i, 0, 0)))
    return pl.pallas_call(
        _merge_moe_kernel,
        out_shape=out_shape,
        grid=(n // tm,), in_specs=in_specs + [full(wr), full(earlier)],
        out_specs=out_specs,
        scratch_shapes=[pltpu.VMEM((N_EXPERTS, LANES), F32)],
        compiler_params=_cparams(("arbitrary",)), name="merge_moe",
    )(*ins, wr, earlier)


def _dest_blocks(dest, tt):
    return jnp.transpose(dest.reshape(2, dest.shape[1] // tt, tt), (1, 0, 2))


def _dispatch_kernel(dest_ref, src_ref, xs_in_ref, xs_ref, sem):
    del xs_in_ref
    tt = dest_ref.shape[2]

    def issue(t, carry):
        rows = pl.ds(pl.multiple_of(t * ROW_TILE, ROW_TILE), ROW_TILE)
        for k in range(2):
            pltpu.make_async_copy(src_ref.at[rows], xs_ref.at[dest_ref[0, k, t]], sem).start(priority=k)
        return carry

    lax.fori_loop(0, tt, issue, 0, unroll=8)

    def drain(t, carry):
        for k in range(2):
            pltpu.make_async_copy(src_ref.at[pl.ds(0, ROW_TILE)], xs_ref.at[0], sem).wait()
        return carry

    lax.fori_loop(0, tt, drain, 0, unroll=8)


def _dispatch(dest, src, xs, tt):
    n = src.shape[0] // ROW_TILE
    dest3 = _dest_blocks(dest, tt)
    return pl.pallas_call(
        _dispatch_kernel,
        out_shape=jax.ShapeDtypeStruct(xs.shape, xs.dtype),
        grid=(n // tt,),
        in_specs=[pl.BlockSpec((1, 2, tt), lambda i: (i, 0, 0), memory_space=pltpu.SMEM),
                  pl.BlockSpec((tt * ROW_TILE, LANES), lambda i: (i, 0)), pl.BlockSpec(memory_space=pl.ANY)],
        out_specs=pl.BlockSpec(memory_space=pl.ANY),
        scratch_shapes=[pltpu.SemaphoreType.DMA(())],
        input_output_aliases={2: 0},
        compiler_params=_cparams(("arbitrary",)), name="moe_dispatch",
    )(dest3, src, xs)


def _combine_kernel(final, dest_ref, next_ref, ys_ref, rf_ref, xmid_ref, gf_ref, o_ref, buf_s, sems):
    tt = xmid_ref.shape[0]
    i = pl.program_id(0)
    n_steps = pl.num_programs(0)

    def gather(idx_ref, slot):
        def issue(t, carry):
            rows = pl.ds(pl.multiple_of(t * ROW_TILE, ROW_TILE), ROW_TILE)
            for k in range(2):
                pltpu.make_async_copy(ys_ref.at[idx_ref[0, k, t]], buf_s.at[slot, k, rows],
                                      sems.at[slot]).start(priority=k)
            return carry

        lax.fori_loop(0, tt, issue, 0, unroll=8)

    slot = i % 2

    @pl.when(i == 0)
    def _():
        gather(dest_ref, 0)

    @pl.when(i + 1 < n_steps)
    def _():
        gather(next_ref, 1 - slot)

    def drain(t, carry):
        for k in range(2):
            pltpu.make_async_copy(ys_ref.at[0], buf_s.at[slot, 0, pl.ds(0, ROW_TILE)], sems.at[slot]).wait()
        return carry

    lax.fori_loop(0, tt, drain, 0, unroll=8)

    y = (xmid_ref[...] + rf_ref[:, 0:1] * _load_token_tiles(buf_s.at[slot, 0])
         + rf_ref[:, 1:2] * _load_token_tiles(buf_s.at[slot, 1]))
    if final:
        y = y * lax.rsqrt(jnp.mean(y * y, axis=-1, keepdims=True) + EPS) * gf_ref[...]
    o_ref[...] = y


def _combine(dest, ys, rf, xmid, gf, tt, final):
    n = xmid.shape[0]
    n_steps = n // tt
    dest3 = _dest_blocks(dest, tt)
    row = lambda w: pl.BlockSpec((tt, w), lambda i: (i, 0))
    return pl.pallas_call(
        functools.partial(_combine_kernel, final),
        out_shape=jax.ShapeDtypeStruct((n, D_MODEL), F32),
        grid=(n_steps,),
        in_specs=[pl.BlockSpec((1, 2, tt), lambda i: (i, 0, 0), memory_space=pltpu.SMEM),
                  pl.BlockSpec((1, 2, tt), lambda i: (jnp.minimum(i + 1, n_steps - 1), 0, 0),
                               memory_space=pltpu.SMEM),
                  pl.BlockSpec(memory_space=pl.ANY), row(LANES), row(D_MODEL),
                  pl.BlockSpec((1, D_MODEL), lambda i: (0, 0))],
        out_specs=row(D_MODEL),
        scratch_shapes=[pltpu.VMEM((2, 2, tt * ROW_TILE, LANES), F32), pltpu.SemaphoreType.DMA((2,))],
        compiler_params=_cparams(("arbitrary",)), name="moe_combine",
    )(dest3, dest3, ys, rf, xmid, gf)


def _swiglu_acc(xn, wg_ref, wu_ref, wd_ref, acc_s):
    g = _dot(xn, wg_ref[0])
    u = _dot(xn, wu_ref[0])
    acc_s[...] += _dot((g * _sigmoid(g) * u).astype(BF16), wd_ref[0])


def _ffn_dense_kernel(final, n_cast, xn_ref, wg_ref, wu_ref, wd_ref, xmid_ref, gf_ref, *refs):
    casts_in, o_ref, casts_out, acc_s = refs[:n_cast], refs[n_cast], refs[n_cast + 1:-1], refs[-1]
    f = pl.program_id(1)

    @pl.when(f == 0)
    def _():
        acc_s[...] = jnp.zeros_like(acc_s)

    _swiglu_acc(xn_ref[...], wg_ref, wu_ref, wd_ref, acc_s)
    for src, dst in zip(casts_in, casts_out):
        dst[...] = src[...].astype(BF16)

    @pl.when(f == pl.num_programs(1) - 1)
    def _():
        y = xmid_ref[...] + acc_s[...]
        if final:
            y = y * lax.rsqrt(jnp.mean(y * y, axis=-1, keepdims=True) + EPS) * gf_ref[...]
        o_ref[...] = y


def _ffn_dense(xn, wg, wu, wd, xmid, gf, tm, final, cast=()):
    n = xn.shape[0]
    tf = FFN_F_TILE
    n_f = D_FF // tf
    steps = (n // tm) * n_f
    row = lambda w: pl.BlockSpec((tm, w), lambda i, f: (i, 0))
    slabs = [a.reshape(steps, a.size // (steps * a.shape[-1]), a.shape[-1]) for a in cast]
    slab_spec = lambda a: pl.BlockSpec((1,) + a.shape[1:], lambda i, f: (i * n_f + f, 0, 0))
    outs = pl.pallas_call(
        functools.partial(_ffn_dense_kernel, final, len(cast)),
        out_shape=[jax.ShapeDtypeStruct((n, D_MODEL), F32)] + [jax.ShapeDtypeStruct(a.shape, BF16) for a in slabs],
        grid=(n // tm, n_f),
        in_specs=[row(D_MODEL),
                  pl.BlockSpec((1, D_MODEL, tf), lambda i, f: (0, 0, f)),
                  pl.BlockSpec((1, D_MODEL, tf), lambda i, f: (0, 0, f)),
                  pl.BlockSpec((1, tf, D_MODEL), lambda i, f: (0, f, 0)),
                  row(D_MODEL), pl.BlockSpec((1, D_MODEL), lambda i, f: (0, 0))] + [slab_spec(a) for a in slabs],
        out_specs=[row(D_MODEL)] + [slab_spec(a) for a in slabs],
        scratch_shapes=[pltpu.VMEM((tm, D_MODEL), F32)],
        compiler_params=_cparams(("parallel", "arbitrary")),
        name="ffn_dense",
    )(xn, wg, wu, wd, xmid, gf, *slabs)
    return outs[0], [o.reshape(a.shape) for o, a in zip(outs[1:], cast)]


def _ffn_routed_kernel(te_ref, nu_ref, xs_ref, wg_ref, wu_ref, wd_ref, ys_ref, xb_s, acc_s):
    del te_ref
    i = pl.program_id(0)
    f = pl.program_id(1)

    @pl.when(i < nu_ref[0])
    def _():
        @pl.when(f == 0)
        def _():
            acc_s[...] = jnp.zeros_like(acc_s)
            xb_s[...] = _load_token_tiles(xs_ref).astype(BF16)

        _swiglu_acc(xb_s[...], wg_ref, wu_ref, wd_ref, acc_s)

        @pl.when(f == pl.num_programs(1) - 1)
        def _():
            _store_token_tiles(ys_ref, acc_s[...])

    @pl.when(jnp.logical_and(i >= nu_ref[0], f == 0))
    def _():
        ys_ref[...] = jnp.zeros_like(ys_ref)


def _ffn_routed(tile_expert, n_used, xs, wg, wu, wd):
    n_tiles = tile_expert.shape[0]
    tm, tf = MOE_TILE, FFN_F_TILE
    n_f = D_FF // tf
    last = lambda i, nu: jnp.minimum(i, nu[0] - 1)
    fcol = lambda i, f, nu: jnp.where(i < nu[0], f, n_f - 1)
    return pl.pallas_call(
        _ffn_routed_kernel,
        out_shape=jax.ShapeDtypeStruct(xs.shape, F32),
        grid_spec=pltpu.PrefetchScalarGridSpec(
            num_scalar_prefetch=2,
            grid=(n_tiles, n_f),
            in_specs=[pl.BlockSpec((tm * ROW_TILE, LANES), lambda i, f, te, nu: (last(i, nu), 0)),
                      pl.BlockSpec((1, D_MODEL, tf), lambda i, f, te, nu: (te[last(i, nu)], 0, fcol(i, f, nu))),
                      pl.BlockSpec((1, D_MODEL, tf), lambda i, f, te, nu: (te[last(i, nu)], 0, fcol(i, f, nu))),
                      pl.BlockSpec((1, tf, D_MODEL), lambda i, f, te, nu: (te[last(i, nu)], fcol(i, f, nu), 0))],
            out_specs=pl.BlockSpec((tm * ROW_TILE, LANES), lambda i, f, te, nu: (i, 0)),
            scratch_shapes=[pltpu.VMEM((tm, D_MODEL), BF16), pltpu.VMEM((tm, D_MODEL), F32)]),
        compiler_params=_cparams(("arbitrary", "arbitrary")),
        name="ffn_routed",
    )(tile_expert, n_used, xs, wg, wu, wd)


def _route_plan(ri_p, cnt_p, ri_s, cnt_s, n_tiles):
    cnt_p, cnt_s = cnt_p[:, 0], cnt_s[:, 0]
    tiles = (cnt_p + cnt_s + MOE_TILE - 1) // MOE_TILE
    ends = jnp.cumsum(tiles)
    start = (ends - tiles) * MOE_TILE
    lookup = lambda table, idx: jnp.sum(
        jnp.where(idx[..., None] == jnp.arange(N_EXPERTS, dtype=jnp.int32), table, 0), axis=-1)
    dest_p = lookup(start, ri_p[0:2]) + ri_p[2:4]
    dest_s = lookup(start + cnt_p, ri_s[0:2]) + ri_s[2:4]
    tile_expert = jnp.minimum(jnp.sum(jnp.arange(n_tiles, dtype=jnp.int32)[:, None] >= ends[None, :], axis=-1),
                              N_EXPERTS - 1).astype(jnp.int32)
    return dest_p, dest_s, tile_expert, ends[-1:].astype(jnp.int32)


def _pad_heads_cols(w):
    w = w.reshape(w.shape[0], C_HEADS, C_HEAD_DIM)
    return jnp.pad(w, ((0, 0), (0, 0), (0, C_HEAD_PAD - C_HEAD_DIM))).reshape(w.shape[0], C_WIDTH_PAD)


def _layer_params(l, w_in, b_gate, w_s, b_s, gn_c, w_out):
    w = w_in[l]
    gates = jnp.pad(w[:, OFF_CG:OFF_CG + 2 * C_HEADS], ((0, 0), (0, LANES - 2 * C_HEADS)))
    c_part = w[:, OFF_CQ:OFF_CG].reshape(D_MODEL, 4 * C_HEADS, C_HEAD_DIM)
    c_part = jnp.pad(c_part, ((0, 0), (0, 0), (0, C_HEAD_PAD - C_HEAD_DIM))).reshape(D_MODEL, 4 * C_WIDTH_PAD)
    wp = jnp.concatenate([w[:, :OFF_CQ], c_part, gates], axis=1).astype(BF16)
    bg = jnp.pad(b_gate[l], (0, LANES - 2 * C_HEADS))[None, :]
    pos = jnp.arange(GMLP_CHUNK)
    mask = (pos[None, :] // CHUNK) <= (pos[:, None] // CHUNK)
    wm = jnp.where(mask[None], w_s[l], 0.0)
    bs = jnp.repeat(b_s[l].T, A_GROUP_DIM, axis=1)
    wo = w_out[l]
    wc = wo[A_WIDTH + B_WIDTH:].reshape(C_HEADS, C_HEAD_DIM, D_MODEL)
    wc = jnp.pad(wc, ((0, 0), (0, C_HEAD_PAD - C_HEAD_DIM), (0, 0))).reshape(C_WIDTH_PAD, D_MODEL)
    gnc = _pad_heads_cols(gn_c[l][None, :])
    return dict(wp=wp, bg=bg, wm=wm, bs=bs, wa=wo[:A_WIDTH].astype(BF16),
                wb=wo[A_WIDTH:A_WIDTH + B_WIDTH].astype(BF16), wc=wc.astype(BF16), gnc=gnc)


def _pad_state(c, n, m):
    p = C_HEAD_PAD - C_HEAD_DIM
    c = jnp.pad(c, ((0, 0), (0, 0), (0, p), (0, p)))
    n = jnp.pad(n, ((0, 0), (0, 0), (0, p)))
    n = jnp.broadcast_to(n[:, :, :, None], n.shape + (LANES,))
    m = jnp.broadcast_to(m[:, :, None, None], m.shape + (1, LANES))
    return c, n, m


def _unpad_state(c, n, m):
    return c[:, :, :C_HEAD_DIM, :C_HEAD_DIM], n[:, :, :C_HEAD_DIM, 0], m[:, :, 0, 0]


def kernel(x_prompt, x_sample, cache_k_b, cache_v_b, state_c_mlstm, state_n_mlstm, state_m_mlstm,
           g_mix, w_in, b_gate, ln_a_g, ln_a_b, w_s, b_s, gn_b, gn_c, w_out,
           g_ffn, w_gate_d, w_up_d, w_down_d, w_router, w_gate_e, w_up_e, w_down_e, g_final):
    n_seq = x_prompt.shape[1]
    n_dec, n_new = x_sample.shape[0], x_sample.shape[1]
    past = cache_k_b.shape[2]
    n_samp = n_dec * n_new

    xp = x_prompt.reshape(n_seq, D_MODEL)
    xs = x_sample.reshape(n_samp, D_MODEL)
    gfin = g_final[None, :]
    keys_last = lambda a: jnp.transpose(a, (0, 1, 3, 4, 2)).reshape(DEPTH, n_dec, B_WIDTH, past)
    cache_kt, cache_vt = keys_last(cache_k_b), keys_last(cache_v_b)

    outs = {k: [] for k in ("kbp", "vbp", "cp", "np", "mp", "kbs", "vbs", "cs", "ns", "ms", "vas")}
    for l in range(DEPTH):
        p = _layer_params(l, w_in, b_gate, w_s, b_s, gn_c, w_out)
        gmix = g_mix[l][None, :]
        lng, lnb = ln_a_g[l][None, :], ln_a_b[l][None, :]
        gnb = gn_b[l][None, :]
        gffn = g_ffn[l][None, :]
        moe = l % 2 == 1
        j = l // 2
        if moe:
            wg, wu, wd = expert_bf16
            wr = jnp.concatenate(_split_bf16(w_router[j].T), axis=0)
        else:
            wr = None
        final = l == DEPTH - 1
        n_tiles = 2 * (n_seq + n_samp) // MOE_TILE + N_EXPERTS

        wm_s = jnp.kron(jnp.eye(n_dec, dtype=F32), p["wm"][:, :n_new, :n_new])
        bs_s = jnp.tile(p["bs"][:n_new], (n_dec, 1))

        (ya, qb, qc, kc, kct, vc, oc, gt, gtt, kbt, vbt) = _proj(
            xp, gmix, p["wp"], p["bg"], lng, lnb, p["wm"].astype(BF16), p["bs"], ROW_BLOCK, GMLP_CHUNK, False,
            l, None if l == 0 else (kbt, vbt))
        yb = _sb_prompt(qb, kbt, vbt, gnb, l)
        c0, n0, m0 = _pad_state(jnp.zeros((1, C_HEADS, C_HEAD_DIM, C_HEAD_DIM), F32),
                                jnp.zeros((1, C_HEADS, C_HEAD_DIM), F32), jnp.zeros((1, C_HEADS), F32))
        yc, c_f, n_f, m_f = _mlstm(qc[None], kc[None], kct[None], vc[None], oc[None], gt[None], gtt[None],
                                   p["gnc"], c0, n0, m0, MLSTM_SUB_BLOCKS)
        merged_p = _merge(xp, ya, yb, yc[0], p["wa"], p["wb"], p["wc"], gffn, wr, ROW_BLOCK,
                          () if moe else (w_gate_d[j], w_up_d[j], w_down_d[j]), n_tiles * MOE_TILE if moe else 0)
        c_f, n_f, m_f = _unpad_state(c_f, n_f, m_f)
        outs["cp"].append(c_f)
        outs["np"].append(n_f)
        outs["mp"].append(m_f)

        (ya, qb, qc, kc, kct, vc, oc, gt, gtt, kbf, vbf, kbh, vbh, va) = _proj(
            xs, gmix, p["wp"], p["bg"], lng, lnb, wm_s.astype(BF16), bs_s, n_samp, n_samp, True)
        r3 = lambda a: a.reshape(n_dec, n_new, a.shape[-1])
        yb = _sb_sample(r3(qb), r3(kbh), r3(vbh), cache_kt, cache_vt, gnb, l)
        n_pad = MLSTM_BLOCK - n_new
        padr = lambda a: jnp.pad(r3(a), ((0, 0), (0, n_pad), (0, 0)))
        per_stream = lambda a: jnp.transpose(a.reshape(a.shape[0], n_dec, n_new), (1, 0, 2))
        kct_s = jnp.pad(per_stream(kct), ((0, 0), (0, 0), (0, n_pad)))
        gate_row = jnp.arange(GATE_ROWS)[None, :, None]
        gtt_pad = jnp.broadcast_to(jnp.where(gate_row < C_HEADS, NEG_BIG, 0.0).astype(F32),
                                   (n_dec, GATE_ROWS, n_pad))
        gtt_s = jnp.concatenate([per_stream(gtt), gtt_pad], axis=2)
        gt_s = jnp.concatenate([r3(gt), jnp.broadcast_to(
            jnp.where(jnp.arange(LANES) < C_HEADS, NEG_BIG, 0.0).astype(F32), (n_dec, n_pad, LANES))], axis=1)
        c0, n0, m0 = _pad_state(state_c_mlstm[l], state_n_mlstm[l], state_m_mlstm[l])
        yc, c_u, n_u, m_u = _mlstm(padr(qc), padr(kc), kct_s, padr(vc), padr(oc), gt_s, gtt_s, p["gnc"],
                                   c0, n0, m0, 1)
        yc = yc[:, :n_new].reshape(n_samp, C_WIDTH_PAD)
        merged_s = _merge(xs, ya, yb.reshape(n_samp, B_WIDTH), yc, p["wa"], p["wb"], p["wc"],
                          gffn, wr, n_samp)
        c_u, n_u, m_u = _unpad_state(c_u, n_u, m_u)

        if moe:
            xmid_p, xrow_p, ri_p, rf_p, cnt_p, xsort = merged_p
            xmid_s, xrow_s, ri_s, rf_s, cnt_s = merged_s
            dest_p, dest_s, tile_expert, n_used = _route_plan(ri_p, cnt_p, ri_s, cnt_s, n_tiles)
            as_tiles = lambda a: a.reshape(a.shape[0] // ROW_TILE, ROW_TILE, LANES)
            gates = lambda rf: jnp.pad(rf[:2].T, ((0, 0), (0, LANES - 2)))
            xsort = _dispatch(dest_p, xrow_p, xsort, MOE_DISPATCH_BLOCK)
            xsort = _dispatch(dest_s, xrow_s, xsort, n_samp)
            ysort = _ffn_routed(tile_expert, n_used, xsort.reshape(-1, LANES), wg, wu, wd)
            xp = _combine(dest_p, as_tiles(ysort), gates(rf_p), xmid_p, gfin, MOE_COMBINE_BLOCK, final)
            xs = _combine(dest_s, as_tiles(ysort), gates(rf_s), xmid_s, gfin, n_samp, final)
        else:
            nxt = (l + 1) // 2
            to_cast = (w_gate_e[nxt], w_up_e[nxt], w_down_e[nxt]) if l + 1 < DEPTH else ()
            wg, wu, wd = (w[None] for w in merged_p[2])
            xp, expert_bf16 = _ffn_dense(merged_p[1], wg, wu, wd, merged_p[0], gfin, ROW_BLOCK, final, to_cast)
            xs, _ = _ffn_dense(merged_s[1], wg, wu, wd, merged_s[0], gfin, n_samp, final)

        outs["kbs"].append(kbf.reshape(n_dec, n_new, B_HEADS, B_HEAD_DIM))
        outs["vbs"].append(vbf.reshape(n_dec, n_new, B_HEADS, B_HEAD_DIM))
        outs["cs"].append(c_u)
        outs["ns"].append(n_u)
        outs["ms"].append(m_u)
        outs["vas"].append(va.reshape(n_dec, n_new, A_WIDTH))

    st = lambda k: jnp.stack(outs[k])
    heads_last = lambda a: jnp.transpose(a.reshape(DEPTH, 1, B_HEADS, B_HEAD_DIM, n_seq), (0, 1, 4, 2, 3))
    return (xp.reshape(1, n_seq, D_MODEL), xs.reshape(n_dec, n_new, D_MODEL),
            heads_last(kbt), heads_last(vbt), st("cp"), st("np"), st("mp"),
            st("kbs"), st("vbs"), st("cs"), st("ns"), st("ms"), st("vas"))
```

```python
import functools
import math

import jax
import jax.numpy as jnp
from jax import lax
from jax.experimental import pallas as pl
from jax.experimental.pallas import tpu as pltpu

F32 = jnp.float32
BF16 = jnp.bfloat16

D_MODEL = 1024
DEPTH = 2
EPS = 1e-6
CHUNK = 64
A_WIDTH = 256
A_GROUPS = 4
A_GROUP_DIM = 64
GMLP_CHUNK = 128
B_HEAD_DIM = 64
B_WIDTH = 384
B_HEADS = 6
B_PAIRS = 3
C_HEADS = 4
C_HEAD_DIM = 96
C_WIDTH = 384
D_FF = 2816
N_EXPERTS = 8

LANES = 128
C_HEAD_PAD = LANES
C_WIDTH_PAD = C_HEADS * C_HEAD_PAD

OFF_AU, OFF_AV, OFF_BQ, OFF_BK, OFF_BV = 0, 256, 512, 896, 1280
OFF_CQ, OFF_CK, OFF_CV, OFF_CO, OFF_CG = 1664, 2048, 2432, 2816, 3200
P_A = 0
P_BQ = 512
P_BK = P_BQ + B_WIDTH
P_BV = P_BK + B_WIDTH
P_CQ = P_BV + B_WIDTH
P_CK = P_CQ + C_WIDTH_PAD
P_CV = P_CK + C_WIDTH_PAD
P_CO = P_CV + C_WIDTH_PAD
P_CG = P_CO + C_WIDTH_PAD

SB_DEAD_LOG_WEIGHT = -110.0
SB_BLOCK = 256
SB_BLOCKS_PER_STEP = 8
MLSTM_BLOCK = 128
MLSTM_SUB_BLOCKS = 8
GATE_ROWS = 2 * C_HEADS
NEG_BIG = -1e30
ROW_TILE = 8
MOE_TILE = 512
FFN_F_TILE = 1408
ROW_BLOCK = 512
MOE_DISPATCH_BLOCK = 1024
MOE_COMBINE_BLOCK = 512
CAST_SLAB_ROWS = 128

VMEM_LIMIT = 56 * 1024 * 1024


def _cparams(sem):
    return pltpu.CompilerParams(dimension_semantics=sem, vmem_limit_bytes=VMEM_LIMIT)


def _gelu(x):
    return 0.5 * x * (1.0 + lax.erf(x * (1.0 / math.sqrt(2.0))))


def _log_sigmoid(x):
    return jnp.minimum(x, 0.0) - jnp.log(1.0 + jnp.exp(-jnp.abs(x)))


def _sigmoid(x):
    return 1.0 / (1.0 + jnp.exp(-x))


def _split_bf16(x):
    hi = x.astype(BF16)
    lo = (x - hi.astype(F32)).astype(BF16)
    return hi, lo


def _store_token_tiles(ref, x):
    t = x.shape[0]
    for s in range(D_MODEL // LANES):
        ref[pl.ds(s, t, stride=ROW_TILE), :] = x[:, s * LANES:(s + 1) * LANES]


def _load_token_tiles(ref):
    t = ref.shape[0] // ROW_TILE
    return jnp.concatenate([ref[pl.ds(s, t, stride=ROW_TILE), :] for s in range(D_MODEL // LANES)], axis=1)


def _dot(a, b):
    return jnp.dot(a, b, preferred_element_type=F32)


def _dot_nt(a, b):
    return lax.dot_general(a, b, (((1,), (1,)), ((), ())), preferred_element_type=F32)


def _proj_kernel(n_chunks, chunk, sample, x_ref, gmix_ref, w_ref, bg_ref, lng_ref, lnb_ref, ws_ref, bs_ref,
                 *refs):
    n_out = 14 if sample else 11
    ya_ref, qb_ref, qc_ref, kc_ref, kct_ref, vc_ref, oc_ref, gt_ref, gtt_ref, *kv_refs = refs[len(refs) - n_out:]
    x = x_ref[...]
    xn = (x * lax.rsqrt(jnp.mean(x * x, axis=-1, keepdims=True) + EPS) * gmix_ref[...]).astype(BF16)

    def proj(off, width):
        return _dot(xn, w_ref[:, off:off + width])

    za = proj(P_A, 2 * A_WIDTH)
    u = _gelu(za[:, :A_WIDTH])
    gv = _gelu(za[:, A_WIDTH:])
    xc = gv - jnp.mean(gv, axis=-1, keepdims=True)
    va = xc * lax.rsqrt(jnp.mean(xc * xc, axis=-1, keepdims=True) + EPS) * lng_ref[...] + lnb_ref[...]
    if sample:
        kv_refs[4][...] = va
    vab = va.astype(BF16)
    lane_group = lax.broadcasted_iota(jnp.int32, (chunk, A_WIDTH), 1) // A_GROUP_DIM
    for c in range(n_chunks):
        rows = slice(c * chunk, (c + 1) * chunk)
        vch = vab[rows]
        s = jnp.zeros((chunk, A_WIDTH), F32)
        for g in range(A_GROUPS):
            s = jnp.where(lane_group == g, _dot(ws_ref[g], vch), s)
        ya_ref[rows, :] = (u[rows] * (s + bs_ref[...])).astype(BF16)

    qb_ref[...] = (proj(P_BQ, B_WIDTH) * (1.0 / math.sqrt(B_HEAD_DIM))).astype(BF16)
    zk = proj(P_BK, B_WIDTH)
    zv = proj(P_BV, B_WIDTH)
    if sample:
        kv_refs[0][...] = zk
        kv_refs[1][...] = zv
        kv_refs[2][...] = zk.astype(BF16)
        kv_refs[3][...] = zv.astype(BF16)
    else:
        kv_refs[0][0] = zk.T
        kv_refs[1][0] = zv.T
        for other in range(1, kv_refs[0].shape[0]):
            kv_refs[0][other] = jnp.zeros(kv_refs[0].shape[1:], F32)
            kv_refs[1][other] = jnp.zeros(kv_refs[1].shape[1:], F32)

    qc_ref[...] = proj(P_CQ, C_WIDTH_PAD).astype(BF16)
    zk = proj(P_CK, C_WIDTH_PAD) * (C_HEAD_DIM ** -0.5)
    kc_ref[...] = zk.astype(BF16)
    kct_ref[...] = zk.T.astype(BF16)
    vc_ref[...] = proj(P_CV, C_WIDTH_PAD).astype(BF16)
    oc_ref[...] = proj(P_CO, C_WIDTH_PAD)
    g = proj(P_CG, LANES) + bg_ref[...]
    lane = lax.broadcasted_iota(jnp.int32, g.shape, 1)
    gates = jnp.where(lane < C_HEADS, g, _log_sigmoid(g))
    gt_ref[...] = gates
    gtt_ref[...] = gates.T[:GATE_ROWS, :]


def _proj(x, gmix, wp, bg, lng, lnb, ws, bs, tm, chunk, sample, layer=0, kv_all=None):
    n = x.shape[0]
    row = lambda w: pl.BlockSpec((tm, w), lambda i: (i, 0))
    full = lambda a: pl.BlockSpec(a.shape, lambda i: (0,) * a.ndim)
    col = lambda h: pl.BlockSpec((h, tm), lambda i: (0, i))
    outs = [((n, A_WIDTH), BF16, row(A_WIDTH)), ((n, B_WIDTH), BF16, row(B_WIDTH)),
            ((n, C_WIDTH_PAD), BF16, row(C_WIDTH_PAD)), ((n, C_WIDTH_PAD), BF16, row(C_WIDTH_PAD)),
            ((C_WIDTH_PAD, n), BF16, col(C_WIDTH_PAD)), ((n, C_WIDTH_PAD), BF16, row(C_WIDTH_PAD)),
            ((n, C_WIDTH_PAD), F32, row(C_WIDTH_PAD)), ((n, LANES), F32, row(LANES)),
            ((GATE_ROWS, n), F32, col(GATE_ROWS))]
    out_shape = [jax.ShapeDtypeStruct(s, dt) for s, dt, _ in outs]
    out_specs = [spec for _, _, spec in outs]
    if sample:
        extra = [(B_WIDTH, F32), (B_WIDTH, F32), (B_WIDTH, BF16), (B_WIDTH, BF16), (A_WIDTH, F32)]
        out_shape += [jax.ShapeDtypeStruct((n, w), dt) for w, dt in extra]
        out_specs += [row(w) for w, _ in extra]
    ins = [x, gmix, wp, bg, lng, lnb, ws, bs]
    in_specs = [row(D_MODEL), full(gmix), full(wp), full(bg), full(lng), full(lnb), full(ws), full(bs)]
    aliases = {}
    if not sample:
        out_shape += [jax.ShapeDtypeStruct((DEPTH, B_WIDTH, n), F32)] * 2
        if kv_all is None:
            out_specs += [pl.BlockSpec((DEPTH, B_WIDTH, tm), lambda i: (0, 0, i))] * 2
        else:
            out_specs += [pl.BlockSpec((1, B_WIDTH, tm), lambda i: (layer, 0, i))] * 2
            aliases = {len(ins): len(out_shape) - 2, len(ins) + 1: len(out_shape) - 1}
            ins += list(kv_all)
            in_specs += [pl.BlockSpec(memory_space=pl.ANY)] * 2
    return pl.pallas_call(
        functools.partial(_proj_kernel, tm // chunk, chunk, sample),
        out_shape=out_shape,
        grid=(n // tm,),
        in_specs=in_specs,
        out_specs=out_specs,
        input_output_aliases=aliases,
        compiler_params=_cparams(("parallel",)),
        name="proj",
    )(*ins)


def _sb_weights(qh, kblk, carry, tri2, mask, transposed, shift=None):
    z = _dot(qh, kblk) if transposed else _dot_nt(qh, kblk)
    if shift is not None:
        z = z + shift
    tk = z.shape[1]
    drop = jnp.maximum(z, 0.0) + jnp.log(1.0 + jnp.exp(-jnp.abs(z)))
    if mask is not None:
        drop = jnp.where(mask, drop, 0.0)
    cs = _dot(jnp.concatenate(_split_bf16(drop), axis=1), tri2)
    if tk >= LANES:
        carry_b = jnp.concatenate([carry] * (tk // LANES), axis=1)
    else:
        carry_b = carry[:, :tk]
    a = jnp.exp(z - cs - carry_b)
    if mask is not None:
        a = jnp.where(mask, a, 0.0)
    return a.astype(BF16), carry + jnp.broadcast_to(cs[:, :1], carry.shape)


def _sb_step(qh, kblk, vblk, carry, acc, tri2, mask, transposed):
    a, carry = _sb_weights(qh, kblk, carry, tri2, mask, transposed)
    return carry, acc + (_dot_nt(a, vblk) if transposed else _dot(a, vblk))


def _tri2(tk):
    j = lax.broadcasted_iota(jnp.int32, (2 * tk, tk), 0)
    s = lax.broadcasted_iota(jnp.int32, (2 * tk, tk), 1)
    return (jnp.where(j >= tk, j - tk, j) >= s).astype(BF16)


def _sb_finish(acc_s, gn_ref, o_ref, head0):
    out = jnp.where(head0, acc_s[0], acc_s[1])
    r = lax.broadcasted_iota(jnp.int32, (LANES, LANES), 0) // B_HEAD_DIM
    c = lax.broadcasted_iota(jnp.int32, (LANES, LANES), 1) // B_HEAD_DIM
    same_head = (r == c).astype(BF16)
    hi, lo = _split_bf16(out * out)
    ms = (_dot(hi, same_head) + _dot(lo, same_head)) * (1.0 / B_HEAD_DIM)
    return (out * lax.rsqrt(ms + EPS) * gn_ref[...]).astype(o_ref.dtype)


def _sb_walk(qh, load_kv, first_block, carry_s, acc_s, tri):
    def alive():
        return jnp.minimum(jnp.min(carry_s[0]), jnp.min(carry_s[1])) <= -SB_DEAD_LOG_WEIGHT

    def cond(st):
        j, live = st
        return jnp.logical_and(j >= 0, live)

    def body(st):
        j, _ = st
        kblk, vblk = load_kv(j)
        for h in range(2):
            carry, acc = _sb_step(qh[h], kblk, vblk, carry_s[h], acc_s[h], tri, None, True)
            carry_s[h] = carry
            acc_s[h] = acc
        return j - 1, alive()

    lax.while_loop(cond, body, (first_block, alive()))


def _sb_prompt_kernel(q_ref, k_ref, v_ref, gn_ref, o_ref, carry_s, acc_s):
    tq = SB_BLOCK
    n_q = q_ref.shape[0] // tq
    head0 = lax.broadcasted_iota(jnp.int32, (tq, LANES), 1) < B_HEAD_DIM
    tri = _tri2(tq)
    t = lax.broadcasted_iota(jnp.int32, (tq, tq), 0)
    s = lax.broadcasted_iota(jnp.int32, (tq, tq), 1)
    causal = s < t
    zeros = jnp.zeros((tq, LANES), F32)

    def load_kv(j):
        cols = pl.ds(pl.multiple_of(j * tq, tq), tq)
        return k_ref[:, cols].astype(BF16), v_ref[:, cols].astype(BF16)

    first = pl.program_id(1) * n_q
    kv = [load_kv(jnp.maximum(first + b - 1, 0)) for b in range(n_q + 1)]
    qhs = []
    for b in range(n_q):
        q = q_ref[b * tq:(b + 1) * tq, :]
        qh = [jnp.where(head0, q, 0), jnp.where(head0, 0, q)]
        qhs.append(qh)
        no_prev = jnp.where(first + b >= 1, 0.0, NEG_BIG)
        v_both = jnp.concatenate([kv[b + 1][1], kv[b][1]], axis=1)
        for h in range(2):
            a_diag, carry = _sb_weights(qh[h], kv[b + 1][0], zeros, tri, causal, True)
            a_prev, carry = _sb_weights(qh[h], kv[b][0], carry, tri, None, True, no_prev)
            carry_s[b, h] = carry
            acc_s[b, h] = _dot_nt(jnp.concatenate([a_diag, a_prev], axis=1), v_both)
    for b in range(n_q):
        _sb_walk(qhs[b], load_kv, first + b - 2, carry_s.at[b], acc_s.at[b], tri)
        o_ref[b * tq:(b + 1) * tq, :] = _sb_finish(acc_s.at[b], gn_ref, o_ref, head0)


def _sb_prompt(q, k, v, gn, layer):
    n = q.shape[0]
    n_q = SB_BLOCKS_PER_STEP
    tq = SB_BLOCK * n_q
    blk = pl.BlockSpec((tq, LANES), lambda p, i: (i, p))
    seq = pl.BlockSpec((None, LANES, n), lambda p, i: (layer, p, 0))
    state = pltpu.VMEM((n_q, 2, SB_BLOCK, LANES), F32)
    return pl.pallas_call(
        _sb_prompt_kernel,
        out_shape=jax.ShapeDtypeStruct((n, B_WIDTH), BF16),
        grid=(B_PAIRS, n // tq),
        in_specs=[blk, seq, seq, pl.BlockSpec((1, LANES), lambda p, i: (0, p))],
        out_specs=blk,
        scratch_shapes=[state, state],
        compiler_params=_cparams(("parallel", "parallel")),
        name="sb_prompt",
    )(q, k, v, gn)


def _sb_sample_kernel(q_ref, kn_ref, vn_ref, kc_ref, vc_ref, gn_ref, o_ref, carry_s, acc_s):
    tq = q_ref.shape[1]
    tk = SB_BLOCK
    head0 = lax.broadcasted_iota(jnp.int32, (tq, LANES), 1) < B_HEAD_DIM
    q = q_ref[0]
    zero = jnp.zeros_like(q)
    qh = [jnp.where(head0, q, zero), jnp.where(head0, zero, q)]
    t = lax.broadcasted_iota(jnp.int32, (tq, tq), 0)
    s = lax.broadcasted_iota(jnp.int32, (tq, tq), 1)
    causal = s < t
    zeros = jnp.zeros((tq, LANES), F32)
    def load_kv(j):
        cols = pl.ds(pl.multiple_of(j * tk, tk), tk)
        return kc_ref[0, :, cols].astype(BF16), vc_ref[0, :, cols].astype(BF16)

    last = kc_ref.shape[2] // tk - 1
    kp, vp = load_kv(last)
    tri = _tri2(tk)
    for h in range(2):
        carry, acc = _sb_step(qh[h], kn_ref[0], vn_ref[0], zeros, zeros, _tri2(tq), causal, False)
        carry, acc = _sb_step(qh[h], kp, vp, carry, acc, tri, None, True)
        carry_s[h] = carry
        acc_s[h] = acc
    _sb_walk(qh, load_kv, last - 1, carry_s, acc_s, tri)
    o_ref[0] = _sb_finish(acc_s, gn_ref, o_ref, head0)


def _sb_sample(q, kn, vn, kc, vc, gn, layer):
    nb, tq, _ = q.shape
    past = kc.shape[3]
    new = pl.BlockSpec((1, tq, LANES), lambda b, p: (b, 0, p))
    old = pl.BlockSpec((None, 1, LANES, past), lambda b, p: (layer, b, p, 0))
    return pl.pallas_call(
        _sb_sample_kernel,
        out_shape=jax.ShapeDtypeStruct((nb, tq, B_WIDTH), BF16),
        grid=(nb, B_PAIRS),
        in_specs=[new, new, new, old, old, pl.BlockSpec((1, LANES), lambda b, p: (0, p))],
        out_specs=new,
        scratch_shapes=[pltpu.VMEM((2, tq, LANES), F32), pltpu.VMEM((2, tq, LANES), F32)],
        compiler_params=_cparams(("parallel", "parallel")),
        name="sb_sample",
    )(q, kn, vn, kc, vc, gn)


def _split3(x):
    h1 = x.astype(BF16)
    r1 = x - h1.astype(F32)
    h2 = r1.astype(BF16)
    return h1, h2, (r1 - h2.astype(F32)).astype(BF16)


def _dot3(x, rhs01):
    return _dot(jnp.concatenate(_split3(x), axis=1), jnp.concatenate([rhs01] * 3, axis=0))


def _dot3_left(lhs01, x):
    return _dot(jnp.concatenate([lhs01] * 3, axis=1), jnp.concatenate(_split3(x), axis=0))


def _mlstm_kernel(n_sub, q_ref, k_ref, kt_ref, v_ref, o_ref, gt_ref, gtt_ref, gn_ref, c0_ref, n0_ref, m0_ref,
                  yc_ref, c_out, n_out, m_out, c_s, n_s, m_s):
    L = MLSTM_BLOCK
    t_blk = pl.program_id(1)

    @pl.when(t_blk == 0)
    def _():
        c_s[...] = c0_ref[0]
        n_s[...] = n0_ref[0]
        m_s[...] = m0_ref[0]

    r = lax.broadcasted_iota(jnp.int32, (L, L), 0)
    c = lax.broadcasted_iota(jnp.int32, (L, L), 1)
    causal = c <= r
    upper = (r <= c).astype(BF16)
    ones_sq = jnp.ones((L, LANES), BF16)
    sel_r = lax.broadcasted_iota(jnp.int32, (LANES, C_WIDTH_PAD), 0)
    sel_c = lax.broadcasted_iota(jnp.int32, (LANES, C_WIDTH_PAD), 1) // C_HEAD_PAD
    sel_p = (sel_r == sel_c).astype(BF16)
    sel_b = (sel_r == sel_c + C_HEADS).astype(BF16)
    lane = c

    a_rows_all, pb_cols_all = [], []
    for sub in range(n_sub):
        rows = slice(sub * L, (sub + 1) * L)
        gtt = gtt_ref[0, :, rows]
        bct = _dot3(gtt, upper)
        a_rows_all.append(gtt[:C_HEADS] - bct[C_HEADS:])
        gt = gt_ref[0, rows, :]
        bc = _dot3_left(causal.astype(BF16), gt)
        pmax = gt - pltpu.roll(bc, LANES - C_HEADS, axis=1)
        for sh in (1, 2, 4, 8, 16, 32, 64):
            pmax = jnp.maximum(pmax, jnp.where(r >= sh, pltpu.roll(pmax, sh, axis=0), -jnp.inf))
        pb_cols_all.append(jnp.where(lane < C_HEADS, pmax, bc))
    pb_cols_all = jnp.concatenate(pb_cols_all, axis=0)
    p_all = _dot3(pb_cols_all, sel_p)
    b_all = _dot3(pb_cols_all, sel_b)

    state = [(c_s[h], n_s[h], m_s[h]) for h in range(C_HEADS)]
    for sub in range(n_sub):
        rows = slice(sub * L, (sub + 1) * L)
        a_rows = a_rows_all[sub]
        heads = []
        for h in range(C_HEADS):
            lanes = slice(h * C_HEAD_PAD, (h + 1) * C_HEAD_PAD)
            q = q_ref[0, rows, lanes]
            v1 = jnp.concatenate([v_ref[0, rows, lanes], ones_sq], axis=1)
            p_rep = p_all[rows, lanes]
            b_rep = b_all[rows, lanes]
            a_row = a_rows[h:h + 1, :]
            p_last = p_rep[L - 1:L, :]
            c_prev, n_prev, m_prev = state[h]

            w = jnp.exp(jnp.where(causal, a_row - p_rep, -jnp.inf)) * _dot_nt(q, k_ref[0, rows, lanes])
            kwt = (kt_ref[0, lanes, rows].astype(F32) * jnp.exp(a_row - p_last)).astype(BF16)
            both = _dot(jnp.concatenate([w.astype(BF16), kwt], axis=0), v1)
            here_sums = both[:L]
            fresh = both[L:]
            past_sums = _dot(q, jnp.concatenate([c_prev, n_prev], axis=1).astype(BF16))
            top = jnp.maximum(m_prev, p_rep)
            past = jnp.exp(m_prev - top)
            here = jnp.exp(p_rep - top)
            mix = jnp.concatenate([past, past], axis=1) * past_sums + jnp.concatenate([here, here], axis=1) * here_sums
            den = jnp.maximum(jnp.abs(mix[:, LANES:]), jnp.exp(-(b_rep + top)))
            heads.append(mix[:, :LANES] / den)

            top_last = jnp.maximum(m_prev, p_last)
            decay = jnp.exp(m_prev - top_last)
            gain = jnp.exp(p_last - top_last)
            state[h] = (decay * c_prev + gain * fresh[:, :LANES], decay * n_prev + gain * fresh[:, LANES:],
                        b_rep[L - 1:L, :] + top_last)

        hh = jnp.concatenate(heads, axis=0)
        ms = _dot(jnp.concatenate(_split_bf16(hh * hh), axis=1),
                  jnp.concatenate([ones_sq, ones_sq], axis=0)) * (1.0 / C_HEAD_DIM)
        hn = hh * lax.rsqrt(ms + EPS)
        for h in range(C_HEADS):
            lanes = slice(h * C_HEAD_PAD, (h + 1) * C_HEAD_PAD)
            yc_ref[0, rows, lanes] = (hn[h * L:(h + 1) * L] * gn_ref[:, lanes]
                                      * _sigmoid(o_ref[0, rows, lanes])).astype(BF16)

    for h in range(C_HEADS):
        c_s[h], n_s[h], m_s[h] = state[h]

    @pl.when(t_blk == pl.num_programs(1) - 1)
    def _():
        c_out[0] = c_s[...]
        n_out[0] = n_s[...]
        m_out[0] = m_s[...]


def _mlstm(q, k, kt, v, o, gt, gtt, gn, c0, n0, m0, n_sub):
    nb, n, _ = q.shape
    tb = n_sub * MLSTM_BLOCK
    seq = lambda w: pl.BlockSpec((1, tb, w), lambda b, t: (b, t, 0))
    seq_t = lambda h: pl.BlockSpec((1, h, tb), lambda b, t: (b, 0, t))
    st = lambda a: pl.BlockSpec((1,) + a.shape[1:], lambda b, t: (b,) + (0,) * (a.ndim - 1))
    return pl.pallas_call(
        functools.partial(_mlstm_kernel, n_sub),
        out_shape=[jax.ShapeDtypeStruct((nb, n, C_WIDTH_PAD), BF16),
                   jax.ShapeDtypeStruct(c0.shape, F32),
                   jax.ShapeDtypeStruct(n0.shape, F32),
                   jax.ShapeDtypeStruct(m0.shape, F32)],
        grid=(nb, n // tb),
        in_specs=[seq(C_WIDTH_PAD), seq(C_WIDTH_PAD), seq_t(C_WIDTH_PAD), seq(C_WIDTH_PAD), seq(C_WIDTH_PAD),
                  seq(LANES), seq_t(GATE_ROWS), pl.BlockSpec((1, C_WIDTH_PAD), lambda b, t: (0, 0)),
                  st(c0), st(n0), st(m0)],
        out_specs=[seq(C_WIDTH_PAD), st(c0), st(n0), st(m0)],
        scratch_shapes=[pltpu.VMEM(c0.shape[1:], F32), pltpu.VMEM(n0.shape[1:], F32),
                        pltpu.VMEM(m0.shape[1:], F32)],
        compiler_params=_cparams(("parallel", "arbitrary")),
        name="mlstm",
    )(q, k, kt, v, o, gt, gtt, gn, c0, n0, m0)


def _mixer_out(x_ref, ya_ref, yb_ref, yc_ref, wa_ref, wb_ref, wc_ref, g_ref, xmid_ref):
    y = _dot(ya_ref[...], wa_ref[...]) + _dot(yb_ref[...], wb_ref[...]) + _dot(yc_ref[...], wc_ref[...])
    x = x_ref[...] + y
    xmid_ref[...] = x
    return x * lax.rsqrt(jnp.mean(x * x, axis=-1, keepdims=True) + EPS) * g_ref[...]


def _merge_dense_kernel(n_cast, x_ref, ya_ref, yb_ref, yc_ref, wa_ref, wb_ref, wc_ref, g_ref, *refs):
    casts_in, xmid_ref, xn_ref, casts_out = refs[:n_cast], refs[n_cast], refs[n_cast + 1], refs[n_cast + 2:]
    xn_ref[...] = _mixer_out(x_ref, ya_ref, yb_ref, yc_ref, wa_ref, wb_ref, wc_ref, g_ref,
                             xmid_ref).astype(BF16)
    for src, dst in zip(casts_in, casts_out):
        dst[...] = src[...].astype(BF16)


def _merge_moe_kernel(x_ref, ya_ref, yb_ref, yc_ref, wa_ref, wb_ref, wc_ref, g_ref, wrt_ref,
                      earlier_ref, xmid_ref, xrow_ref, ri_ref, rf_ref, cnt_ref, *rest):
    tm = x_ref.shape[0]
    run_s = rest[-1]
    if len(rest) == 2:
        rest[0][...] = jnp.zeros(rest[0].shape, F32)

    @pl.when(pl.program_id(0) == 0)
    def _():
        run_s[...] = jnp.zeros_like(run_s)

    xn = _mixer_out(x_ref, ya_ref, yb_ref, yc_ref, wa_ref, wb_ref, wc_ref, g_ref, xmid_ref)
    _store_token_tiles(xrow_ref, xn)

    hi, lo = _split_bf16(xn)
    part = _dot_nt(wrt_ref[...], hi)
    lg = part[:N_EXPERTS] + part[N_EXPERTS:] + _dot_nt(wrt_ref[:N_EXPERTS, :], lo)
    expert = lax.broadcasted_iota(jnp.int32, lg.shape, 0)
    m1 = jnp.max(lg, axis=0, keepdims=True)
    i1 = jnp.min(jnp.where(lg == m1, expert, N_EXPERTS), axis=0, keepdims=True)
    lg2 = jnp.where(expert == i1, -jnp.inf, lg)
    m2 = jnp.max(lg2, axis=0, keepdims=True)
    i2 = jnp.min(jnp.where(lg2 == m2, expert, N_EXPERTS), axis=0, keepdims=True)
    e2 = jnp.exp(m2 - m1)
    g1 = 1.0 / (1.0 + e2)
    g2 = e2 * g1

    sel1 = expert == i1
    sel2 = expert == i2
    onehot = jnp.logical_or(sel1, sel2).astype(BF16)
    before = _dot(onehot, earlier_ref[...]) + jnp.concatenate([run_s[...]] * (tm // LANES), axis=1)
    rank1 = jnp.sum(jnp.where(sel1, before, 0.0), axis=0, keepdims=True).astype(jnp.int32)
    rank2 = jnp.sum(jnp.where(sel2, before, 0.0), axis=0, keepdims=True).astype(jnp.int32)
    run_s[...] += _dot(onehot, jnp.ones((tm, LANES), BF16))
    cnt_ref[...] = run_s[...].astype(jnp.int32)
    ri_ref[...] = jnp.where(expert == 0, i1, jnp.where(expert == 1, i2,
                            jnp.where(expert == 2, rank1, jnp.where(expert == 3, rank2, 0))))
    rf_ref[...] = jnp.where(expert == 0, g1, jnp.where(expert == 1, g2, 0.0))


def _merge(x, ya, yb, yc, wa, wb, wc, g, wr, tm, cast=(), zero_rows=0):
    n = x.shape[0]
    assert zero_rows % (n // tm) == 0
    assert all(a.shape[0] // CAST_SLAB_ROWS <= n // tm for a in cast)
    row = lambda w: pl.BlockSpec((tm, w), lambda i: (i, 0))
    full = lambda a: pl.BlockSpec(a.shape, lambda i: (0,) * a.ndim)
    ins = [x, ya, yb, yc, wa, wb, wc, g]
    in_specs = [row(D_MODEL), row(A_WIDTH), row(B_WIDTH), row(C_WIDTH_PAD), full(wa), full(wb), full(wc), full(g)]
    if wr is None:
        slab = lambda a: pl.BlockSpec((CAST_SLAB_ROWS, a.shape[1]),
                                      lambda i: (jnp.minimum(i, a.shape[0] // CAST_SLAB_ROWS - 1), 0))
        outs = pl.pallas_call(
            functools.partial(_merge_dense_kernel, len(cast)),
            out_shape=[jax.ShapeDtypeStruct((n, D_MODEL), F32), jax.ShapeDtypeStruct((n, D_MODEL), BF16)]
            + [jax.ShapeDtypeStruct(a.shape, BF16) for a in cast],
            grid=(n // tm,), in_specs=in_specs + [slab(a) for a in cast],
            out_specs=[row(D_MODEL), row(D_MODEL)] + [slab(a) for a in cast],
            compiler_params=_cparams(("arbitrary",)), name="merge_dense",
        )(*ins, *cast)
        return outs[0], outs[1], outs[2:]
    earlier = (jnp.arange(tm)[:, None] < jnp.arange(tm)[None, :]).astype(BF16)
    col = pl.BlockSpec((N_EXPERTS, tm), lambda i: (0, i))
    out_shape = [jax.ShapeDtypeStruct((n, D_MODEL), F32),
                 jax.ShapeDtypeStruct((n * ROW_TILE, LANES), F32),
                 jax.ShapeDtypeStruct((N_EXPERTS, n), jnp.int32),
                 jax.ShapeDtypeStruct((N_EXPERTS, n), F32),
                 jax.ShapeDtypeStruct((N_EXPERTS, LANES), jnp.int32)]
    out_specs = [row(D_MODEL), pl.BlockSpec((tm * ROW_TILE, LANES), lambda i: (i, 0)),
                 col, col, pl.BlockSpec((N_EXPERTS, LANES), lambda i: (0, 0))]
    if zero_rows:
        steps = n // tm
        out_shape.append(jax.ShapeDtypeStruct((zero_rows, ROW_TILE, LANES), F32))
        out_specs.append(pl.BlockSpec((zero_rows // steps, ROW_TILE, LANES), lambda i: (i, 0, 0)))
    return pl.pallas_call(
        _merge_moe_kernel,
        out_shape=out_shape,
        grid=(n // tm,), in_specs=in_specs + [full(wr), full(earlier)],
        out_specs=out_specs,
        scratch_shapes=[pltpu.VMEM((N_EXPERTS, LANES), F32)],
        compiler_params=_cparams(("arbitrary",)), name="merge_moe",
    )(*ins, wr, earlier)


def _dest_blocks(dest, tt):
    return jnp.transpose(dest.reshape(2, dest.shape[1] // tt, tt), (1, 0, 2))


def _dispatch_kernel(dest_ref, src_ref, xs_in_ref, xs_ref, sem):
    del xs_in_ref
    tt = dest_ref.shape[2]

    def issue(t, carry):
        rows = pl.ds(pl.multiple_of(t * ROW_TILE, ROW_TILE), ROW_TILE)
        for k in range(2):
            pltpu.make_async_copy(src_ref.at[rows], xs_ref.at[dest_ref[0, k, t]], sem).start(priority=k)
        return carry

    lax.fori_loop(0, tt, issue, 0, unroll=8)

    def drain(t, carry):
        for k in range(2):
            pltpu.make_async_copy(src_ref.at[pl.ds(0, ROW_TILE)], xs_ref.at[0], sem).wait()
        return carry

    lax.fori_loop(0, tt, drain, 0, unroll=8)


def _dispatch(dest, src, xs, tt):
    n = src.shape[0] // ROW_TILE
    dest3 = _dest_blocks(dest, tt)
    return pl.pallas_call(
        _dispatch_kernel,
        out_shape=jax.ShapeDtypeStruct(xs.shape, xs.dtype),
        grid=(n // tt,),
        in_specs=[pl.BlockSpec((1, 2, tt), lambda i: (i, 0, 0), memory_space=pltpu.SMEM),
                  pl.BlockSpec((tt * ROW_TILE, LANES), lambda i: (i, 0)), pl.BlockSpec(memory_space=pl.ANY)],
        out_specs=pl.BlockSpec(memory_space=pl.ANY),
        scratch_shapes=[pltpu.SemaphoreType.DMA(())],
        input_output_aliases={2: 0},
        compiler_params=_cparams(("arbitrary",)), name="moe_dispatch",
    )(dest3, src, xs)


def _combine_kernel(final, dest_ref, next_ref, ys_ref, rf_ref, xmid_ref, gf_ref, o_ref, buf_s, sems):
    tt = xmid_ref.shape[0]
    i = pl.program_id(0)
    n_steps = pl.num_programs(0)

    def gather(idx_ref, slot):
        def issue(t, carry):
            rows = pl.ds(pl.multiple_of(t * ROW_TILE, ROW_TILE), ROW_TILE)
            for k in range(2):
                pltpu.make_async_copy(ys_ref.at[idx_ref[0, k, t]], buf_s.at[slot, k, rows],
                                      sems.at[slot]).start(priority=k)
            return carry

        lax.fori_loop(0, tt, issue, 0, unroll=8)

    slot = i % 2

    @pl.when(i == 0)
    def _():
        gather(dest_ref, 0)

    @pl.when(i + 1 < n_steps)
    def _():
        gather(next_ref, 1 - slot)

    def drain(t, carry):
        for k in range(2):
            pltpu.make_async_copy(ys_ref.at[0], buf_s.at[slot, 0, pl.ds(0, ROW_TILE)], sems.at[slot]).wait()
        return carry

    lax.fori_loop(0, tt, drain, 0, unroll=8)

    y = (xmid_ref[...] + rf_ref[:, 0:1] * _load_token_tiles(buf_s.at[slot, 0])
         + rf_ref[:, 1:2] * _load_token_tiles(buf_s.at[slot, 1]))
    if final:
        y = y * lax.rsqrt(jnp.mean(y * y, axis=-1, keepdims=True) + EPS) * gf_ref[...]
    o_ref[...] = y


def _combine(dest, ys, rf, xmid, gf, tt, final):
    n = xmid.shape[0]
    n_steps = n // tt
    dest3 = _dest_blocks(dest, tt)
    row = lambda w: pl.BlockSpec((tt, w), lambda i: (i, 0))
    return pl.pallas_call(
        functools.partial(_combine_kernel, final),
        out_shape=jax.ShapeDtypeStruct((n, D_MODEL), F32),
        grid=(n_steps,),
        in_specs=[pl.BlockSpec((1, 2, tt), lambda i: (i, 0, 0), memory_space=pltpu.SMEM),
                  pl.BlockSpec((1, 2, tt), lambda i: (jnp.minimum(i + 1, n_steps - 1), 0, 0),
                               memory_space=pltpu.SMEM),
                  pl.BlockSpec(memory_space=pl.ANY), row(LANES), row(D_MODEL),
                  pl.BlockSpec((1, D_MODEL), lambda i: (0, 0))],
        out_specs=row(D_MODEL),
        scratch_shapes=[pltpu.VMEM((2, 2, tt * ROW_TILE, LANES), F32), pltpu.SemaphoreType.DMA((2,))],
        compiler_params=_cparams(("arbitrary",)), name="moe_combine",
    )(dest3, dest3, ys, rf, xmid, gf)


def _swiglu_acc(xn, wg_ref, wu_ref, wd_ref, acc_s):
    g = _dot(xn, wg_ref[0])
    u = _dot(xn, wu_ref[0])
    acc_s[...] += _dot((g * _sigmoid(g) * u).astype(BF16), wd_ref[0])


def _ffn_dense_kernel(final, n_cast, xn_ref, wg_ref, wu_ref, wd_ref, xmid_ref, gf_ref, *refs):
    casts_in, o_ref, casts_out, acc_s = refs[:n_cast], refs[n_cast], refs[n_cast + 1:-1], refs[-1]
    f = pl.program_id(1)

    @pl.when(f == 0)
    def _():
        acc_s[...] = jnp.zeros_like(acc_s)

    _swiglu_acc(xn_ref[...], wg_ref, wu_ref, wd_ref, acc_s)
    for src, dst in zip(casts_in, casts_out):
        dst[...] = src[...].astype(BF16)

    @pl.when(f == pl.num_programs(1) - 1)
    def _():
        y = xmid_ref[...] + acc_s[...]
        if final:
            y = y * lax.rsqrt(jnp.mean(y * y, axis=-1, keepdims=True) + EPS) * gf_ref[...]
        o_ref[...] = y


def _ffn_dense(xn, wg, wu, wd, xmid, gf, tm, final, cast=()):
    n = xn.shape[0]
    tf = FFN_F_TILE
    n_f = D_FF // tf
    steps = (n // tm) * n_f
    row = lambda w: pl.BlockSpec((tm, w), lambda i, f: (i, 0))
    slabs = [a.reshape(steps, a.size // (steps * a.shape[-1]), a.shape[-1]) for a in cast]
    slab_spec = lambda a: pl.BlockSpec((1,) + a.shape[1:], lambda i, f: (i * n_f + f, 0, 0))
    fcol = lambda i, f: jnp.where(i % 2 == 0, f, n_f - 1 - f)
    outs = pl.pallas_call(
        functools.partial(_ffn_dense_kernel, final, len(cast)),
        out_shape=[jax.ShapeDtypeStruct((n, D_MODEL), F32)] + [jax.ShapeDtypeStruct(a.shape, BF16) for a in slabs],
        grid=(n // tm, n_f),
        in_specs=[row(D_MODEL),
                  pl.BlockSpec((1, D_MODEL, tf), lambda i, f: (0, 0, fcol(i, f))),
                  pl.BlockSpec((1, D_MODEL, tf), lambda i, f: (0, 0, fcol(i, f))),
                  pl.BlockSpec((1, tf, D_MODEL), lambda i, f: (0, fcol(i, f), 0)),
                  row(D_MODEL), pl.BlockSpec((1, D_MODEL), lambda i, f: (0, 0))] + [slab_spec(a) for a in slabs],
        out_specs=[row(D_MODEL)] + [slab_spec(a) for a in slabs],
        scratch_shapes=[pltpu.VMEM((tm, D_MODEL), F32)],
        compiler_params=_cparams(("parallel", "arbitrary")),
        name="ffn_dense",
    )(xn, wg, wu, wd, xmid, gf, *slabs)
    return outs[0], [o.reshape(a.shape) for o, a in zip(outs[1:], cast)]


def _ffn_routed_kernel(te_ref, nu_ref, xs_ref, wg_ref, wu_ref, wd_ref, ys_ref, xb_s, acc_s):
    del te_ref
    i = pl.program_id(0)
    f = pl.program_id(1)

    @pl.when(i < nu_ref[0])
    def _():
        @pl.when(f == 0)
        def _():
            acc_s[...] = jnp.zeros_like(acc_s)
            xb_s[...] = _load_token_tiles(xs_ref).astype(BF16)

        _swiglu_acc(xb_s[...], wg_ref, wu_ref, wd_ref, acc_s)

        @pl.when(f == pl.num_programs(1) - 1)
        def _():
            _store_token_tiles(ys_ref, acc_s[...])

    @pl.when(jnp.logical_and(i >= nu_ref[0], f == 0))
    def _():
        ys_ref[...] = jnp.zeros_like(ys_ref)


def _ffn_routed(tile_expert, n_used, xs, wg, wu, wd):
    n_tiles = tile_expert.shape[0]
    tm, tf = MOE_TILE, FFN_F_TILE
    n_f = D_FF // tf
    last = lambda i, nu: jnp.minimum(i, nu[0] - 1)
    fcol = lambda i, f, nu: jnp.where(i < nu[0], jnp.where(i % 2 == 0, f, n_f - 1 - f),
                                      jnp.where((nu[0] - 1) % 2 == 0, n_f - 1, 0))
    return pl.pallas_call(
        _ffn_routed_kernel,
        out_shape=jax.ShapeDtypeStruct(xs.shape, F32),
        grid_spec=pltpu.PrefetchScalarGridSpec(
            num_scalar_prefetch=2,
            grid=(n_tiles, n_f),
            in_specs=[pl.BlockSpec((tm * ROW_TILE, LANES), lambda i, f, te, nu: (last(i, nu), 0)),
                      pl.BlockSpec((1, D_MODEL, tf), lambda i, f, te, nu: (te[last(i, nu)], 0, fcol(i, f, nu))),
                      pl.BlockSpec((1, D_MODEL, tf), lambda i, f, te, nu: (te[last(i, nu)], 0, fcol(i, f, nu))),
                      pl.BlockSpec((1, tf, D_MODEL), lambda i, f, te, nu: (te[last(i, nu)], fcol(i, f, nu), 0))],
            out_specs=pl.BlockSpec((tm * ROW_TILE, LANES), lambda i, f, te, nu: (i, 0)),
            scratch_shapes=[pltpu.VMEM((tm, D_MODEL), BF16), pltpu.VMEM((tm, D_MODEL), F32)]),
        compiler_params=_cparams(("arbitrary", "arbitrary")),
        name="ffn_routed",
    )(tile_expert, n_used, xs, wg, wu, wd)


def _route_plan(ri_p, cnt_p, ri_s, cnt_s, n_tiles):
    cnt_p, cnt_s = cnt_p[:, 0], cnt_s[:, 0]
    tiles = (cnt_p + cnt_s + MOE_TILE - 1) // MOE_TILE
    ends = jnp.cumsum(tiles)
    start = (ends - tiles) * MOE_TILE
    lookup = lambda table, idx: jnp.sum(
        jnp.where(idx[..., None] == jnp.arange(N_EXPERTS, dtype=jnp.int32), table, 0), axis=-1)
    dest_p = lookup(start, ri_p[0:2]) + ri_p[2:4]
    dest_s = lookup(start + cnt_p, ri_s[0:2]) + ri_s[2:4]
    tile_expert = jnp.minimum(jnp.sum(jnp.arange(n_tiles, dtype=jnp.int32)[:, None] >= ends[None, :], axis=-1),
                              N_EXPERTS - 1).astype(jnp.int32)
    return dest_p, dest_s, tile_expert, ends[-1:].astype(jnp.int32)


def _pad_heads_cols(w):
    w = w.reshape(w.shape[0], C_HEADS, C_HEAD_DIM)
    return jnp.pad(w, ((0, 0), (0, 0), (0, C_HEAD_PAD - C_HEAD_DIM))).reshape(w.shape[0], C_WIDTH_PAD)


def _layer_params(l, w_in, b_gate, w_s, b_s, gn_c, w_out):
    w = w_in[l]
    gates = jnp.pad(w[:, OFF_CG:OFF_CG + 2 * C_HEADS], ((0, 0), (0, LANES - 2 * C_HEADS)))
    c_part = w[:, OFF_CQ:OFF_CG].reshape(D_MODEL, 4 * C_HEADS, C_HEAD_DIM)
    c_part = jnp.pad(c_part, ((0, 0), (0, 0), (0, C_HEAD_PAD - C_HEAD_DIM))).reshape(D_MODEL, 4 * C_WIDTH_PAD)
    wp = jnp.concatenate([w[:, :OFF_CQ], c_part, gates], axis=1).astype(BF16)
    bg = jnp.pad(b_gate[l], (0, LANES - 2 * C_HEADS))[None, :]
    pos = jnp.arange(GMLP_CHUNK)
    mask = (pos[None, :] // CHUNK) <= (pos[:, None] // CHUNK)
    wm = jnp.where(mask[None], w_s[l], 0.0)
    bs = jnp.repeat(b_s[l].T, A_GROUP_DIM, axis=1)
    wo = w_out[l]
    wc = wo[A_WIDTH + B_WIDTH:].reshape(C_HEADS, C_HEAD_DIM, D_MODEL)
    wc = jnp.pad(wc, ((0, 0), (0, C_HEAD_PAD - C_HEAD_DIM), (0, 0))).reshape(C_WIDTH_PAD, D_MODEL)
    gnc = _pad_heads_cols(gn_c[l][None, :])
    return dict(wp=wp, bg=bg, wm=wm, bs=bs, wa=wo[:A_WIDTH].astype(BF16),
                wb=wo[A_WIDTH:A_WIDTH + B_WIDTH].astype(BF16), wc=wc.astype(BF16), gnc=gnc)


def _pad_state(c, n, m):
    p = C_HEAD_PAD - C_HEAD_DIM
    c = jnp.pad(c, ((0, 0), (0, 0), (0, p), (0, p)))
    n = jnp.pad(n, ((0, 0), (0, 0), (0, p)))
    n = jnp.broadcast_to(n[:, :, :, None], n.shape + (LANES,))
    m = jnp.broadcast_to(m[:, :, None, None], m.shape + (1, LANES))
    return c, n, m


def _unpad_state(c, n, m):
    return c[:, :, :C_HEAD_DIM, :C_HEAD_DIM], n[:, :, :C_HEAD_DIM, 0], m[:, :, 0, 0]


def kernel(x_prompt, x_sample, cache_k_b, cache_v_b, state_c_mlstm, state_n_mlstm, state_m_mlstm,
           g_mix, w_in, b_gate, ln_a_g, ln_a_b, w_s, b_s, gn_b, gn_c, w_out,
           g_ffn, w_gate_d, w_up_d, w_down_d, w_router, w_gate_e, w_up_e, w_down_e, g_final):
    n_seq = x_prompt.shape[1]
    n_dec, n_new = x_sample.shape[0], x_sample.shape[1]
    past = cache_k_b.shape[2]
    n_samp = n_dec * n_new

    xp = x_prompt.reshape(n_seq, D_MODEL)
    xs = x_sample.reshape(n_samp, D_MODEL)
    gfin = g_final[None, :]
    keys_last = lambda a: jnp.transpose(a, (0, 1, 3, 4, 2)).reshape(DEPTH, n_dec, B_WIDTH, past)
    cache_kt, cache_vt = keys_last(cache_k_b), keys_last(cache_v_b)

    outs = {k: [] for k in ("kbp", "vbp", "cp", "np", "mp", "kbs", "vbs", "cs", "ns", "ms", "vas")}
    for l in range(DEPTH):
        p = _layer_params(l, w_in, b_gate, w_s, b_s, gn_c, w_out)
        gmix = g_mix[l][None, :]
        lng, lnb = ln_a_g[l][None, :], ln_a_b[l][None, :]
        gnb = gn_b[l][None, :]
        gffn = g_ffn[l][None, :]
        moe = l % 2 == 1
        j = l // 2
        if moe:
            wg, wu, wd = expert_bf16
            wr = jnp.concatenate(_split_bf16(w_router[j].T), axis=0)
        else:
            wr = None
        final = l == DEPTH - 1
        n_tiles = 2 * (n_seq + n_samp) // MOE_TILE + N_EXPERTS

        wm_s = jnp.kron(jnp.eye(n_dec, dtype=F32), p["wm"][:, :n_new, :n_new])
        bs_s = jnp.tile(p["bs"][:n_new], (n_dec, 1))

        (ya, qb, qc, kc, kct, vc, oc, gt, gtt, kbt, vbt) = _proj(
            xp, gmix, p["wp"], p["bg"], lng, lnb, p["wm"].astype(BF16), p["bs"], ROW_BLOCK, GMLP_CHUNK, False,
            l, None if l == 0 else (kbt, vbt))
        yb = _sb_prompt(qb, kbt, vbt, gnb, l)
        c0, n0, m0 = _pad_state(jnp.zeros((1, C_HEADS, C_HEAD_DIM, C_HEAD_DIM), F32),
                                jnp.zeros((1, C_HEADS, C_HEAD_DIM), F32), jnp.zeros((1, C_HEADS), F32))
        yc, c_f, n_f, m_f = _mlstm(qc[None], kc[None], kct[None], vc[None], oc[None], gt[None], gtt[None],
                                   p["gnc"], c0, n0, m0, MLSTM_SUB_BLOCKS)
        merged_p = _merge(xp, ya, yb, yc[0], p["wa"], p["wb"], p["wc"], gffn, wr, ROW_BLOCK,
                          () if moe else (w_gate_d[j], w_up_d[j], w_down_d[j]), n_tiles * MOE_TILE if moe else 0)
        c_f, n_f, m_f = _unpad_state(c_f, n_f, m_f)
        outs["cp"].append(c_f)
        outs["np"].append(n_f)
        outs["mp"].append(m_f)

        (ya, qb, qc, kc, kct, vc, oc, gt, gtt, kbf, vbf, kbh, vbh, va) = _proj(
            xs, gmix, p["wp"], p["bg"], lng, lnb, wm_s.astype(BF16), bs_s, n_samp, n_samp, True)
        r3 = lambda a: a.reshape(n_dec, n_new, a.shape[-1])
        yb = _sb_sample(r3(qb), r3(kbh), r3(vbh), cache_kt, cache_vt, gnb, l)
        n_pad = MLSTM_BLOCK - n_new
        padr = lambda a: jnp.pad(r3(a), ((0, 0), (0, n_pad), (0, 0)))
        per_stream = lambda a: jnp.transpose(a.reshape(a.shape[0], n_dec, n_new), (1, 0, 2))
        kct_s = jnp.pad(per_stream(kct), ((0, 0), (0, 0), (0, n_pad)))
        gate_row = jnp.arange(GATE_ROWS)[None, :, None]
        gtt_pad = jnp.broadcast_to(jnp.where(gate_row < C_HEADS, NEG_BIG, 0.0).astype(F32),
                                   (n_dec, GATE_ROWS, n_pad))
        gtt_s = jnp.concatenate([per_stream(gtt), gtt_pad], axis=2)
        gt_s = jnp.concatenate([r3(gt), jnp.broadcast_to(
            jnp.where(jnp.arange(LANES) < C_HEADS, NEG_BIG, 0.0).astype(F32), (n_dec, n_pad, LANES))], axis=1)
        c0, n0, m0 = _pad_state(state_c_mlstm[l], state_n_mlstm[l], state_m_mlstm[l])
        yc, c_u, n_u, m_u = _mlstm(padr(qc), padr(kc), kct_s, padr(vc), padr(oc), gt_s, gtt_s, p["gnc"],
                                   c0, n0, m0, 1)
        yc = yc[:, :n_new].reshape(n_samp, C_WIDTH_PAD)
        merged_s = _merge(xs, ya, yb.reshape(n_samp, B_WIDTH), yc, p["wa"], p["wb"], p["wc"],
                          gffn, wr, n_samp)
        c_u, n_u, m_u = _unpad_state(c_u, n_u, m_u)

        if moe:
            xmid_p, xrow_p, ri_p, rf_p, cnt_p, xsort = merged_p
            xmid_s, xrow_s, ri_s, rf_s, cnt_s = merged_s
            dest_p, dest_s, tile_expert, n_used = _route_plan(ri_p, cnt_p, ri_s, cnt_s, n_tiles)
            as_tiles = lambda a: a.reshape(a.shape[0] // ROW_TILE, ROW_TILE, LANES)
            gates = lambda rf: jnp.pad(rf[:2].T, ((0, 0), (0, LANES - 2)))
            xsort = _dispatch(dest_p, xrow_p, xsort, MOE_DISPATCH_BLOCK)
            xsort = _dispatch(dest_s, xrow_s, xsort, n_samp)
            ysort = _ffn_routed(tile_expert, n_used, xsort.reshape(-1, LANES), wg, wu, wd)
            xp = _combine(dest_p, as_tiles(ysort), gates(rf_p), xmid_p, gfin, MOE_COMBINE_BLOCK, final)
            xs = _combine(dest_s, as_tiles(ysort), gates(rf_s), xmid_s, gfin, n_samp, final)
        else:
            nxt = (l + 1) // 2
            to_cast = (w_gate_e[nxt], w_up_e[nxt], w_down_e[nxt]) if l + 1 < DEPTH else ()
            wg, wu, wd = (w[None] for w in merged_p[2])
            xp, expert_bf16 = _ffn_dense(merged_p[1], wg, wu, wd, merged_p[0], gfin, ROW_BLOCK, final, to_cast)
            xs, _ = _ffn_dense(merged_s[1], wg, wu, wd, merged_s[0], gfin, n_samp, final)

        outs["kbs"].append(kbf.reshape(n_dec, n_new, B_HEADS, B_HEAD_DIM))
        outs["vbs"].append(vbf.reshape(n_dec, n_new, B_HEADS, B_HEAD_DIM))
        outs["cs"].append(c_u)
        outs["ns"].append(n_u)
        outs["ms"].append(m_u)
        outs["vas"].append(va.reshape(n_dec, n_new, A_WIDTH))

    st = lambda k: jnp.stack(outs[k])
    heads_last = lambda a: jnp.transpose(a.reshape(DEPTH, 1, B_HEADS, B_HEAD_DIM, n_seq), (0, 1, 4, 2, 3))
    return (xp.reshape(1, n_seq, D_MODEL), xs.reshape(n_dec, n_new, D_MODEL),
            heads_last(kbt), heads_last(vbt), st("cp"), st("np"), st("mp"),
            st("kbs"), st("vbs"), st("cs"), st("ns"), st("ms"), st("vas"))
```

```python
import functools
import math

import jax
import jax.numpy as jnp
from jax import lax
from jax.experimental import pallas as pl
from jax.experimental.pallas import tpu as pltpu

F32 = jnp.float32
BF16 = jnp.bfloat16

D_MODEL = 1024
DEPTH = 2
EPS = 1e-6
CHUNK = 64
A_WIDTH = 256
A_GROUPS = 4
A_GROUP_DIM = 64
GMLP_CHUNK = 128
B_HEAD_DIM = 64
B_WIDTH = 384
B_HEADS = 6
B_PAIRS = 3
C_HEADS = 4
C_HEAD_DIM = 96
C_WIDTH = 384
D_FF = 2816
N_EXPERTS = 8

LANES = 128
C_HEAD_PAD = LANES
C_WIDTH_PAD = C_HEADS * C_HEAD_PAD

OFF_AU, OFF_AV, OFF_BQ, OFF_BK, OFF_BV = 0, 256, 512, 896, 1280
OFF_CQ, OFF_CK, OFF_CV, OFF_CO, OFF_CG = 1664, 2048, 2432, 2816, 3200
P_A = 0
P_BQ = 512
P_BK = P_BQ + B_WIDTH
P_BV = P_BK + B_WIDTH
P_CQ = P_BV + B_WIDTH
P_CK = P_CQ + C_WIDTH_PAD
P_CV = P_CK + C_WIDTH_PAD
P_CO = P_CV + C_WIDTH_PAD
P_CG = P_CO + C_WIDTH_PAD

SB_DEAD_LOG_WEIGHT = -110.0
SB_BLOCK = 256
SB_BLOCKS_PER_STEP = 8
MLSTM_BLOCK = 128
MLSTM_SUB_BLOCKS = 8
GATE_ROWS = 2 * C_HEADS
NEG_BIG = -1e30
ROW_TILE = 8
MOE_TILE = 512
FFN_F_TILE = 1408
ROW_BLOCK = 512
MOE_DISPATCH_BLOCK = 1024
MOE_COMBINE_BLOCK = 512
CAST_SLAB_ROWS = 128

VMEM_LIMIT = 56 * 1024 * 1024


def _cparams(sem):
    return pltpu.CompilerParams(dimension_semantics=sem, vmem_limit_bytes=VMEM_LIMIT)


def _gelu(x):
    return 0.5 * x * (1.0 + lax.erf(x * (1.0 / math.sqrt(2.0))))


def _log_sigmoid(x):
    return jnp.minimum(x, 0.0) - jnp.log(1.0 + jnp.exp(-jnp.abs(x)))


def _sigmoid(x):
    return 1.0 / (1.0 + jnp.exp(-x))


def _split_bf16(x):
    hi = x.astype(BF16)
    lo = (x - hi.astype(F32)).astype(BF16)
    return hi, lo


def _store_token_tiles(ref, x):
    t = x.shape[0]
    for s in range(D_MODEL // LANES):
        ref[pl.ds(s, t, stride=ROW_TILE), :] = x[:, s * LANES:(s + 1) * LANES]


def _load_token_tiles(ref):
    t = ref.shape[0] // ROW_TILE
    return jnp.concatenate([ref[pl.ds(s, t, stride=ROW_TILE), :] for s in range(D_MODEL // LANES)], axis=1)


def _dot(a, b):
    return jnp.dot(a, b, preferred_element_type=F32)


def _dot_nt(a, b):
    return lax.dot_general(a, b, (((1,), (1,)), ((), ())), preferred_element_type=F32)


def _proj_kernel(n_chunks, chunk, sample, x_ref, gmix_ref, w_ref, bg_ref, lng_ref, lnb_ref, ws_ref, bs_ref,
                 *refs):
    n_out = 14 if sample else 11
    ya_ref, qb_ref, qc_ref, kc_ref, kct_ref, vc_ref, oc_ref, gt_ref, gtt_ref, *kv_refs = refs[len(refs) - n_out:]
    x = x_ref[...]
    xn = (x * lax.rsqrt(jnp.mean(x * x, axis=-1, keepdims=True) + EPS) * gmix_ref[...]).astype(BF16)

    def proj(off, width):
        return _dot(xn, w_ref[:, off:off + width])

    za = proj(P_A, 2 * A_WIDTH)
    u = _gelu(za[:, :A_WIDTH])
    gv = _gelu(za[:, A_WIDTH:])
    xc = gv - jnp.mean(gv, axis=-1, keepdims=True)
    va = xc * lax.rsqrt(jnp.mean(xc * xc, axis=-1, keepdims=True) + EPS) * lng_ref[...] + lnb_ref[...]
    if sample:
        kv_refs[4][...] = va
    vab = va.astype(BF16)
    lane_group = lax.broadcasted_iota(jnp.int32, (chunk, A_WIDTH), 1) // A_GROUP_DIM
    for c in range(n_chunks):
        rows = slice(c * chunk, (c + 1) * chunk)
        vch = vab[rows]
        s = jnp.zeros((chunk, A_WIDTH), F32)
        for g in range(A_GROUPS):
            s = jnp.where(lane_group == g, _dot(ws_ref[g], vch), s)
        ya_ref[rows, :] = (u[rows] * (s + bs_ref[...])).astype(BF16)

    qb_ref[...] = (proj(P_BQ, B_WIDTH) * (1.0 / math.sqrt(B_HEAD_DIM))).astype(BF16)
    zk = proj(P_BK, B_WIDTH)
    zv = proj(P_BV, B_WIDTH)
    if sample:
        kv_refs[0][...] = zk
        kv_refs[1][...] = zv
        kv_refs[2][...] = zk.astype(BF16)
        kv_refs[3][...] = zv.astype(BF16)
    else:
        kv_refs[0][0] = zk.T
        kv_refs[1][0] = zv.T
        for other in range(1, kv_refs[0].shape[0]):
            kv_refs[0][other] = jnp.zeros(kv_refs[0].shape[1:], F32)
            kv_refs[1][other] = jnp.zeros(kv_refs[1].shape[1:], F32)

    qc_ref[...] = proj(P_CQ, C_WIDTH_PAD).astype(BF16)
    zk = proj(P_CK, C_WIDTH_PAD) * (C_HEAD_DIM ** -0.5)
    kc_ref[...] = zk.astype(BF16)
    kct_ref[...] = zk.T.astype(BF16)
    vc_ref[...] = proj(P_CV, C_WIDTH_PAD).astype(BF16)
    oc_ref[...] = proj(P_CO, C_WIDTH_PAD)
    g = proj(P_CG, LANES) + bg_ref[...]
    lane = lax.broadcasted_iota(jnp.int32, g.shape, 1)
    gates = jnp.where(lane < C_HEADS, g, _log_sigmoid(g))
    gt_ref[...] = gates
    gtt_ref[...] = gates.T[:GATE_ROWS, :]


def _proj(x, gmix, wp, bg, lng, lnb, ws, bs, tm, chunk, sample, layer=0, kv_all=None):
    n = x.shape[0]
    row = lambda w: pl.BlockSpec((tm, w), lambda i: (i, 0))
    full = lambda a: pl.BlockSpec(a.shape, lambda i: (0,) * a.ndim)
    col = lambda h: pl.BlockSpec((h, tm), lambda i: (0, i))
    outs = [((n, A_WIDTH), BF16, row(A_WIDTH)), ((n, B_WIDTH), BF16, row(B_WIDTH)),
            ((n, C_WIDTH_PAD), BF16, row(C_WIDTH_PAD)), ((n, C_WIDTH_PAD), BF16, row(C_WIDTH_PAD)),
            ((C_WIDTH_PAD, n), BF16, col(C_WIDTH_PAD)), ((n, C_WIDTH_PAD), BF16, row(C_WIDTH_PAD)),
            ((n, C_WIDTH_PAD), F32, row(C_WIDTH_PAD)), ((n, LANES), F32, row(LANES)),
            ((GATE_ROWS, n), F32, col(GATE_ROWS))]
    out_shape = [jax.ShapeDtypeStruct(s, dt) for s, dt, _ in outs]
    out_specs = [spec for _, _, spec in outs]
    if sample:
        extra = [(B_WIDTH, F32), (B_WIDTH, F32), (B_WIDTH, BF16), (B_WIDTH, BF16), (A_WIDTH, F32)]
        out_shape += [jax.ShapeDtypeStruct((n, w), dt) for w, dt in extra]
        out_specs += [row(w) for w, _ in extra]
    ins = [x, gmix, wp, bg, lng, lnb, ws, bs]
    in_specs = [row(D_MODEL), full(gmix), full(wp), full(bg), full(lng), full(lnb), full(ws), full(bs)]
    aliases = {}
    if not sample:
        out_shape += [jax.ShapeDtypeStruct((DEPTH, B_WIDTH, n), F32)] * 2
        if kv_all is None:
            out_specs += [pl.BlockSpec((DEPTH, B_WIDTH, tm), lambda i: (0, 0, i))] * 2
        else:
            out_specs += [pl.BlockSpec((1, B_WIDTH, tm), lambda i: (layer, 0, i))] * 2
            aliases = {len(ins): len(out_shape) - 2, len(ins) + 1: len(out_shape) - 1}
            ins += list(kv_all)
            in_specs += [pl.BlockSpec(memory_space=pl.ANY)] * 2
    return pl.pallas_call(
        functools.partial(_proj_kernel, tm // chunk, chunk, sample),
        out_shape=out_shape,
        grid=(n // tm,),
        in_specs=in_specs,
        out_specs=out_specs,
        input_output_aliases=aliases,
        compiler_params=_cparams(("parallel",)),
        name="proj",
    )(*ins)


def _sb_weights(qh, kblk, carry, tri2, mask, transposed, shift=None):
    z = _dot(qh, kblk) if transposed else _dot_nt(qh, kblk)
    if shift is not None:
        z = z + shift
    tk = z.shape[1]
    drop = jnp.maximum(z, 0.0) + jnp.log(1.0 + jnp.exp(-jnp.abs(z)))
    if mask is not None:
        drop = jnp.where(mask, drop, 0.0)
    cs = _dot(jnp.concatenate(_split_bf16(drop), axis=1), tri2)
    if tk >= LANES:
        carry_b = jnp.concatenate([carry] * (tk // LANES), axis=1)
    else:
        carry_b = carry[:, :tk]
    a = jnp.exp(z - cs - carry_b)
    if mask is not None:
        a = jnp.where(mask, a, 0.0)
    return a.astype(BF16), carry + jnp.broadcast_to(cs[:, :1], carry.shape)


def _sb_step(qh, kblk, vblk, carry, acc, tri2, mask, transposed):
    a, carry = _sb_weights(qh, kblk, carry, tri2, mask, transposed)
    return carry, acc + (_dot_nt(a, vblk) if transposed else _dot(a, vblk))


def _tri2(tk):
    j = lax.broadcasted_iota(jnp.int32, (2 * tk, tk), 0)
    s = lax.broadcasted_iota(jnp.int32, (2 * tk, tk), 1)
    return (jnp.where(j >= tk, j - tk, j) >= s).astype(BF16)


def _sb_finish(acc_s, gn_ref, o_ref, head0):
    out = jnp.where(head0, acc_s[0], acc_s[1])
    r = lax.broadcasted_iota(jnp.int32, (LANES, LANES), 0) // B_HEAD_DIM
    c = lax.broadcasted_iota(jnp.int32, (LANES, LANES), 1) // B_HEAD_DIM
    same_head = (r == c).astype(BF16)
    hi, lo = _split_bf16(out * out)
    ms = (_dot(hi, same_head) + _dot(lo, same_head)) * (1.0 / B_HEAD_DIM)
    return (out * lax.rsqrt(ms + EPS) * gn_ref[...]).astype(o_ref.dtype)


def _sb_walk(qh, load_kv, first_block, carry_s, acc_s, tri):
    def alive():
        return jnp.minimum(jnp.min(carry_s[0]), jnp.min(carry_s[1])) <= -SB_DEAD_LOG_WEIGHT

    def cond(st):
        j, live = st
        return jnp.logical_and(j >= 0, live)

    def body(st):
        j, _ = st
        kblk, vblk = load_kv(j)
        for h in range(2):
            carry, acc = _sb_step(qh[h], kblk, vblk, carry_s[h], acc_s[h], tri, None, True)
            carry_s[h] = carry
            acc_s[h] = acc
        return j - 1, alive()

    lax.while_loop(cond, body, (first_block, alive()))


def _sb_prompt_kernel(q_ref, k_ref, v_ref, gn_ref, o_ref, carry_s, acc_s):
    tq = SB_BLOCK
    n_q = q_ref.shape[0] // tq
    head0 = lax.broadcasted_iota(jnp.int32, (tq, LANES), 1) < B_HEAD_DIM
    tri = _tri2(tq)
    t = lax.broadcasted_iota(jnp.int32, (tq, tq), 0)
    s = lax.broadcasted_iota(jnp.int32, (tq, tq), 1)
    causal = s < t
    zeros = jnp.zeros((tq, LANES), F32)

    def load_kv(j):
        cols = pl.ds(pl.multiple_of(j * tq, tq), tq)
        return k_ref[:, cols].astype(BF16), v_ref[:, cols].astype(BF16)

    first = pl.program_id(1) * n_q
    kv = [load_kv(jnp.maximum(first + b - 1, 0)) for b in range(n_q + 1)]
    qhs = []
    for b in range(n_q):
        q = q_ref[b * tq:(b + 1) * tq, :]
        qh = [jnp.where(head0, q, 0), jnp.where(head0, 0, q)]
        qhs.append(qh)
        no_prev = jnp.where(first + b >= 1, 0.0, NEG_BIG)
        v_both = jnp.concatenate([kv[b + 1][1], kv[b][1]], axis=1)
        for h in range(2):
            a_diag, carry = _sb_weights(qh[h], kv[b + 1][0], zeros, tri, causal, True)
            a_prev, carry = _sb_weights(qh[h], kv[b][0], carry, tri, None, True, no_prev)
            carry_s[b, h] = carry
            acc_s[b, h] = _dot_nt(jnp.concatenate([a_diag, a_prev], axis=1), v_both)
    for b in range(n_q):
        _sb_walk(qhs[b], load_kv, first + b - 2, carry_s.at[b], acc_s.at[b], tri)
        o_ref[b * tq:(b + 1) * tq, :] = _sb_finish(acc_s.at[b], gn_ref, o_ref, head0)


def _sb_prompt(q, k, v, gn, layer):
    n = q.shape[0]
    n_q = SB_BLOCKS_PER_STEP
    tq = SB_BLOCK * n_q
    blk = pl.BlockSpec((tq, LANES), lambda p, i: (i, p))
    seq = pl.BlockSpec((None, LANES, n), lambda p, i: (layer, p, 0))
    state = pltpu.VMEM((n_q, 2, SB_BLOCK, LANES), F32)
    return pl.pallas_call(
        _sb_prompt_kernel,
        out_shape=jax.ShapeDtypeStruct((n, B_WIDTH), BF16),
        grid=(B_PAIRS, n // tq),
        in_specs=[blk, seq, seq, pl.BlockSpec((1, LANES), lambda p, i: (0, p))],
        out_specs=blk,
        scratch_shapes=[state, state],
        compiler_params=_cparams(("parallel", "parallel")),
        name="sb_prompt",
    )(q, k, v, gn)


def _sb_sample_kernel(q_ref, kn_ref, vn_ref, kc_ref, vc_ref, gn_ref, o_ref, carry_s, acc_s):
    tq = q_ref.shape[1]
    tk = SB_BLOCK
    head0 = lax.broadcasted_iota(jnp.int32, (tq, LANES), 1) < B_HEAD_DIM
    t = lax.broadcasted_iota(jnp.int32, (tq, tq), 0)
    s = lax.broadcasted_iota(jnp.int32, (tq, tq), 1)
    causal = s < t
    zeros = jnp.zeros((tq, LANES), F32)
    tri = _tri2(tk)
    tri_new = _tri2(tq)
    last = kc_ref.shape[2] // tk - 1

    def pair_loader(p):
        def load_kv(j):
            cols = pl.ds(pl.multiple_of(j * tk, tk), tk)
            dims = slice(p * LANES, (p + 1) * LANES)
            return kc_ref[0, dims, cols].astype(BF16), vc_ref[0, dims, cols].astype(BF16)
        return load_kv

    qhs = []
    for p in range(B_PAIRS):
        lanes = slice(p * LANES, (p + 1) * LANES)
        q = q_ref[0, :, lanes]
        qh = [jnp.where(head0, q, 0), jnp.where(head0, 0, q)]
        qhs.append(qh)
        kp, vp = pair_loader(p)(last)
        for h in range(2):
            carry, acc = _sb_step(qh[h], kn_ref[0, :, lanes], vn_ref[0, :, lanes], zeros, zeros, tri_new,
                                  causal, False)
            carry, acc = _sb_step(qh[h], kp, vp, carry, acc, tri, None, True)
            carry_s[p, h] = carry
            acc_s[p, h] = acc
    for p in range(B_PAIRS):
        lanes = slice(p * LANES, (p + 1) * LANES)
        _sb_walk(qhs[p], pair_loader(p), last - 1, carry_s.at[p], acc_s.at[p], tri)
        o_ref[0, :, lanes] = _sb_finish(acc_s.at[p], gn_ref.at[:, lanes], o_ref, head0)


def _sb_sample(q, kn, vn, kc, vc, gn, layer):
    nb, tq, _ = q.shape
    past = kc.shape[3]
    new = pl.BlockSpec((1, tq, B_WIDTH), lambda b: (b, 0, 0))
    old = pl.BlockSpec((None, 1, B_WIDTH, past), lambda b: (layer, b, 0, 0))
    state = pltpu.VMEM((B_PAIRS, 2, tq, LANES), F32)
    return pl.pallas_call(
        _sb_sample_kernel,
        out_shape=jax.ShapeDtypeStruct((nb, tq, B_WIDTH), BF16),
        grid=(nb,),
        in_specs=[new, new, new, old, old, pl.BlockSpec((1, B_WIDTH), lambda b: (0, 0))],
        out_specs=new,
        scratch_shapes=[state, state],
        compiler_params=_cparams(("parallel",)),
        name="sb_sample",
    )(q, kn, vn, kc, vc, gn)


def _split3(x):
    h1 = x.astype(BF16)
    r1 = x - h1.astype(F32)
    h2 = r1.astype(BF16)
    return h1, h2, (r1 - h2.astype(F32)).astype(BF16)


def _dot3(x, rhs01):
    return _dot(jnp.concatenate(_split3(x), axis=1), jnp.concatenate([rhs01] * 3, axis=0))


def _dot3_left(lhs01, x):
    return _dot(jnp.concatenate([lhs01] * 3, axis=1), jnp.concatenate(_split3(x), axis=0))


def _mlstm_kernel(n_sub, q_ref, k_ref, kt_ref, v_ref, o_ref, gt_ref, gtt_ref, gn_ref, c0_ref, n0_ref, m0_ref,
                  yc_ref, c_out, n_out, m_out, c_s, n_s, m_s):
    L = MLSTM_BLOCK
    t_blk = pl.program_id(1)

    @pl.when(t_blk == 0)
    def _():
        c_s[...] = c0_ref[0]
        n_s[...] = n0_ref[0]
        m_s[...] = m0_ref[0]

    r = lax.broadcasted_iota(jnp.int32, (L, L), 0)
    c = lax.broadcasted_iota(jnp.int32, (L, L), 1)
    causal = c <= r
    upper = (r <= c).astype(BF16)
    ones_sq = jnp.ones((L, LANES), BF16)
    sel_r = lax.broadcasted_iota(jnp.int32, (LANES, C_WIDTH_PAD), 0)
    sel_c = lax.broadcasted_iota(jnp.int32, (LANES, C_WIDTH_PAD), 1) // C_HEAD_PAD
    sel_p = (sel_r == sel_c).astype(BF16)
    sel_b = (sel_r == sel_c + C_HEADS).astype(BF16)
    lane = c

    a_rows_all, pb_cols_all = [], []
    for sub in range(n_sub):
        rows = slice(sub * L, (sub + 1) * L)
        gtt = gtt_ref[0, :, rows]
        bct = _dot3(gtt, upper)
        a_rows_all.append(gtt[:C_HEADS] - bct[C_HEADS:])
        gt = gt_ref[0, rows, :]
        bc = _dot3_left(causal.astype(BF16), gt)
        pmax = gt - pltpu.roll(bc, LANES - C_HEADS, axis=1)
        for sh in (1, 2, 4, 8, 16, 32, 64):
            pmax = jnp.maximum(pmax, jnp.where(r >= sh, pltpu.roll(pmax, sh, axis=0), -jnp.inf))
        pb_cols_all.append(jnp.where(lane < C_HEADS, pmax, bc))
    pb_cols_all = jnp.concatenate(pb_cols_all, axis=0)
    p_all = _dot3(pb_cols_all, sel_p)
    b_all = _dot3(pb_cols_all, sel_b)

    state = [(c_s[h], n_s[h], m_s[h]) for h in range(C_HEADS)]
    for sub in range(n_sub):
        rows = slice(sub * L, (sub + 1) * L)
        a_rows = a_rows_all[sub]
        heads = []
        for h in range(C_HEADS):
            lanes = slice(h * C_HEAD_PAD, (h + 1) * C_HEAD_PAD)
            q = q_ref[0, rows, lanes]
            v1 = jnp.concatenate([v_ref[0, rows, lanes], ones_sq], axis=1)
            p_rep = p_all[rows, lanes]
            b_rep = b_all[rows, lanes]
            a_row = a_rows[h:h + 1, :]
            p_last = p_rep[L - 1:L, :]
            c_prev, n_prev, m_prev = state[h]

            w = jnp.exp(jnp.where(causal, a_row - p_rep, -jnp.inf)) * _dot_nt(q, k_ref[0, rows, lanes])
            kwt = (kt_ref[0, lanes, rows].astype(F32) * jnp.exp(a_row - p_last)).astype(BF16)
            both = _dot(jnp.concatenate([w.astype(BF16), kwt], axis=0), v1)
            here_sums = both[:L]
            fresh = both[L:]
            past_sums = _dot(q, jnp.concatenate([c_prev, n_prev], axis=1).astype(BF16))
            top = jnp.maximum(m_prev, p_rep)
            past = jnp.exp(m_prev - top)
            here = jnp.exp(p_rep - top)
            mix = jnp.concatenate([past, past], axis=1) * past_sums + jnp.concatenate([here, here], axis=1) * here_sums
            den = jnp.maximum(jnp.abs(mix[:, LANES:]), jnp.exp(-(b_rep + top)))
            heads.append(mix[:, :LANES] / den)

            top_last = jnp.maximum(m_prev, p_last)
            decay = jnp.exp(m_prev - top_last)
            gain = jnp.exp(p_last - top_last)
            state[h] = (decay * c_prev + gain * fresh[:, :LANES], decay * n_prev + gain * fresh[:, LANES:],
                        b_rep[L - 1:L, :] + top_last)

        hh = jnp.concatenate(heads, axis=0)
        ms = _dot(jnp.concatenate(_split_bf16(hh * hh), axis=1),
                  jnp.concatenate([ones_sq, ones_sq], axis=0)) * (1.0 / C_HEAD_DIM)
        hn = hh * lax.rsqrt(ms + EPS)
        for h in range(C_HEADS):
            lanes = slice(h * C_HEAD_PAD, (h + 1) * C_HEAD_PAD)
            yc_ref[0, rows, lanes] = (hn[h * L:(h + 1) * L] * gn_ref[:, lanes]
                                      * _sigmoid(o_ref[0, rows, lanes])).astype(BF16)

    for h in range(C_HEADS):
        c_s[h], n_s[h], m_s[h] = state[h]

    @pl.when(t_blk == pl.num_programs(1) - 1)
    def _():
        c_out[0] = c_s[...]
        n_out[0] = n_s[...]
        m_out[0] = m_s[...]


def _mlstm(q, k, kt, v, o, gt, gtt, gn, c0, n0, m0, n_sub):
    nb, n, _ = q.shape
    tb = n_sub * MLSTM_BLOCK
    seq = lambda w: pl.BlockSpec((1, tb, w), lambda b, t: (b, t, 0))
    seq_t = lambda h: pl.BlockSpec((1, h, tb), lambda b, t: (b, 0, t))
    st = lambda a: pl.BlockSpec((1,) + a.shape[1:], lambda b, t: (b,) + (0,) * (a.ndim - 1))
    return pl.pallas_call(
        functools.partial(_mlstm_kernel, n_sub),
        out_shape=[jax.ShapeDtypeStruct((nb, n, C_WIDTH_PAD), BF16),
                   jax.ShapeDtypeStruct(c0.shape, F32),
                   jax.ShapeDtypeStruct(n0.shape, F32),
                   jax.ShapeDtypeStruct(m0.shape, F32)],
        grid=(nb, n // tb),
        in_specs=[seq(C_WIDTH_PAD), seq(C_WIDTH_PAD), seq_t(C_WIDTH_PAD), seq(C_WIDTH_PAD), seq(C_WIDTH_PAD),
                  seq(LANES), seq_t(GATE_ROWS), pl.BlockSpec((1, C_WIDTH_PAD), lambda b, t: (0, 0)),
                  st(c0), st(n0), st(m0)],
        out_specs=[seq(C_WIDTH_PAD), st(c0), st(n0), st(m0)],
        scratch_shapes=[pltpu.VMEM(c0.shape[1:], F32), pltpu.VMEM(n0.shape[1:], F32),
                        pltpu.VMEM(m0.shape[1:], F32)],
        compiler_params=_cparams(("parallel", "arbitrary")),
        name="mlstm",
    )(q, k, kt, v, o, gt, gtt, gn, c0, n0, m0)


def _mixer_out(x_ref, ya_ref, yb_ref, yc_ref, wa_ref, wb_ref, wc_ref, g_ref, xmid_ref):
    y = _dot(ya_ref[...], wa_ref[...]) + _dot(yb_ref[...], wb_ref[...]) + _dot(yc_ref[...], wc_ref[...])
    x = x_ref[...] + y
    xmid_ref[...] = x
    return x * lax.rsqrt(jnp.mean(x * x, axis=-1, keepdims=True) + EPS) * g_ref[...]


def _merge_dense_kernel(n_cast, x_ref, ya_ref, yb_ref, yc_ref, wa_ref, wb_ref, wc_ref, g_ref, *refs):
    casts_in, xmid_ref, xn_ref, casts_out = refs[:n_cast], refs[n_cast], refs[n_cast + 1], refs[n_cast + 2:]
    xn_ref[...] = _mixer_out(x_ref, ya_ref, yb_ref, yc_ref, wa_ref, wb_ref, wc_ref, g_ref,
                             xmid_ref).astype(BF16)
    for src, dst in zip(casts_in, casts_out):
        dst[...] = src[...].astype(BF16)


def _merge_moe_kernel(per_step, x_ref, ya_ref, yb_ref, yc_ref, wa_ref, wb_ref, wc_ref, g_ref, wrt_ref,
                      earlier_ref, xmid_ref, xrow_ref, ri_ref, rf_ref, cnt_ref, *rest):
    tm = x_ref.shape[0]
    step = pl.program_id(0)
    if len(rest) == 1:
        (run_s,) = rest
    else:
        xs_ref, run_s, zero_s, sem = rest
        tile_rows = zero_s.shape[0]
        n_tiles = xs_ref.shape[0] // tile_rows

        def zero_copy(tile):
            return pltpu.make_async_copy(zero_s, xs_ref.at[pl.ds(tile * tile_rows, tile_rows)], sem)

        @pl.when(step == 0)
        def _():
            zero_s[...] = jnp.zeros_like(zero_s)

        for j in range(per_step):
            @pl.when(step * per_step + j < n_tiles)
            def _():
                zero_copy(step * per_step + j).start()

    @pl.when(step == 0)
    def _():
        run_s[...] = jnp.zeros_like(run_s)

    xn = _mixer_out(x_ref, ya_ref, yb_ref, yc_ref, wa_ref, wb_ref, wc_ref, g_ref, xmid_ref)
    _store_token_tiles(xrow_ref, xn)

    hi, lo = _split_bf16(xn)
    part = _dot_nt(wrt_ref[...], hi)
    lg = part[:N_EXPERTS] + part[N_EXPERTS:] + _dot_nt(wrt_ref[:N_EXPERTS, :], lo)
    expert = lax.broadcasted_iota(jnp.int32, lg.shape, 0)
    m1 = jnp.max(lg, axis=0, keepdims=True)
    i1 = jnp.min(jnp.where(lg == m1, expert, N_EXPERTS), axis=0, keepdims=True)
    lg2 = jnp.where(expert == i1, -jnp.inf, lg)
    m2 = jnp.max(lg2, axis=0, keepdims=True)
    i2 = jnp.min(jnp.where(lg2 == m2, expert, N_EXPERTS), axis=0, keepdims=True)
    e2 = jnp.exp(m2 - m1)
    g1 = 1.0 / (1.0 + e2)
    g2 = e2 * g1

    sel1 = expert == i1
    sel2 = expert == i2
    onehot = jnp.logical_or(sel1, sel2).astype(BF16)
    before = _dot(onehot, earlier_ref[...]) + jnp.concatenate([run_s[...]] * (tm // LANES), axis=1)
    rank1 = jnp.sum(jnp.where(sel1, before, 0.0), axis=0, keepdims=True).astype(jnp.int32)
    rank2 = jnp.sum(jnp.where(sel2, before, 0.0), axis=0, keepdims=True).astype(jnp.int32)
    run_s[...] += _dot(onehot, jnp.ones((tm, LANES), BF16))
    cnt_ref[...] = run_s[...].astype(jnp.int32)
    ri_ref[...] = jnp.where(expert == 0, i1, jnp.where(expert == 1, i2,
                            jnp.where(expert == 2, rank1, jnp.where(expert == 3, rank2, 0))))
    rf = jnp.where(expert == 0, g1, jnp.where(expert == 1, g2, 0.0))
    rf_ref[...] = jnp.concatenate([rf, jnp.zeros((LANES - N_EXPERTS, tm), F32)], axis=0).T

    if len(rest) > 1:
        for j in range(per_step):
            @pl.when(step * per_step + j < n_tiles)
            def _():
                zero_copy(step * per_step + j).wait()


def _merge(x, ya, yb, yc, wa, wb, wc, g, wr, tm, cast=(), zero_rows=0):
    n = x.shape[0]
    assert zero_rows % MOE_TILE == 0
    assert all(a.shape[0] // CAST_SLAB_ROWS <= n // tm for a in cast)
    row = lambda w: pl.BlockSpec((tm, w), lambda i: (i, 0))
    full = lambda a: pl.BlockSpec(a.shape, lambda i: (0,) * a.ndim)
    ins = [x, ya, yb, yc, wa, wb, wc, g]
    in_specs = [row(D_MODEL), row(A_WIDTH), row(B_WIDTH), row(C_WIDTH_PAD), full(wa), full(wb), full(wc), full(g)]
    if wr is None:
        slab = lambda a: pl.BlockSpec((CAST_SLAB_ROWS, a.shape[1]),
                                      lambda i: (jnp.minimum(i, a.shape[0] // CAST_SLAB_ROWS - 1), 0))
        outs = pl.pallas_call(
            functools.partial(_merge_dense_kernel, len(cast)),
            out_shape=[jax.ShapeDtypeStruct((n, D_MODEL), F32), jax.ShapeDtypeStruct((n, D_MODEL), BF16)]
            + [jax.ShapeDtypeStruct(a.shape, BF16) for a in cast],
            grid=(n // tm,), in_specs=in_specs + [slab(a) for a in cast],
            out_specs=[row(D_MODEL), row(D_MODEL)] + [slab(a) for a in cast],
            compiler_params=_cparams(("arbitrary",)), name="merge_dense",
        )(*ins, *cast)
        return outs[0], outs[1], outs[2:]
    earlier = (jnp.arange(tm)[:, None] < jnp.arange(tm)[None, :]).astype(BF16)
    col = pl.BlockSpec((N_EXPERTS, tm), lambda i: (0, i))
    out_shape = [jax.ShapeDtypeStruct((n, D_MODEL), F32),
                 jax.ShapeDtypeStruct((n * ROW_TILE, LANES), F32),
                 jax.ShapeDtypeStruct((N_EXPERTS, n), jnp.int32),
                 jax.ShapeDtypeStruct((n, LANES), F32),
                 jax.ShapeDtypeStruct((N_EXPERTS, LANES), jnp.int32)]
    out_specs = [row(D_MODEL), pl.BlockSpec((tm * ROW_TILE, LANES), lambda i: (i, 0)),
                 col, row(LANES), pl.BlockSpec((N_EXPERTS, LANES), lambda i: (0, 0))]
    scratch = [pltpu.VMEM((N_EXPERTS, LANES), F32)]
    per_step = 0
    if zero_rows:
        out_shape.append(jax.ShapeDtypeStruct((zero_rows, ROW_TILE, LANES), F32))
        out_specs.append(pl.BlockSpec(memory_space=pl.ANY))
        scratch += [pltpu.VMEM((MOE_TILE, ROW_TILE, LANES), F32), pltpu.SemaphoreType.DMA(())]
        per_step = -(-(zero_rows // MOE_TILE) // (n // tm))
    return pl.pallas_call(
        functools.partial(_merge_moe_kernel, per_step),
        out_shape=out_shape,
        grid=(n // tm,), in_specs=in_specs + [full(wr), full(earlier)],
        out_specs=out_specs,
        scratch_shapes=scratch,
        compiler_params=_cparams(("arbitrary",)), name="merge_moe",
    )(*ins, wr, earlier)


def _dest_blocks(dest, tt):
    return jnp.transpose(dest.reshape(2, dest.shape[1] // tt, tt), (1, 0, 2))


def _dispatch_kernel(dest_ref, src_ref, xs_in_ref, xs_ref, sem):
    del xs_in_ref
    tt = dest_ref.shape[2]

    def issue(t, carry):
        rows = pl.ds(pl.multiple_of(t * ROW_TILE, ROW_TILE), ROW_TILE)
        for k in range(2):
            pltpu.make_async_copy(src_ref.at[rows], xs_ref.at[dest_ref[0, k, t]], sem).start(priority=k)
        return carry

    lax.fori_loop(0, tt, issue, 0, unroll=8)

    def drain(t, carry):
        for k in range(2):
            pltpu.make_async_copy(src_ref.at[pl.ds(0, ROW_TILE)], xs_ref.at[0], sem).wait()
        return carry

    lax.fori_loop(0, tt, drain, 0, unroll=8)


def _dispatch(dest, src, xs, tt):
    n = src.shape[0] // ROW_TILE
    dest3 = _dest_blocks(dest, tt)
    return pl.pallas_call(
        _dispatch_kernel,
        out_shape=jax.ShapeDtypeStruct(xs.shape, xs.dtype),
        grid=(n // tt,),
        in_specs=[pl.BlockSpec((1, 2, tt), lambda i: (i, 0, 0), memory_space=pltpu.SMEM),
                  pl.BlockSpec((tt * ROW_TILE, LANES), lambda i: (i, 0)), pl.BlockSpec(memory_space=pl.ANY)],
        out_specs=pl.BlockSpec(memory_space=pl.ANY),
        scratch_shapes=[pltpu.SemaphoreType.DMA(())],
        input_output_aliases={2: 0},
        compiler_params=_cparams(("arbitrary",)), name="moe_dispatch",
    )(dest3, src, xs)


def _combine_kernel(final, dest_ref, next_ref, ys_ref, rf_ref, xmid_ref, gf_ref, o_ref, buf_s, sems):
    tt = xmid_ref.shape[0]
    i = pl.program_id(0)
    n_steps = pl.num_programs(0)

    def gather(idx_ref, slot):
        def issue(t, carry):
            rows = pl.ds(pl.multiple_of(t * ROW_TILE, ROW_TILE), ROW_TILE)
            for k in range(2):
                pltpu.make_async_copy(ys_ref.at[idx_ref[0, k, t]], buf_s.at[slot, k, rows],
                                      sems.at[slot]).start(priority=k)
            return carry

        lax.fori_loop(0, tt, issue, 0, unroll=8)

    slot = i % 2

    @pl.when(i == 0)
    def _():
        gather(dest_ref, 0)

    @pl.when(i + 1 < n_steps)
    def _():
        gather(next_ref, 1 - slot)

    def drain(t, carry):
        for k in range(2):
            pltpu.make_async_copy(ys_ref.at[0], buf_s.at[slot, 0, pl.ds(0, ROW_TILE)], sems.at[slot]).wait()
        return carry

    lax.fori_loop(0, tt, drain, 0, unroll=8)

    y = (xmid_ref[...] + rf_ref[:, 0:1] * _load_token_tiles(buf_s.at[slot, 0])
         + rf_ref[:, 1:2] * _load_token_tiles(buf_s.at[slot, 1]))
    if final:
        y = y * lax.rsqrt(jnp.mean(y * y, axis=-1, keepdims=True) + EPS) * gf_ref[...]
    o_ref[...] = y


def _combine(dest, ys, rf, xmid, gf, tt, final):
    n = xmid.shape[0]
    n_steps = n // tt
    dest3 = _dest_blocks(dest, tt)
    row = lambda w: pl.BlockSpec((tt, w), lambda i: (i, 0))
    return pl.pallas_call(
        functools.partial(_combine_kernel, final),
        out_shape=jax.ShapeDtypeStruct((n, D_MODEL), F32),
        grid=(n_steps,),
        in_specs=[pl.BlockSpec((1, 2, tt), lambda i: (i, 0, 0), memory_space=pltpu.SMEM),
                  pl.BlockSpec((1, 2, tt), lambda i: (jnp.minimum(i + 1, n_steps - 1), 0, 0),
                               memory_space=pltpu.SMEM),
                  pl.BlockSpec(memory_space=pl.ANY), row(LANES), row(D_MODEL),
                  pl.BlockSpec((1, D_MODEL), lambda i: (0, 0))],
        out_specs=row(D_MODEL),
        scratch_shapes=[pltpu.VMEM((2, 2, tt * ROW_TILE, LANES), F32), pltpu.SemaphoreType.DMA((2,))],
        compiler_params=_cparams(("arbitrary",)), name="moe_combine",
    )(dest3, dest3, ys, rf, xmid, gf)


def _swiglu_acc(xn, wg_ref, wu_ref, wd_ref, acc_s):
    g = _dot(xn, wg_ref[0])
    u = _dot(xn, wu_ref[0])
    acc_s[...] += _dot((g * _sigmoid(g) * u).astype(BF16), wd_ref[0])


def _ffn_dense_kernel(final, n_cast, xn_ref, wg_ref, wu_ref, wd_ref, xmid_ref, gf_ref, *refs):
    casts_in, o_ref, casts_out, acc_s = refs[:n_cast], refs[n_cast], refs[n_cast + 1:-1], refs[-1]
    f = pl.program_id(1)

    @pl.when(f == 0)
    def _():
        acc_s[...] = jnp.zeros_like(acc_s)

    _swiglu_acc(xn_ref[...], wg_ref, wu_ref, wd_ref, acc_s)
    for src, dst in zip(casts_in, casts_out):
        dst[...] = src[...].astype(BF16)

    @pl.when(f == pl.num_programs(1) - 1)
    def _():
        y = xmid_ref[...] + acc_s[...]
        if final:
            y = y * lax.rsqrt(jnp.mean(y * y, axis=-1, keepdims=True) + EPS) * gf_ref[...]
        o_ref[...] = y


def _ffn_dense(xn, wg, wu, wd, xmid, gf, tm, final, cast=()):
    n = xn.shape[0]
    tf = FFN_F_TILE
    n_f = D_FF // tf
    steps = (n // tm) * n_f
    row = lambda w: pl.BlockSpec((tm, w), lambda i, f: (i, 0))
    slabs = [a.reshape(steps, a.size // (steps * a.shape[-1]), a.shape[-1]) for a in cast]
    slab_spec = lambda a: pl.BlockSpec((1,) + a.shape[1:], lambda i, f: (i * n_f + f, 0, 0))
    fcol = lambda i, f: jnp.where(i % 2 == 0, f, n_f - 1 - f)
    outs = pl.pallas_call(
        functools.partial(_ffn_dense_kernel, final, len(cast)),
        out_shape=[jax.ShapeDtypeStruct((n, D_MODEL), F32)] + [jax.ShapeDtypeStruct(a.shape, BF16) for a in slabs],
        grid=(n // tm, n_f),
        in_specs=[row(D_MODEL),
                  pl.BlockSpec((1, D_MODEL, tf), lambda i, f: (0, 0, fcol(i, f))),
                  pl.BlockSpec((1, D_MODEL, tf), lambda i, f: (0, 0, fcol(i, f))),
                  pl.BlockSpec((1, tf, D_MODEL), lambda i, f: (0, fcol(i, f), 0)),
                  row(D_MODEL), pl.BlockSpec((1, D_MODEL), lambda i, f: (0, 0))] + [slab_spec(a) for a in slabs],
        out_specs=[row(D_MODEL)] + [slab_spec(a) for a in slabs],
        scratch_shapes=[pltpu.VMEM((tm, D_MODEL), F32)],
        compiler_params=_cparams(("parallel", "arbitrary")),
        name="ffn_dense",
    )(xn, wg, wu, wd, xmid, gf, *slabs)
    return outs[0], [o.reshape(a.shape) for o, a in zip(outs[1:], cast)]


def _ffn_routed_kernel(te_ref, nu_ref, xs_ref, wg_ref, wu_ref, wd_ref, ys_ref, xb_s, acc_s):
    del te_ref
    i = pl.program_id(0)
    f = pl.program_id(1)

    @pl.when(i < nu_ref[0])
    def _():
        @pl.when(f == 0)
        def _():
            acc_s[...] = jnp.zeros_like(acc_s)
            xb_s[...] = _load_token_tiles(xs_ref).astype(BF16)

        _swiglu_acc(xb_s[...], wg_ref, wu_ref, wd_ref, acc_s)

        @pl.when(f == pl.num_programs(1) - 1)
        def _():
            _store_token_tiles(ys_ref, acc_s[...])

    @pl.when(jnp.logical_and(i >= nu_ref[0], f == 0))
    def _():
        ys_ref[...] = jnp.zeros_like(ys_ref)


def _ffn_routed(tile_expert, n_used, xs, wg, wu, wd):
    n_tiles = tile_expert.shape[0]
    tm, tf = MOE_TILE, FFN_F_TILE
    n_f = D_FF // tf
    last = lambda i, nu: jnp.minimum(i, nu[0] - 1)
    fcol = lambda i, f, nu: jnp.where(i < nu[0], jnp.where(i % 2 == 0, f, n_f - 1 - f),
                                      jnp.where((nu[0] - 1) % 2 == 0, n_f - 1, 0))
    return pl.pallas_call(
        _ffn_routed_kernel,
        out_shape=jax.ShapeDtypeStruct(xs.shape, F32),
        grid_spec=pltpu.PrefetchScalarGridSpec(
            num_scalar_prefetch=2,
            grid=(n_tiles, n_f),
            in_specs=[pl.BlockSpec((tm * ROW_TILE, LANES), lambda i, f, te, nu: (last(i, nu), 0)),
                      pl.BlockSpec((1, D_MODEL, tf), lambda i, f, te, nu: (te[last(i, nu)], 0, fcol(i, f, nu))),
                      pl.BlockSpec((1, D_MODEL, tf), lambda i, f, te, nu: (te[last(i, nu)], 0, fcol(i, f, nu))),
                      pl.BlockSpec((1, tf, D_MODEL), lambda i, f, te, nu: (te[last(i, nu)], fcol(i, f, nu), 0))],
            out_specs=pl.BlockSpec((tm * ROW_TILE, LANES), lambda i, f, te, nu: (i, 0)),
            scratch_shapes=[pltpu.VMEM((tm, D_MODEL), BF16), pltpu.VMEM((tm, D_MODEL), F32)]),
        compiler_params=_cparams(("arbitrary", "arbitrary")),
        name="ffn_routed",
    )(tile_expert, n_used, xs, wg, wu, wd)


def _route_plan(ri_p, cnt_p, ri_s, cnt_s, n_tiles):
    cnt_p, cnt_s = cnt_p[:, 0], cnt_s[:, 0]
    tiles = (cnt_p + cnt_s + MOE_TILE - 1) // MOE_TILE
    ends = jnp.cumsum(tiles)
    start = (ends - tiles) * MOE_TILE
    lookup = lambda table, idx: jnp.sum(
        jnp.where(idx[..., None] == jnp.arange(N_EXPERTS, dtype=jnp.int32), table, 0), axis=-1)
    dest_p = lookup(start, ri_p[0:2]) + ri_p[2:4]
    dest_s = lookup(start + cnt_p, ri_s[0:2]) + ri_s[2:4]
    tile_expert = jnp.minimum(jnp.sum(jnp.arange(n_tiles, dtype=jnp.int32)[:, None] >= ends[None, :], axis=-1),
                              N_EXPERTS - 1).astype(jnp.int32)
    return dest_p, dest_s, tile_expert, ends[-1:].astype(jnp.int32)


def _pad_heads_cols(w):
    w = w.reshape(w.shape[0], C_HEADS, C_HEAD_DIM)
    return jnp.pad(w, ((0, 0), (0, 0), (0, C_HEAD_PAD - C_HEAD_DIM))).reshape(w.shape[0], C_WIDTH_PAD)


def _layer_params(l, w_in, b_gate, w_s, b_s, gn_c, w_out):
    w = w_in[l]
    gates = jnp.pad(w[:, OFF_CG:OFF_CG + 2 * C_HEADS], ((0, 0), (0, LANES - 2 * C_HEADS)))
    c_part = w[:, OFF_CQ:OFF_CG].reshape(D_MODEL, 4 * C_HEADS, C_HEAD_DIM)
    c_part = jnp.pad(c_part, ((0, 0), (0, 0), (0, C_HEAD_PAD - C_HEAD_DIM))).reshape(D_MODEL, 4 * C_WIDTH_PAD)
    wp = jnp.concatenate([w[:, :OFF_CQ], c_part, gates], axis=1).astype(BF16)
    bg = jnp.pad(b_gate[l], (0, LANES - 2 * C_HEADS))[None, :]
    pos = jnp.arange(GMLP_CHUNK)
    mask = (pos[None, :] // CHUNK) <= (pos[:, None] // CHUNK)
    wm = jnp.where(mask[None], w_s[l], 0.0)
    bs = jnp.repeat(b_s[l].T, A_GROUP_DIM, axis=1)
    wo = w_out[l]
    wc = wo[A_WIDTH + B_WIDTH:].reshape(C_HEADS, C_HEAD_DIM, D_MODEL)
    wc = jnp.pad(wc, ((0, 0), (0, C_HEAD_PAD - C_HEAD_DIM), (0, 0))).reshape(C_WIDTH_PAD, D_MODEL)
    gnc = _pad_heads_cols(gn_c[l][None, :])
    return dict(wp=wp, bg=bg, wm=wm, bs=bs, wa=wo[:A_WIDTH].astype(BF16),
                wb=wo[A_WIDTH:A_WIDTH + B_WIDTH].astype(BF16), wc=wc.astype(BF16), gnc=gnc)


def _pad_state(c, n, m):
    p = C_HEAD_PAD - C_HEAD_DIM
    c = jnp.pad(c, ((0, 0), (0, 0), (0, p), (0, p)))
    n = jnp.pad(n, ((0, 0), (0, 0), (0, p)))
    n = jnp.broadcast_to(n[:, :, :, None], n.shape + (LANES,))
    m = jnp.broadcast_to(m[:, :, None, None], m.shape + (1, LANES))
    return c, n, m


def _unpad_state(c, n, m):
    return c[:, :, :C_HEAD_DIM, :C_HEAD_DIM], n[:, :, :C_HEAD_DIM, 0], m[:, :, 0, 0]


def kernel(x_prompt, x_sample, cache_k_b, cache_v_b, state_c_mlstm, state_n_mlstm, state_m_mlstm,
           g_mix, w_in, b_gate, ln_a_g, ln_a_b, w_s, b_s, gn_b, gn_c, w_out,
           g_ffn, w_gate_d, w_up_d, w_down_d, w_router, w_gate_e, w_up_e, w_down_e, g_final):
    n_seq = x_prompt.shape[1]
    n_dec, n_new = x_sample.shape[0], x_sample.shape[1]
    past = cache_k_b.shape[2]
    n_samp = n_dec * n_new

    xp = x_prompt.reshape(n_seq, D_MODEL)
    xs = x_sample.reshape(n_samp, D_MODEL)
    gfin = g_final[None, :]
    keys_last = lambda a: jnp.transpose(a, (0, 1, 3, 4, 2)).reshape(DEPTH, n_dec, B_WIDTH, past)
    cache_kt, cache_vt = keys_last(cache_k_b), keys_last(cache_v_b)

    outs = {k: [] for k in ("kbp", "vbp", "cp", "np", "mp", "kbs", "vbs", "cs", "ns", "ms", "vas")}
    for l in range(DEPTH):
        p = _layer_params(l, w_in, b_gate, w_s, b_s, gn_c, w_out)
        gmix = g_mix[l][None, :]
        lng, lnb = ln_a_g[l][None, :], ln_a_b[l][None, :]
        gnb = gn_b[l][None, :]
        gffn = g_ffn[l][None, :]
        moe = l % 2 == 1
        j = l // 2
        if moe:
            wg, wu, wd = expert_bf16
            wr = jnp.concatenate(_split_bf16(w_router[j].T), axis=0)
        else:
            wr = None
        final = l == DEPTH - 1
        n_tiles = 2 * (n_seq + n_samp) // MOE_TILE + N_EXPERTS

        wm_s = jnp.kron(jnp.eye(n_dec, dtype=F32), p["wm"][:, :n_new, :n_new])
        bs_s = jnp.tile(p["bs"][:n_new], (n_dec, 1))

        (ya, qb, qc, kc, kct, vc, oc, gt, gtt, kbt, vbt) = _proj(
            xp, gmix, p["wp"], p["bg"], lng, lnb, p["wm"].astype(BF16), p["bs"], ROW_BLOCK, GMLP_CHUNK, False,
            l, None if l == 0 else (kbt, vbt))
        yb = _sb_prompt(qb, kbt, vbt, gnb, l)
        c0, n0, m0 = _pad_state(jnp.zeros((1, C_HEADS, C_HEAD_DIM, C_HEAD_DIM), F32),
                                jnp.zeros((1, C_HEADS, C_HEAD_DIM), F32), jnp.zeros((1, C_HEADS), F32))
        yc, c_f, n_f, m_f = _mlstm(qc[None], kc[None], kct[None], vc[None], oc[None], gt[None], gtt[None],
                                   p["gnc"], c0, n0, m0, MLSTM_SUB_BLOCKS)
        merged_p = _merge(xp, ya, yb, yc[0], p["wa"], p["wb"], p["wc"], gffn, wr, ROW_BLOCK,
                          () if moe else (w_gate_d[j], w_up_d[j], w_down_d[j]), n_tiles * MOE_TILE if moe else 0)
        c_f, n_f, m_f = _unpad_state(c_f, n_f, m_f)
        outs["cp"].append(c_f)
        outs["np"].append(n_f)
        outs["mp"].append(m_f)

        (ya, qb, qc, kc, kct, vc, oc, gt, gtt, kbf, vbf, kbh, vbh, va) = _proj(
            xs, gmix, p["wp"], p["bg"], lng, lnb, wm_s.astype(BF16), bs_s, n_samp, n_samp, True)
        r3 = lambda a: a.reshape(n_dec, n_new, a.shape[-1])
        yb = _sb_sample(r3(qb), r3(kbh), r3(vbh), cache_kt, cache_vt, gnb, l)
        n_pad = MLSTM_BLOCK - n_new
        padr = lambda a: jnp.pad(r3(a), ((0, 0), (0, n_pad), (0, 0)))
        per_stream = lambda a: jnp.transpose(a.reshape(a.shape[0], n_dec, n_new), (1, 0, 2))
        kct_s = jnp.pad(per_stream(kct), ((0, 0), (0, 0), (0, n_pad)))
        gate_row = jnp.arange(GATE_ROWS)[None, :, None]
        gtt_pad = jnp.broadcast_to(jnp.where(gate_row < C_HEADS, NEG_BIG, 0.0).astype(F32),
                                   (n_dec, GATE_ROWS, n_pad))
        gtt_s = jnp.concatenate([per_stream(gtt), gtt_pad], axis=2)
        gt_s = jnp.concatenate([r3(gt), jnp.broadcast_to(
            jnp.where(jnp.arange(LANES) < C_HEADS, NEG_BIG, 0.0).astype(F32), (n_dec, n_pad, LANES))], axis=1)
        c0, n0, m0 = _pad_state(state_c_mlstm[l], state_n_mlstm[l], state_m_mlstm[l])
        yc, c_u, n_u, m_u = _mlstm(padr(qc), padr(kc), kct_s, padr(vc), padr(oc), gt_s, gtt_s, p["gnc"],
                                   c0, n0, m0, 1)
        yc = yc[:, :n_new].reshape(n_samp, C_WIDTH_PAD)
        merged_s = _merge(xs, ya, yb.reshape(n_samp, B_WIDTH), yc, p["wa"], p["wb"], p["wc"],
                          gffn, wr, n_samp)
        c_u, n_u, m_u = _unpad_state(c_u, n_u, m_u)

        if moe:
            xmid_p, xrow_p, ri_p, rf_p, cnt_p, xsort = merged_p
            xmid_s, xrow_s, ri_s, rf_s, cnt_s = merged_s
            dest_p, dest_s, tile_expert, n_used = _route_plan(ri_p, cnt_p, ri_s, cnt_s, n_tiles)
            as_tiles = lambda a: a.reshape(a.shape[0] // ROW_TILE, ROW_TILE, LANES)
            xsort = _dispatch(dest_p, xrow_p, xsort, MOE_DISPATCH_BLOCK)
            xsort = _dispatch(dest_s, xrow_s, xsort, n_samp)
            ysort = _ffn_routed(tile_expert, n_used, xsort.reshape(-1, LANES), wg, wu, wd)
            xp = _combine(dest_p, as_tiles(ysort), rf_p, xmid_p, gfin, MOE_COMBINE_BLOCK, final)
            xs = _combine(dest_s, as_tiles(ysort), rf_s, xmid_s, gfin, n_samp, final)
        else:
            nxt = (l + 1) // 2
            to_cast = (w_gate_e[nxt], w_up_e[nxt], w_down_e[nxt]) if l + 1 < DEPTH else ()
            wg, wu, wd = (w[None] for w in merged_p[2])
            xp, expert_bf16 = _ffn_dense(merged_p[1], wg, wu, wd, merged_p[0], gfin, ROW_BLOCK, final, to_cast)
            xs, _ = _ffn_dense(merged_s[1], wg, wu, wd, merged_s[0], gfin, n_samp, final)

        outs["kbs"].append(kbf.reshape(n_dec, n_new, B_HEADS, B_HEAD_DIM))
        outs["vbs"].append(vbf.reshape(n_dec, n_new, B_HEADS, B_HEAD_DIM))
        outs["cs"].append(c_u)
        outs["ns"].append(n_u)
        outs["ms"].append(m_u)
        outs["vas"].append(va.reshape(n_dec, n_new, A_WIDTH))

    st = lambda k: jnp.stack(outs[k])
    heads_last = lambda a: jnp.transpose(a.reshape(DEPTH, 1, B_HEADS, B_HEAD_DIM, n_seq), (0, 1, 4, 2, 3))
    return (xp.reshape(1, n_seq, D_MODEL), xs.reshape(n_dec, n_new, D_MODEL),
            heads_last(kbt), heads_last(vbt), st("cp"), st("np"), st("mp"),
            st("kbs"), st("vbs"), st("cs"), st("ns"), st("ms"), st("vas"))
```
